```python
import jax
import jax.numpy as jnp
from jax import lax
import numpy as np

D_MODEL = 1024
BATCH = 8
SEQ = 8192
DEPTH = 4

BRANCH_WIDTH = D_MODEL // 2
N_BRANCHES = 3
GM_CHUNK = 128
GM_GROUPS = 8
GM_GROUP_DIM = BRANCH_WIDTH // GM_GROUPS
DN_HEADS = 4
DN_HEAD_DIM = BRANCH_WIDTH // DN_HEADS
DN_CHUNK = 64
CONV_WIDTH = 4
SB_HEADS = 8
SB_HEAD_DIM = BRANCH_WIDTH // SB_HEADS
SB_BLOCK = 128
D_FF = 4 * D_MODEL
NORM_EPS = 1e-6
PROJ_SIZES = (BRANCH_WIDTH, BRANCH_WIDTH, BRANCH_WIDTH, BRANCH_WIDTH, BRANCH_WIDTH, BRANCH_WIDTH, DN_HEADS, DN_HEADS, BRANCH_WIDTH, BRANCH_WIDTH, BRANCH_WIDTH, N_BRANCHES * D_MODEL)
P_IN = 9 * BRANCH_WIDTH + 2 * DN_HEADS + N_BRANCHES * D_MODEL

kernel_name = "hybrid_gated_gmlp_deltanet_stickbreaking"


def rms_norm(x, g):
    xf = x.astype(jnp.float32)
    y = xf * lax.rsqrt(jnp.mean(xf * xf, axis=-1, keepdims=True) + NORM_EPS)
    return (y * g.astype(jnp.float32)).astype(x.dtype)


def layer_norm(x, g, b):
    xf = x.astype(jnp.float32)
    xc = xf - jnp.mean(xf, axis=-1, keepdims=True)
    y = xc * lax.rsqrt(jnp.mean(xc * xc, axis=-1, keepdims=True) + NORM_EPS)
    return (y * g.astype(jnp.float32) + b.astype(jnp.float32)).astype(x.dtype)


def l2_normalize(x):
    xf = x.astype(jnp.float32)
    return xf * lax.rsqrt(jnp.sum(xf * xf, axis=-1, keepdims=True) + NORM_EPS)


def split_columns(p):
    idx = []
    off = 0
    for s in PROJ_SIZES[:-1]:
        off += s
        idx.append(off)
    return jnp.split(p, idx, axis=-1)


def chunked_spatial_gating(u, v, w_s, b_s):
    bsz, s, _ = u.shape
    n = s // GM_CHUNK
    vh = v.reshape(bsz, n, GM_CHUNK, GM_GROUPS, GM_GROUP_DIM)
    causal = jnp.tril(jnp.ones((GM_CHUNK, GM_CHUNK), dtype=bool))
    w = jnp.where(causal[None], w_s, 0).astype(v.dtype)
    mixed = jnp.einsum('gts,bnsgc->bntgc', w, vh) + b_s.T.astype(v.dtype)[:, :, None]
    return u * mixed.reshape(bsz, s, BRANCH_WIDTH)


def causal_depthwise_conv(x, w):
    c = x.shape[-1]
    return lax.conv_general_dilated(
        x, w.astype(x.dtype)[:, None, :], window_strides=(1,),
        padding=[(CONV_WIDTH - 1, 0)], dimension_numbers=('NWC', 'WIO', 'NWC'),
        feature_group_count=c)


def gated_delta_rule(q, k, v, beta, g):
    f32 = jnp.float32
    bsz, s, h, dk = q.shape
    dv = v.shape[-1]
    c = DN_CHUNK
    n = s // c
    q = q.astype(f32) * (dk ** -0.5)

    def to_chunks(t):
        return t.astype(f32).reshape(bsz, n, c, h, -1).transpose(0, 3, 1, 2, 4)

    qc, kc, vc = to_chunks(q), to_chunks(k), to_chunks(v)
    bc = beta.astype(f32).reshape(bsz, n, c, h).transpose(0, 3, 1, 2)
    gcum = lax.cumsum(g.astype(f32).reshape(bsz, n, c, h).transpose(0, 3, 1, 2), axis=3)
    tri = jnp.tril(jnp.ones((c, c), dtype=bool))
    strict = jnp.tril(jnp.ones((c, c), dtype=bool), -1)
    decay = jnp.exp(jnp.where(tri, gcum[..., :, None] - gcum[..., None, :], -jnp.inf))
    kk = jnp.einsum('bhnid,bhnjd->bhnij', kc, kc)
    a_strict = jnp.where(strict, bc[..., :, None] * kk * decay, 0.0)
    lhs = a_strict + jnp.eye(c, dtype=f32)
    rhs = jnp.concatenate([vc * bc[..., None], kc * (bc * jnp.exp(gcum))[..., None]], axis=-1)
    sol = lax.linalg.triangular_solve(lhs, rhs, left_side=True, lower=True, unit_diagonal=True)
    u_val, w_kd = sol[..., :dv], sol[..., dv:]
    qk = jnp.where(tri, jnp.einsum('bhnid,bhnjd->bhnij', qc, kc) * decay, 0.0)
    q_dec = qc * jnp.exp(gcum)[..., None]
    g_last = gcum[..., -1]
    k_tail = kc * jnp.exp(g_last[..., None] - gcum)[..., None]

    def step(state, inp):
        u_i, w_i, qd_i, qk_i, kt_i, gl_i = inp
        v_new = u_i - jnp.einsum('bhcd,bhde->bhce', w_i, state)
        o = jnp.einsum('bhcd,bhde->bhce', qd_i, state) + jnp.einsum('bhcj,bhje->bhce', qk_i, v_new)
        state = state * jnp.exp(gl_i)[..., None, None] + jnp.einsum('bhcd,bhce->bhde', kt_i, v_new)
        return state, o

    xs = tuple(jnp.moveaxis(t, 2, 0) for t in (u_val, w_kd, q_dec, qk, k_tail, g_last))
    s0 = jnp.zeros((bsz, h, dk, dv), f32)
    _, o = lax.scan(step, s0, xs)
    return o.transpose(1, 0, 3, 2, 4).reshape(bsz, s, h, dv)


def stick_breaking_attention(q, k, v):
    f32 = jnp.float32
    bsz, s, h, dh = q.shape
    nb = s // SB_BLOCK
    qb = q.astype(f32).reshape(bsz, nb, SB_BLOCK, h, dh).transpose(1, 0, 2, 3, 4)
    kf = k.astype(f32)
    vf = v.astype(f32)
    kpos = jnp.arange(s)
    scale = dh ** -0.5

    def block(args):
        qi, i = args
        z = jnp.einsum('bthd,bshd->bhts', qi, kf) * scale
        tpos = i * SB_BLOCK + jnp.arange(SB_BLOCK)
        causal = kpos[None, :] < tpos[:, None]
        log_keep = jnp.where(causal, jax.nn.log_sigmoid(-z), 0.0)
        later = lax.cumsum(log_keep, axis=3, reverse=True) - log_keep
        w = jnp.where(causal, jnp.exp(jax.nn.log_sigmoid(z) + later), 0.0)
        return jnp.einsum('bhts,bshd->bthd', w, vf)

    o = lax.map(block, (qb, jnp.arange(nb)))
    return o.transpose(1, 0, 2, 3, 4).reshape(bsz, s, h * dh).astype(q.dtype)


def _fwd_setup_inputs(seed: int = 0) -> dict:
    key = jax.random.key(seed)
    ks = jax.random.split(key, 16)
    f32 = jnp.float32
    nrm = lambda k, shp: jax.random.normal(k, shp, f32)
    x = nrm(ks[0], (BATCH, SEQ, D_MODEL))
    norm_g = 1.0 + 0.05 * nrm(ks[1], (DEPTH, 4, D_MODEL))
    w_in = nrm(ks[2], (DEPTH, D_MODEL, P_IN)) * D_MODEL ** -0.5
    b_in = 0.01 * nrm(ks[3], (DEPTH, P_IN))
    sgu_ln_g = 1.0 + 0.05 * nrm(ks[4], (DEPTH, BRANCH_WIDTH))
    sgu_ln_b = 0.01 * nrm(ks[5], (DEPTH, BRANCH_WIDTH))
    w_spatial = nrm(ks[6], (DEPTH, GM_GROUPS, GM_CHUNK, GM_CHUNK)) * GM_CHUNK ** -0.5
    b_spatial = 1.0 + 0.05 * nrm(ks[7], (DEPTH, GM_GROUPS, GM_CHUNK))
    conv_w = nrm(ks[8], (DEPTH, CONV_WIDTH, 3 * BRANCH_WIDTH)) * CONV_WIDTH ** -0.5
    a_log = jnp.log(jax.random.uniform(ks[9], (DEPTH, DN_HEADS), f32, 1.0, 16.0))
    dt = jnp.exp(jax.random.uniform(ks[10], (DEPTH, DN_HEADS), f32, float(np.log(1e-3)), float(np.log(1e-1))))
    dt_bias = dt + jnp.log(-jnp.expm1(-dt))
    dn_norm_g = 1.0 + 0.05 * nrm(ks[11], (DEPTH, DN_HEAD_DIM))
    w_branch = nrm(ks[12], (DEPTH, N_BRANCHES, BRANCH_WIDTH, D_MODEL)) * BRANCH_WIDTH ** -0.5
    w_out = nrm(ks[13], (DEPTH, D_MODEL, D_MODEL)) * D_MODEL ** -0.5
    w_ff1 = nrm(ks[14], (DEPTH, D_MODEL, D_FF)) * D_MODEL ** -0.5
    w_ff2 = nrm(ks[15], (DEPTH, D_FF, D_MODEL)) * D_FF ** -0.5
    return {"x": x, "norm_g": norm_g, "w_in": w_in, "b_in": b_in,
            "sgu_ln_g": sgu_ln_g, "sgu_ln_b": sgu_ln_b, "w_spatial": w_spatial, "b_spatial": b_spatial,
            "conv_w": conv_w, "a_log": a_log, "dt_bias": dt_bias, "dn_norm_g": dn_norm_g,
            "w_branch": w_branch, "w_out": w_out, "w_ff1": w_ff1, "w_ff2": w_ff2}


def _fwd_reference(x, norm_g, w_in, b_in, sgu_ln_g, sgu_ln_b, w_spatial, b_spatial, conv_w, a_log, dt_bias, dn_norm_g, w_branch, w_out, w_ff1, w_ff2):
    bsz, s, _ = x.shape
    for l in range(DEPTH):
        h = rms_norm(x, norm_g[l, 0])
        p = h @ w_in[l] + b_in[l]
        (a_u, a_v, b_q, b_k, b_v, b_z, b_beta, b_dec, c_q, c_k, c_v, gates) = split_columns(p)

        a_u = jax.nn.gelu(a_u, approximate=False)
        a_v = layer_norm(jax.nn.gelu(a_v, approximate=False), sgu_ln_g[l], sgu_ln_b[l])
        y_a = chunked_spatial_gating(a_u, a_v, w_spatial[l], b_spatial[l])

        qkv = jax.nn.silu(causal_depthwise_conv(jnp.concatenate([b_q, b_k, b_v], axis=-1), conv_w[l]))
        dq, dk_, dv_ = jnp.split(qkv, 3, axis=-1)
        dq = l2_normalize(dq.reshape(bsz, s, DN_HEADS, DN_HEAD_DIM))
        dk_ = l2_normalize(dk_.reshape(bsz, s, DN_HEADS, DN_HEAD_DIM))
        dv_ = dv_.reshape(bsz, s, DN_HEADS, DN_HEAD_DIM)
        beta = jax.nn.sigmoid(b_beta.astype(jnp.float32))
        g = -jnp.exp(a_log[l].astype(jnp.float32)) * jax.nn.softplus(b_dec.astype(jnp.float32) + dt_bias[l].astype(jnp.float32))
        o = gated_delta_rule(dq, dk_, dv_, beta, g).astype(x.dtype)
        o = rms_norm(o, dn_norm_g[l]) * jax.nn.silu(b_z.reshape(bsz, s, DN_HEADS, DN_HEAD_DIM))
        y_b = o.reshape(bsz, s, BRANCH_WIDTH)

        y_c = stick_breaking_attention(c_q.reshape(bsz, s, SB_HEADS, SB_HEAD_DIM),
                                       c_k.reshape(bsz, s, SB_HEADS, SB_HEAD_DIM),
                                       c_v.reshape(bsz, s, SB_HEADS, SB_HEAD_DIM))

        branches = jnp.stack([y_a, y_b, y_c], axis=0)
        proj = jnp.einsum('nbsw,nwd->bsnd', branches, w_branch[l])
        gate = jax.nn.sigmoid(gates.reshape(bsz, s, N_BRANCHES, D_MODEL))
        mixed = jnp.sum(gate * proj, axis=2) @ w_out[l]
        x = x + rms_norm(mixed, norm_g[l, 1])

        h = rms_norm(x, norm_g[l, 2])
        f = jnp.square(jax.nn.relu(h @ w_ff1[l])) @ w_ff2[l]
        x = x + rms_norm(f, norm_g[l, 3])
    return x


import jax as _jax
import jax.numpy as _jnp

TWIN_FORMAT = 'train_step'
FWD_PARAMS = ['x', 'norm_g', 'w_in', 'b_in', 'sgu_ln_g', 'sgu_ln_b', 'w_spatial', 'b_spatial', 'conv_w', 'a_log', 'dt_bias', 'dn_norm_g', 'w_branch', 'w_out', 'w_ff1', 'w_ff2']
TWIN_WEIGHTS = ['norm_g', 'w_in', 'b_in', 'sgu_ln_g', 'sgu_ln_b', 'w_spatial', 'b_spatial', 'conv_w', 'a_log', 'dt_bias', 'dn_norm_g', 'w_branch', 'w_out', 'w_ff1', 'w_ff2']
TWIN_DIFF_INPUT = 'x'
TWIN_INPUTS = ['x', 'norm_g', 'w_in', 'b_in', 'sgu_ln_g', 'sgu_ln_b', 'w_spatial', 'b_spatial', 'conv_w', 'a_log', 'dt_bias', 'dn_norm_g', 'w_branch', 'w_out', 'w_ff1', 'w_ff2', 'loss_target', 'm_norm_g', 'm_w_in', 'm_b_in', 'm_sgu_ln_g', 'm_sgu_ln_b', 'm_w_spatial', 'm_b_spatial', 'm_conv_w', 'm_a_log', 'm_dt_bias', 'm_dn_norm_g', 'm_w_branch', 'm_w_out', 'm_w_ff1', 'm_w_ff2', 'v_norm_g', 'v_w_in', 'v_b_in', 'v_sgu_ln_g', 'v_sgu_ln_b', 'v_w_spatial', 'v_b_spatial', 'v_conv_w', 'v_a_log', 'v_dt_bias', 'v_dn_norm_g', 'v_w_branch', 'v_w_out', 'v_w_ff1', 'v_w_ff2']
TWIN_OUTPUTS = ['loss', 'grad_x', 'grad_norm_g', 'grad_w_in', 'grad_b_in', 'grad_sgu_ln_g', 'grad_sgu_ln_b', 'grad_w_spatial', 'grad_b_spatial', 'grad_conv_w', 'grad_a_log', 'grad_dt_bias', 'grad_dn_norm_g', 'grad_w_branch', 'grad_w_out', 'grad_w_ff1', 'grad_w_ff2', 'delta_norm_g', 'delta_w_in', 'delta_b_in', 'delta_sgu_ln_g', 'delta_sgu_ln_b', 'delta_w_spatial', 'delta_b_spatial', 'delta_conv_w', 'delta_a_log', 'delta_dt_bias', 'delta_dn_norm_g', 'delta_w_branch', 'delta_w_out', 'delta_w_ff1', 'delta_w_ff2', 'new_m_norm_g', 'new_m_w_in', 'new_m_b_in', 'new_m_sgu_ln_g', 'new_m_sgu_ln_b', 'new_m_w_spatial', 'new_m_b_spatial', 'new_m_conv_w', 'new_m_a_log', 'new_m_dt_bias', 'new_m_dn_norm_g', 'new_m_w_branch', 'new_m_w_out', 'new_m_w_ff1', 'new_m_w_ff2', 'new_v_norm_g', 'new_v_w_in', 'new_v_b_in', 'new_v_sgu_ln_g', 'new_v_sgu_ln_b', 'new_v_w_spatial', 'new_v_b_spatial', 'new_v_conv_w', 'new_v_a_log', 'new_v_dt_bias', 'new_v_dn_norm_g', 'new_v_w_branch', 'new_v_w_out', 'new_v_w_ff1', 'new_v_w_ff2']
TWIN_LEAF_KINDS = {'loss': 'loss', 'grad_x': 'grad_x', 'grad_norm_g': 'grad_w', 'grad_w_in': 'grad_w', 'grad_b_in': 'grad_w', 'grad_sgu_ln_g': 'grad_w', 'grad_sgu_ln_b': 'grad_w', 'grad_w_spatial': 'grad_w', 'grad_b_spatial': 'grad_w', 'grad_conv_w': 'grad_w', 'grad_a_log': 'grad_w', 'grad_dt_bias': 'grad_w', 'grad_dn_norm_g': 'grad_w', 'grad_w_branch': 'grad_w', 'grad_w_out': 'grad_w', 'grad_w_ff1': 'grad_w', 'grad_w_ff2': 'grad_w', 'delta_norm_g': 'delta_w', 'delta_w_in': 'delta_w', 'delta_b_in': 'delta_w', 'delta_sgu_ln_g': 'delta_w', 'delta_sgu_ln_b': 'delta_w', 'delta_w_spatial': 'delta_w', 'delta_b_spatial': 'delta_w', 'delta_conv_w': 'delta_w', 'delta_a_log': 'delta_w', 'delta_dt_bias': 'delta_w', 'delta_dn_norm_g': 'delta_w', 'delta_w_branch': 'delta_w', 'delta_w_out': 'delta_w', 'delta_w_ff1': 'delta_w', 'delta_w_ff2': 'delta_w', 'new_m_norm_g': 'new_m', 'new_m_w_in': 'new_m', 'new_m_b_in': 'new_m', 'new_m_sgu_ln_g': 'new_m', 'new_m_sgu_ln_b': 'new_m', 'new_m_w_spatial': 'new_m', 'new_m_b_spatial': 'new_m', 'new_m_conv_w': 'new_m', 'new_m_a_log': 'new_m', 'new_m_dt_bias': 'new_m', 'new_m_dn_norm_g': 'new_m', 'new_m_w_branch': 'new_m', 'new_m_w_out': 'new_m', 'new_m_w_ff1': 'new_m', 'new_m_w_ff2': 'new_m', 'new_v_norm_g': 'new_v', 'new_v_w_in': 'new_v', 'new_v_b_in': 'new_v', 'new_v_sgu_ln_g': 'new_v', 'new_v_sgu_ln_b': 'new_v', 'new_v_w_spatial': 'new_v', 'new_v_b_spatial': 'new_v', 'new_v_conv_w': 'new_v', 'new_v_a_log': 'new_v', 'new_v_dt_bias': 'new_v', 'new_v_dn_norm_g': 'new_v', 'new_v_w_branch': 'new_v', 'new_v_w_out': 'new_v', 'new_v_w_ff1': 'new_v', 'new_v_w_ff2': 'new_v'}


def _forward(args):
    return _fwd_reference(*[args[k] for k in FWD_PARAMS])


def _output_shape():
    def fwd():
        inp = _fwd_setup_inputs(0)
        return _fwd_reference(*[inp[k] for k in FWD_PARAMS])
    out = _jax.eval_shape(fwd)
    return out.shape, out.dtype

N_MICROBATCH = 1
ADAM_LR = 0.001
ADAM_B1 = 0.9
ADAM_B2 = 0.999
ADAM_EPS = 1e-08
ADAM_WD = 0.01
ADAM_STEP = 10
PER_EXAMPLE_BATCH_AXIS = {'x': 0, 'loss_target': 0}
SHARED_INPUTS = []
_WEIGHT_DTYPES = {'norm_g': _jnp.float32, 'w_in': _jnp.float32, 'b_in': _jnp.float32, 'sgu_ln_g': _jnp.float32, 'sgu_ln_b': _jnp.float32, 'w_spatial': _jnp.float32, 'b_spatial': _jnp.float32, 'conv_w': _jnp.float32, 'a_log': _jnp.float32, 'dt_bias': _jnp.float32, 'dn_norm_g': _jnp.float32, 'w_branch': _jnp.float32, 'w_out': _jnp.float32, 'w_ff1': _jnp.float32, 'w_ff2': _jnp.float32}
MOMENT_SCALE = {'norm_g': 5.515398e+01, 'w_in': 7.686145e+00, 'b_in': 3.249753e+01, 'sgu_ln_g': 9.138379e-01, 'sgu_ln_b': 1.312051e+00, 'w_spatial': 5.939543e-01, 'b_spatial': 1.606598e+00, 'conv_w': 5.956070e+00, 'a_log': 1.715413e+01, 'dt_bias': 1.689330e+01, 'dn_norm_g': 2.851161e+01, 'w_branch': 1.800977e+01, 'w_out': 3.129048e+01, 'w_ff1': 8.803336e+00, 'w_ff2': 3.890147e+01}


def _to_microbatches(a, axis):
    t = _jnp.moveaxis(a, axis, 0)
    t = t.reshape((N_MICROBATCH, t.shape[0] // N_MICROBATCH) + t.shape[1:])
    return _jnp.moveaxis(t, 1, axis + 1)


def setup_inputs(seed: int = 0) -> dict:
    inp = _fwd_setup_inputs(seed)
    key = _jax.random.fold_in(_jax.random.key(seed), 7919)
    shape, _ = _output_shape()
    out = dict(inp)
    out["loss_target"] = _jax.random.normal(_jax.random.fold_in(key, 0), shape, _jnp.float32)
    for i, name in enumerate(TWIN_WEIGHTS):
        w = inp[name].astype(_jnp.float32)
        if MOMENT_SCALE is None:
            s = _jnp.sqrt(_jnp.mean(_jnp.square(w)) + 1e-30)
        else:
            s = MOMENT_SCALE[name]
        km, kv = _jax.random.split(_jax.random.fold_in(key, i + 1))
        out[name] = w
        out["m_" + name] = s * _jax.random.normal(km, w.shape, _jnp.float32)
        out["v_" + name] = (s * s) * _jax.random.uniform(kv, w.shape, _jnp.float32, 0.5, 1.5)
    if N_MICROBATCH > 1:
        for name, axis in PER_EXAMPLE_BATCH_AXIS.items():
            out[name] = _to_microbatches(out[name], axis)
    return {'x': out['x'], 'norm_g': out['norm_g'], 'w_in': out['w_in'], 'b_in': out['b_in'], 'sgu_ln_g': out['sgu_ln_g'], 'sgu_ln_b': out['sgu_ln_b'], 'w_spatial': out['w_spatial'], 'b_spatial': out['b_spatial'], 'conv_w': out['conv_w'], 'a_log': out['a_log'], 'dt_bias': out['dt_bias'], 'dn_norm_g': out['dn_norm_g'], 'w_branch': out['w_branch'], 'w_out': out['w_out'], 'w_ff1': out['w_ff1'], 'w_ff2': out['w_ff2'], 'loss_target': out['loss_target'], 'm_norm_g': out['m_norm_g'], 'm_w_in': out['m_w_in'], 'm_b_in': out['m_b_in'], 'm_sgu_ln_g': out['m_sgu_ln_g'], 'm_sgu_ln_b': out['m_sgu_ln_b'], 'm_w_spatial': out['m_w_spatial'], 'm_b_spatial': out['m_b_spatial'], 'm_conv_w': out['m_conv_w'], 'm_a_log': out['m_a_log'], 'm_dt_bias': out['m_dt_bias'], 'm_dn_norm_g': out['m_dn_norm_g'], 'm_w_branch': out['m_w_branch'], 'm_w_out': out['m_w_out'], 'm_w_ff1': out['m_w_ff1'], 'm_w_ff2': out['m_w_ff2'], 'v_norm_g': out['v_norm_g'], 'v_w_in': out['v_w_in'], 'v_b_in': out['v_b_in'], 'v_sgu_ln_g': out['v_sgu_ln_g'], 'v_sgu_ln_b': out['v_sgu_ln_b'], 'v_w_spatial': out['v_w_spatial'], 'v_b_spatial': out['v_b_spatial'], 'v_conv_w': out['v_conv_w'], 'v_a_log': out['v_a_log'], 'v_dt_bias': out['v_dt_bias'], 'v_dn_norm_g': out['v_dn_norm_g'], 'v_w_branch': out['v_w_branch'], 'v_w_out': out['v_w_out'], 'v_w_ff1': out['v_w_ff1'], 'v_w_ff2': out['v_w_ff2']}


def _loss(weights, diff, rest, loss_target):
    with _jax.named_scope("forward"):
        args = {**rest, TWIN_DIFF_INPUT: diff, **{k: w.astype(_WEIGHT_DTYPES[k]) for k, w in weights.items()}}
        y = _forward(args)
    with _jax.named_scope("loss_head"):
        err = _jnp.square(y.astype(_jnp.float32) - loss_target)
        return 0.5 * _jnp.sum(_jnp.mean(err, axis=-1)) if err.ndim else 0.5 * err


def _adamw(w, g, m, v):
    m = ADAM_B1 * m + (1.0 - ADAM_B1) * g
    v = ADAM_B2 * v + (1.0 - ADAM_B2) * _jnp.square(g)
    m_hat = m / (1.0 - ADAM_B1 ** ADAM_STEP)
    v_hat = v / (1.0 - ADAM_B2 ** ADAM_STEP)
    delta = -ADAM_LR * (m_hat / (_jnp.sqrt(v_hat) + ADAM_EPS) + ADAM_WD * w)
    return delta, m, v


def reference(x, norm_g, w_in, b_in, sgu_ln_g, sgu_ln_b, w_spatial, b_spatial, conv_w, a_log, dt_bias, dn_norm_g, w_branch, w_out, w_ff1, w_ff2, loss_target, m_norm_g, m_w_in, m_b_in, m_sgu_ln_g, m_sgu_ln_b, m_w_spatial, m_b_spatial, m_conv_w, m_a_log, m_dt_bias, m_dn_norm_g, m_w_branch, m_w_out, m_w_ff1, m_w_ff2, v_norm_g, v_w_in, v_b_in, v_sgu_ln_g, v_sgu_ln_b, v_w_spatial, v_b_spatial, v_conv_w, v_a_log, v_dt_bias, v_dn_norm_g, v_w_branch, v_w_out, v_w_ff1, v_w_ff2):
    given = dict(x=x, norm_g=norm_g, w_in=w_in, b_in=b_in, sgu_ln_g=sgu_ln_g, sgu_ln_b=sgu_ln_b, w_spatial=w_spatial, b_spatial=b_spatial, conv_w=conv_w, a_log=a_log, dt_bias=dt_bias, dn_norm_g=dn_norm_g, w_branch=w_branch, w_out=w_out, w_ff1=w_ff1, w_ff2=w_ff2, loss_target=loss_target, m_norm_g=m_norm_g, m_w_in=m_w_in, m_b_in=m_b_in, m_sgu_ln_g=m_sgu_ln_g, m_sgu_ln_b=m_sgu_ln_b, m_w_spatial=m_w_spatial, m_b_spatial=m_b_spatial, m_conv_w=m_conv_w, m_a_log=m_a_log, m_dt_bias=m_dt_bias, m_dn_norm_g=m_dn_norm_g, m_w_branch=m_w_branch, m_w_out=m_w_out, m_w_ff1=m_w_ff1, m_w_ff2=m_w_ff2, v_norm_g=v_norm_g, v_w_in=v_w_in, v_b_in=v_b_in, v_sgu_ln_g=v_sgu_ln_g, v_sgu_ln_b=v_sgu_ln_b, v_w_spatial=v_w_spatial, v_b_spatial=v_b_spatial, v_conv_w=v_conv_w, v_a_log=v_a_log, v_dt_bias=v_dt_bias, v_dn_norm_g=v_dn_norm_g, v_w_branch=v_w_branch, v_w_out=v_w_out, v_w_ff1=v_w_ff1, v_w_ff2=v_w_ff2)
    weights = {n: given[n] for n in TWIN_WEIGHTS}
    shared = {n: given[n] for n in SHARED_INPUTS}
    per_example = {n: given[n] for n in ['x']}
    grad_fn = _jax.value_and_grad(_loss, argnums=(0, 1))

    def one_microbatch(ex, loss_target):
        ex = dict(ex)
        diff = ex.pop(TWIN_DIFF_INPUT)
        return grad_fn(weights, diff, {**shared, **ex}, loss_target)

    if N_MICROBATCH == 1:
        loss, (grad_w, grad_x) = one_microbatch(per_example, given["loss_target"])
    else:
        def body(carry, xs):
            loss_sum, grad_sum = carry
            l_k, (gw_k, gx_k) = one_microbatch(xs[0], xs[1])
            with _jax.named_scope("update"):
                return (loss_sum + l_k, _jax.tree.map(_jnp.add, grad_sum, gw_k)), gx_k

        init = (_jnp.zeros((), _jnp.float32), _jax.tree.map(_jnp.zeros_like, weights))
        (loss, grad_w), grad_x = _jax.lax.scan(body, init, (per_example, given["loss_target"]))
    with _jax.named_scope("update"):
        delta_w, new_m, new_v = {}, {}, {}
        for n in TWIN_WEIGHTS:
            delta_w[n], new_m[n], new_v[n] = _adamw(weights[n], grad_w[n], given["m_" + n], given["v_" + n])
    return (loss, grad_x, *[grad_w[n] for n in TWIN_WEIGHTS], *[delta_w[n] for n in TWIN_WEIGHTS],
            *[new_m[n] for n in TWIN_WEIGHTS], *[new_v[n] for n in TWIN_WEIGHTS])
```

```python
import functools

import jax
import jax.numpy as jnp
from jax import lax
from jax.experimental import pallas as pl
from jax.experimental.pallas import tpu as pltpu

F32 = jnp.float32
BF16 = jnp.bfloat16

D_MODEL = 1024
DEPTH = 4
BRANCH_WIDTH = 512
GM_CHUNK = 128
GM_GROUPS = 8
DN_HEADS = 4
DN_HEAD_DIM = 128
DN_BLOCK = 128
SB_HEAD_DIM = 64
SB_BLOCK = 128
D_FF = 4096
P_IN = 7688
P_PAD = 8192
NORM_EPS = 1e-6
ADAM_LR, ADAM_B1, ADAM_B2, ADAM_EPS, ADAM_WD, ADAM_STEP = 0.001, 0.9, 0.999, 1e-08, 0.01, 10

OFF_AU, OFF_AV, OFF_BQ, OFF_BK, OFF_BV, OFF_BZ = 0, 512, 1024, 1536, 2048, 2560
OFF_GATE = 3072
OFF_CQ, OFF_CK, OFF_CV = 6144, 6656, 7168
OFF_BD = 7680

LANES = 1024
VMEM_LIMIT_BYTES = 56 * 1024 * 1024

_HI = lax.Precision.HIGHEST


def _params(sem):
    return pltpu.CompilerParams(dimension_semantics=sem, vmem_limit_bytes=VMEM_LIMIT_BYTES)


def _mm(a, b, *, name, out_dtype=F32, bias=None, trans_b=False, tm=1024, tn=1024, tk=1024):
    m, k = a.shape
    n = b.shape[0] if trans_b else b.shape[1]
    tm, tn, tk = min(tm, m), min(tn, n), min(tk, k)
    assert m % tm == 0 and n % tn == 0 and k % tk == 0, (a.shape, b.shape)
    nk = k // tk
    dn = (((1,), (1,)), ((), ())) if trans_b else (((1,), (0,)), ((), ()))

    def body(*refs):
        if bias is None:
            a_ref, b_ref, o_ref, acc = refs
        else:
            a_ref, b_ref, bias_ref, o_ref, acc = refs
        kk = pl.program_id(2)
        part = lax.dot_general(a_ref[...].astype(BF16), b_ref[...].astype(BF16), dn, preferred_element_type=F32)

        @pl.when(kk == 0)
        def _():
            acc[...] = part

        @pl.when(kk > 0)
        def _():
            acc[...] += part

        @pl.when(kk == nk - 1)
        def _():
            r = acc[...]
            if bias is not None:
                r = r + bias_ref[...]
            o_ref[...] = r.astype(out_dtype)

    in_specs = [pl.BlockSpec((tm, tk), lambda i, j, kk: (i, kk))]
    if trans_b:
        in_specs.append(pl.BlockSpec((tn, tk), lambda i, j, kk: (j, kk)))
    else:
        in_specs.append(pl.BlockSpec((tk, tn), lambda i, j, kk: (kk, j)))
    args = [a, b]
    if bias is not None:
        in_specs.append(pl.BlockSpec((1, tn), lambda i, j, kk: (0, j)))
        args.append(bias)
    return pl.pallas_call(
        body, name=name, grid=(m // tm, n // tn, nk),
        in_specs=in_specs, out_specs=pl.BlockSpec((tm, tn), lambda i, j, kk: (i, j)),
        out_shape=jax.ShapeDtypeStruct((m, n), out_dtype),
        scratch_shapes=[pltpu.VMEM((tm, tn), F32)],
        compiler_params=_params(("parallel", "parallel", "arbitrary")),
    )(*args)


def _mm_tn(a, b, *, name, tm=1024, tn=1024, ts=1024):
    s, ka = a.shape
    n = b.shape[1]
    tm, tn, ts = min(tm, ka), min(tn, n), min(ts, s)
    assert ka % tm == 0 and n % tn == 0 and s % ts == 0, (a.shape, b.shape)

    def body(a_ref, b_ref, o_ref):
        part = lax.dot_general(a_ref[...].astype(BF16), b_ref[...].astype(BF16), (((0,), (0,)), ((), ())),
                               preferred_element_type=F32)

        @pl.when(pl.program_id(2) == 0)
        def _():
            o_ref[...] = part

        @pl.when(pl.program_id(2) > 0)
        def _():
            o_ref[...] += part

    return pl.pallas_call(
        body, name=name, grid=(ka // tm, n // tn, s // ts),
        in_specs=[pl.BlockSpec((ts, tm), lambda i, j, r: (r, i)), pl.BlockSpec((ts, tn), lambda i, j, r: (r, j))],
        out_specs=pl.BlockSpec((tm, tn), lambda i, j, r: (i, j)),
        out_shape=jax.ShapeDtypeStruct((ka, n), F32),
        compiler_params=_params(("parallel", "parallel", "arbitrary")),
    )(a, b)


def _col_block(i, *, c):
    return (i, c)


def _whole(i, *, nd):
    return (0,) * nd


def _row_specs(rows, ts):
    specs = []
    for arr, off, w in rows:
        assert off % w == 0 and arr.shape[0] % ts == 0
        specs.append(pl.BlockSpec((ts, w), functools.partial(_col_block, c=off // w)))
    return specs


def _row_fwd(fn, rows, params, outs, *, ts, name):
    s = rows[0][0].shape[0]
    ts = min(ts, s)
    nr, npar = len(rows), len(params)

    def body(*refs):
        rv = [r[...].astype(F32) for r in refs[:nr]]
        pv = [p[...] for p in refs[nr:nr + npar]]
        for o_ref, val in zip(refs[nr + npar:], fn(pv, rv)):
            o_ref[...] = val.astype(o_ref.dtype)

    in_specs = _row_specs(rows, ts) + [pl.BlockSpec(p.shape, functools.partial(_whole, nd=p.ndim)) for p in params]
    res = pl.pallas_call(
        body, name=name, grid=(s // ts,), in_specs=in_specs,
        out_specs=[pl.BlockSpec((ts, w), lambda i: (i, 0)) for w, _ in outs],
        out_shape=[jax.ShapeDtypeStruct((s, w), dt) for w, dt in outs],
        compiler_params=_params(("parallel",)),
    )(*[r[0] for r in rows], *params)
    return list(res)


def _row_bwd(fn, rows, params, cts, *, ts, name, row_grads=True):
    s = rows[0][0].shape[0]
    ts = min(ts, s)
    nr, npar, nc = len(rows), len(params), len(cts)
    n_dr = nr if row_grads else 0

    def body(*refs):
        rv = [r[...].astype(F32) for r in refs[:nr]]
        pv = [p[...] for p in refs[nr:nr + npar]]
        cv = [c[...].astype(F32) for c in refs[nr + npar:nr + npar + nc]]
        out_refs = refs[nr + npar + nc:]
        _, vjp = jax.vjp(lambda p, r: tuple(fn(p, r)), pv, rv)
        dp, dr = vjp(tuple(cv))
        for o_ref, val in zip(out_refs[:n_dr], dr):
            o_ref[...] = val

        @pl.when(pl.program_id(0) == 0)
        def _():
            for o_ref in out_refs[n_dr:]:
                o_ref[...] = jnp.zeros_like(o_ref)

        for o_ref, val in zip(out_refs[n_dr:], dp):
            o_ref[...] += val

    in_specs = (_row_specs(rows, ts) + [pl.BlockSpec(p.shape, functools.partial(_whole, nd=p.ndim)) for p in params]
                + _row_specs(cts, ts))
    out_specs = [pl.BlockSpec((ts, w), lambda i: (i, 0)) for _, _, w in rows[:n_dr]]
    out_specs += [pl.BlockSpec(p.shape, functools.partial(_whole, nd=p.ndim)) for p in params]
    out_shape = [jax.ShapeDtypeStruct((s, w), F32) for _, _, w in rows[:n_dr]]
    out_shape += [jax.ShapeDtypeStruct(p.shape, F32) for p in params]
    res = pl.pallas_call(
        body, name=name, grid=(s // ts,), in_specs=in_specs, out_specs=out_specs, out_shape=out_shape,
        compiler_params=_params(("arbitrary",)),
    )(*[r[0] for r in rows], *params, *[c[0] for c in cts])
    res = list(res)
    return res[:n_dr], res[n_dr:]


def _rms(x, g):
    return x * lax.rsqrt(jnp.mean(x * x, axis=-1, keepdims=True) + NORM_EPS) * g


def _gelu(x):
    return 0.5 * x * (1.0 + lax.erf(x * (2.0 ** -0.5)))


def _softplus(x):
    return jnp.maximum(x, 0.0) + jnp.log1p(jnp.exp(-jnp.abs(x)))


def _iota2(shape, dim):
    return lax.broadcasted_iota(jnp.int32, shape, dim)


def _fn_rms(pv, rv):
    return [_rms(rv[0], pv[0])]


def _fn_rms_keep(pv, rv):
    return [_rms(rv[0], pv[0]), rv[0]]


def _fn_resid_rms(pv, rv):
    return [rv[0] + _rms(rv[1], pv[0])]


def _fn_rms_branch(pv, rv):
    return [_rms(rv[0], pv[0])]


def _fn_relu2(pv, rv):
    return [jnp.square(jnp.maximum(rv[0], 0.0))]


def _fn_bias(pv, rv):
    return [rv[0] + pv[0]]


def _fn_merge(pv, rv):
    return [sum(jax.nn.sigmoid(rv[3 + i]) * rv[i] for i in range(3))]


def _fn_gmlp(pv, rv):
    ln_g, ln_b, w_sp, b_t = pv
    u = _gelu(rv[0])
    v = _gelu(rv[1])
    vc = v - jnp.mean(v, axis=-1, keepdims=True)
    v = vc * lax.rsqrt(jnp.mean(vc * vc, axis=-1, keepdims=True) + NORM_EPS) * ln_g + ln_b
    t = GM_CHUNK
    causal = _iota2((t, t), 1) <= _iota2((t, t), 0)
    first = _iota2((t, 128), 1) < 64
    expand = (_iota2((128, BRANCH_WIDTH), 0) == _iota2((128, BRANCH_WIDTH), 1) // 64).astype(F32)
    b_full = jnp.dot(b_t, expand, precision=_HI, preferred_element_type=F32)
    w_bf = [jnp.where(causal, w_sp[g], 0.0).astype(BF16) for g in range(GM_GROUPS)]
    chunks = []
    for c in range(rv[0].shape[0] // t):
        pairs = []
        for p in range(GM_GROUPS // 2):
            vp = v[c * t:(c + 1) * t, 128 * p:128 * (p + 1)].astype(BF16)
            m0 = jnp.dot(w_bf[2 * p], vp, preferred_element_type=F32)
            m1 = jnp.dot(w_bf[2 * p + 1], vp, preferred_element_type=F32)
            pairs.append(jnp.where(first, m0, m1))
        chunks.append(jnp.concatenate(pairs, axis=1) + b_full)
    return [u * jnp.concatenate(chunks, axis=0)]


def _head_expand(col0):
    return (_iota2((128, BRANCH_WIDTH), 0) == _iota2((128, BRANCH_WIDTH), 1) // DN_HEAD_DIM + col0).astype(F32)


def _fn_dn_in(pv, rv):
    conv_w, a_log, dt_b = pv
    c = sum(conv_w[j:j + 1, :] * rv[j] for j in range(4))
    a = c * jax.nn.sigmoid(c)
    outs = []
    for part in range(3):
        heads = []
        for h in range(DN_HEADS):
            lo = part * BRANCH_WIDTH + h * DN_HEAD_DIM
            xh = a[:, lo:lo + DN_HEAD_DIM]
            if part < 2:
                xh = xh * lax.rsqrt(jnp.sum(xh * xh, axis=-1, keepdims=True) + NORM_EPS)
            heads.append(xh)
        outs.append(jnp.concatenate(heads, axis=1))
    bd = rv[4]
    beta = jax.nn.sigmoid(bd)
    g = -jnp.exp(a_log) * _softplus(bd + dt_b)
    outs.append(jnp.dot(beta, _head_expand(0), precision=_HI, preferred_element_type=F32))
    outs.append(jnp.dot(g, _head_expand(DN_HEADS), precision=_HI, preferred_element_type=F32))
    return outs


def _fn_dn_out(pv, rv):
    heads = []
    for h in range(DN_HEADS):
        sl = slice(h * DN_HEAD_DIM, (h + 1) * DN_HEAD_DIM)
        z = rv[1][:, sl]
        heads.append(_rms(rv[0][:, sl], pv[0]) * (z * jax.nn.sigmoid(z)))
    return [jnp.concatenate(heads, axis=1)]


def _dot(a, b):
    return jnp.dot(a, b, precision=_HI, preferred_element_type=F32)


def _dot_nt(a, b):
    return lax.dot_general(a, b, (((1,), (1,)), ((), ())), precision=_HI, preferred_element_type=F32)


def _dot_tn(a, b):
    return lax.dot_general(a, b, (((0,), (0,)), ((), ())), precision=_HI, preferred_element_type=F32)


@jax.custom_vjp
def _unit_lower_inverse(a):
    c = a.shape[0]
    eye = (_iota2((c, c), 0) == _iota2((c, c), 1)).astype(F32)
    pw = -a
    x = eye + pw
    span = 2
    while span < c + 1:
        pw = _dot(pw, pw)
        x = x + _dot(x, pw)
        span *= 2
    return x


def _unit_lower_inverse_fwd(a):
    x = _unit_lower_inverse(a)
    return x, x


def _unit_lower_inverse_bwd(x, dx):
    return (-_dot_tn(x, _dot_nt(dx, x)),)


_unit_lower_inverse.defvjp(_unit_lower_inverse_fwd, _unit_lower_inverse_bwd)


def _delta_chunk(state, q, k, v, beta, g):
    c = DN_BLOCK
    row, col = _iota2((c, c), 0), _iota2((c, c), 1)
    tri, strict = col <= row, col < row
    gc = _dot(tri.astype(F32), g)
    gl = _dot(jnp.ones((c, c), F32), g)
    decay = jnp.where(tri, jnp.exp(jnp.where(tri, gc - gc.T, 0.0)), 0.0)
    qs = q * (DN_HEAD_DIM ** -0.5)
    a = jnp.where(strict, beta * _dot_nt(k, k) * decay, 0.0)
    x = _unit_lower_inverse(a)
    eg = jnp.exp(gc)
    u = _dot(x, v * beta)
    wk = _dot(x, k * (beta * eg))
    qk = jnp.where(tri, _dot_nt(qs, k) * decay, 0.0)
    v_new = u - _dot(wk, state)
    o = _dot(qs * eg, state) + _dot(qk, v_new)
    nxt = state * jnp.exp(gl) + _dot_tn(k * jnp.exp(gl - gc), v_new)
    return o, nxt


def _head(h):
    return slice(h * DN_HEAD_DIM, (h + 1) * DN_HEAD_DIM)


def _delta_fwd(q, k, v, beta, g, *, name):
    s = q.shape[0]
    c = DN_BLOCK
    nc = s // c

    def body(q_ref, k_ref, v_ref, b_ref, g_ref, o_ref, sp_ref, st):
        @pl.when(pl.program_id(0) == 0)
        def _():
            st[...] = jnp.zeros_like(st)

        outs = []
        for h in range(DN_HEADS):
            sl = _head(h)
            state = st[h]
            sp_ref[0, h] = state
            o, nxt = _delta_chunk(state, q_ref[:, sl], k_ref[:, sl], v_ref[:, sl], b_ref[:, sl], g_ref[:, sl])
            st[h] = nxt
            outs.append(o)
        o_ref[...] = jnp.concatenate(outs, axis=1)

    blk = pl.BlockSpec((c, BRANCH_WIDTH), lambda n: (n, 0))
    return pl.pallas_call(
        body, name=name, grid=(nc,), in_specs=[blk] * 5,
        out_specs=[blk, pl.BlockSpec((1, DN_HEADS, DN_HEAD_DIM, DN_HEAD_DIM), lambda n: (n, 0, 0, 0))],
        out_shape=[jax.ShapeDtypeStruct((s, BRANCH_WIDTH), F32),
                   jax.ShapeDtypeStruct((nc, DN_HEADS, DN_HEAD_DIM, DN_HEAD_DIM), F32)],
        scratch_shapes=[pltpu.VMEM((DN_HEADS, DN_HEAD_DIM, DN_HEAD_DIM), F32)],
        compiler_params=_params(("arbitrary",)),
    )(q, k, v, beta, g)


def _delta_bwd(q, k, v, beta, g, states, do, *, name):
    s = q.shape[0]
    c = DN_BLOCK
    nc = s // c

    def body(q_ref, k_ref, v_ref, b_ref, g_ref, sp_ref, do_ref, dq_ref, dk_ref, dv_ref, db_ref, dg_ref, dst):
        @pl.when(pl.program_id(0) == 0)
        def _():
            dst[...] = jnp.zeros_like(dst)

        grads = [[] for _ in range(5)]
        for h in range(DN_HEADS):
            sl = _head(h)
            _, vjp = jax.vjp(_delta_chunk, sp_ref[0, h], q_ref[:, sl], k_ref[:, sl], v_ref[:, sl], b_ref[:, sl],
                             g_ref[:, sl])
            d = vjp((do_ref[:, sl], dst[h]))
            dst[h] = d[0]
            for lst, val in zip(grads, d[1:]):
                lst.append(val)
        for o_ref, lst in zip((dq_ref, dk_ref, dv_ref, db_ref, dg_ref), grads):
            o_ref[...] = jnp.concatenate(lst, axis=1)

    blk = pl.BlockSpec((c, BRANCH_WIDTH), lambda n: (nc - 1 - n, 0))
    res = pl.pallas_call(
        body, name=name, grid=(nc,),
        in_specs=[blk] * 5 + [pl.BlockSpec((1, DN_HEADS, DN_HEAD_DIM, DN_HEAD_DIM), lambda n: (nc - 1 - n, 0, 0, 0)), blk],
        out_specs=[blk] * 5, out_shape=[jax.ShapeDtypeStruct((s, BRANCH_WIDTH), F32)] * 5,
        scratch_shapes=[pltpu.VMEM((DN_HEADS, DN_HEAD_DIM, DN_HEAD_DIM), F32)],
        compiler_params=_params(("arbitrary",)),
    )(q, k, v, beta, g, states, do)
    return list(res)


def _split_bf16(x):
    hi = x.astype(BF16)
    return hi, (x - hi.astype(F32)).astype(BF16)


def _sb_consts():
    t = SB_BLOCK
    row, col = _iota2((t, t), 0), _iota2((t, t), 1)
    ones = jnp.ones((t, t), BF16)
    later = jnp.concatenate([(row > col).astype(BF16), ones], axis=1)
    from_here = jnp.concatenate([(row >= col).astype(BF16), ones], axis=1)
    causal = col < row
    first = _iota2((t, 128), 1) < SB_HEAD_DIM
    return later, from_here, causal, first


def _sums(x, mat):
    hi, lo = _split_bf16(x)
    return jnp.dot(hi, mat, preferred_element_type=F32) + jnp.dot(lo, mat, preferred_element_type=F32)


def _sb_weights(qh, kb, acc, later, causal, masked):
    t = SB_BLOCK
    z = lax.dot_general(qh, kb, (((1,), (1,)), ((), ())), preferred_element_type=F32)
    lk = -_softplus(z)
    if masked:
        lk = jnp.where(causal, lk, 0.0)
    cs = _sums(lk, later)
    e = z + lk + cs[:, :t] + acc
    if masked:
        e = jnp.where(causal, e, -1e30)
    return lk, jnp.exp(e), cs[:, t:]


def _sb_fwd(p, *, name):
    s = p.shape[0]
    t = SB_BLOCK
    scale = SB_HEAD_DIM ** -0.5

    def body(q_ref, k_ref, v_ref, o_ref):
        i = pl.program_id(1)
        later, _, causal, first = _sb_consts()
        q = q_ref[...] * scale
        qs = (jnp.where(first, q, 0.0).astype(BF16), jnp.where(first, 0.0, q).astype(BF16))

        def block(j, carry, masked):
            start = pl.multiple_of(j * t, t)
            kb = k_ref[pl.ds(start, t), :].astype(BF16)
            vb = v_ref[pl.ds(start, t), :].astype(BF16)
            out = []
            for h in range(2):
                o_h, acc = carry[h]
                _, w, tot = _sb_weights(qs[h], kb, acc, later, causal, masked)
                out.append((o_h + jnp.dot(w.astype(BF16), vb, preferred_element_type=F32), acc + tot))
            return tuple(out)

        zero = jnp.zeros((t, 128), F32)
        carry = block(i, ((zero, zero), (zero, zero)), True)
        carry = lax.fori_loop(0, i, lambda jj, cr: block(i - 1 - jj, cr, False), carry)
        o_ref[...] = jnp.where(first, carry[0][0], carry[1][0])

    return pl.pallas_call(
        body, name=name, grid=(BRANCH_WIDTH // 128, s // t),
        in_specs=[pl.BlockSpec((t, 128), lambda pr, i: (i, OFF_CQ // 128 + pr)),
                  pl.BlockSpec((s, 128), lambda pr, i: (0, OFF_CK // 128 + pr)),
                  pl.BlockSpec((s, 128), lambda pr, i: (0, OFF_CV // 128 + pr))],
        out_specs=pl.BlockSpec((t, 128), lambda pr, i: (i, pr)),
        out_shape=jax.ShapeDtypeStruct((s, BRANCH_WIDTH), F32),
        compiler_params=_params(("arbitrary", "arbitrary")),
    )(p, p, p)


def _sb_bwd(p, do, *, name):
    s = p.shape[0]
    t = SB_BLOCK
    scale = SB_HEAD_DIM ** -0.5

    def body(q_ref, k_ref, v_ref, do_ref, dq_ref, dk_ref, dv_ref):
        i = pl.program_id(1)

        @pl.when(i == 0)
        def _():
            dk_ref[...] = jnp.zeros_like(dk_ref)
            dv_ref[...] = jnp.zeros_like(dv_ref)

        later, from_here, causal, first = _sb_consts()
        q = q_ref[...] * scale
        do = do_ref[...]
        qs = (jnp.where(first, q, 0.0).astype(BF16), jnp.where(first, 0.0, q).astype(BF16))
        dos = (jnp.where(first, do, 0.0).astype(BF16), jnp.where(first, 0.0, do).astype(BF16))

        def total(j, carry, masked):
            start = pl.multiple_of(j * t, t)
            kb = k_ref[pl.ds(start, t), :].astype(BF16)
            vb = v_ref[pl.ds(start, t), :].astype(BF16)
            out = []
            for h in range(2):
                acc, acc_de = carry[h]
                _, w, tot = _sb_weights(qs[h], kb, acc, later, causal, masked)
                dw = lax.dot_general(dos[h], vb, (((1,), (1,)), ((), ())), preferred_element_type=F32)
                out.append((acc + tot, acc_de + _sums(dw * w, from_here)[:, t:]))
            return tuple(out)

        zero = jnp.zeros((t, 128), F32)
        sums = total(i, ((zero, zero), (zero, zero)), True)
        sums = lax.fori_loop(0, i, lambda jj, cr: total(i - 1 - jj, cr, False), sums)
        deltas = (sums[0][1], sums[1][1])

        def block(j, carry, masked):
            start = pl.multiple_of(j * t, t)
            kb = k_ref[pl.ds(start, t), :].astype(BF16)
            vb = v_ref[pl.ds(start, t), :].astype(BF16)
            dk_blk = jnp.zeros((t, 128), F32)
            dv_blk = jnp.zeros((t, 128), F32)
            out = []
            for h in range(2):
                dq_h, acc, acc_de = carry[h]
                lk, w, tot = _sb_weights(qs[h], kb, acc, later, causal, masked)
                dw = lax.dot_general(dos[h], vb, (((1,), (1,)), ((), ())), preferred_element_type=F32)
                de = dw * w
                cs = _sums(de, from_here)
                g_before = deltas[h] - (cs[:, :t] + acc_de)
                keep = jnp.exp(lk)
                dz = de * keep - g_before * (1.0 - keep)
                if masked:
                    dz = jnp.where(causal, dz, 0.0)
                dzb = dz.astype(BF16)
                dq_h = dq_h + jnp.dot(dzb, kb, preferred_element_type=F32)
                dk_blk = dk_blk + lax.dot_general(dzb, qs[h], (((0,), (0,)), ((), ())), preferred_element_type=F32)
                dv_blk = dv_blk + lax.dot_general(w.astype(BF16), dos[h], (((0,), (0,)), ((), ())),
                                                  preferred_element_type=F32)
                out.append((dq_h, acc + tot, acc_de + cs[:, t:]))
            dk_ref[pl.ds(start, t), :] += dk_blk
            dv_ref[pl.ds(start, t), :] += dv_blk
            return tuple(out)

        carry = block(i, ((zero, zero, zero), (zero, zero, zero)), True)
        carry = lax.fori_loop(0, i, lambda jj, cr: block(i - 1 - jj, cr, False), carry)
        dq_ref[...] = jnp.where(first, carry[0][0], carry[1][0]) * scale

    qblk = lambda off: pl.BlockSpec((t, 128), lambda pr, i: (i, off // 128 + pr))
    full = lambda off: pl.BlockSpec((s, 128), lambda pr, i: (0, off // 128 + pr))
    res = pl.pallas_call(
        body, name=name, grid=(BRANCH_WIDTH // 128, s // t),
        in_specs=[qblk(OFF_CQ), full(OFF_CK), full(OFF_CV), qblk(0)],
        out_specs=[qblk(0), full(0), full(0)],
        out_shape=[jax.ShapeDtypeStruct((s, BRANCH_WIDTH), F32)] * 3,
        compiler_params=_params(("arbitrary", "arbitrary")),
    )(p, p, p, do)
    return list(res)


def _loss_head(y, target, *, name, ts=512):
    s, d = y.shape
    ts = min(ts, s)

    def body(y_ref, t_ref, sq_ref, dy_ref):
        @pl.when(pl.program_id(0) == 0)
        def _():
            sq_ref[...] = jnp.zeros_like(sq_ref)

        err = y_ref[...] - t_ref[...]
        dy_ref[...] = err * (1.0 / d)
        tot = jnp.sum(jnp.sum(err * err, axis=1, keepdims=True), axis=0, keepdims=True)
        sq_ref[...] += jnp.broadcast_to(tot, sq_ref.shape)

    blk = pl.BlockSpec((ts, d), lambda i: (i, 0))
    return pl.pallas_call(
        body, name=name, grid=(s // ts,), in_specs=[blk, blk],
        out_specs=[pl.BlockSpec((1, 128), lambda i: (0, 0)), blk],
        out_shape=[jax.ShapeDtypeStruct((1, 128), F32), jax.ShapeDtypeStruct((s, d), F32)],
        compiler_params=_params(("arbitrary",)),
    )(y, target)


def _row_tile(r, limit=512):
    return max(t for t in range(8, limit + 1, 8) if r % t == 0)


def _adamw(w, g, m, v, *, name):
    r = w.shape[0]
    ts = _row_tile(r)

    def body(w_ref, g_ref, m_ref, v_ref, d_ref, nm_ref, nv_ref):
        gv = g_ref[...]
        m_new = ADAM_B1 * m_ref[...] + (1.0 - ADAM_B1) * gv
        v_new = ADAM_B2 * v_ref[...] + (1.0 - ADAM_B2) * jnp.square(gv)
        m_hat = m_new / (1.0 - ADAM_B1 ** ADAM_STEP)
        v_hat = v_new / (1.0 - ADAM_B2 ** ADAM_STEP)
        d_ref[...] = -ADAM_LR * (m_hat / (jnp.sqrt(v_hat) + ADAM_EPS) + ADAM_WD * w_ref[...])
        nm_ref[...] = m_new
        nv_ref[...] = v_new

    blk = pl.BlockSpec((ts, LANES), lambda i: (i, 0))
    return pl.pallas_call(
        body, name=name, grid=(r // ts,), in_specs=[blk] * 4, out_specs=[blk] * 3,
        out_shape=[jax.ShapeDtypeStruct((r, LANES), F32)] * 3,
        compiler_params=_params(("parallel",)),
    )(w, g, m, v)


def _add_rows(terms, *, name):
    r = terms[0].shape[0]
    ts = _row_tile(r)

    def body(*refs):
        acc = refs[0][...]
        for ref in refs[1:-1]:
            acc = acc + ref[...]
        refs[-1][...] = acc

    blk = pl.BlockSpec((ts, LANES), lambda i: (i, 0))
    return pl.pallas_call(
        body, name=name, grid=(r // ts,), in_specs=[blk] * len(terms), out_specs=blk,
        out_shape=jax.ShapeDtypeStruct((r, LANES), F32), compiler_params=_params(("parallel",)),
    )(*terms)


_HBM = pl.BlockSpec(memory_space=pltpu.HBM)
_MESH = pl.DeviceIdType.MESH


def _other_chips(x, y):
    return [(1 - x, y), (x, 1 - y), (1 - x, 1 - y)]


def _gather_chips(shard, *, name):
    r, lanes = shard.shape

    def body(in_ref, out_ref, send_sems, recv_sems, local_sem):
        x, y, c = lax.axis_index("x"), lax.axis_index("y"), lax.axis_index("c")
        me = 2 * x + y
        mine = pltpu.make_async_copy(in_ref, out_ref.at[me], local_sem)
        mine.start()
        copies = []
        for kk, (px, py) in enumerate(_other_chips(x, y)):
            cp = pltpu.make_async_remote_copy(src_ref=in_ref, dst_ref=out_ref.at[me], send_sem=send_sems.at[kk],
                                              recv_sem=recv_sems.at[kk], device_id=(px, py, c), device_id_type=_MESH)
            cp.start()
            copies.append(cp)
        for kk, (px, py) in enumerate(_other_chips(x, y)):
            pltpu.make_async_remote_copy(src_ref=in_ref, dst_ref=out_ref.at[2 * px + py], send_sem=send_sems.at[kk],
                                         recv_sem=recv_sems.at[kk], device_id=(px, py, c),
                                         device_id_type=_MESH).wait_recv()
        for cp in copies:
            cp.wait_send()
        mine.wait()

    return pl.pallas_call(
        body, name=name, in_specs=[_HBM], out_specs=_HBM,
        out_shape=jax.ShapeDtypeStruct((4, r, lanes), shard.dtype),
        scratch_shapes=[pltpu.SemaphoreType.DMA((3,)), pltpu.SemaphoreType.DMA((3,)), pltpu.SemaphoreType.DMA],
    )(shard)


def _to_sibling(block, *, name):
    def body(in_ref, out_ref, send_sem, recv_sem):
        x, y, c = lax.axis_index("x"), lax.axis_index("y"), lax.axis_index("c")
        cp = pltpu.make_async_remote_copy(src_ref=in_ref, dst_ref=out_ref, send_sem=send_sem, recv_sem=recv_sem,
                                          device_id=(x, y, 1 - c), device_id_type=_MESH)
        cp.start()
        cp.wait()

    return pl.pallas_call(
        body, name=name, in_specs=[_HBM], out_specs=_HBM, out_shape=jax.ShapeDtypeStruct(block.shape, block.dtype),
        scratch_shapes=[pltpu.SemaphoreType.DMA, pltpu.SemaphoreType.DMA],
    )(block)


def _join_halves(half, *, name):
    r, lanes = half.shape

    def body(in_ref, out_ref, send_sem, recv_sem, local_sem):
        x, y, c = lax.axis_index("x"), lax.axis_index("y"), lax.axis_index("c")
        mine = pltpu.make_async_copy(in_ref, out_ref.at[c], local_sem)
        mine.start()
        cp = pltpu.make_async_remote_copy(src_ref=in_ref, dst_ref=out_ref.at[c], send_sem=send_sem, recv_sem=recv_sem,
                                          device_id=(x, y, 1 - c), device_id_type=_MESH)
        cp.start()
        pltpu.make_async_remote_copy(src_ref=in_ref, dst_ref=out_ref.at[1 - c], send_sem=send_sem, recv_sem=recv_sem,
                                     device_id=(x, y, 1 - c), device_id_type=_MESH).wait_recv()
        cp.wait_send()
        mine.wait()

    return pl.pallas_call(
        body, name=name, in_specs=[_HBM], out_specs=_HBM, out_shape=jax.ShapeDtypeStruct((2, r, lanes), half.dtype),
        scratch_shapes=[pltpu.SemaphoreType.DMA, pltpu.SemaphoreType.DMA, pltpu.SemaphoreType.DMA],
    )(half)


def _scatter_chips(blocks, *, name):
    _, r, lanes = blocks.shape

    def body(in_ref, out_ref, send_sems, recv_sems, local_sem):
        x, y, c = lax.axis_index("x"), lax.axis_index("y"), lax.axis_index("c")
        me = 2 * x + y
        mine = pltpu.make_async_copy(in_ref.at[me], out_ref.at[me], local_sem)
        mine.start()
        copies = []
        for kk, (px, py) in enumerate(_other_chips(x, y)):
            cp = pltpu.make_async_remote_copy(src_ref=in_ref.at[2 * px + py], dst_ref=out_ref.at[me],
                                              send_sem=send_sems.at[kk], recv_sem=recv_sems.at[kk],
                                              device_id=(px, py, c), device_id_type=_MESH)
            cp.start()
            copies.append(cp)
        for kk, (px, py) in enumerate(_other_chips(x, y)):
            pltpu.make_async_remote_copy(src_ref=in_ref.at[me], dst_ref=out_ref.at[2 * px + py],
                                         send_sem=send_sems.at[kk], recv_sem=recv_sems.at[kk], device_id=(px, py, c),
                                         device_id_type=_MESH).wait_recv()
        for cp in copies:
            cp.wait_send()
        mine.wait()

    return pl.pallas_call(
        body, name=name, in_specs=[_HBM], out_specs=_HBM, out_shape=jax.ShapeDtypeStruct((4, r, lanes), blocks.dtype),
        scratch_shapes=[pltpu.SemaphoreType.DMA((3,)), pltpu.SemaphoreType.DMA((3,)), pltpu.SemaphoreType.DMA],
    )(blocks)


SHARDED = (("norm_g", 2), ("w_in", 2), ("conv_w", 2), ("w_branch", 3), ("w_out", 1), ("w_ff1", 2), ("w_ff2", 1))
MATMUL_WEIGHTS = ("w_in", "w_branch", "w_out", "w_ff1", "w_ff2")
VECTOR_WEIGHTS = ("norm_g", "conv_w")
REPLICATED = ("b_in", "sgu_ln_g", "sgu_ln_b", "w_spatial", "b_spatial", "a_log", "dt_bias", "dn_norm_g")
WEIGHT_ORDER = ("norm_g", "w_in", "b_in", "sgu_ln_g", "sgu_ln_b", "w_spatial", "b_spatial", "conv_w", "a_log",
                "dt_bias", "dn_norm_g", "w_branch", "w_out", "w_ff1", "w_ff2")
PACK_ROW_MULTIPLE = 32


def _rows_of(shape):
    n = 1
    for dim in shape:
        n *= dim
    return -(-n // LANES)


def _pack(arrays):
    parts = []
    for a in arrays:
        flat = a.reshape(-1)
        pad = _rows_of(a.shape) * LANES - flat.shape[0]
        if pad:
            flat = jnp.concatenate([flat, jnp.zeros((pad,), flat.dtype)])
        parts.append(flat.reshape(-1, LANES))
    rows = sum(p.shape[0] for p in parts)
    pad = -rows % PACK_ROW_MULTIPLE
    if pad:
        parts.append(jnp.zeros((pad, LANES), parts[0].dtype))
    return jnp.concatenate(parts, axis=0)


def _unpack(buf, shapes):
    out, row = [], 0
    for shape in shapes:
        n = 1
        for dim in shape:
            n *= dim
        rows = _rows_of(shape)
        out.append(buf[row:row + rows].reshape(-1)[:n].reshape(shape))
        row += rows
    return out


def _chip_slice(a, axis, k):
    size = a.shape[axis] // 4
    return lax.slice_in_dim(a, k * size, (k + 1) * size, axis=axis)


def _rearrange_w_in(w):
    pad = jnp.zeros(w.shape[:-1] + (P_PAD - P_IN,), w.dtype)
    return jnp.concatenate([w[..., 0:3072], w[..., 4616:7688], w[..., 3080:4616], w[..., 3072:3080], pad], axis=-1)


def _restore_w_in(w):
    return jnp.concatenate([w[..., 0:3072], w[..., 7680:7688], w[..., 6144:7680], w[..., 3072:6144]], axis=-1)


def _shift_rows(a, n):
    if n == 0:
        return a
    return jnp.concatenate([jnp.zeros((n, a.shape[1]), a.dtype), a[:-n]], axis=0)


def _unshift_rows(a, n):
    if n == 0:
        return a
    return jnp.concatenate([a[n:], jnp.zeros((n, a.shape[1]), a.dtype)], axis=0)


def _layer_params(wl):
    row = lambda v: v.reshape(1, -1)
    pad128 = lambda v, at: jnp.pad(v, (at, 128 - at - v.shape[0])).reshape(1, 128)
    return dict(
        g=[row(wl["norm_g"][i]) for i in range(4)],
        gmlp=[row(wl["sgu_ln_g"]), row(wl["sgu_ln_b"]), wl["w_spatial"],
              jnp.pad(wl["b_spatial"].T, ((0, 0), (0, 128 - GM_GROUPS)))],
        dn_in=[wl["conv_w"], pad128(wl["a_log"], DN_HEADS), pad128(wl["dt_bias"], DN_HEADS)],
        dn_g=[row(wl["dn_norm_g"])],
    )


def _layer_fwd(x0, wl, l):
    tag = lambda s: f"{s}_l{l}"
    pr = _layer_params(wl)
    w = BRANCH_WIDTH
    h0 = _row_fwd(_fn_rms, [(x0, 0, D_MODEL)], [pr["g"][0]], [(D_MODEL, BF16)], ts=512, name=tag("rms0"))[0]
    p = _mm(h0, wl["w_in"], bias=wl["b_in"].reshape(1, -1), name=tag("proj_in"))
    ya = _row_fwd(_fn_gmlp, [(p, OFF_AU, w), (p, OFF_AV, w)], pr["gmlp"], [(w, BF16)], ts=256, name=tag("gmlp"))[0]
    xq = p[:, OFF_BQ:OFF_BQ + 3 * w]
    shifted = [_shift_rows(xq, 3 - j) for j in range(4)]
    dn_rows = [(a, 0, 3 * w) for a in shifted] + [(p, OFF_BD, 128)]
    q, k, v, beta, g = _row_fwd(_fn_dn_in, dn_rows, pr["dn_in"], [(w, F32)] * 5, ts=256, name=tag("dn_in"))
    o, states = _delta_fwd(q, k, v, beta, g, name=tag("delta"))
    yb = _row_fwd(_fn_dn_out, [(o, 0, w), (p, OFF_BZ, w)], pr["dn_g"], [(w, BF16)], ts=512, name=tag("dn_out"))[0]
    yc = _sb_fwd(p, name=tag("sb"))
    ys = [ya, yb, yc]
    proj = [_mm(ys[i], wl["w_branch"][i], name=tag(f"branch{i}")) for i in range(3)]
    merge_rows = [(a, 0, D_MODEL) for a in proj] + [(p, OFF_GATE + i * D_MODEL, D_MODEL) for i in range(3)]
    m = _row_fwd(_fn_merge, merge_rows, [], [(D_MODEL, BF16)], ts=256, name=tag("merge"))[0]
    mixed = _mm(m, wl["w_out"], name=tag("out"))
    x1 = _row_fwd(_fn_resid_rms, [(x0, 0, D_MODEL), (mixed, 0, D_MODEL)], [pr["g"][1]], [(D_MODEL, F32)], ts=512,
                  name=tag("resid1"))[0]
    h2 = _row_fwd(_fn_rms, [(x1, 0, D_MODEL)], [pr["g"][2]], [(D_MODEL, BF16)], ts=512, name=tag("rms2"))[0]
    a = _mm(h2, wl["w_ff1"], name=tag("ff1"))
    r = _row_fwd(_fn_relu2, [(a, 0, D_FF)], [], [(D_FF, BF16)], ts=256, name=tag("relu2"))[0]
    f = _mm(r, wl["w_ff2"], name=tag("ff2"))
    x2 = _row_fwd(_fn_resid_rms, [(x1, 0, D_MODEL), (f, 0, D_MODEL)], [pr["g"][3]], [(D_MODEL, F32)], ts=512,
                  name=tag("resid2"))[0]
    saved = dict(x0=x0, h0=h0, p=p, shifted=shifted, q=q, k=k, v=v, beta=beta, g=g, states=states, o=o, ys=ys,
                 proj=proj, m=m, mixed=mixed, x1=x1, h2=h2, a=a, r=r, f=f)
    return x2, saved


def _layer_bwd(dx2, sv, wl, l):
    tag = lambda s: f"{s}_l{l}"
    pr = _layer_params(wl)
    w = BRANCH_WIDTH
    full = lambda a: (a, 0, a.shape[1])
    p = sv["p"]
    (df,), (dg3,) = _row_bwd(_fn_rms_branch, [full(sv["f"])], [pr["g"][3]], [full(dx2)], ts=512, name=tag("resid2_b"))
    dr = _mm(df, wl["w_ff2"], trans_b=True, name=tag("ff2_dx"))
    dw_ff2 = _mm_tn(sv["r"], df, name=tag("ff2_dw"))
    (da,), _ = _row_bwd(_fn_relu2, [full(sv["a"])], [], [full(dr)], ts=256, name=tag("relu2_b"))
    dh2 = _mm(da, wl["w_ff1"], trans_b=True, name=tag("ff1_dx"))
    dw_ff1 = _mm_tn(sv["h2"], da, name=tag("ff1_dw"))
    (dx1,), (dg2,) = _row_bwd(_fn_rms_keep, [full(sv["x1"])], [pr["g"][2]], [full(dh2), full(dx2)], ts=512,
                              name=tag("rms2_b"))
    (dmixed,), (dg1,) = _row_bwd(_fn_rms_branch, [full(sv["mixed"])], [pr["g"][1]], [full(dx1)], ts=512,
                                 name=tag("resid1_b"))
    dm = _mm(dmixed, wl["w_out"], trans_b=True, name=tag("out_dx"))
    dw_out = _mm_tn(sv["m"], dmixed, name=tag("out_dw"))
    merge_rows = [full(a) for a in sv["proj"]] + [(p, OFF_GATE + i * D_MODEL, D_MODEL) for i in range(3)]
    dmerge, _ = _row_bwd(_fn_merge, merge_rows, [], [full(dm)], ts=256, name=tag("merge_b"))
    dproj, dgates = dmerge[:3], dmerge[3:]
    dys = [_mm(dproj[i], wl["w_branch"][i], trans_b=True, name=tag(f"branch{i}_dx")) for i in range(3)]
    dw_branch = jnp.stack([_mm_tn(sv["ys"][i], dproj[i], name=tag(f"branch{i}_dw")) for i in range(3)])
    (du, dv_a), dgm = _row_bwd(_fn_gmlp, [(p, OFF_AU, w), (p, OFF_AV, w)], pr["gmlp"], [full(dys[0])], ts=256,
                               name=tag("gmlp_b"))
    (do, dz), (d_dn_g,) = _row_bwd(_fn_dn_out, [full(sv["o"]), (p, OFF_BZ, w)], pr["dn_g"], [full(dys[1])], ts=512,
                                   name=tag("dn_out_b"))
    dqkvbg = _delta_bwd(sv["q"], sv["k"], sv["v"], sv["beta"], sv["g"], sv["states"], do, name=tag("delta_b"))
    dn_rows = [full(a) for a in sv["shifted"]] + [(p, OFF_BD, 128)]
    d_in, d_dn_in = _row_bwd(_fn_dn_in, dn_rows, pr["dn_in"], [full(a) for a in dqkvbg], ts=256, name=tag("dn_in_b"))
    dxq = sum(_unshift_rows(d_in[j], 3 - j) for j in range(4))
    dcq, dck, dcv = _sb_bwd(p, dys[2], name=tag("sb_b"))
    s = p.shape[0]
    dp = jnp.concatenate([du, dv_a, dxq, dz] + dgates + [dcq, dck, dcv, d_in[4],
                                                         jnp.zeros((s, P_PAD - OFF_BD - 128), F32)], axis=1)
    dh0 = _mm(dp, wl["w_in"], trans_b=True, name=tag("proj_in_dx"))
    dw_in = _mm_tn(sv["h0"], dp, name=tag("proj_in_dw"))
    _, (db_in,) = _row_bwd(_fn_bias, [full(p)], [wl["b_in"].reshape(1, -1)], [full(dp)], ts=256, name=tag("bias_b"),
                           row_grads=False)
    (dx0,), (dg0,) = _row_bwd(_fn_rms_keep, [full(sv["x0"])], [pr["g"][0]], [full(dh0), full(dx1)], ts=512,
                              name=tag("rms0_b"))
    grads = dict(
        norm_g=jnp.concatenate([dg0, dg1, dg2, dg3], axis=0), w_in=dw_in, b_in=db_in.reshape(-1),
        sgu_ln_g=dgm[0].reshape(-1), sgu_ln_b=dgm[1].reshape(-1), w_spatial=dgm[2],
        b_spatial=dgm[3][:, :GM_GROUPS].T, conv_w=d_dn_in[0], a_log=d_dn_in[1][0, DN_HEADS:2 * DN_HEADS],
        dt_bias=d_dn_in[2][0, DN_HEADS:2 * DN_HEADS], dn_norm_g=d_dn_g.reshape(-1), w_branch=dw_branch,
        w_out=dw_out, w_ff1=dw_ff1, w_ff2=dw_ff2)
    return dx0, grads


def _local_step(x, target, weights):
    saved = []
    h = x
    layers = []
    for l in range(DEPTH):
        wl = {n: weights[n][l] for n in WEIGHT_ORDER}
        layers.append(wl)
        h, sv = _layer_fwd(h, wl, l)
        saved.append(sv)
    sq, dh = _loss_head(h, target, name="loss_head")
    grads = [None] * DEPTH
    for l in reversed(range(DEPTH)):
        dh, grads[l] = _layer_bwd(dh, saved[l], layers[l], l)
    stacked = {n: jnp.stack([grads[l][n] for l in range(DEPTH)]) for n in WEIGHT_ORDER}
    return sq[0, 0], dh, stacked


def kernel(x, norm_g, w_in, b_in, sgu_ln_g, sgu_ln_b, w_spatial, b_spatial, conv_w, a_log, dt_bias, dn_norm_g, w_branch, w_out, w_ff1, w_ff2, loss_target, m_norm_g, m_w_in, m_b_in, m_sgu_ln_g, m_sgu_ln_b, m_w_spatial, m_b_spatial, m_conv_w, m_a_log, m_dt_bias, m_dn_norm_g, m_w_branch, m_w_out, m_w_ff1, m_w_ff2, v_norm_g, v_w_in, v_b_in, v_sgu_ln_g, v_sgu_ln_b, v_w_spatial, v_b_spatial, v_conv_w, v_a_log, v_dt_bias, v_dn_norm_g, v_w_branch, v_w_out, v_w_ff1, v_w_ff2):
    local = dict(norm_g=norm_g, w_in=w_in, b_in=b_in, sgu_ln_g=sgu_ln_g, sgu_ln_b=sgu_ln_b, w_spatial=w_spatial,
                 b_spatial=b_spatial, conv_w=conv_w, a_log=a_log, dt_bias=dt_bias, dn_norm_g=dn_norm_g,
                 w_branch=w_branch, w_out=w_out, w_ff1=w_ff1, w_ff2=w_ff2)
    mom1 = dict(norm_g=m_norm_g, w_in=m_w_in, b_in=m_b_in, sgu_ln_g=m_sgu_ln_g, sgu_ln_b=m_sgu_ln_b,
                w_spatial=m_w_spatial, b_spatial=m_b_spatial, conv_w=m_conv_w, a_log=m_a_log, dt_bias=m_dt_bias,
                dn_norm_g=m_dn_norm_g, w_branch=m_w_branch, w_out=m_w_out, w_ff1=m_w_ff1, w_ff2=m_w_ff2)
    mom2 = dict(norm_g=v_norm_g, w_in=v_w_in, b_in=v_b_in, sgu_ln_g=v_sgu_ln_g, sgu_ln_b=v_sgu_ln_b,
                w_spatial=v_w_spatial, b_spatial=v_b_spatial, conv_w=v_conv_w, a_log=v_a_log, dt_bias=v_dt_bias,
                dn_norm_g=v_dn_norm_g, w_branch=v_w_branch, w_out=v_w_out, w_ff1=v_w_ff1, w_ff2=v_w_ff2)
    shard_names = [n for n, _ in SHARDED]
    shard_shapes = [local[n].shape for n in shard_names]
    repl_shapes = [local[n].shape for n in REPLICATED]

    weights = {n: local[n] for n in REPLICATED}
    for names, dtype, call in ((MATMUL_WEIGHTS, BF16, "gather_matmul_weights"), (VECTOR_WEIGHTS, F32, "gather_vectors")):
        gathered = _gather_chips(_pack([local[n] for n in names]).astype(dtype), name=call)
        per_chip = [_unpack(gathered[k], [local[n].shape for n in names]) for k in range(4)]
        for i, n in enumerate(names):
            weights[n] = jnp.concatenate([per_chip[k][i] for k in range(4)], axis=dict(SHARDED)[n])
    weights["w_in"] = _rearrange_w_in(weights["w_in"])
    weights["b_in"] = _rearrange_w_in(weights["b_in"])

    sq, dx, grads = _local_step(x[0], loss_target[0], weights)
    loss = lax.psum(0.5 * sq / D_MODEL, ("x", "y", "c"))
    grads["w_in"] = _restore_w_in(grads["w_in"])
    grads["b_in"] = _restore_w_in(grads["b_in"])

    blocks = jnp.stack([_pack([_chip_slice(grads[n], axis, k) for n, axis in SHARDED] + [grads[n] for n in REPLICATED])
                        for k in range(4)])
    rows = blocks.shape[1]
    half = rows // 2
    c = lax.axis_index("c")
    halves = blocks.reshape(4, 2, half, LANES)
    keep = lax.dynamic_index_in_dim(halves, c, axis=1, keepdims=False)
    give = lax.dynamic_index_in_dim(halves, 1 - c, axis=1, keepdims=False)
    got = _to_sibling(give, name="grads_to_sibling")
    chip_sum = _add_rows([keep.reshape(4 * half, LANES), got.reshape(4 * half, LANES)], name="grads_chip_sum")
    by_chip = _scatter_chips(chip_sum.reshape(4, half, LANES), name="grads_scatter")
    my_half = _add_rows([by_chip[k] for k in range(4)], name="grads_sum")
    total = _join_halves(my_half, name="grads_join_halves").reshape(rows, LANES)

    pack_local = lambda d: _pack([d[n] for n in shard_names] + [d[n] for n in REPLICATED])
    delta, new_m, new_v = _adamw(pack_local(local), total, pack_local(mom1), pack_local(mom2), name="adamw")
    unpack = lambda buf: dict(zip(shard_names + list(REPLICATED), _unpack(buf, shard_shapes + repl_shapes)))
    g_out, d_out, m_out, v_out = unpack(total), unpack(delta), unpack(new_m), unpack(new_v)
    return (loss, dx[None], *[g_out[n] for n in WEIGHT_ORDER], *[d_out[n] for n in WEIGHT_ORDER],
            *[m_out[n] for n in WEIGHT_ORDER], *[v_out[n] for n in WEIGHT_ORDER])
```

```python
import functools

import jax
import jax.numpy as jnp
from jax import lax
from jax.experimental import pallas as pl
from jax.experimental.pallas import tpu as pltpu

F32 = jnp.float32
BF16 = jnp.bfloat16

D_MODEL = 1024
DEPTH = 4
BRANCH_WIDTH = 512
GM_CHUNK = 128
GM_GROUPS = 8
DN_HEADS = 4
DN_HEAD_DIM = 128
DN_BLOCK = 128
SB_HEAD_DIM = 64
SB_BLOCK = 128
SB_QUERY_ROWS = 256
SB_DEAD_LOG = -105.0
D_FF = 4096
P_IN = 7688
P_PAD = 8192
NORM_EPS = 1e-6
ADAM_LR, ADAM_B1, ADAM_B2, ADAM_EPS, ADAM_WD, ADAM_STEP = 0.001, 0.9, 0.999, 1e-08, 0.01, 10

OFF_AU, OFF_AV, OFF_BQ, OFF_BK, OFF_BV, OFF_BZ = 0, 512, 1024, 1536, 2048, 2560
OFF_GATE = 3072
OFF_CQ, OFF_CK, OFF_CV = 6144, 6656, 7168
OFF_BD = 7680

LANES = 1024
VMEM_LIMIT_BYTES = 56 * 1024 * 1024

_HI = lax.Precision.HIGHEST


def _params(sem):
    return pltpu.CompilerParams(dimension_semantics=sem, vmem_limit_bytes=VMEM_LIMIT_BYTES)


def _mm(a, b, *, name, out_dtype=F32, bias=None, trans_b=False, tm=1024, tn=1024, tk=1024):
    m, k = a.shape
    n = b.shape[0] if trans_b else b.shape[1]
    tm, tn, tk = min(tm, m), min(tn, n), min(tk, k)
    assert m % tm == 0 and n % tn == 0 and k % tk == 0, (a.shape, b.shape)
    nk = k // tk
    dn = (((1,), (1,)), ((), ())) if trans_b else (((1,), (0,)), ((), ()))

    def body(*refs):
        if bias is None:
            a_ref, b_ref, o_ref, acc = refs
        else:
            a_ref, b_ref, bias_ref, o_ref, acc = refs
        kk = pl.program_id(2)
        part = lax.dot_general(a_ref[...].astype(BF16), b_ref[...].astype(BF16), dn, preferred_element_type=F32)

        @pl.when(kk == 0)
        def _():
            acc[...] = part

        @pl.when(kk > 0)
        def _():
            acc[...] += part

        @pl.when(kk == nk - 1)
        def _():
            r = acc[...]
            if bias is not None:
                r = r + bias_ref[...]
            o_ref[...] = r.astype(out_dtype)

    in_specs = [pl.BlockSpec((tm, tk), lambda i, j, kk: (i, kk))]
    if trans_b:
        in_specs.append(pl.BlockSpec((tn, tk), lambda i, j, kk: (j, kk)))
    else:
        in_specs.append(pl.BlockSpec((tk, tn), lambda i, j, kk: (kk, j)))
    args = [a, b]
    if bias is not None:
        in_specs.append(pl.BlockSpec((1, tn), lambda i, j, kk: (0, j)))
        args.append(bias)
    return pl.pallas_call(
        body, name=name, grid=(m // tm, n // tn, nk),
        in_specs=in_specs, out_specs=pl.BlockSpec((tm, tn), lambda i, j, kk: (i, j)),
        out_shape=jax.ShapeDtypeStruct((m, n), out_dtype),
        scratch_shapes=[pltpu.VMEM((tm, tn), F32)],
        compiler_params=_params(("parallel", "parallel", "arbitrary")),
    )(*args)


def _mm_tn(a, b, *, name, tm=1024, tn=1024, ts=1024):
    s, ka = a.shape
    n = b.shape[1]
    tm, tn, ts = min(tm, ka), min(tn, n), min(ts, s)
    assert ka % tm == 0 and n % tn == 0 and s % ts == 0, (a.shape, b.shape)

    def body(a_ref, b_ref, o_ref):
        part = lax.dot_general(a_ref[...].astype(BF16), b_ref[...].astype(BF16), (((0,), (0,)), ((), ())),
                               preferred_element_type=F32)

        @pl.when(pl.program_id(2) == 0)
        def _():
            o_ref[...] = part

        @pl.when(pl.program_id(2) > 0)
        def _():
            o_ref[...] += part

    return pl.pallas_call(
        body, name=name, grid=(ka // tm, n // tn, s // ts),
        in_specs=[pl.BlockSpec((ts, tm), lambda i, j, r: (r, i)), pl.BlockSpec((ts, tn), lambda i, j, r: (r, j))],
        out_specs=pl.BlockSpec((tm, tn), lambda i, j, r: (i, j)),
        out_shape=jax.ShapeDtypeStruct((ka, n), F32),
        compiler_params=_params(("parallel", "parallel", "arbitrary")),
    )(a, b)


def _col_block(i, *, c):
    return (i, c)


def _whole(i, *, nd):
    return (0,) * nd


def _row_specs(rows, ts):
    specs = []
    for arr, off, w in rows:
        assert off % w == 0 and arr.shape[0] % ts == 0
        specs.append(pl.BlockSpec((ts, w), functools.partial(_col_block, c=off // w)))
    return specs


def _row_fwd(fn, rows, params, outs, *, ts, name):
    s = rows[0][0].shape[0]
    ts = min(ts, s)
    nr, npar = len(rows), len(params)

    def body(*refs):
        rv = [r[...].astype(F32) for r in refs[:nr]]
        pv = [p[...] for p in refs[nr:nr + npar]]
        for o_ref, val in zip(refs[nr + npar:], fn(pv, rv)):
            o_ref[...] = val.astype(o_ref.dtype)

    in_specs = _row_specs(rows, ts) + [pl.BlockSpec(p.shape, functools.partial(_whole, nd=p.ndim)) for p in params]
    res = pl.pallas_call(
        body, name=name, grid=(s // ts,), in_specs=in_specs,
        out_specs=[pl.BlockSpec((ts, w), lambda i: (i, 0)) for w, _ in outs],
        out_shape=[jax.ShapeDtypeStruct((s, w), dt) for w, dt in outs],
        compiler_params=_params(("parallel",)),
    )(*[r[0] for r in rows], *params)
    return list(res)


def _row_bwd(fn, rows, params, cts, *, ts, name, row_grads=True):
    s = rows[0][0].shape[0]
    ts = min(ts, s)
    nr, npar, nc = len(rows), len(params), len(cts)
    n_dr = nr if row_grads else 0

    def body(*refs):
        rv = [r[...].astype(F32) for r in refs[:nr]]
        pv = [p[...] for p in refs[nr:nr + npar]]
        cv = [c[...].astype(F32) for c in refs[nr + npar:nr + npar + nc]]
        out_refs = refs[nr + npar + nc:]
        _, vjp = jax.vjp(lambda p, r: tuple(fn(p, r)), pv, rv)
        dp, dr = vjp(tuple(cv))
        for o_ref, val in zip(out_refs[:n_dr], dr):
            o_ref[...] = val

        @pl.when(pl.program_id(0) == 0)
        def _():
            for o_ref in out_refs[n_dr:]:
                o_ref[...] = jnp.zeros_like(o_ref)

        for o_ref, val in zip(out_refs[n_dr:], dp):
            o_ref[...] += val

    in_specs = (_row_specs(rows, ts) + [pl.BlockSpec(p.shape, functools.partial(_whole, nd=p.ndim)) for p in params]
                + _row_specs(cts, ts))
    out_specs = [pl.BlockSpec((ts, w), lambda i: (i, 0)) for _, _, w in rows[:n_dr]]
    out_specs += [pl.BlockSpec(p.shape, functools.partial(_whole, nd=p.ndim)) for p in params]
    out_shape = [jax.ShapeDtypeStruct((s, w), F32) for _, _, w in rows[:n_dr]]
    out_shape += [jax.ShapeDtypeStruct(p.shape, F32) for p in params]
    res = pl.pallas_call(
        body, name=name, grid=(s // ts,), in_specs=in_specs, out_specs=out_specs, out_shape=out_shape,
        compiler_params=_params(("arbitrary",)),
    )(*[r[0] for r in rows], *params, *[c[0] for c in cts])
    res = list(res)
    return res[:n_dr], res[n_dr:]


def _rms(x, g):
    return x * lax.rsqrt(jnp.mean(x * x, axis=-1, keepdims=True) + NORM_EPS) * g


def _gelu(x):
    return 0.5 * x * (1.0 + lax.erf(x * (2.0 ** -0.5)))


def _softplus(x):
    return jnp.maximum(x, 0.0) + jnp.log1p(jnp.exp(-jnp.abs(x)))


def _iota2(shape, dim):
    return lax.broadcasted_iota(jnp.int32, shape, dim)


def _fn_rms(pv, rv):
    return [_rms(rv[0], pv[0])]


def _fn_rms_keep(pv, rv):
    return [_rms(rv[0], pv[0]), rv[0]]


def _fn_resid_rms(pv, rv):
    return [rv[0] + _rms(rv[1], pv[0])]


def _fn_rms_branch(pv, rv):
    return [_rms(rv[0], pv[0])]


def _fn_relu2(pv, rv):
    return [jnp.square(jnp.maximum(rv[0], 0.0))]


def _fn_bias(pv, rv):
    return [rv[0] + pv[0]]


def _fn_merge(pv, rv):
    return [sum(jax.nn.sigmoid(rv[3 + i]) * rv[i] for i in range(3))]


def _fn_gmlp(pv, rv):
    ln_g, ln_b, w_sp, b_t = pv
    u = _gelu(rv[0])
    v = _gelu(rv[1])
    vc = v - jnp.mean(v, axis=-1, keepdims=True)
    v = vc * lax.rsqrt(jnp.mean(vc * vc, axis=-1, keepdims=True) + NORM_EPS) * ln_g + ln_b
    t = GM_CHUNK
    causal = _iota2((t, t), 1) <= _iota2((t, t), 0)
    first = _iota2((t, 128), 1) < 64
    expand = (_iota2((128, BRANCH_WIDTH), 0) == _iota2((128, BRANCH_WIDTH), 1) // 64).astype(F32)
    b_full = jnp.dot(b_t, expand, precision=_HI, preferred_element_type=F32)
    w_bf = [jnp.where(causal, w_sp[g], 0.0).astype(BF16) for g in range(GM_GROUPS)]
    chunks = []
    for c in range(rv[0].shape[0] // t):
        pairs = []
        for p in range(GM_GROUPS // 2):
            vp = v[c * t:(c + 1) * t, 128 * p:128 * (p + 1)].astype(BF16)
            m0 = jnp.dot(w_bf[2 * p], vp, preferred_element_type=F32)
            m1 = jnp.dot(w_bf[2 * p + 1], vp, preferred_element_type=F32)
            pairs.append(jnp.where(first, m0, m1))
        chunks.append(jnp.concatenate(pairs, axis=1) + b_full)
    return [u * jnp.concatenate(chunks, axis=0)]


def _head_expand(col0):
    return (_iota2((128, BRANCH_WIDTH), 0) == _iota2((128, BRANCH_WIDTH), 1) // DN_HEAD_DIM + col0).astype(F32)


def _fn_dn_in(pv, rv):
    conv_w, a_log, dt_b = pv
    c = sum(conv_w[j:j + 1, :] * rv[j] for j in range(4))
    a = c * jax.nn.sigmoid(c)
    outs = []
    for part in range(3):
        heads = []
        for h in range(DN_HEADS):
            lo = part * BRANCH_WIDTH + h * DN_HEAD_DIM
            xh = a[:, lo:lo + DN_HEAD_DIM]
            if part < 2:
                xh = xh * lax.rsqrt(jnp.sum(xh * xh, axis=-1, keepdims=True) + NORM_EPS)
            heads.append(xh)
        outs.append(jnp.concatenate(heads, axis=1))
    bd = rv[4]
    beta = jax.nn.sigmoid(bd)
    g = -jnp.exp(a_log) * _softplus(bd + dt_b)
    outs.append(jnp.dot(beta, _head_expand(0), precision=_HI, preferred_element_type=F32))
    outs.append(jnp.dot(g, _head_expand(DN_HEADS), precision=_HI, preferred_element_type=F32))
    return outs


def _fn_dn_out(pv, rv):
    heads = []
    for h in range(DN_HEADS):
        sl = slice(h * DN_HEAD_DIM, (h + 1) * DN_HEAD_DIM)
        z = rv[1][:, sl]
        heads.append(_rms(rv[0][:, sl], pv[0]) * (z * jax.nn.sigmoid(z)))
    return [jnp.concatenate(heads, axis=1)]


def _dot(a, b):
    return jnp.dot(a, b, precision=_HI, preferred_element_type=F32)


def _dot_nt(a, b):
    return lax.dot_general(a, b, (((1,), (1,)), ((), ())), precision=_HI, preferred_element_type=F32)


def _dot_tn(a, b):
    return lax.dot_general(a, b, (((0,), (0,)), ((), ())), precision=_HI, preferred_element_type=F32)


@jax.custom_vjp
def _unit_lower_inverse(a):
    c = a.shape[0]
    eye = (_iota2((c, c), 0) == _iota2((c, c), 1)).astype(F32)
    pw = -a
    x = eye + pw
    span = 2
    while span < c + 1:
        pw = _dot(pw, pw)
        x = x + _dot(x, pw)
        span *= 2
    return x


def _unit_lower_inverse_fwd(a):
    x = _unit_lower_inverse(a)
    return x, x


def _unit_lower_inverse_bwd(x, dx):
    return (-_dot_tn(x, _dot_nt(dx, x)),)


_unit_lower_inverse.defvjp(_unit_lower_inverse_fwd, _unit_lower_inverse_bwd)


def _delta_chunk(state, q, k, v, beta, g):
    c = DN_BLOCK
    row, col = _iota2((c, c), 0), _iota2((c, c), 1)
    tri, strict = col <= row, col < row
    gc = _dot(tri.astype(F32), g)
    gl = _dot(jnp.ones((c, c), F32), g)
    decay = jnp.where(tri, jnp.exp(jnp.where(tri, gc - gc.T, 0.0)), 0.0)
    qs = q * (DN_HEAD_DIM ** -0.5)
    a = jnp.where(strict, beta * _dot_nt(k, k) * decay, 0.0)
    x = _unit_lower_inverse(a)
    eg = jnp.exp(gc)
    u = _dot(x, v * beta)
    wk = _dot(x, k * (beta * eg))
    qk = jnp.where(tri, _dot_nt(qs, k) * decay, 0.0)
    v_new = u - _dot(wk, state)
    o = _dot(qs * eg, state) + _dot(qk, v_new)
    nxt = state * jnp.exp(gl) + _dot_tn(k * jnp.exp(gl - gc), v_new)
    return o, nxt


def _head(h):
    return slice(h * DN_HEAD_DIM, (h + 1) * DN_HEAD_DIM)


def _delta_fwd(q, k, v, beta, g, *, name):
    s = q.shape[0]
    c = DN_BLOCK
    nc = s // c

    def body(q_ref, k_ref, v_ref, b_ref, g_ref, o_ref, sp_ref, st):
        @pl.when(pl.program_id(0) == 0)
        def _():
            st[...] = jnp.zeros_like(st)

        outs = []
        for h in range(DN_HEADS):
            sl = _head(h)
            state = st[h]
            sp_ref[0, h] = state
            o, nxt = _delta_chunk(state, q_ref[:, sl], k_ref[:, sl], v_ref[:, sl], b_ref[:, sl], g_ref[:, sl])
            st[h] = nxt
            outs.append(o)
        o_ref[...] = jnp.concatenate(outs, axis=1)

    blk = pl.BlockSpec((c, BRANCH_WIDTH), lambda n: (n, 0))
    return pl.pallas_call(
        body, name=name, grid=(nc,), in_specs=[blk] * 5,
        out_specs=[blk, pl.BlockSpec((1, DN_HEADS, DN_HEAD_DIM, DN_HEAD_DIM), lambda n: (n, 0, 0, 0))],
        out_shape=[jax.ShapeDtypeStruct((s, BRANCH_WIDTH), F32),
                   jax.ShapeDtypeStruct((nc, DN_HEADS, DN_HEAD_DIM, DN_HEAD_DIM), F32)],
        scratch_shapes=[pltpu.VMEM((DN_HEADS, DN_HEAD_DIM, DN_HEAD_DIM), F32)],
        compiler_params=_params(("arbitrary",)),
    )(q, k, v, beta, g)


def _delta_bwd(q, k, v, beta, g, states, do, *, name):
    s = q.shape[0]
    c = DN_BLOCK
    nc = s // c

    def body(q_ref, k_ref, v_ref, b_ref, g_ref, sp_ref, do_ref, dq_ref, dk_ref, dv_ref, db_ref, dg_ref, dst):
        @pl.when(pl.program_id(0) == 0)
        def _():
            dst[...] = jnp.zeros_like(dst)

        grads = [[] for _ in range(5)]
        for h in range(DN_HEADS):
            sl = _head(h)
            _, vjp = jax.vjp(_delta_chunk, sp_ref[0, h], q_ref[:, sl], k_ref[:, sl], v_ref[:, sl], b_ref[:, sl],
                             g_ref[:, sl])
            d = vjp((do_ref[:, sl], dst[h]))
            dst[h] = d[0]
            for lst, val in zip(grads, d[1:]):
                lst.append(val)
        for o_ref, lst in zip((dq_ref, dk_ref, dv_ref, db_ref, dg_ref), grads):
            o_ref[...] = jnp.concatenate(lst, axis=1)

    blk = pl.BlockSpec((c, BRANCH_WIDTH), lambda n: (nc - 1 - n, 0))
    res = pl.pallas_call(
        body, name=name, grid=(nc,),
        in_specs=[blk] * 5 + [pl.BlockSpec((1, DN_HEADS, DN_HEAD_DIM, DN_HEAD_DIM), lambda n: (nc - 1 - n, 0, 0, 0)), blk],
        out_specs=[blk] * 5, out_shape=[jax.ShapeDtypeStruct((s, BRANCH_WIDTH), F32)] * 5,
        scratch_shapes=[pltpu.VMEM((DN_HEADS, DN_HEAD_DIM, DN_HEAD_DIM), F32)],
        compiler_params=_params(("arbitrary",)),
    )(q, k, v, beta, g, states, do)
    return list(res)


def _split_bf16(x):
    hi = x.astype(BF16)
    return hi, (x - hi.astype(F32)).astype(BF16)


def _sb_consts():
    tq, tk = SB_QUERY_ROWS, SB_BLOCK
    row, col = _iota2((tk, tk), 0), _iota2((tk, tk), 1)
    ones = jnp.ones((tk, tk), BF16)
    later = jnp.concatenate([(row > col).astype(BF16), ones], axis=1)
    from_here = jnp.concatenate([(row >= col).astype(BF16), ones], axis=1)
    first = _iota2((tq, 128), 1) < SB_HEAD_DIM
    return later, from_here, first


def _sb_causal(d):
    tq, tk = SB_QUERY_ROWS, SB_BLOCK
    return _iota2((tq, tk), 1) + d * tk < _iota2((tq, tk), 0)


def _sums(x, mat):
    hi, lo = _split_bf16(x)
    return jnp.dot(hi, mat, preferred_element_type=F32) + jnp.dot(lo, mat, preferred_element_type=F32)


def _sb_weights(qh, kb, acc, later, causal):
    tk = SB_BLOCK
    z = lax.dot_general(qh, kb, (((1,), (1,)), ((), ())), preferred_element_type=F32)
    lk = -_softplus(z)
    if causal is not None:
        lk = jnp.where(causal, lk, 0.0)
    cs = _sums(lk, later)
    e = z + lk + cs[:, :tk] + acc
    if causal is not None:
        e = jnp.where(causal, e, -1e30)
    return lk, jnp.exp(e), cs[:, tk:]


def _sb_alive(accs):
    return jnp.max(jnp.maximum(accs[0], accs[1])) > SB_DEAD_LOG


def _sb_sweep(i, block, carry, accs_of, stop=None):
    per = SB_QUERY_ROWS // SB_BLOCK
    for d in reversed(range(per)):
        carry = block(i * per + d, carry, _sb_causal(d))
    if stop is not None:
        return lax.fori_loop(0, i * per - 1 - stop, lambda jj, cr: block(i * per - 1 - jj, cr, None), carry)

    def step(state):
        j, _, cr = state
        cr = block(j, cr, None)
        return j - 1, _sb_alive(accs_of(cr)), cr

    j, _, carry = lax.while_loop(lambda st: jnp.logical_and(st[0] >= 0, st[1]), step,
                                 (i * per - 1, _sb_alive(accs_of(carry)), carry))
    return carry, j


def _sb_fwd(p, *, name):
    s = p.shape[0]
    t, tk = SB_QUERY_ROWS, SB_BLOCK
    scale = SB_HEAD_DIM ** -0.5

    def body(q_ref, k_ref, v_ref, o_ref):
        i = pl.program_id(1)
        later, _, first = _sb_consts()
        q = q_ref[...] * scale
        qs = (jnp.where(first, q, 0.0).astype(BF16), jnp.where(first, 0.0, q).astype(BF16))

        def block(j, carry, causal):
            start = pl.multiple_of(j * tk, tk)
            kb = k_ref[pl.ds(start, tk), :].astype(BF16)
            vb = v_ref[pl.ds(start, tk), :].astype(BF16)
            out = []
            for h in range(2):
                o_h, acc = carry[h]
                _, w, tot = _sb_weights(qs[h], kb, acc, later, causal)
                out.append((o_h + jnp.dot(w.astype(BF16), vb, preferred_element_type=F32), acc + tot))
            return tuple(out)

        zero = jnp.zeros((t, 128), F32)
        carry, _ = _sb_sweep(i, block, ((zero, zero), (zero, zero)), lambda cr: (cr[0][1], cr[1][1]))
        o_ref[...] = jnp.where(first, carry[0][0], carry[1][0])

    return pl.pallas_call(
        body, name=name, grid=(BRANCH_WIDTH // 128, s // t),
        in_specs=[pl.BlockSpec((t, 128), lambda pr, i: (i, OFF_CQ // 128 + pr)),
                  pl.BlockSpec((s, 128), lambda pr, i: (0, OFF_CK // 128 + pr)),
                  pl.BlockSpec((s, 128), lambda pr, i: (0, OFF_CV // 128 + pr))],
        out_specs=pl.BlockSpec((t, 128), lambda pr, i: (i, pr)),
        out_shape=jax.ShapeDtypeStruct((s, BRANCH_WIDTH), F32),
        compiler_params=_params(("arbitrary", "arbitrary")),
    )(p, p, p)


def _sb_bwd(p, do, *, name):
    s = p.shape[0]
    t, tk = SB_QUERY_ROWS, SB_BLOCK
    scale = SB_HEAD_DIM ** -0.5

    def body(q_ref, k_ref, v_ref, do_ref, dq_ref, dk_ref, dv_ref):
        i = pl.program_id(1)

        @pl.when(i == 0)
        def _():
            dk_ref[...] = jnp.zeros_like(dk_ref)
            dv_ref[...] = jnp.zeros_like(dv_ref)

        later, from_here, first = _sb_consts()
        q = q_ref[...] * scale
        do = do_ref[...]
        qs = (jnp.where(first, q, 0.0).astype(BF16), jnp.where(first, 0.0, q).astype(BF16))
        dos = (jnp.where(first, do, 0.0).astype(BF16), jnp.where(first, 0.0, do).astype(BF16))

        def total(j, carry, causal):
            start = pl.multiple_of(j * tk, tk)
            kb = k_ref[pl.ds(start, tk), :].astype(BF16)
            vb = v_ref[pl.ds(start, tk), :].astype(BF16)
            out = []
            for h in range(2):
                acc, acc_de = carry[h]
                _, w, tot = _sb_weights(qs[h], kb, acc, later, causal)
                dw = lax.dot_general(dos[h], vb, (((1,), (1,)), ((), ())), preferred_element_type=F32)
                out.append((acc + tot, acc_de + _sums(dw * w, from_here)[:, tk:]))
            return tuple(out)

        zero = jnp.zeros((t, 128), F32)
        sums, stop = _sb_sweep(i, total, ((zero, zero), (zero, zero)), lambda cr: (cr[0][0], cr[1][0]))
        deltas = (sums[0][1], sums[1][1])

        def block(j, carry, causal):
            start = pl.multiple_of(j * tk, tk)
            kb = k_ref[pl.ds(start, tk), :].astype(BF16)
            vb = v_ref[pl.ds(start, tk), :].astype(BF16)
            dk_blk = jnp.zeros((tk, 128), F32)
            dv_blk = jnp.zeros((tk, 128), F32)
            out = []
            for h in range(2):
                dq_h, acc, acc_de = carry[h]
                lk, w, tot = _sb_weights(qs[h], kb, acc, later, causal)
                dw = lax.dot_general(dos[h], vb, (((1,), (1,)), ((), ())), preferred_element_type=F32)
                de = dw * w
                cs = _sums(de, from_here)
                g_before = deltas[h] - (cs[:, :tk] + acc_de)
                keep = jnp.exp(lk)
                dz = de * keep - g_before * (1.0 - keep)
                if causal is not None:
                    dz = jnp.where(causal, dz, 0.0)
                dzb = dz.astype(BF16)
                dq_h = dq_h + jnp.dot(dzb, kb, preferred_element_type=F32)
                dk_blk = dk_blk + lax.dot_general(dzb, qs[h], (((0,), (0,)), ((), ())), preferred_element_type=F32)
                dv_blk = dv_blk + lax.dot_general(w.astype(BF16), dos[h], (((0,), (0,)), ((), ())),
                                                  preferred_element_type=F32)
                out.append((dq_h, acc + tot, acc_de + cs[:, tk:]))
            dk_ref[pl.ds(start, tk), :] += dk_blk
            dv_ref[pl.ds(start, tk), :] += dv_blk
            return tuple(out)

        carry = _sb_sweep(i, block, ((zero, zero, zero), (zero, zero, zero)), None, stop=stop)
        dq_ref[...] = jnp.where(first, carry[0][0], carry[1][0]) * scale

    qblk = lambda off: pl.BlockSpec((t, 128), lambda pr, i: (i, off // 128 + pr))
    full = lambda off: pl.BlockSpec((s, 128), lambda pr, i: (0, off // 128 + pr))
    res = pl.pallas_call(
        body, name=name, grid=(BRANCH_WIDTH // 128, s // t),
        in_specs=[qblk(OFF_CQ), full(OFF_CK), full(OFF_CV), qblk(0)],
        out_specs=[qblk(0), full(0), full(0)],
        out_shape=[jax.ShapeDtypeStruct((s, BRANCH_WIDTH), F32)] * 3,
        compiler_params=_params(("arbitrary", "arbitrary")),
    )(p, p, p, do)
    return list(res)


def _loss_head(y, target, *, name, ts=512):
    s, d = y.shape
    ts = min(ts, s)

    def body(y_ref, t_ref, sq_ref, dy_ref):
        @pl.when(pl.program_id(0) == 0)
        def _():
            sq_ref[...] = jnp.zeros_like(sq_ref)

        err = y_ref[...] - t_ref[...]
        dy_ref[...] = err * (1.0 / d)
        tot = jnp.sum(jnp.sum(err * err, axis=1, keepdims=True), axis=0, keepdims=True)
        sq_ref[...] += jnp.broadcast_to(tot, sq_ref.shape)

    blk = pl.BlockSpec((ts, d), lambda i: (i, 0))
    return pl.pallas_call(
        body, name=name, grid=(s // ts,), in_specs=[blk, blk],
        out_specs=[pl.BlockSpec((1, 128), lambda i: (0, 0)), blk],
        out_shape=[jax.ShapeDtypeStruct((1, 128), F32), jax.ShapeDtypeStruct((s, d), F32)],
        compiler_params=_params(("arbitrary",)),
    )(y, target)


def _row_tile(r, limit=512):
    return max(t for t in range(8, limit + 1, 8) if r % t == 0)


def _adamw(w, g, m, v, *, name):
    r = w.shape[0]
    ts = _row_tile(r)

    def body(w_ref, g_ref, m_ref, v_ref, d_ref, nm_ref, nv_ref):
        gv = g_ref[...]
        m_new = ADAM_B1 * m_ref[...] + (1.0 - ADAM_B1) * gv
        v_new = ADAM_B2 * v_ref[...] + (1.0 - ADAM_B2) * jnp.square(gv)
        m_hat = m_new / (1.0 - ADAM_B1 ** ADAM_STEP)
        v_hat = v_new / (1.0 - ADAM_B2 ** ADAM_STEP)
        d_ref[...] = -ADAM_LR * (m_hat / (jnp.sqrt(v_hat) + ADAM_EPS) + ADAM_WD * w_ref[...])
        nm_ref[...] = m_new
        nv_ref[...] = v_new

    blk = pl.BlockSpec((ts, LANES), lambda i: (i, 0))
    return pl.pallas_call(
        body, name=name, grid=(r // ts,), in_specs=[blk] * 4, out_specs=[blk] * 3,
        out_shape=[jax.ShapeDtypeStruct((r, LANES), F32)] * 3,
        compiler_params=_params(("parallel",)),
    )(w, g, m, v)


def _add_rows(terms, *, name):
    r = terms[0].shape[0]
    ts = _row_tile(r)

    def body(*refs):
        acc = refs[0][...]
        for ref in refs[1:-1]:
            acc = acc + ref[...]
        refs[-1][...] = acc

    blk = pl.BlockSpec((ts, LANES), lambda i: (i, 0))
    return pl.pallas_call(
        body, name=name, grid=(r // ts,), in_specs=[blk] * len(terms), out_specs=blk,
        out_shape=jax.ShapeDtypeStruct((r, LANES), F32), compiler_params=_params(("parallel",)),
    )(*terms)


_HBM = pl.BlockSpec(memory_space=pltpu.HBM)
_MESH = pl.DeviceIdType.MESH


def _other_chips(x, y):
    return [(1 - x, y), (x, 1 - y), (1 - x, 1 - y)]


def _gather_chips(shard, *, name):
    r, lanes = shard.shape

    def body(in_ref, out_ref, send_sems, recv_sems, local_sem):
        x, y, c = lax.axis_index("x"), lax.axis_index("y"), lax.axis_index("c")
        me = 2 * x + y
        mine = pltpu.make_async_copy(in_ref, out_ref.at[me], local_sem)
        mine.start()
        copies = []
        for kk, (px, py) in enumerate(_other_chips(x, y)):
            cp = pltpu.make_async_remote_copy(src_ref=in_ref, dst_ref=out_ref.at[me], send_sem=send_sems.at[kk],
                                              recv_sem=recv_sems.at[kk], device_id=(px, py, c), device_id_type=_MESH)
            cp.start()
            copies.append(cp)
        for kk, (px, py) in enumerate(_other_chips(x, y)):
            pltpu.make_async_remote_copy(src_ref=in_ref, dst_ref=out_ref.at[2 * px + py], send_sem=send_sems.at[kk],
                                         recv_sem=recv_sems.at[kk], device_id=(px, py, c),
                                         device_id_type=_MESH).wait_recv()
        for cp in copies:
            cp.wait_send()
        mine.wait()

    return pl.pallas_call(
        body, name=name, in_specs=[_HBM], out_specs=_HBM,
        out_shape=jax.ShapeDtypeStruct((4, r, lanes), shard.dtype),
        scratch_shapes=[pltpu.SemaphoreType.DMA((3,)), pltpu.SemaphoreType.DMA((3,)), pltpu.SemaphoreType.DMA],
    )(shard)


def _to_sibling(block, *, name):
    def body(in_ref, out_ref, send_sem, recv_sem):
        x, y, c = lax.axis_index("x"), lax.axis_index("y"), lax.axis_index("c")
        cp = pltpu.make_async_remote_copy(src_ref=in_ref, dst_ref=out_ref, send_sem=send_sem, recv_sem=recv_sem,
                                          device_id=(x, y, 1 - c), device_id_type=_MESH)
        cp.start()
        cp.wait()

    return pl.pallas_call(
        body, name=name, in_specs=[_HBM], out_specs=_HBM, out_shape=jax.ShapeDtypeStruct(block.shape, block.dtype),
        scratch_shapes=[pltpu.SemaphoreType.DMA, pltpu.SemaphoreType.DMA],
    )(block)


def _join_halves(half, *, name):
    r, lanes = half.shape

    def body(in_ref, out_ref, send_sem, recv_sem, local_sem):
        x, y, c = lax.axis_index("x"), lax.axis_index("y"), lax.axis_index("c")
        mine = pltpu.make_async_copy(in_ref, out_ref.at[c], local_sem)
        mine.start()
        cp = pltpu.make_async_remote_copy(src_ref=in_ref, dst_ref=out_ref.at[c], send_sem=send_sem, recv_sem=recv_sem,
                                          device_id=(x, y, 1 - c), device_id_type=_MESH)
        cp.start()
        pltpu.make_async_remote_copy(src_ref=in_ref, dst_ref=out_ref.at[1 - c], send_sem=send_sem, recv_sem=recv_sem,
                                     device_id=(x, y, 1 - c), device_id_type=_MESH).wait_recv()
        cp.wait_send()
        mine.wait()

    return pl.pallas_call(
        body, name=name, in_specs=[_HBM], out_specs=_HBM, out_shape=jax.ShapeDtypeStruct((2, r, lanes), half.dtype),
        scratch_shapes=[pltpu.SemaphoreType.DMA, pltpu.SemaphoreType.DMA, pltpu.SemaphoreType.DMA],
    )(half)


def _scatter_chips(blocks, *, name):
    _, r, lanes = blocks.shape

    def body(in_ref, out_ref, send_sems, recv_sems, local_sem):
        x, y, c = lax.axis_index("x"), lax.axis_index("y"), lax.axis_index("c")
        me = 2 * x + y
        mine = pltpu.make_async_copy(in_ref.at[me], out_ref.at[me], local_sem)
        mine.start()
        copies = []
        for kk, (px, py) in enumerate(_other_chips(x, y)):
            cp = pltpu.make_async_remote_copy(src_ref=in_ref.at[2 * px + py], dst_ref=out_ref.at[me],
                                              send_sem=send_sems.at[kk], recv_sem=recv_sems.at[kk],
                                              device_id=(px, py, c), device_id_type=_MESH)
            cp.start()
            copies.append(cp)
        for kk, (px, py) in enumerate(_other_chips(x, y)):
            pltpu.make_async_remote_copy(src_ref=in_ref.at[me], dst_ref=out_ref.at[2 * px + py],
                                         send_sem=send_sems.at[kk], recv_sem=recv_sems.at[kk], device_id=(px, py, c),
                                         device_id_type=_MESH).wait_recv()
        for cp in copies:
            cp.wait_send()
        mine.wait()

    return pl.pallas_call(
        body, name=name, in_specs=[_HBM], out_specs=_HBM, out_shape=jax.ShapeDtypeStruct((4, r, lanes), blocks.dtype),
        scratch_shapes=[pltpu.SemaphoreType.DMA((3,)), pltpu.SemaphoreType.DMA((3,)), pltpu.SemaphoreType.DMA],
    )(blocks)


SHARDED = (("norm_g", 2), ("w_in", 2), ("conv_w", 2), ("w_branch", 3), ("w_out", 1), ("w_ff1", 2), ("w_ff2", 1))
MATMUL_WEIGHTS = ("w_in", "w_branch", "w_out", "w_ff1", "w_ff2")
VECTOR_WEIGHTS = ("norm_g", "conv_w")
REPLICATED = ("b_in", "sgu_ln_g", "sgu_ln_b", "w_spatial", "b_spatial", "a_log", "dt_bias", "dn_norm_g")
WEIGHT_ORDER = ("norm_g", "w_in", "b_in", "sgu_ln_g", "sgu_ln_b", "w_spatial", "b_spatial", "conv_w", "a_log",
                "dt_bias", "dn_norm_g", "w_branch", "w_out", "w_ff1", "w_ff2")
PACK_ROW_MULTIPLE = 32


def _rows_of(shape):
    n = 1
    for dim in shape:
        n *= dim
    return -(-n // LANES)


def _pack(arrays):
    parts = []
    for a in arrays:
        flat = a.reshape(-1)
        pad = _rows_of(a.shape) * LANES - flat.shape[0]
        if pad:
            flat = jnp.concatenate([flat, jnp.zeros((pad,), flat.dtype)])
        parts.append(flat.reshape(-1, LANES))
    rows = sum(p.shape[0] for p in parts)
    pad = -rows % PACK_ROW_MULTIPLE
    if pad:
        parts.append(jnp.zeros((pad, LANES), parts[0].dtype))
    return jnp.concatenate(parts, axis=0)


def _unpack(buf, shapes):
    out, row = [], 0
    for shape in shapes:
        n = 1
        for dim in shape:
            n *= dim
        rows = _rows_of(shape)
        out.append(buf[row:row + rows].reshape(-1)[:n].reshape(shape))
        row += rows
    return out


def _chip_slice(a, axis, k):
    size = a.shape[axis] // 4
    return lax.slice_in_dim(a, k * size, (k + 1) * size, axis=axis)


def _rearrange_w_in(w):
    pad = jnp.zeros(w.shape[:-1] + (P_PAD - P_IN,), w.dtype)
    return jnp.concatenate([w[..., 0:3072], w[..., 4616:7688], w[..., 3080:4616], w[..., 3072:3080], pad], axis=-1)


def _restore_w_in(w):
    return jnp.concatenate([w[..., 0:3072], w[..., 7680:7688], w[..., 6144:7680], w[..., 3072:6144]], axis=-1)


def _shift_rows(a, n):
    if n == 0:
        return a
    return jnp.concatenate([jnp.zeros((n, a.shape[1]), a.dtype), a[:-n]], axis=0)


def _unshift_rows(a, n):
    if n == 0:
        return a
    return jnp.concatenate([a[n:], jnp.zeros((n, a.shape[1]), a.dtype)], axis=0)


def _layer_params(wl):
    row = lambda v: v.reshape(1, -1)
    pad128 = lambda v, at: jnp.pad(v, (at, 128 - at - v.shape[0])).reshape(1, 128)
    return dict(
        g=[row(wl["norm_g"][i]) for i in range(4)],
        gmlp=[row(wl["sgu_ln_g"]), row(wl["sgu_ln_b"]), wl["w_spatial"],
              jnp.pad(wl["b_spatial"].T, ((0, 0), (0, 128 - GM_GROUPS)))],
        dn_in=[wl["conv_w"], pad128(wl["a_log"], DN_HEADS), pad128(wl["dt_bias"], DN_HEADS)],
        dn_g=[row(wl["dn_norm_g"])],
    )


def _layer_fwd(x0, wl, l):
    tag = lambda s: f"{s}_l{l}"
    pr = _layer_params(wl)
    w = BRANCH_WIDTH
    h0 = _row_fwd(_fn_rms, [(x0, 0, D_MODEL)], [pr["g"][0]], [(D_MODEL, BF16)], ts=512, name=tag("rms0"))[0]
    p = _mm(h0, wl["w_in"], bias=wl["b_in"].reshape(1, -1), name=tag("proj_in"))
    ya = _row_fwd(_fn_gmlp, [(p, OFF_AU, w), (p, OFF_AV, w)], pr["gmlp"], [(w, BF16)], ts=256, name=tag("gmlp"))[0]
    xq = p[:, OFF_BQ:OFF_BQ + 3 * w]
    shifted = [_shift_rows(xq, 3 - j) for j in range(4)]
    dn_rows = [(a, 0, 3 * w) for a in shifted] + [(p, OFF_BD, 128)]
    q, k, v, beta, g = _row_fwd(_fn_dn_in, dn_rows, pr["dn_in"], [(w, F32)] * 5, ts=256, name=tag("dn_in"))
    o, states = _delta_fwd(q, k, v, beta, g, name=tag("delta"))
    yb = _row_fwd(_fn_dn_out, [(o, 0, w), (p, OFF_BZ, w)], pr["dn_g"], [(w, BF16)], ts=512, name=tag("dn_out"))[0]
    yc = _sb_fwd(p, name=tag("sb"))
    ys = [ya, yb, yc]
    proj = [_mm(ys[i], wl["w_branch"][i], name=tag(f"branch{i}")) for i in range(3)]
    merge_rows = [(a, 0, D_MODEL) for a in proj] + [(p, OFF_GATE + i * D_MODEL, D_MODEL) for i in range(3)]
    m = _row_fwd(_fn_merge, merge_rows, [], [(D_MODEL, BF16)], ts=256, name=tag("merge"))[0]
    mixed = _mm(m, wl["w_out"], name=tag("out"))
    x1 = _row_fwd(_fn_resid_rms, [(x0, 0, D_MODEL), (mixed, 0, D_MODEL)], [pr["g"][1]], [(D_MODEL, F32)], ts=512,
                  name=tag("resid1"))[0]
    h2 = _row_fwd(_fn_rms, [(x1, 0, D_MODEL)], [pr["g"][2]], [(D_MODEL, BF16)], ts=512, name=tag("rms2"))[0]
    a = _mm(h2, wl["w_ff1"], name=tag("ff1"))
    r = _row_fwd(_fn_relu2, [(a, 0, D_FF)], [], [(D_FF, BF16)], ts=256, name=tag("relu2"))[0]
    f = _mm(r, wl["w_ff2"], name=tag("ff2"))
    x2 = _row_fwd(_fn_resid_rms, [(x1, 0, D_MODEL), (f, 0, D_MODEL)], [pr["g"][3]], [(D_MODEL, F32)], ts=512,
                  name=tag("resid2"))[0]
    saved = dict(x0=x0, h0=h0, p=p, shifted=shifted, q=q, k=k, v=v, beta=beta, g=g, states=states, o=o, ys=ys,
                 proj=proj, m=m, mixed=mixed, x1=x1, h2=h2, a=a, r=r, f=f)
    return x2, saved


def _layer_bwd(dx2, sv, wl, l):
    tag = lambda s: f"{s}_l{l}"
    pr = _layer_params(wl)
    w = BRANCH_WIDTH
    full = lambda a: (a, 0, a.shape[1])
    p = sv["p"]
    (df,), (dg3,) = _row_bwd(_fn_rms_branch, [full(sv["f"])], [pr["g"][3]], [full(dx2)], ts=512, name=tag("resid2_b"))
    dr = _mm(df, wl["w_ff2"], trans_b=True, name=tag("ff2_dx"))
    dw_ff2 = _mm_tn(sv["r"], df, name=tag("ff2_dw"))
    (da,), _ = _row_bwd(_fn_relu2, [full(sv["a"])], [], [full(dr)], ts=256, name=tag("relu2_b"))
    dh2 = _mm(da, wl["w_ff1"], trans_b=True, name=tag("ff1_dx"))
    dw_ff1 = _mm_tn(sv["h2"], da, name=tag("ff1_dw"))
    (dx1,), (dg2,) = _row_bwd(_fn_rms_keep, [full(sv["x1"])], [pr["g"][2]], [full(dh2), full(dx2)], ts=512,
                              name=tag("rms2_b"))
    (dmixed,), (dg1,) = _row_bwd(_fn_rms_branch, [full(sv["mixed"])], [pr["g"][1]], [full(dx1)], ts=512,
                                 name=tag("resid1_b"))
    dm = _mm(dmixed, wl["w_out"], trans_b=True, name=tag("out_dx"))
    dw_out = _mm_tn(sv["m"], dmixed, name=tag("out_dw"))
    merge_rows = [full(a) for a in sv["proj"]] + [(p, OFF_GATE + i * D_MODEL, D_MODEL) for i in range(3)]
    dmerge, _ = _row_bwd(_fn_merge, merge_rows, [], [full(dm)], ts=256, name=tag("merge_b"))
    dproj, dgates = dmerge[:3], dmerge[3:]
    dys = [_mm(dproj[i], wl["w_branch"][i], trans_b=True, name=tag(f"branch{i}_dx")) for i in range(3)]
    dw_branch = jnp.stack([_mm_tn(sv["ys"][i], dproj[i], name=tag(f"branch{i}_dw")) for i in range(3)])
    (du, dv_a), dgm = _row_bwd(_fn_gmlp, [(p, OFF_AU, w), (p, OFF_AV, w)], pr["gmlp"], [full(dys[0])], ts=256,
                               name=tag("gmlp_b"))
    (do, dz), (d_dn_g,) = _row_bwd(_fn_dn_out, [full(sv["o"]), (p, OFF_BZ, w)], pr["dn_g"], [full(dys[1])], ts=512,
                                   name=tag("dn_out_b"))
    dqkvbg = _delta_bwd(sv["q"], sv["k"], sv["v"], sv["beta"], sv["g"], sv["states"], do, name=tag("delta_b"))
    dn_rows = [full(a) for a in sv["shifted"]] + [(p, OFF_BD, 128)]
    d_in, d_dn_in = _row_bwd(_fn_dn_in, dn_rows, pr["dn_in"], [full(a) for a in dqkvbg], ts=256, name=tag("dn_in_b"))
    dxq = sum(_unshift_rows(d_in[j], 3 - j) for j in range(4))
    dcq, dck, dcv = _sb_bwd(p, dys[2], name=tag("sb_b"))
    s = p.shape[0]
    dp = jnp.concatenate([du, dv_a, dxq, dz] + dgates + [dcq, dck, dcv, d_in[4],
                                                         jnp.zeros((s, P_PAD - OFF_BD - 128), F32)], axis=1)
    dh0 = _mm(dp, wl["w_in"], trans_b=True, name=tag("proj_in_dx"))
    dw_in = _mm_tn(sv["h0"], dp, name=tag("proj_in_dw"))
    _, (db_in,) = _row_bwd(_fn_bias, [full(p)], [wl["b_in"].reshape(1, -1)], [full(dp)], ts=256, name=tag("bias_b"),
                           row_grads=False)
    (dx0,), (dg0,) = _row_bwd(_fn_rms_keep, [full(sv["x0"])], [pr["g"][0]], [full(dh0), full(dx1)], ts=512,
                              name=tag("rms0_b"))
    grads = dict(
        norm_g=jnp.concatenate([dg0, dg1, dg2, dg3], axis=0), w_in=dw_in, b_in=db_in.reshape(-1),
        sgu_ln_g=dgm[0].reshape(-1), sgu_ln_b=dgm[1].reshape(-1), w_spatial=dgm[2],
        b_spatial=dgm[3][:, :GM_GROUPS].T, conv_w=d_dn_in[0], a_log=d_dn_in[1][0, DN_HEADS:2 * DN_HEADS],
        dt_bias=d_dn_in[2][0, DN_HEADS:2 * DN_HEADS], dn_norm_g=d_dn_g.reshape(-1), w_branch=dw_branch,
        w_out=dw_out, w_ff1=dw_ff1, w_ff2=dw_ff2)
    return dx0, grads


def _local_step(x, target, weights):
    saved = []
    h = x
    layers = []
    for l in range(DEPTH):
        wl = {n: weights[n][l] for n in WEIGHT_ORDER}
        layers.append(wl)
        h, sv = _layer_fwd(h, wl, l)
        saved.append(sv)
    sq, dh = _loss_head(h, target, name="loss_head")
    grads = [None] * DEPTH
    for l in reversed(range(DEPTH)):
        dh, grads[l] = _layer_bwd(dh, saved[l], layers[l], l)
    stacked = {n: jnp.stack([grads[l][n] for l in range(DEPTH)]) for n in WEIGHT_ORDER}
    return sq[0, 0], dh, stacked


def kernel(x, norm_g, w_in, b_in, sgu_ln_g, sgu_ln_b, w_spatial, b_spatial, conv_w, a_log, dt_bias, dn_norm_g, w_branch, w_out, w_ff1, w_ff2, loss_target, m_norm_g, m_w_in, m_b_in, m_sgu_ln_g, m_sgu_ln_b, m_w_spatial, m_b_spatial, m_conv_w, m_a_log, m_dt_bias, m_dn_norm_g, m_w_branch, m_w_out, m_w_ff1, m_w_ff2, v_norm_g, v_w_in, v_b_in, v_sgu_ln_g, v_sgu_ln_b, v_w_spatial, v_b_spatial, v_conv_w, v_a_log, v_dt_bias, v_dn_norm_g, v_w_branch, v_w_out, v_w_ff1, v_w_ff2):
    local = dict(norm_g=norm_g, w_in=w_in, b_in=b_in, sgu_ln_g=sgu_ln_g, sgu_ln_b=sgu_ln_b, w_spatial=w_spatial,
                 b_spatial=b_spatial, conv_w=conv_w, a_log=a_log, dt_bias=dt_bias, dn_norm_g=dn_norm_g,
                 w_branch=w_branch, w_out=w_out, w_ff1=w_ff1, w_ff2=w_ff2)
    mom1 = dict(norm_g=m_norm_g, w_in=m_w_in, b_in=m_b_in, sgu_ln_g=m_sgu_ln_g, sgu_ln_b=m_sgu_ln_b,
                w_spatial=m_w_spatial, b_spatial=m_b_spatial, conv_w=m_conv_w, a_log=m_a_log, dt_bias=m_dt_bias,
                dn_norm_g=m_dn_norm_g, w_branch=m_w_branch, w_out=m_w_out, w_ff1=m_w_ff1, w_ff2=m_w_ff2)
    mom2 = dict(norm_g=v_norm_g, w_in=v_w_in, b_in=v_b_in, sgu_ln_g=v_sgu_ln_g, sgu_ln_b=v_sgu_ln_b,
                w_spatial=v_w_spatial, b_spatial=v_b_spatial, conv_w=v_conv_w, a_log=v_a_log, dt_bias=v_dt_bias,
                dn_norm_g=v_dn_norm_g, w_branch=v_w_branch, w_out=v_w_out, w_ff1=v_w_ff1, w_ff2=v_w_ff2)
    shard_names = [n for n, _ in SHARDED]
    shard_shapes = [local[n].shape for n in shard_names]
    repl_shapes = [local[n].shape for n in REPLICATED]

    weights = {n: local[n] for n in REPLICATED}
    for names, dtype, call in ((MATMUL_WEIGHTS, BF16, "gather_matmul_weights"), (VECTOR_WEIGHTS, F32, "gather_vectors")):
        gathered = _gather_chips(_pack([local[n] for n in names]).astype(dtype), name=call)
        per_chip = [_unpack(gathered[k], [local[n].shape for n in names]) for k in range(4)]
        for i, n in enumerate(names):
            weights[n] = jnp.concatenate([per_chip[k][i] for k in range(4)], axis=dict(SHARDED)[n])
    weights["w_in"] = _rearrange_w_in(weights["w_in"])
    weights["b_in"] = _rearrange_w_in(weights["b_in"])

    sq, dx, grads = _local_step(x[0], loss_target[0], weights)
    loss = lax.psum(0.5 * sq / D_MODEL, ("x", "y", "c"))
    grads["w_in"] = _restore_w_in(grads["w_in"])
    grads["b_in"] = _restore_w_in(grads["b_in"])

    blocks = jnp.stack([_pack([_chip_slice(grads[n], axis, k) for n, axis in SHARDED] + [grads[n] for n in REPLICATED])
                        for k in range(4)])
    rows = blocks.shape[1]
    half = rows // 2
    c = lax.axis_index("c")
    halves = blocks.reshape(4, 2, half, LANES)
    keep = lax.dynamic_index_in_dim(halves, c, axis=1, keepdims=False)
    give = lax.dynamic_index_in_dim(halves, 1 - c, axis=1, keepdims=False)
    got = _to_sibling(give, name="grads_to_sibling")
    chip_sum = _add_rows([keep.reshape(4 * half, LANES), got.reshape(4 * half, LANES)], name="grads_chip_sum")
    by_chip = _scatter_chips(chip_sum.reshape(4, half, LANES), name="grads_scatter")
    my_half = _add_rows([by_chip[k] for k in range(4)], name="grads_sum")
    total = _join_halves(my_half, name="grads_join_halves").reshape(rows, LANES)

    pack_local = lambda d: _pack([d[n] for n in shard_names] + [d[n] for n in REPLICATED])
    delta, new_m, new_v = _adamw(pack_local(local), total, pack_local(mom1), pack_local(mom2), name="adamw")
    unpack = lambda buf: dict(zip(shard_names + list(REPLICATED), _unpack(buf, shard_shapes + repl_shapes)))
    g_out, d_out, m_out, v_out = unpack(total), unpack(delta), unpack(new_m), unpack(new_v)
    return (loss, dx[None], *[g_out[n] for n in WEIGHT_ORDER], *[d_out[n] for n in WEIGHT_ORDER],
            *[m_out[n] for n in WEIGHT_ORDER], *[v_out[n] for n in WEIGHT_ORDER])
```

```python
import functools

import jax
import jax.numpy as jnp
from jax import lax
from jax.experimental import pallas as pl
from jax.experimental.pallas import tpu as pltpu

F32 = jnp.float32
BF16 = jnp.bfloat16

D_MODEL = 1024
DEPTH = 4
BRANCH_WIDTH = 512
GM_CHUNK = 128
GM_GROUPS = 8
DN_HEADS = 4
DN_HEAD_DIM = 128
DN_BLOCK = 128
DN_SOLVE_PASSES, DN_STATE_PASSES, DN_OUT_PASSES = 3, 1, 1
SB_HEAD_DIM = 64
SB_BLOCK = 128
SB_QUERY_ROWS = 256
SB_DEAD_LOG = -105.0
D_FF = 4096
P_IN = 7688
P_PAD = 8192
NORM_EPS = 1e-6
ADAM_LR, ADAM_B1, ADAM_B2, ADAM_EPS, ADAM_WD, ADAM_STEP = 0.001, 0.9, 0.999, 1e-08, 0.01, 10

OFF_AU, OFF_AV, OFF_BQ, OFF_BK, OFF_BV, OFF_BZ = 0, 512, 1024, 1536, 2048, 2560
OFF_GATE = 3072
OFF_CQ, OFF_CK, OFF_CV = 6144, 6656, 7168
OFF_BD = 7680

LANES = 1024
VMEM_LIMIT_BYTES = 56 * 1024 * 1024

_HI = lax.Precision.HIGHEST


def _params(sem):
    return pltpu.CompilerParams(dimension_semantics=sem, vmem_limit_bytes=VMEM_LIMIT_BYTES)


def _mm(a, b, *, name, out_dtype=F32, bias=None, trans_b=False, tm=1024, tn=1024, tk=1024):
    m, k = a.shape
    n = b.shape[0] if trans_b else b.shape[1]
    tm, tn, tk = min(tm, m), min(tn, n), min(tk, k)
    assert m % tm == 0 and n % tn == 0 and k % tk == 0, (a.shape, b.shape)
    nk = k // tk
    dn = (((1,), (1,)), ((), ())) if trans_b else (((1,), (0,)), ((), ()))

    def body(*refs):
        if bias is None:
            a_ref, b_ref, o_ref, acc = refs
        else:
            a_ref, b_ref, bias_ref, o_ref, acc = refs
        kk = pl.program_id(2)
        part = lax.dot_general(a_ref[...].astype(BF16), b_ref[...].astype(BF16), dn, preferred_element_type=F32)

        @pl.when(kk == 0)
        def _():
            acc[...] = part

        @pl.when(kk > 0)
        def _():
            acc[...] += part

        @pl.when(kk == nk - 1)
        def _():
            r = acc[...]
            if bias is not None:
                r = r + bias_ref[...]
            o_ref[...] = r.astype(out_dtype)

    in_specs = [pl.BlockSpec((tm, tk), lambda i, j, kk: (i, kk))]
    if trans_b:
        in_specs.append(pl.BlockSpec((tn, tk), lambda i, j, kk: (j, kk)))
    else:
        in_specs.append(pl.BlockSpec((tk, tn), lambda i, j, kk: (kk, j)))
    args = [a, b]
    if bias is not None:
        in_specs.append(pl.BlockSpec((1, tn), lambda i, j, kk: (0, j)))
        args.append(bias)
    return pl.pallas_call(
        body, name=name, grid=(m // tm, n // tn, nk),
        in_specs=in_specs, out_specs=pl.BlockSpec((tm, tn), lambda i, j, kk: (i, j)),
        out_shape=jax.ShapeDtypeStruct((m, n), out_dtype),
        scratch_shapes=[pltpu.VMEM((tm, tn), F32)],
        compiler_params=_params(("parallel", "parallel", "arbitrary")),
    )(*args)


def _mm_tn(a, b, *, name, tm=1024, tn=1024, ts=1024):
    s, ka = a.shape
    n = b.shape[1]
    tm, tn, ts = min(tm, ka), min(tn, n), min(ts, s)
    assert ka % tm == 0 and n % tn == 0 and s % ts == 0, (a.shape, b.shape)

    def body(a_ref, b_ref, o_ref):
        part = lax.dot_general(a_ref[...].astype(BF16), b_ref[...].astype(BF16), (((0,), (0,)), ((), ())),
                               preferred_element_type=F32)

        @pl.when(pl.program_id(2) == 0)
        def _():
            o_ref[...] = part

        @pl.when(pl.program_id(2) > 0)
        def _():
            o_ref[...] += part

    return pl.pallas_call(
        body, name=name, grid=(ka // tm, n // tn, s // ts),
        in_specs=[pl.BlockSpec((ts, tm), lambda i, j, r: (r, i)), pl.BlockSpec((ts, tn), lambda i, j, r: (r, j))],
        out_specs=pl.BlockSpec((tm, tn), lambda i, j, r: (i, j)),
        out_shape=jax.ShapeDtypeStruct((ka, n), F32),
        compiler_params=_params(("parallel", "parallel", "arbitrary")),
    )(a, b)


def _col_block(i, *, c):
    return (i, c)


def _whole(i, *, nd):
    return (0,) * nd


def _row_specs(rows, ts):
    specs = []
    for arr, off, w in rows:
        assert off % w == 0 and arr.shape[0] % ts == 0
        specs.append(pl.BlockSpec((ts, w), functools.partial(_col_block, c=off // w)))
    return specs


def _row_fwd(fn, rows, params, outs, *, ts, name):
    s = rows[0][0].shape[0]
    ts = min(ts, s)
    nr, npar = len(rows), len(params)

    def body(*refs):
        rv = [r[...].astype(F32) for r in refs[:nr]]
        pv = [p[...] for p in refs[nr:nr + npar]]
        for o_ref, val in zip(refs[nr + npar:], fn(pv, rv)):
            o_ref[...] = val.astype(o_ref.dtype)

    in_specs = _row_specs(rows, ts) + [pl.BlockSpec(p.shape, functools.partial(_whole, nd=p.ndim)) for p in params]
    res = pl.pallas_call(
        body, name=name, grid=(s // ts,), in_specs=in_specs,
        out_specs=[pl.BlockSpec((ts, w), lambda i: (i, 0)) for w, _ in outs],
        out_shape=[jax.ShapeDtypeStruct((s, w), dt) for w, dt in outs],
        compiler_params=_params(("parallel",)),
    )(*[r[0] for r in rows], *params)
    return list(res)


def _row_bwd(fn, rows, params, cts, *, ts, name, row_grads=True):
    s = rows[0][0].shape[0]
    ts = min(ts, s)
    nr, npar, nc = len(rows), len(params), len(cts)
    n_dr = nr if row_grads else 0

    def body(*refs):
        rv = [r[...].astype(F32) for r in refs[:nr]]
        pv = [p[...] for p in refs[nr:nr + npar]]
        cv = [c[...].astype(F32) for c in refs[nr + npar:nr + npar + nc]]
        out_refs = refs[nr + npar + nc:]
        _, vjp = jax.vjp(lambda p, r: tuple(fn(p, r)), pv, rv)
        dp, dr = vjp(tuple(cv))
        for o_ref, val in zip(out_refs[:n_dr], dr):
            o_ref[...] = val

        @pl.when(pl.program_id(0) == 0)
        def _():
            for o_ref in out_refs[n_dr:]:
                o_ref[...] = jnp.zeros_like(o_ref)

        for o_ref, val in zip(out_refs[n_dr:], dp):
            o_ref[...] += val

    in_specs = (_row_specs(rows, ts) + [pl.BlockSpec(p.shape, functools.partial(_whole, nd=p.ndim)) for p in params]
                + _row_specs(cts, ts))
    out_specs = [pl.BlockSpec((ts, w), lambda i: (i, 0)) for _, _, w in rows[:n_dr]]
    out_specs += [pl.BlockSpec(p.shape, functools.partial(_whole, nd=p.ndim)) for p in params]
    out_shape = [jax.ShapeDtypeStruct((s, w), F32) for _, _, w in rows[:n_dr]]
    out_shape += [jax.ShapeDtypeStruct(p.shape, F32) for p in params]
    res = pl.pallas_call(
        body, name=name, grid=(s // ts,), in_specs=in_specs, out_specs=out_specs, out_shape=out_shape,
        compiler_params=_params(("arbitrary",)),
    )(*[r[0] for r in rows], *params, *[c[0] for c in cts])
    res = list(res)
    return res[:n_dr], res[n_dr:]


def _rms(x, g):
    return x * lax.rsqrt(jnp.mean(x * x, axis=-1, keepdims=True) + NORM_EPS) * g


def _gelu(x):
    return 0.5 * x * (1.0 + lax.erf(x * (2.0 ** -0.5)))


def _softplus(x):
    return jnp.maximum(x, 0.0) + jnp.log1p(jnp.exp(-jnp.abs(x)))


def _iota2(shape, dim):
    return lax.broadcasted_iota(jnp.int32, shape, dim)


def _fn_rms(pv, rv):
    return [_rms(rv[0], pv[0])]


def _fn_rms_keep(pv, rv):
    return [_rms(rv[0], pv[0]), rv[0]]


def _fn_resid_rms(pv, rv):
    return [rv[0] + _rms(rv[1], pv[0])]


def _fn_rms_branch(pv, rv):
    return [_rms(rv[0], pv[0])]


def _fn_relu2(pv, rv):
    return [jnp.square(jnp.maximum(rv[0], 0.0))]


def _fn_merge(pv, rv):
    return [sum(jax.nn.sigmoid(rv[3 + i]) * rv[i] for i in range(3))]


def _fn_gmlp(pv, rv):
    ln_g, ln_b, w_sp, b_t = pv
    u = _gelu(rv[0])
    v = _gelu(rv[1])
    vc = v - jnp.mean(v, axis=-1, keepdims=True)
    v = vc * lax.rsqrt(jnp.mean(vc * vc, axis=-1, keepdims=True) + NORM_EPS) * ln_g + ln_b
    t = GM_CHUNK
    causal = _iota2((t, t), 1) <= _iota2((t, t), 0)
    first = _iota2((t, 128), 1) < 64
    expand = (_iota2((128, BRANCH_WIDTH), 0) == _iota2((128, BRANCH_WIDTH), 1) // 64).astype(F32)
    b_full = jnp.dot(b_t, expand, precision=_HI, preferred_element_type=F32)
    w_bf = [jnp.where(causal, w_sp[g], 0.0).astype(BF16) for g in range(GM_GROUPS)]
    chunks = []
    for c in range(rv[0].shape[0] // t):
        pairs = []
        for p in range(GM_GROUPS // 2):
            vp = v[c * t:(c + 1) * t, 128 * p:128 * (p + 1)].astype(BF16)
            m0 = jnp.dot(w_bf[2 * p], vp, preferred_element_type=F32)
            m1 = jnp.dot(w_bf[2 * p + 1], vp, preferred_element_type=F32)
            pairs.append(jnp.where(first, m0, m1))
        chunks.append(jnp.concatenate(pairs, axis=1) + b_full)
    return [u * jnp.concatenate(chunks, axis=0)]


def _head_expand(col0):
    return (_iota2((128, BRANCH_WIDTH), 0) == _iota2((128, BRANCH_WIDTH), 1) // DN_HEAD_DIM + col0).astype(F32)


def _fn_dn_in(pv, rv):
    conv_w, a_log, dt_b = pv
    c = sum(conv_w[j:j + 1, :] * rv[j] for j in range(4))
    a = c * jax.nn.sigmoid(c)
    outs = []
    for part in range(3):
        heads = []
        for h in range(DN_HEADS):
            lo = part * BRANCH_WIDTH + h * DN_HEAD_DIM
            xh = a[:, lo:lo + DN_HEAD_DIM]
            if part < 2:
                xh = xh * lax.rsqrt(jnp.sum(xh * xh, axis=-1, keepdims=True) + NORM_EPS)
            heads.append(xh)
        outs.append(jnp.concatenate(heads, axis=1))
    bd = rv[4]
    beta = jax.nn.sigmoid(bd)
    g = -jnp.exp(a_log) * _softplus(bd + dt_b)
    outs.append(jnp.dot(beta, _head_expand(0), precision=_HI, preferred_element_type=F32))
    outs.append(jnp.dot(g, _head_expand(DN_HEADS), precision=_HI, preferred_element_type=F32))
    return outs


def _fn_dn_out(pv, rv):
    heads = []
    for h in range(DN_HEADS):
        sl = slice(h * DN_HEAD_DIM, (h + 1) * DN_HEAD_DIM)
        z = rv[1][:, sl]
        heads.append(_rms(rv[0][:, sl], pv[0]) * (z * jax.nn.sigmoid(z)))
    return [jnp.concatenate(heads, axis=1)]


_DIMS = {"nn": (((1,), (0,)), ((), ())), "nt": (((1,), (1,)), ((), ())), "tn": (((0,), (0,)), ((), ()))}
_DIMS_BWD = {"nn": (("nt", "c", "b"), ("tn", "a", "c")), "nt": (("nn", "c", "b"), ("tn", "c", "a")),
             "tn": (("nt", "b", "c"), ("nn", "a", "c"))}


def _bf16_dot(a, b, kind):
    return lax.dot_general(a.astype(BF16), b.astype(BF16), _DIMS[kind], preferred_element_type=F32)


def _pdot_raw(a, b, kind, mode):
    if mode == 1:
        return _bf16_dot(a, b, kind)
    if mode == 6:
        return lax.dot_general(a, b, _DIMS[kind], precision=_HI, preferred_element_type=F32)
    b_hi, b_lo = _split_bf16(b)
    if mode == 3:
        a_hi, a_lo = _split_bf16(a)
        return _bf16_dot(a_hi, b_hi, kind) + (_bf16_dot(a_hi, b_lo, kind) + _bf16_dot(a_lo, b_hi, kind))
    b_rest = (b - b_hi.astype(F32) - b_lo.astype(F32)).astype(BF16)
    return _bf16_dot(a, b_hi, kind) + (_bf16_dot(a, b_lo, kind) + _bf16_dot(a, b_rest, kind))


@functools.partial(jax.custom_vjp, nondiff_argnums=(2, 3))
def _pdot(a, b, kind, mode):
    return _pdot_raw(a, b, kind, mode)


def _pdot_fwd(a, b, kind, mode):
    return _pdot_raw(a, b, kind, mode), (a, b)


def _pdot_bwd(kind, mode, res, ct):
    ops = {"a": res[0], "b": res[1], "c": ct}
    (ka, a1, a2), (kb, b1, b2) = _DIMS_BWD[kind]
    if mode == "count":
        return jnp.zeros_like(res[0]), _pdot(ops[b1], ops[b2], kb, mode)
    return _pdot(ops[a1], ops[a2], ka, mode), _pdot(ops[b1], ops[b2], kb, mode)


_pdot.defvjp(_pdot_fwd, _pdot_bwd)


@jax.custom_vjp
def _unit_lower_inverses(mats):
    c = mats[0].shape[0]
    row, col = _iota2((c, c), 0), _iota2((c, c), 1)
    x = [(row == col).astype(F32) for _ in mats]
    shift = 0
    while (1 << shift) < c:
        pair = jnp.right_shift(row, shift + 1) == jnp.right_shift(col, shift + 1)
        between = pair & (jnp.right_shift(row, shift) != jnp.right_shift(col, shift))
        q = [jnp.where(between, a, 0.0) for a in mats]
        qd = [_pdot_raw(qi, xi, "nn", DN_SOLVE_PASSES) for qi, xi in zip(q, x)]
        x = [xi - _pdot_raw(xi, m, "nn", DN_SOLVE_PASSES) for xi, m in zip(x, qd)]
        shift += 1
    return tuple(x)


def _unit_lower_inverses_fwd(mats):
    x = _unit_lower_inverses(mats)
    return x, x


def _unit_lower_inverses_bwd(x, dx):
    inner = [_pdot_raw(d, xi, "nt", DN_SOLVE_PASSES) for d, xi in zip(dx, x)]
    return (tuple(-_pdot_raw(xi, m, "tn", DN_SOLVE_PASSES) for xi, m in zip(x, inner)),)


_unit_lower_inverses.defvjp(_unit_lower_inverses_fwd, _unit_lower_inverses_bwd)


def _head(h):
    return slice(h * DN_HEAD_DIM, (h + 1) * DN_HEAD_DIM)


def _delta_chunk(states, q, k, v, beta, g):
    c = DN_BLOCK
    heads = range(DN_HEADS)
    row, col = _iota2((c, c), 0), _iota2((c, c), 1)
    tri, strict = col <= row, col < row
    counts = jnp.concatenate([tri.astype(BF16), jnp.ones((c, c), BF16)], axis=0)
    sums = _pdot(counts, g, "nn", "count")
    gc = [sums[:c, _head(h)] for h in heads]
    gl = [sums[c:, _head(h)] for h in heads]
    qh, kh, vh, bh = ([t[:, _head(h)] for h in heads] for t in (q * (DN_HEAD_DIM ** -0.5), k, v, beta))
    decay = [jnp.where(tri, jnp.exp(jnp.where(tri, gc[h] - gc[h].T, 0.0)), 0.0) for h in heads]
    solve, carry, out = DN_SOLVE_PASSES, DN_STATE_PASSES, DN_OUT_PASSES
    kk = [_pdot(kh[h], kh[h], "nt", solve) for h in heads]
    x = _unit_lower_inverses(tuple(jnp.where(strict, bh[h] * kk[h] * decay[h], 0.0) for h in heads))
    eg = [jnp.exp(gc[h]) for h in heads]
    u = [_pdot(x[h], vh[h] * bh[h], "nn", solve) for h in heads]
    wk = [_pdot(x[h], kh[h] * (bh[h] * eg[h]), "nn", solve) for h in heads]
    qk = [jnp.where(tri, _pdot(qh[h], kh[h], "nt", out) * decay[h], 0.0) for h in heads]
    v_new = [u[h] - _pdot(wk[h], states[h], "nn", carry) for h in heads]
    o = [_pdot(qh[h] * eg[h], states[h], "nn", out) + _pdot(qk[h], v_new[h], "nn", out) for h in heads]
    nxt = [states[h] * jnp.exp(gl[h]) + _pdot(kh[h] * jnp.exp(gl[h] - gc[h]), v_new[h], "tn", carry)
           for h in heads]
    return jnp.concatenate(o, axis=1), tuple(nxt)


def _delta_fwd(q, k, v, beta, g, *, name):
    s = q.shape[0]
    c = DN_BLOCK
    nc = s // c

    def body(q_ref, k_ref, v_ref, b_ref, g_ref, o_ref, sp_ref, st):
        @pl.when(pl.program_id(0) == 0)
        def _():
            st[...] = jnp.zeros_like(st)

        states = tuple(st[h] for h in range(DN_HEADS))
        for h in range(DN_HEADS):
            sp_ref[0, h] = states[h]
        o, nxt = _delta_chunk(states, q_ref[...], k_ref[...], v_ref[...], b_ref[...], g_ref[...])
        for h in range(DN_HEADS):
            st[h] = nxt[h]
        o_ref[...] = o

    blk = pl.BlockSpec((c, BRANCH_WIDTH), lambda n: (n, 0))
    return pl.pallas_call(
        body, name=name, grid=(nc,), in_specs=[blk] * 5,
        out_specs=[blk, pl.BlockSpec((1, DN_HEADS, DN_HEAD_DIM, DN_HEAD_DIM), lambda n: (n, 0, 0, 0))],
        out_shape=[jax.ShapeDtypeStruct((s, BRANCH_WIDTH), F32),
                   jax.ShapeDtypeStruct((nc, DN_HEADS, DN_HEAD_DIM, DN_HEAD_DIM), F32)],
        scratch_shapes=[pltpu.VMEM((DN_HEADS, DN_HEAD_DIM, DN_HEAD_DIM), F32)],
        compiler_params=_params(("arbitrary",)),
    )(q, k, v, beta, g)


def _delta_bwd(q, k, v, beta, g, states, do, *, name):
    s = q.shape[0]
    c = DN_BLOCK
    nc = s // c

    def body(q_ref, k_ref, v_ref, b_ref, g_ref, sp_ref, do_ref, dq_ref, dk_ref, dv_ref, db_ref, dg_ref, dst):
        @pl.when(pl.program_id(0) == 0)
        def _():
            dst[...] = jnp.zeros_like(dst)

        states = tuple(sp_ref[0, h] for h in range(DN_HEADS))
        _, vjp = jax.vjp(_delta_chunk, states, q_ref[...], k_ref[...], v_ref[...], b_ref[...], g_ref[...])
        d = vjp((do_ref[...], tuple(dst[h] for h in range(DN_HEADS))))
        for h in range(DN_HEADS):
            dst[h] = d[0][h]
        for o_ref, val in zip((dq_ref, dk_ref, dv_ref, db_ref, dg_ref), d[1:]):
            o_ref[...] = val

    blk = pl.BlockSpec((c, BRANCH_WIDTH), lambda n: (nc - 1 - n, 0))
    res = pl.pallas_call(
        body, name=name, grid=(nc,),
        in_specs=[blk] * 5 + [pl.BlockSpec((1, DN_HEADS, DN_HEAD_DIM, DN_HEAD_DIM), lambda n: (nc - 1 - n, 0, 0, 0)), blk],
        out_specs=[blk] * 5, out_shape=[jax.ShapeDtypeStruct((s, BRANCH_WIDTH), F32)] * 5,
        scratch_shapes=[pltpu.VMEM((DN_HEADS, DN_HEAD_DIM, DN_HEAD_DIM), F32)],
        compiler_params=_params(("arbitrary",)),
    )(q, k, v, beta, g, states, do)
    return list(res)


def _split_bf16(x):
    hi = x.astype(BF16)
    return hi, (x - hi.astype(F32)).astype(BF16)


def _sb_consts():
    tq, tk = SB_QUERY_ROWS, SB_BLOCK
    row, col = _iota2((tk, tk), 0), _iota2((tk, tk), 1)
    ones = jnp.ones((tk, tk), BF16)
    later = jnp.concatenate([(row > col).astype(BF16), ones], axis=1)
    from_here = jnp.concatenate([(row >= col).astype(BF16), ones], axis=1)
    first = _iota2((tq, 128), 1) < SB_HEAD_DIM
    return later, from_here, first


def _sb_causal(d):
    tq, tk = SB_QUERY_ROWS, SB_BLOCK
    return _iota2((tq, tk), 1) + d * tk < _iota2((tq, tk), 0)


def _sums(x, mat):
    hi, lo = _split_bf16(x)
    return jnp.dot(hi, mat, preferred_element_type=F32) + jnp.dot(lo, mat, preferred_element_type=F32)


def _sb_weights(qs, kb, accs, later, causal):
    tk = SB_BLOCK
    z = [lax.dot_general(qh, kb, (((1,), (1,)), ((), ())), preferred_element_type=F32) for qh in qs]
    lk = [-_softplus(zh) for zh in z]
    if causal is not None:
        lk = [jnp.where(causal, v, 0.0) for v in lk]
    cs = [_sums(v, later) for v in lk]
    e = [z[h] + lk[h] + cs[h][:, :tk] + accs[h] for h in range(2)]
    if causal is not None:
        e = [jnp.where(causal, v, -1e30) for v in e]
    return lk, [jnp.exp(v) for v in e], [v[:, tk:] for v in cs]


def _sb_alive(accs):
    return jnp.max(jnp.maximum(accs[0], accs[1])) > SB_DEAD_LOG


def _sb_sweep(i, block, carry, accs_of, stop=None):
    per = SB_QUERY_ROWS // SB_BLOCK
    for d in reversed(range(per)):
        carry = block(i * per + d, carry, _sb_causal(d))
    if stop is not None:
        return lax.fori_loop(0, i * per - 1 - stop, lambda jj, cr: block(i * per - 1 - jj, cr, None), carry)

    def step(state):
        j, _, cr = state
        cr = block(j, cr, None)
        return j - 1, _sb_alive(accs_of(cr)), cr

    j, _, carry = lax.while_loop(lambda st: jnp.logical_and(st[0] >= 0, st[1]), step,
                                 (i * per - 1, _sb_alive(accs_of(carry)), carry))
    return carry, j


def _sb_fwd(p, *, name):
    s = p.shape[0]
    t, tk = SB_QUERY_ROWS, SB_BLOCK
    scale = SB_HEAD_DIM ** -0.5

    def body(q_ref, k_ref, v_ref, o_ref):
        i = pl.program_id(1)
        later, _, first = _sb_consts()
        q = q_ref[...] * scale
        qs = (jnp.where(first, q, 0.0).astype(BF16), jnp.where(first, 0.0, q).astype(BF16))

        def block(j, carry, causal):
            start = pl.multiple_of(j * tk, tk)
            kb = k_ref[pl.ds(start, tk), :].astype(BF16)
            vb = v_ref[pl.ds(start, tk), :].astype(BF16)
            _, w, tot = _sb_weights(qs, kb, [carry[h][1] for h in range(2)], later, causal)
            out = [jnp.dot(w[h].astype(BF16), vb, preferred_element_type=F32) for h in range(2)]
            return tuple((carry[h][0] + out[h], carry[h][1] + tot[h]) for h in range(2))

        zero = jnp.zeros((t, 128), F32)
        carry, _ = _sb_sweep(i, block, ((zero, zero), (zero, zero)), lambda cr: (cr[0][1], cr[1][1]))
        o_ref[...] = jnp.where(first, carry[0][0], carry[1][0])

    return pl.pallas_call(
        body, name=name, grid=(BRANCH_WIDTH // 128, s // t),
        in_specs=[pl.BlockSpec((t, 128), lambda pr, i: (i, OFF_CQ // 128 + pr)),
                  pl.BlockSpec((s, 128), lambda pr, i: (0, OFF_CK // 128 + pr)),
                  pl.BlockSpec((s, 128), lambda pr, i: (0, OFF_CV // 128 + pr))],
        out_specs=pl.BlockSpec((t, 128), lambda pr, i: (i, pr)),
        out_shape=jax.ShapeDtypeStruct((s, BRANCH_WIDTH), F32),
        compiler_params=_params(("arbitrary", "arbitrary")),
    )(p, p, p)


def _sb_bwd(p, do, *, name):
    s = p.shape[0]
    t, tk = SB_QUERY_ROWS, SB_BLOCK
    scale = SB_HEAD_DIM ** -0.5

    def body(q_ref, k_ref, v_ref, do_ref, dq_ref, dk_ref, dv_ref):
        i = pl.program_id(1)

        @pl.when(i == 0)
        def _():
            dk_ref[...] = jnp.zeros_like(dk_ref)
            dv_ref[...] = jnp.zeros_like(dv_ref)

        later, from_here, first = _sb_consts()
        q = q_ref[...] * scale
        do = do_ref[...]
        qs = (jnp.where(first, q, 0.0).astype(BF16), jnp.where(first, 0.0, q).astype(BF16))
        dos = (jnp.where(first, do, 0.0).astype(BF16), jnp.where(first, 0.0, do).astype(BF16))

        def total(j, carry, causal):
            start = pl.multiple_of(j * tk, tk)
            kb = k_ref[pl.ds(start, tk), :].astype(BF16)
            vb = v_ref[pl.ds(start, tk), :].astype(BF16)
            _, w, tot = _sb_weights(qs, kb, [carry[h][0] for h in range(2)], later, causal)
            dw = [lax.dot_general(dos[h], vb, (((1,), (1,)), ((), ())), preferred_element_type=F32) for h in range(2)]
            tde = [_sums(dw[h] * w[h], from_here)[:, tk:] for h in range(2)]
            return tuple((carry[h][0] + tot[h], carry[h][1] + tde[h]) for h in range(2))

        zero = jnp.zeros((t, 128), F32)
        sums, stop = _sb_sweep(i, total, ((zero, zero), (zero, zero)), lambda cr: (cr[0][0], cr[1][0]))
        deltas = (sums[0][1], sums[1][1])

        def block(j, carry, causal):
            start = pl.multiple_of(j * tk, tk)
            kb = k_ref[pl.ds(start, tk), :].astype(BF16)
            vb = v_ref[pl.ds(start, tk), :].astype(BF16)
            both = range(2)
            tn = (((0,), (0,)), ((), ()))
            lk, w, tot = _sb_weights(qs, kb, [carry[h][1] for h in both], later, causal)
            dw = [lax.dot_general(dos[h], vb, (((1,), (1,)), ((), ())), preferred_element_type=F32) for h in both]
            de = [dw[h] * w[h] for h in both]
            cs = [_sums(de[h], from_here) for h in both]
            keep = [jnp.exp(lk[h]) for h in both]
            dz = [de[h] * keep[h] - (deltas[h] - (cs[h][:, :tk] + carry[h][2])) * (1.0 - keep[h]) for h in both]
            if causal is not None:
                dz = [jnp.where(causal, v, 0.0) for v in dz]
            dzb = [v.astype(BF16) for v in dz]
            dq = [jnp.dot(dzb[h], kb, preferred_element_type=F32) for h in both]
            dk = [lax.dot_general(dzb[h], qs[h], tn, preferred_element_type=F32) for h in both]
            dv = [lax.dot_general(w[h].astype(BF16), dos[h], tn, preferred_element_type=F32) for h in both]
            dk_ref[pl.ds(start, tk), :] += dk[0] + dk[1]
            dv_ref[pl.ds(start, tk), :] += dv[0] + dv[1]
            return tuple((carry[h][0] + dq[h], carry[h][1] + tot[h], carry[h][2] + cs[h][:, tk:]) for h in both)

        carry = _sb_sweep(i, block, ((zero, zero, zero), (zero, zero, zero)), None, stop=stop)
        dq_ref[...] = jnp.where(first, carry[0][0], carry[1][0]) * scale

    qblk = lambda off: pl.BlockSpec((t, 128), lambda pr, i: (i, off // 128 + pr))
    full = lambda off: pl.BlockSpec((s, 128), lambda pr, i: (0, off // 128 + pr))
    res = pl.pallas_call(
        body, name=name, grid=(BRANCH_WIDTH // 128, s // t),
        in_specs=[qblk(OFF_CQ), full(OFF_CK), full(OFF_CV), qblk(0)],
        out_specs=[qblk(0), full(0), full(0)],
        out_shape=[jax.ShapeDtypeStruct((s, BRANCH_WIDTH), F32)] * 3,
        compiler_params=_params(("arbitrary", "arbitrary")),
    )(p, p, p, do)
    return list(res)


def _loss_head(y, target, *, name, ts=512):
    s, d = y.shape
    ts = min(ts, s)

    def body(y_ref, t_ref, sq_ref, dy_ref):
        @pl.when(pl.program_id(0) == 0)
        def _():
            sq_ref[...] = jnp.zeros_like(sq_ref)

        err = y_ref[...] - t_ref[...]
        dy_ref[...] = err * (1.0 / d)
        tot = jnp.sum(jnp.sum(err * err, axis=1, keepdims=True), axis=0, keepdims=True)
        sq_ref[...] += jnp.broadcast_to(tot, sq_ref.shape)

    blk = pl.BlockSpec((ts, d), lambda i: (i, 0))
    return pl.pallas_call(
        body, name=name, grid=(s // ts,), in_specs=[blk, blk],
        out_specs=[pl.BlockSpec((1, 128), lambda i: (0, 0)), blk],
        out_shape=[jax.ShapeDtypeStruct((1, 128), F32), jax.ShapeDtypeStruct((s, d), F32)],
        compiler_params=_params(("arbitrary",)),
    )(y, target)


def _row_tile(r, limit=512):
    return max(t for t in range(8, limit + 1, 8) if r % t == 0)


def _adamw(w, g, m, v, *, name):
    r = w.shape[0]
    ts = _row_tile(r)

    def body(w_ref, g_ref, m_ref, v_ref, d_ref, nm_ref, nv_ref):
        gv = g_ref[...]
        m_new = ADAM_B1 * m_ref[...] + (1.0 - ADAM_B1) * gv
        v_new = ADAM_B2 * v_ref[...] + (1.0 - ADAM_B2) * jnp.square(gv)
        m_hat = m_new / (1.0 - ADAM_B1 ** ADAM_STEP)
        v_hat = v_new / (1.0 - ADAM_B2 ** ADAM_STEP)
        d_ref[...] = -ADAM_LR * (m_hat / (jnp.sqrt(v_hat) + ADAM_EPS) + ADAM_WD * w_ref[...])
        nm_ref[...] = m_new
        nv_ref[...] = v_new

    blk = pl.BlockSpec((ts, LANES), lambda i: (i, 0))
    return pl.pallas_call(
        body, name=name, grid=(r // ts,), in_specs=[blk] * 4, out_specs=[blk] * 3,
        out_shape=[jax.ShapeDtypeStruct((r, LANES), F32)] * 3,
        compiler_params=_params(("parallel",)),
    )(w, g, m, v)


def _add_rows(terms, *, name, out_dtype=F32):
    r = terms[0].shape[0]
    ts = _row_tile(r)

    def body(*refs):
        acc = refs[0][...].astype(F32)
        for ref in refs[1:-1]:
            acc = acc + ref[...].astype(F32)
        refs[-1][...] = acc.astype(out_dtype)

    blk = pl.BlockSpec((ts, LANES), lambda i: (i, 0))
    return pl.pallas_call(
        body, name=name, grid=(r // ts,), in_specs=[blk] * len(terms), out_specs=blk,
        out_shape=jax.ShapeDtypeStruct((r, LANES), out_dtype), compiler_params=_params(("parallel",)),
    )(*terms)


def _col_sums(a, *, name, ts=256):
    s, n = a.shape
    ts = min(ts, s)

    def body(a_ref, o_ref):
        @pl.when(pl.program_id(0) == 0)
        def _():
            o_ref[...] = jnp.zeros_like(o_ref)

        o_ref[...] += jnp.sum(a_ref[...], axis=0, keepdims=True)

    return pl.pallas_call(
        body, name=name, grid=(s // ts,), in_specs=[pl.BlockSpec((ts, n), lambda i: (i, 0))],
        out_specs=pl.BlockSpec((1, n), lambda i: (0, 0)), out_shape=jax.ShapeDtypeStruct((1, n), F32),
        compiler_params=_params(("arbitrary",)),
    )(a)


_HBM = pl.BlockSpec(memory_space=pltpu.HBM)
_MESH = pl.DeviceIdType.MESH


def _other_chips(x, y):
    return [(1 - x, y), (x, 1 - y), (1 - x, 1 - y)]


def _gather_chips(shard, *, name):
    r, lanes = shard.shape
    half = r // 2
    assert half * 2 == r

    def body(in_ref, out_ref, send_sems, recv_sems, local_sem):
        x, y, c = lax.axis_index("x"), lax.axis_index("y"), lax.axis_index("c")
        me = 2 * x + y
        sibling = (x, y, 1 - c)
        chips = _other_chips(x, y)

        def copy(sem, chip, core_half, to):
            rows = out_ref.at[chip, pl.ds(core_half * half, half)]
            return pltpu.make_async_remote_copy(src_ref=rows, dst_ref=rows, send_sem=send_sems.at[sem],
                                                recv_sem=recv_sems.at[sem], device_id=to, device_id_type=_MESH)

        mine = pltpu.make_async_copy(in_ref, out_ref.at[me], local_sem)
        mine.start()
        first = []
        for kk, (px, py) in enumerate(chips):
            cp = pltpu.make_async_remote_copy(
                src_ref=in_ref.at[pl.ds(c * half, half)], dst_ref=out_ref.at[me, pl.ds(c * half, half)],
                send_sem=send_sems.at[kk], recv_sem=recv_sems.at[kk], device_id=(px, py, c), device_id_type=_MESH)
            cp.start()
            first.append(cp)
        passed = [copy(3 + kk, 2 * px + py, c, sibling) for kk, (px, py) in enumerate(chips)]
        for kk, (px, py) in enumerate(chips):
            copy(kk, 2 * px + py, c, (px, py, c)).wait_recv()
            passed[kk].start()
        for kk, (px, py) in enumerate(chips):
            copy(3 + kk, 2 * px + py, 1 - c, sibling).wait_recv()
        for cp in first + passed:
            cp.wait_send()
        mine.wait()

    return pl.pallas_call(
        body, name=name, in_specs=[_HBM], out_specs=_HBM,
        out_shape=jax.ShapeDtypeStruct((4, r, lanes), shard.dtype),
        scratch_shapes=[pltpu.SemaphoreType.DMA((6,)), pltpu.SemaphoreType.DMA((6,)), pltpu.SemaphoreType.DMA],
    )(shard)


def _to_sibling(block, *, name):
    def body(in_ref, out_ref, send_sem, recv_sem):
        x, y, c = lax.axis_index("x"), lax.axis_index("y"), lax.axis_index("c")
        cp = pltpu.make_async_remote_copy(src_ref=in_ref, dst_ref=out_ref, send_sem=send_sem, recv_sem=recv_sem,
                                          device_id=(x, y, 1 - c), device_id_type=_MESH)
        cp.start()
        cp.wait()

    return pl.pallas_call(
        body, name=name, in_specs=[_HBM], out_specs=_HBM, out_shape=jax.ShapeDtypeStruct(block.shape, block.dtype),
        scratch_shapes=[pltpu.SemaphoreType.DMA, pltpu.SemaphoreType.DMA],
    )(block)


def _join_halves(half, *, name):
    r, lanes = half.shape

    def body(in_ref, out_ref, send_sem, recv_sem, local_sem):
        x, y, c = lax.axis_index("x"), lax.axis_index("y"), lax.axis_index("c")
        mine = pltpu.make_async_copy(in_ref, out_ref.at[c], local_sem)
        mine.start()
        cp = pltpu.make_async_remote_copy(src_ref=in_ref, dst_ref=out_ref.at[c], send_sem=send_sem, recv_sem=recv_sem,
                                          device_id=(x, y, 1 - c), device_id_type=_MESH)
        cp.start()
        pltpu.make_async_remote_copy(src_ref=in_ref, dst_ref=out_ref.at[1 - c], send_sem=send_sem, recv_sem=recv_sem,
                                     device_id=(x, y, 1 - c), device_id_type=_MESH).wait_recv()
        cp.wait_send()
        mine.wait()

    return pl.pallas_call(
        body, name=name, in_specs=[_HBM], out_specs=_HBM, out_shape=jax.ShapeDtypeStruct((2, r, lanes), half.dtype),
        scratch_shapes=[pltpu.SemaphoreType.DMA, pltpu.SemaphoreType.DMA, pltpu.SemaphoreType.DMA],
    )(half)


def _scatter_chips(blocks, *, name):
    _, r, lanes = blocks.shape

    def body(in_ref, out_ref, send_sems, recv_sems, local_sem):
        x, y, c = lax.axis_index("x"), lax.axis_index("y"), lax.axis_index("c")
        me = 2 * x + y
        mine = pltpu.make_async_copy(in_ref.at[me], out_ref.at[me], local_sem)
        mine.start()
        copies = []
        for kk, (px, py) in enumerate(_other_chips(x, y)):
            cp = pltpu.make_async_remote_copy(src_ref=in_ref.at[2 * px + py], dst_ref=out_ref.at[me],
                                              send_sem=send_sems.at[kk], recv_sem=recv_sems.at[kk],
                                              device_id=(px, py, c), device_id_type=_MESH)
            cp.start()
            copies.append(cp)
        for kk, (px, py) in enumerate(_other_chips(x, y)):
            pltpu.make_async_remote_copy(src_ref=in_ref.at[me], dst_ref=out_ref.at[2 * px + py],
                                         send_sem=send_sems.at[kk], recv_sem=recv_sems.at[kk], device_id=(px, py, c),
                                         device_id_type=_MESH).wait_recv()
        for cp in copies:
            cp.wait_send()
        mine.wait()

    return pl.pallas_call(
        body, name=name, in_specs=[_HBM], out_specs=_HBM, out_shape=jax.ShapeDtypeStruct((4, r, lanes), blocks.dtype),
        scratch_shapes=[pltpu.SemaphoreType.DMA((3,)), pltpu.SemaphoreType.DMA((3,)), pltpu.SemaphoreType.DMA],
    )(blocks)


SHARDED = (("norm_g", 2), ("w_in", 2), ("conv_w", 2), ("w_branch", 3), ("w_out", 1), ("w_ff1", 2), ("w_ff2", 1))
MATMUL_WEIGHTS = ("w_in", "w_branch", "w_out", "w_ff1", "w_ff2")
VECTOR_WEIGHTS = ("norm_g", "conv_w")
REPLICATED = ("b_in", "sgu_ln_g", "sgu_ln_b", "w_spatial", "b_spatial", "a_log", "dt_bias", "dn_norm_g")
WEIGHT_ORDER = ("norm_g", "w_in", "b_in", "sgu_ln_g", "sgu_ln_b", "w_spatial", "b_spatial", "conv_w", "a_log",
                "dt_bias", "dn_norm_g", "w_branch", "w_out", "w_ff1", "w_ff2")
PACK_ROW_MULTIPLE = 32


def _rows_of(shape):
    n = 1
    for dim in shape:
        n *= dim
    return -(-n // LANES)


def _pack(arrays):
    parts = []
    for a in arrays:
        flat = a.reshape(-1)
        pad = _rows_of(a.shape) * LANES - flat.shape[0]
        if pad:
            flat = jnp.concatenate([flat, jnp.zeros((pad,), flat.dtype)])
        parts.append(flat.reshape(-1, LANES))
    rows = sum(p.shape[0] for p in parts)
    pad = -rows % PACK_ROW_MULTIPLE
    if pad:
        parts.append(jnp.zeros((pad, LANES), parts[0].dtype))
    return jnp.concatenate(parts, axis=0)


def _unpack(buf, shapes):
    out, row = [], 0
    for shape in shapes:
        n = 1
        for dim in shape:
            n *= dim
        rows = _rows_of(shape)
        out.append(buf[row:row + rows].reshape(-1)[:n].reshape(shape))
        row += rows
    return out


def _chip_slice(a, axis, k):
    size = a.shape[axis] // 4
    return lax.slice_in_dim(a, k * size, (k + 1) * size, axis=axis)


def _rearrange_w_in(w):
    pad = jnp.zeros(w.shape[:-1] + (P_PAD - P_IN,), w.dtype)
    return jnp.concatenate([w[..., 0:3072], w[..., 4616:7688], w[..., 3080:4616], w[..., 3072:3080], pad], axis=-1)


def _restore_w_in(w):
    return jnp.concatenate([w[..., 0:3072], w[..., 7680:7688], w[..., 6144:7680], w[..., 3072:6144]], axis=-1)


def _shift_rows(a, n):
    if n == 0:
        return a
    return jnp.concatenate([jnp.zeros((n, a.shape[1]), a.dtype), a[:-n]], axis=0)


def _unshift_rows(a, n):
    if n == 0:
        return a
    return jnp.concatenate([a[n:], jnp.zeros((n, a.shape[1]), a.dtype)], axis=0)


def _layer_params(wl):
    row = lambda v: v.reshape(1, -1)
    pad128 = lambda v, at: jnp.pad(v, (at, 128 - at - v.shape[0])).reshape(1, 128)
    return dict(
        g=[row(wl["norm_g"][i]) for i in range(4)],
        gmlp=[row(wl["sgu_ln_g"]), row(wl["sgu_ln_b"]), wl["w_spatial"],
              jnp.pad(wl["b_spatial"].T, ((0, 0), (0, 128 - GM_GROUPS)))],
        dn_in=[wl["conv_w"], pad128(wl["a_log"], DN_HEADS), pad128(wl["dt_bias"], DN_HEADS)],
        dn_g=[row(wl["dn_norm_g"])],
    )


def _layer_fwd(x0, wl, l):
    tag = lambda s: f"{s}_l{l}"
    pr = _layer_params(wl)
    w = BRANCH_WIDTH
    h0 = _row_fwd(_fn_rms, [(x0, 0, D_MODEL)], [pr["g"][0]], [(D_MODEL, BF16)], ts=512, name=tag("rms0"))[0]
    p = _mm(h0, wl["w_in"], bias=wl["b_in"].reshape(1, -1), name=tag("proj_in"))
    ya = _row_fwd(_fn_gmlp, [(p, OFF_AU, w), (p, OFF_AV, w)], pr["gmlp"], [(w, BF16)], ts=256, name=tag("gmlp"))[0]
    xq = p[:, OFF_BQ:OFF_BQ + 3 * w]
    shifted = [_shift_rows(xq, 3 - j) for j in range(4)]
    dn_rows = [(a, 0, 3 * w) for a in shifted] + [(p, OFF_BD, 128)]
    q, k, v, beta, g = _row_fwd(_fn_dn_in, dn_rows, pr["dn_in"], [(w, F32)] * 5, ts=256, name=tag("dn_in"))
    o, states = _delta_fwd(q, k, v, beta, g, name=tag("delta"))
    yb = _row_fwd(_fn_dn_out, [(o, 0, w), (p, OFF_BZ, w)], pr["dn_g"], [(w, BF16)], ts=512, name=tag("dn_out"))[0]
    yc = _sb_fwd(p, name=tag("sb"))
    ys = [ya, yb, yc]
    proj = [_mm(ys[i], wl["w_branch"][i], name=tag(f"branch{i}")) for i in range(3)]
    merge_rows = [(a, 0, D_MODEL) for a in proj] + [(p, OFF_GATE + i * D_MODEL, D_MODEL) for i in range(3)]
    m = _row_fwd(_fn_merge, merge_rows, [], [(D_MODEL, BF16)], ts=256, name=tag("merge"))[0]
    mixed = _mm(m, wl["w_out"], name=tag("out"))
    x1 = _row_fwd(_fn_resid_rms, [(x0, 0, D_MODEL), (mixed, 0, D_MODEL)], [pr["g"][1]], [(D_MODEL, F32)], ts=512,
                  name=tag("resid1"))[0]
    h2 = _row_fwd(_fn_rms, [(x1, 0, D_MODEL)], [pr["g"][2]], [(D_MODEL, BF16)], ts=512, name=tag("rms2"))[0]
    a = _mm(h2, wl["w_ff1"], name=tag("ff1"))
    r = _row_fwd(_fn_relu2, [(a, 0, D_FF)], [], [(D_FF, BF16)], ts=256, name=tag("relu2"))[0]
    f = _mm(r, wl["w_ff2"], name=tag("ff2"))
    x2 = _row_fwd(_fn_resid_rms, [(x1, 0, D_MODEL), (f, 0, D_MODEL)], [pr["g"][3]], [(D_MODEL, F32)], ts=512,
                  name=tag("resid2"))[0]
    saved = dict(x0=x0, h0=h0, p=p, shifted=shifted, q=q, k=k, v=v, beta=beta, g=g, states=states, o=o, ys=ys,
                 proj=proj, m=m, mixed=mixed, x1=x1, h2=h2, a=a, r=r, f=f)
    return x2, saved


def _layer_bwd(dx2, sv, wl, l):
    tag = lambda s: f"{s}_l{l}"
    pr = _layer_params(wl)
    w = BRANCH_WIDTH
    full = lambda a: (a, 0, a.shape[1])
    p = sv["p"]
    (df,), (dg3,) = _row_bwd(_fn_rms_branch, [full(sv["f"])], [pr["g"][3]], [full(dx2)], ts=512, name=tag("resid2_b"))
    dr = _mm(df, wl["w_ff2"], trans_b=True, name=tag("ff2_dx"))
    dw_ff2 = _mm_tn(sv["r"], df, name=tag("ff2_dw"))
    (da,), _ = _row_bwd(_fn_relu2, [full(sv["a"])], [], [full(dr)], ts=256, name=tag("relu2_b"))
    dh2 = _mm(da, wl["w_ff1"], trans_b=True, name=tag("ff1_dx"))
    dw_ff1 = _mm_tn(sv["h2"], da, name=tag("ff1_dw"))
    (dx1,), (dg2,) = _row_bwd(_fn_rms_keep, [full(sv["x1"])], [pr["g"][2]], [full(dh2), full(dx2)], ts=512,
                              name=tag("rms2_b"))
    (dmixed,), (dg1,) = _row_bwd(_fn_rms_branch, [full(sv["mixed"])], [pr["g"][1]], [full(dx1)], ts=512,
                                 name=tag("resid1_b"))
    dm = _mm(dmixed, wl["w_out"], trans_b=True, name=tag("out_dx"))
    dw_out = _mm_tn(sv["m"], dmixed, name=tag("out_dw"))
    merge_rows = [full(a) for a in sv["proj"]] + [(p, OFF_GATE + i * D_MODEL, D_MODEL) for i in range(3)]
    dmerge, _ = _row_bwd(_fn_merge, merge_rows, [], [full(dm)], ts=256, name=tag("merge_b"))
    dproj, dgates = dmerge[:3], dmerge[3:]
    dys = [_mm(dproj[i], wl["w_branch"][i], trans_b=True, name=tag(f"branch{i}_dx")) for i in range(3)]
    dw_branch = jnp.stack([_mm_tn(sv["ys"][i], dproj[i], name=tag(f"branch{i}_dw")) for i in range(3)])
    (du, dv_a), dgm = _row_bwd(_fn_gmlp, [(p, OFF_AU, w), (p, OFF_AV, w)], pr["gmlp"], [full(dys[0])], ts=256,
                               name=tag("gmlp_b"))
    (do, dz), (d_dn_g,) = _row_bwd(_fn_dn_out, [full(sv["o"]), (p, OFF_BZ, w)], pr["dn_g"], [full(dys[1])], ts=512,
                                   name=tag("dn_out_b"))
    dqkvbg = _delta_bwd(sv["q"], sv["k"], sv["v"], sv["beta"], sv["g"], sv["states"], do, name=tag("delta_b"))
    dn_rows = [full(a) for a in sv["shifted"]] + [(p, OFF_BD, 128)]
    d_in, d_dn_in = _row_bwd(_fn_dn_in, dn_rows, pr["dn_in"], [full(a) for a in dqkvbg], ts=256, name=tag("dn_in_b"))
    dxq = sum(_unshift_rows(d_in[j], 3 - j) for j in range(4))
    dcq, dck, dcv = _sb_bwd(p, dys[2], name=tag("sb_b"))
    s = p.shape[0]
    dp = jnp.concatenate([du, dv_a, dxq, dz] + dgates + [dcq, dck, dcv, d_in[4],
                                                         jnp.zeros((s, P_PAD - OFF_BD - 128), F32)], axis=1)
    dh0 = _mm(dp, wl["w_in"], trans_b=True, name=tag("proj_in_dx"))
    dw_in = _mm_tn(sv["h0"], dp, name=tag("proj_in_dw"))
    db_in = _col_sums(dp, name=tag("bias_b"))
    (dx0,), (dg0,) = _row_bwd(_fn_rms_keep, [full(sv["x0"])], [pr["g"][0]], [full(dh0), full(dx1)], ts=512,
                              name=tag("rms0_b"))
    grads = dict(
        norm_g=jnp.concatenate([dg0, dg1, dg2, dg3], axis=0), w_in=dw_in, b_in=db_in.reshape(-1),
        sgu_ln_g=dgm[0].reshape(-1), sgu_ln_b=dgm[1].reshape(-1), w_spatial=dgm[2],
        b_spatial=dgm[3][:, :GM_GROUPS].T, conv_w=d_dn_in[0], a_log=d_dn_in[1][0, DN_HEADS:2 * DN_HEADS],
        dt_bias=d_dn_in[2][0, DN_HEADS:2 * DN_HEADS], dn_norm_g=d_dn_g.reshape(-1), w_branch=dw_branch,
        w_out=dw_out, w_ff1=dw_ff1, w_ff2=dw_ff2)
    return dx0, grads


def _local_step(x, target, weights):
    saved = []
    h = x
    layers = []
    for l in range(DEPTH):
        wl = {n: weights[n][l] for n in WEIGHT_ORDER}
        layers.append(wl)
        h, sv = _layer_fwd(h, wl, l)
        saved.append(sv)
    sq, dh = _loss_head(h, target, name="loss_head")
    grads = [None] * DEPTH
    for l in reversed(range(DEPTH)):
        dh, grads[l] = _layer_bwd(dh, saved[l], layers[l], l)
    stacked = {n: jnp.stack([grads[l][n] for l in range(DEPTH)]) for n in WEIGHT_ORDER}
    return sq[0, 0], dh, stacked


def kernel(x, norm_g, w_in, b_in, sgu_ln_g, sgu_ln_b, w_spatial, b_spatial, conv_w, a_log, dt_bias, dn_norm_g, w_branch, w_out, w_ff1, w_ff2, loss_target, m_norm_g, m_w_in, m_b_in, m_sgu_ln_g, m_sgu_ln_b, m_w_spatial, m_b_spatial, m_conv_w, m_a_log, m_dt_bias, m_dn_norm_g, m_w_branch, m_w_out, m_w_ff1, m_w_ff2, v_norm_g, v_w_in, v_b_in, v_sgu_ln_g, v_sgu_ln_b, v_w_spatial, v_b_spatial, v_conv_w, v_a_log, v_dt_bias, v_dn_norm_g, v_w_branch, v_w_out, v_w_ff1, v_w_ff2):
    local = dict(norm_g=norm_g, w_in=w_in, b_in=b_in, sgu_ln_g=sgu_ln_g, sgu_ln_b=sgu_ln_b, w_spatial=w_spatial,
                 b_spatial=b_spatial, conv_w=conv_w, a_log=a_log, dt_bias=dt_bias, dn_norm_g=dn_norm_g,
                 w_branch=w_branch, w_out=w_out, w_ff1=w_ff1, w_ff2=w_ff2)
    mom1 = dict(norm_g=m_norm_g, w_in=m_w_in, b_in=m_b_in, sgu_ln_g=m_sgu_ln_g, sgu_ln_b=m_sgu_ln_b,
                w_spatial=m_w_spatial, b_spatial=m_b_spatial, conv_w=m_conv_w, a_log=m_a_log, dt_bias=m_dt_bias,
                dn_norm_g=m_dn_norm_g, w_branch=m_w_branch, w_out=m_w_out, w_ff1=m_w_ff1, w_ff2=m_w_ff2)
    mom2 = dict(norm_g=v_norm_g, w_in=v_w_in, b_in=v_b_in, sgu_ln_g=v_sgu_ln_g, sgu_ln_b=v_sgu_ln_b,
                w_spatial=v_w_spatial, b_spatial=v_b_spatial, conv_w=v_conv_w, a_log=v_a_log, dt_bias=v_dt_bias,
                dn_norm_g=v_dn_norm_g, w_branch=v_w_branch, w_out=v_w_out, w_ff1=v_w_ff1, w_ff2=v_w_ff2)
    shard_names = [n for n, _ in SHARDED]
    shard_shapes = [local[n].shape for n in shard_names]
    repl_shapes = [local[n].shape for n in REPLICATED]

    weights = {n: local[n] for n in REPLICATED}
    for names, dtype, call in ((MATMUL_WEIGHTS, BF16, "gather_matmul_weights"), (VECTOR_WEIGHTS, F32, "gather_vectors")):
        gathered = _gather_chips(_pack([local[n] for n in names]).astype(dtype), name=call)
        per_chip = [_unpack(gathered[k], [local[n].shape for n in names]) for k in range(4)]
        for i, n in enumerate(names):
            weights[n] = jnp.concatenate([per_chip[k][i] for k in range(4)], axis=dict(SHARDED)[n])
    weights["w_in"] = _rearrange_w_in(weights["w_in"])
    weights["b_in"] = _rearrange_w_in(weights["b_in"])

    sq, dx, grads = _local_step(x[0], loss_target[0], weights)
    loss = lax.psum(0.5 * sq / D_MODEL, ("x", "y", "c"))
    grads["w_in"] = _restore_w_in(grads["w_in"])
    grads["b_in"] = _restore_w_in(grads["b_in"])

    blocks = jnp.stack([_pack([_chip_slice(grads[n], axis, k) for n, axis in SHARDED] + [grads[n] for n in REPLICATED])
                        for k in range(4)])
    rows = blocks.shape[1]
    half = rows // 2
    c = lax.axis_index("c")
    halves = blocks.reshape(4, 2, half, LANES)
    keep = lax.dynamic_index_in_dim(halves, c, axis=1, keepdims=False)
    give = lax.dynamic_index_in_dim(halves, 1 - c, axis=1, keepdims=False)
    got = _to_sibling(give, name="grads_to_sibling")
    chip_sum = _add_rows([keep.reshape(4 * half, LANES), got.reshape(4 * half, LANES)], name="grads_chip_sum",
                         out_dtype=BF16)
    by_chip = _scatter_chips(chip_sum.reshape(4, half, LANES), name="grads_scatter")
    my_half = _add_rows([by_chip[k] for k in range(4)], name="grads_sum")
    total = _join_halves(my_half, name="grads_join_halves").reshape(rows, LANES)

    pack_local = lambda d: _pack([d[n] for n in shard_names] + [d[n] for n in REPLICATED])
    delta, new_m, new_v = _adamw(pack_local(local), total, pack_local(mom1), pack_local(mom2), name="adamw")
    unpack = lambda buf: dict(zip(shard_names + list(REPLICATED), _unpack(buf, shard_shapes + repl_shapes)))
    g_out, d_out, m_out, v_out = unpack(total), unpack(delta), unpack(new_m), unpack(new_v)
    return (loss, dx[None], *[g_out[n] for n in WEIGHT_ORDER], *[d_out[n] for n in WEIGHT_ORDER],
            *[m_out[n] for n in WEIGHT_ORDER], *[v_out[n] for n in WEIGHT_ORDER])
```

```python
import functools

import jax
import jax.numpy as jnp
from jax import lax
from jax.experimental import pallas as pl
from jax.experimental.pallas import tpu as pltpu

F32 = jnp.float32
BF16 = jnp.bfloat16

D_MODEL = 1024
DEPTH = 4
BRANCH_WIDTH = 512
GM_CHUNK = 128
GM_GROUPS = 8
DN_HEADS = 4
DN_HEAD_DIM = 128
DN_BLOCK = 128
DN_SOLVE_PASSES, DN_STATE_PASSES, DN_OUT_PASSES = 3, 1, 1
SB_HEAD_DIM = 64
SB_BLOCK = 128
SB_QUERY_ROWS = 256
SB_DEAD_LOG = -105.0
D_FF = 4096
P_IN = 7688
P_PAD = 8192
NORM_EPS = 1e-6
ADAM_LR, ADAM_B1, ADAM_B2, ADAM_EPS, ADAM_WD, ADAM_STEP = 0.001, 0.9, 0.999, 1e-08, 0.01, 10

OFF_AU, OFF_AV, OFF_BQ, OFF_BK, OFF_BV, OFF_BZ = 0, 512, 1024, 1536, 2048, 2560
OFF_GATE = 3072
OFF_CQ, OFF_CK, OFF_CV = 6144, 6656, 7168
OFF_BD = 7680

LANES = 1024
VMEM_LIMIT_BYTES = 56 * 1024 * 1024

_HI = lax.Precision.HIGHEST


def _params(sem):
    return pltpu.CompilerParams(dimension_semantics=sem, vmem_limit_bytes=VMEM_LIMIT_BYTES)


def _mm(a, b, *, name, out_dtype=F32, bias=None, trans_b=False, tm=1024, tn=1024, tk=1024):
    m, k = a.shape
    n = b.shape[0] if trans_b else b.shape[1]
    tm, tn, tk = min(tm, m), min(tn, n), min(tk, k)
    assert m % tm == 0 and n % tn == 0 and k % tk == 0, (a.shape, b.shape)
    nk = k // tk
    dn = (((1,), (1,)), ((), ())) if trans_b else (((1,), (0,)), ((), ()))

    def body(*refs):
        if bias is None:
            a_ref, b_ref, o_ref, acc = refs
        else:
            a_ref, b_ref, bias_ref, o_ref, acc = refs
        kk = pl.program_id(2)
        part = lax.dot_general(a_ref[...].astype(BF16), b_ref[...].astype(BF16), dn, preferred_element_type=F32)

        @pl.when(kk == 0)
        def _():
            acc[...] = part

        @pl.when(kk > 0)
        def _():
            acc[...] += part

        @pl.when(kk == nk - 1)
        def _():
            r = acc[...]
            if bias is not None:
                r = r + bias_ref[...]
            o_ref[...] = r.astype(out_dtype)

    in_specs = [pl.BlockSpec((tm, tk), lambda i, j, kk: (i, kk))]
    if trans_b:
        in_specs.append(pl.BlockSpec((tn, tk), lambda i, j, kk: (j, kk)))
    else:
        in_specs.append(pl.BlockSpec((tk, tn), lambda i, j, kk: (kk, j)))
    args = [a, b]
    if bias is not None:
        in_specs.append(pl.BlockSpec((1, tn), lambda i, j, kk: (0, j)))
        args.append(bias)
    return pl.pallas_call(
        body, name=name, grid=(m // tm, n // tn, nk),
        in_specs=in_specs, out_specs=pl.BlockSpec((tm, tn), lambda i, j, kk: (i, j)),
        out_shape=jax.ShapeDtypeStruct((m, n), out_dtype),
        scratch_shapes=[pltpu.VMEM((tm, tn), F32)],
        compiler_params=_params(("parallel", "parallel", "arbitrary")),
    )(*args)


def _mm_tn(a, b, *, name, tm=1024, tn=1024, ts=1024):
    s, ka = a.shape
    n = b.shape[1]
    tm, tn, ts = min(tm, ka), min(tn, n), min(ts, s)
    assert ka % tm == 0 and n % tn == 0 and s % ts == 0, (a.shape, b.shape)

    def body(a_ref, b_ref, o_ref):
        part = lax.dot_general(a_ref[...].astype(BF16), b_ref[...].astype(BF16), (((0,), (0,)), ((), ())),
                               preferred_element_type=F32)

        @pl.when(pl.program_id(2) == 0)
        def _():
            o_ref[...] = part

        @pl.when(pl.program_id(2) > 0)
        def _():
            o_ref[...] += part

    return pl.pallas_call(
        body, name=name, grid=(ka // tm, n // tn, s // ts),
        in_specs=[pl.BlockSpec((ts, tm), lambda i, j, r: (r, i)), pl.BlockSpec((ts, tn), lambda i, j, r: (r, j))],
        out_specs=pl.BlockSpec((tm, tn), lambda i, j, r: (i, j)),
        out_shape=jax.ShapeDtypeStruct((ka, n), F32),
        compiler_params=_params(("parallel", "parallel", "arbitrary")),
    )(a, b)


def _col_block(i, *, c):
    return (i, c)


def _whole(i, *, nd):
    return (0,) * nd


def _row_specs(rows, ts):
    specs = []
    for arr, off, w in rows:
        assert off % w == 0 and arr.shape[0] % ts == 0
        specs.append(pl.BlockSpec((ts, w), functools.partial(_col_block, c=off // w)))
    return specs


def _row_fwd(fn, rows, params, outs, *, ts, name):
    s = rows[0][0].shape[0]
    ts = min(ts, s)
    nr, npar = len(rows), len(params)

    def body(*refs):
        rv = [r[...].astype(F32) for r in refs[:nr]]
        pv = [p[...] for p in refs[nr:nr + npar]]
        for o_ref, val in zip(refs[nr + npar:], fn(pv, rv)):
            o_ref[...] = val.astype(o_ref.dtype)

    in_specs = _row_specs(rows, ts) + [pl.BlockSpec(p.shape, functools.partial(_whole, nd=p.ndim)) for p in params]
    res = pl.pallas_call(
        body, name=name, grid=(s // ts,), in_specs=in_specs,
        out_specs=[pl.BlockSpec((ts, w), lambda i: (i, 0)) for w, _ in outs],
        out_shape=[jax.ShapeDtypeStruct((s, w), dt) for w, dt in outs],
        compiler_params=_params(("parallel",)),
    )(*[r[0] for r in rows], *params)
    return list(res)


def _row_bwd(fn, rows, params, cts, *, ts, name, row_grads=True):
    s = rows[0][0].shape[0]
    ts = min(ts, s)
    nr, npar, nc = len(rows), len(params), len(cts)
    n_dr = nr if row_grads else 0

    def body(*refs):
        rv = [r[...].astype(F32) for r in refs[:nr]]
        pv = [p[...] for p in refs[nr:nr + npar]]
        cv = [c[...].astype(F32) for c in refs[nr + npar:nr + npar + nc]]
        out_refs = refs[nr + npar + nc:]
        _, vjp = jax.vjp(lambda p, r: tuple(fn(p, r)), pv, rv)
        dp, dr = vjp(tuple(cv))
        for o_ref, val in zip(out_refs[:n_dr], dr):
            o_ref[...] = val

        @pl.when(pl.program_id(0) == 0)
        def _():
            for o_ref in out_refs[n_dr:]:
                o_ref[...] = jnp.zeros_like(o_ref)

        for o_ref, val in zip(out_refs[n_dr:], dp):
            o_ref[...] += val

    in_specs = (_row_specs(rows, ts) + [pl.BlockSpec(p.shape, functools.partial(_whole, nd=p.ndim)) for p in params]
                + _row_specs(cts, ts))
    out_specs = [pl.BlockSpec((ts, w), lambda i: (i, 0)) for _, _, w in rows[:n_dr]]
    out_specs += [pl.BlockSpec(p.shape, functools.partial(_whole, nd=p.ndim)) for p in params]
    out_shape = [jax.ShapeDtypeStruct((s, w), F32) for _, _, w in rows[:n_dr]]
    out_shape += [jax.ShapeDtypeStruct(p.shape, F32) for p in params]
    res = pl.pallas_call(
        body, name=name, grid=(s // ts,), in_specs=in_specs, out_specs=out_specs, out_shape=out_shape,
        compiler_params=_params(("arbitrary",)),
    )(*[r[0] for r in rows], *params, *[c[0] for c in cts])
    res = list(res)
    return res[:n_dr], res[n_dr:]


def _rms(x, g):
    return x * lax.rsqrt(jnp.mean(x * x, axis=-1, keepdims=True) + NORM_EPS) * g


def _gelu(x):
    return 0.5 * x * (1.0 + lax.erf(x * (2.0 ** -0.5)))


def _softplus(x):
    return jnp.maximum(x, 0.0) + jnp.log1p(jnp.exp(-jnp.abs(x)))


def _iota2(shape, dim):
    return lax.broadcasted_iota(jnp.int32, shape, dim)


def _fn_rms(pv, rv):
    return [_rms(rv[0], pv[0])]


def _fn_rms_keep(pv, rv):
    return [_rms(rv[0], pv[0]), rv[0]]


def _fn_resid_rms(pv, rv):
    return [rv[0] + _rms(rv[1], pv[0])]


def _fn_rms_branch(pv, rv):
    return [_rms(rv[0], pv[0])]


def _fn_relu2(pv, rv):
    return [jnp.square(jnp.maximum(rv[0], 0.0))]


def _fn_merge(pv, rv):
    return [sum(jax.nn.sigmoid(rv[3 + i]) * rv[i] for i in range(3))]


def _fn_gmlp(pv, rv):
    ln_g, ln_b, w_sp, b_t = pv
    u = _gelu(rv[0])
    v = _gelu(rv[1])
    vc = v - jnp.mean(v, axis=-1, keepdims=True)
    v = vc * lax.rsqrt(jnp.mean(vc * vc, axis=-1, keepdims=True) + NORM_EPS) * ln_g + ln_b
    t = GM_CHUNK
    causal = _iota2((t, t), 1) <= _iota2((t, t), 0)
    first = _iota2((t, 128), 1) < 64
    expand = (_iota2((128, BRANCH_WIDTH), 0) == _iota2((128, BRANCH_WIDTH), 1) // 64).astype(F32)
    b_full = jnp.dot(b_t, expand, precision=_HI, preferred_element_type=F32)
    w_bf = [jnp.where(causal, w_sp[g], 0.0).astype(BF16) for g in range(GM_GROUPS)]
    chunks = []
    for c in range(rv[0].shape[0] // t):
        pairs = []
        for p in range(GM_GROUPS // 2):
            vp = v[c * t:(c + 1) * t, 128 * p:128 * (p + 1)].astype(BF16)
            m0 = jnp.dot(w_bf[2 * p], vp, preferred_element_type=F32)
            m1 = jnp.dot(w_bf[2 * p + 1], vp, preferred_element_type=F32)
            pairs.append(jnp.where(first, m0, m1))
        chunks.append(jnp.concatenate(pairs, axis=1) + b_full)
    return [u * jnp.concatenate(chunks, axis=0)]


def _head_expand(col0):
    return (_iota2((128, BRANCH_WIDTH), 0) == _iota2((128, BRANCH_WIDTH), 1) // DN_HEAD_DIM + col0).astype(F32)


def _fn_dn_in(pv, rv):
    conv_w, a_log, dt_b = pv
    c = sum(conv_w[j:j + 1, :] * rv[j] for j in range(4))
    a = c * jax.nn.sigmoid(c)
    outs = []
    for part in range(3):
        heads = []
        for h in range(DN_HEADS):
            lo = part * BRANCH_WIDTH + h * DN_HEAD_DIM
            xh = a[:, lo:lo + DN_HEAD_DIM]
            if part < 2:
                xh = xh * lax.rsqrt(jnp.sum(xh * xh, axis=-1, keepdims=True) + NORM_EPS)
            heads.append(xh)
        outs.append(jnp.concatenate(heads, axis=1))
    bd = rv[4]
    beta = jax.nn.sigmoid(bd)
    g = -jnp.exp(a_log) * _softplus(bd + dt_b)
    outs.append(jnp.dot(beta, _head_expand(0), precision=_HI, preferred_element_type=F32))
    outs.append(jnp.dot(g, _head_expand(DN_HEADS), precision=_HI, preferred_element_type=F32))
    return outs


def _fn_dn_out(pv, rv):
    heads = []
    for h in range(DN_HEADS):
        sl = slice(h * DN_HEAD_DIM, (h + 1) * DN_HEAD_DIM)
        z = rv[1][:, sl]
        heads.append(_rms(rv[0][:, sl], pv[0]) * (z * jax.nn.sigmoid(z)))
    return [jnp.concatenate(heads, axis=1)]


_DIMS = {"nn": (((1,), (0,)), ((), ())), "nt": (((1,), (1,)), ((), ())), "tn": (((0,), (0,)), ((), ()))}
_DIMS_BWD = {"nn": (("nt", "c", "b"), ("tn", "a", "c")), "nt": (("nn", "c", "b"), ("tn", "c", "a")),
             "tn": (("nt", "b", "c"), ("nn", "a", "c"))}


def _bf16_dot(a, b, kind):
    return lax.dot_general(a.astype(BF16), b.astype(BF16), _DIMS[kind], preferred_element_type=F32)


def _pdot_raw(a, b, kind, mode):
    if mode == 1:
        return _bf16_dot(a, b, kind)
    if mode == 6:
        return lax.dot_general(a, b, _DIMS[kind], precision=_HI, preferred_element_type=F32)
    b_hi, b_lo = _split_bf16(b)
    if mode == 3:
        a_hi, a_lo = _split_bf16(a)
        return _bf16_dot(a_hi, b_hi, kind) + (_bf16_dot(a_hi, b_lo, kind) + _bf16_dot(a_lo, b_hi, kind))
    b_rest = (b - b_hi.astype(F32) - b_lo.astype(F32)).astype(BF16)
    return _bf16_dot(a, b_hi, kind) + (_bf16_dot(a, b_lo, kind) + _bf16_dot(a, b_rest, kind))


@functools.partial(jax.custom_vjp, nondiff_argnums=(2, 3))
def _pdot(a, b, kind, mode):
    return _pdot_raw(a, b, kind, mode)


def _pdot_fwd(a, b, kind, mode):
    return _pdot_raw(a, b, kind, mode), (a, b)


def _pdot_bwd(kind, mode, res, ct):
    ops = {"a": res[0], "b": res[1], "c": ct}
    (ka, a1, a2), (kb, b1, b2) = _DIMS_BWD[kind]
    if mode == "count":
        return jnp.zeros_like(res[0]), _pdot(ops[b1], ops[b2], kb, mode)
    return _pdot(ops[a1], ops[a2], ka, mode), _pdot(ops[b1], ops[b2], kb, mode)


_pdot.defvjp(_pdot_fwd, _pdot_bwd)


@jax.custom_vjp
def _unit_lower_inverses(mats):
    c = mats[0].shape[0]
    row, col = _iota2((c, c), 0), _iota2((c, c), 1)
    x = [(row == col).astype(F32) for _ in mats]
    shift = 0
    while (1 << shift) < c:
        pair = jnp.right_shift(row, shift + 1) == jnp.right_shift(col, shift + 1)
        between = pair & (jnp.right_shift(row, shift) != jnp.right_shift(col, shift))
        q = [jnp.where(between, a, 0.0) for a in mats]
        qd = [_pdot_raw(qi, xi, "nn", DN_SOLVE_PASSES) for qi, xi in zip(q, x)]
        x = [xi - _pdot_raw(xi, m, "nn", DN_SOLVE_PASSES) for xi, m in zip(x, qd)]
        shift += 1
    return tuple(x)


def _unit_lower_inverses_fwd(mats):
    x = _unit_lower_inverses(mats)
    return x, x


def _unit_lower_inverses_bwd(x, dx):
    inner = [_pdot_raw(d, xi, "nt", DN_SOLVE_PASSES) for d, xi in zip(dx, x)]
    return (tuple(-_pdot_raw(xi, m, "tn", DN_SOLVE_PASSES) for xi, m in zip(x, inner)),)


_unit_lower_inverses.defvjp(_unit_lower_inverses_fwd, _unit_lower_inverses_bwd)


def _head(h):
    return slice(h * DN_HEAD_DIM, (h + 1) * DN_HEAD_DIM)


def _delta_chunk(states, q, k, v, beta, g):
    c = DN_BLOCK
    heads = range(DN_HEADS)
    row, col = _iota2((c, c), 0), _iota2((c, c), 1)
    tri, strict = col <= row, col < row
    counts = jnp.concatenate([tri.astype(BF16), jnp.ones((c, c), BF16)], axis=0)
    sums = _pdot(counts, g, "nn", "count")
    gc = [sums[:c, _head(h)] for h in heads]
    gl = [sums[c:, _head(h)] for h in heads]
    qh, kh, vh, bh = ([t[:, _head(h)] for h in heads] for t in (q * (DN_HEAD_DIM ** -0.5), k, v, beta))
    decay = [jnp.where(tri, jnp.exp(jnp.where(tri, gc[h] - gc[h].T, 0.0)), 0.0) for h in heads]
    solve, carry, out = DN_SOLVE_PASSES, DN_STATE_PASSES, DN_OUT_PASSES
    kk = [_pdot(kh[h], kh[h], "nt", solve) for h in heads]
    x = _unit_lower_inverses(tuple(jnp.where(strict, bh[h] * kk[h] * decay[h], 0.0) for h in heads))
    eg = [jnp.exp(gc[h]) for h in heads]
    u = [_pdot(x[h], vh[h] * bh[h], "nn", solve) for h in heads]
    wk = [_pdot(x[h], kh[h] * (bh[h] * eg[h]), "nn", solve) for h in heads]
    qk = [jnp.where(tri, _pdot(qh[h], kh[h], "nt", out) * decay[h], 0.0) for h in heads]
    v_new = [u[h] - _pdot(wk[h], states[h], "nn", carry) for h in heads]
    o = [_pdot(qh[h] * eg[h], states[h], "nn", out) + _pdot(qk[h], v_new[h], "nn", out) for h in heads]
    nxt = [states[h] * jnp.exp(gl[h]) + _pdot(kh[h] * jnp.exp(gl[h] - gc[h]), v_new[h], "tn", carry)
           for h in heads]
    return jnp.concatenate(o, axis=1), tuple(nxt)


def _delta_fwd(q, k, v, beta, g, *, name):
    s = q.shape[0]
    c = DN_BLOCK
    nc = s // c

    def body(q_ref, k_ref, v_ref, b_ref, g_ref, o_ref, sp_ref, st):
        @pl.when(pl.program_id(0) == 0)
        def _():
            st[...] = jnp.zeros_like(st)

        states = tuple(st[h] for h in range(DN_HEADS))
        for h in range(DN_HEADS):
            sp_ref[0, h] = states[h]
        o, nxt = _delta_chunk(states, q_ref[...], k_ref[...], v_ref[...], b_ref[...], g_ref[...])
        for h in range(DN_HEADS):
            st[h] = nxt[h]
        o_ref[...] = o

    blk = pl.BlockSpec((c, BRANCH_WIDTH), lambda n: (n, 0))
    return pl.pallas_call(
        body, name=name, grid=(nc,), in_specs=[blk] * 5,
        out_specs=[blk, pl.BlockSpec((1, DN_HEADS, DN_HEAD_DIM, DN_HEAD_DIM), lambda n: (n, 0, 0, 0))],
        out_shape=[jax.ShapeDtypeStruct((s, BRANCH_WIDTH), F32),
                   jax.ShapeDtypeStruct((nc, DN_HEADS, DN_HEAD_DIM, DN_HEAD_DIM), F32)],
        scratch_shapes=[pltpu.VMEM((DN_HEADS, DN_HEAD_DIM, DN_HEAD_DIM), F32)],
        compiler_params=_params(("arbitrary",)),
    )(q, k, v, beta, g)


def _delta_bwd(q, k, v, beta, g, states, do, *, name):
    s = q.shape[0]
    c = DN_BLOCK
    nc = s // c

    def body(q_ref, k_ref, v_ref, b_ref, g_ref, sp_ref, do_ref, dq_ref, dk_ref, dv_ref, db_ref, dg_ref, dst):
        @pl.when(pl.program_id(0) == 0)
        def _():
            dst[...] = jnp.zeros_like(dst)

        states = tuple(sp_ref[0, h] for h in range(DN_HEADS))
        _, vjp = jax.vjp(_delta_chunk, states, q_ref[...], k_ref[...], v_ref[...], b_ref[...], g_ref[...])
        d = vjp((do_ref[...], tuple(dst[h] for h in range(DN_HEADS))))
        for h in range(DN_HEADS):
            dst[h] = d[0][h]
        for o_ref, val in zip((dq_ref, dk_ref, dv_ref, db_ref, dg_ref), d[1:]):
            o_ref[...] = val

    blk = pl.BlockSpec((c, BRANCH_WIDTH), lambda n: (nc - 1 - n, 0))
    res = pl.pallas_call(
        body, name=name, grid=(nc,),
        in_specs=[blk] * 5 + [pl.BlockSpec((1, DN_HEADS, DN_HEAD_DIM, DN_HEAD_DIM), lambda n: (nc - 1 - n, 0, 0, 0)), blk],
        out_specs=[blk] * 5, out_shape=[jax.ShapeDtypeStruct((s, BRANCH_WIDTH), F32)] * 5,
        scratch_shapes=[pltpu.VMEM((DN_HEADS, DN_HEAD_DIM, DN_HEAD_DIM), F32)],
        compiler_params=_params(("arbitrary",)),
    )(q, k, v, beta, g, states, do)
    return list(res)


def _split_bf16(x):
    hi = x.astype(BF16)
    return hi, (x - hi.astype(F32)).astype(BF16)


def _sb_consts():
    tq, tk = SB_QUERY_ROWS, SB_BLOCK
    row, col = _iota2((tk, tk), 0), _iota2((tk, tk), 1)
    ones = jnp.ones((tk, tk), BF16)
    later = jnp.concatenate([(row > col).astype(BF16), ones], axis=1)
    from_here = jnp.concatenate([(row >= col).astype(BF16), ones], axis=1)
    first = _iota2((tq, 128), 1) < SB_HEAD_DIM
    return later, from_here, first


def _sb_causal(d):
    tq, tk = SB_QUERY_ROWS, SB_BLOCK
    return _iota2((tq, tk), 1) + d * tk < _iota2((tq, tk), 0)


def _sums(x, mat):
    hi, lo = _split_bf16(x)
    return jnp.dot(hi, mat, preferred_element_type=F32) + jnp.dot(lo, mat, preferred_element_type=F32)


def _sb_weights(qs, kb, accs, later, causal):
    tk = SB_BLOCK
    z = [lax.dot_general(qh, kb, (((1,), (1,)), ((), ())), preferred_element_type=F32) for qh in qs]
    lk = [-_softplus(zh) for zh in z]
    if causal is not None:
        lk = [jnp.where(causal, v, 0.0) for v in lk]
    cs = [_sums(v, later) for v in lk]
    e = [z[h] + lk[h] + cs[h][:, :tk] + accs[h] for h in range(2)]
    if causal is not None:
        e = [jnp.where(causal, v, -1e30) for v in e]
    return lk, [jnp.exp(v) for v in e], [v[:, tk:] for v in cs]


def _sb_alive(accs):
    return jnp.max(jnp.maximum(accs[0], accs[1])) > SB_DEAD_LOG


def _sb_sweep(i, block, carry, accs_of, stop=None):
    per = SB_QUERY_ROWS // SB_BLOCK
    for d in reversed(range(per)):
        carry = block(i * per + d, carry, _sb_causal(d))
    if stop is not None:
        return lax.fori_loop(0, i * per - 1 - stop, lambda jj, cr: block(i * per - 1 - jj, cr, None), carry)

    def step(state):
        j, _, cr = state
        cr = block(j, cr, None)
        return j - 1, _sb_alive(accs_of(cr)), cr

    j, _, carry = lax.while_loop(lambda st: jnp.logical_and(st[0] >= 0, st[1]), step,
                                 (i * per - 1, _sb_alive(accs_of(carry)), carry))
    return carry, j


def _sb_fwd(p, *, name):
    s = p.shape[0]
    t, tk = SB_QUERY_ROWS, SB_BLOCK
    scale = SB_HEAD_DIM ** -0.5

    def body(q_ref, k_ref, v_ref, o_ref):
        i = pl.program_id(1)
        later, _, first = _sb_consts()
        q = q_ref[...] * scale
        qs = (jnp.where(first, q, 0.0).astype(BF16), jnp.where(first, 0.0, q).astype(BF16))

        def block(j, carry, causal):
            start = pl.multiple_of(j * tk, tk)
            kb = k_ref[pl.ds(start, tk), :].astype(BF16)
            vb = v_ref[pl.ds(start, tk), :].astype(BF16)
            _, w, tot = _sb_weights(qs, kb, [carry[h][1] for h in range(2)], later, causal)
            out = [jnp.dot(w[h].astype(BF16), vb, preferred_element_type=F32) for h in range(2)]
            return tuple((carry[h][0] + out[h], carry[h][1] + tot[h]) for h in range(2))

        zero = jnp.zeros((t, 128), F32)
        carry, _ = _sb_sweep(i, block, ((zero, zero), (zero, zero)), lambda cr: (cr[0][1], cr[1][1]))
        o_ref[...] = jnp.where(first, carry[0][0], carry[1][0])

    return pl.pallas_call(
        body, name=name, grid=(BRANCH_WIDTH // 128, s // t),
        in_specs=[pl.BlockSpec((t, 128), lambda pr, i: (i, OFF_CQ // 128 + pr)),
                  pl.BlockSpec((s, 128), lambda pr, i: (0, OFF_CK // 128 + pr)),
                  pl.BlockSpec((s, 128), lambda pr, i: (0, OFF_CV // 128 + pr))],
        out_specs=pl.BlockSpec((t, 128), lambda pr, i: (i, pr)),
        out_shape=jax.ShapeDtypeStruct((s, BRANCH_WIDTH), F32),
        compiler_params=_params(("arbitrary", "arbitrary")),
    )(p, p, p)


def _sb_bwd(p, do, *, name):
    s = p.shape[0]
    t, tk = SB_QUERY_ROWS, SB_BLOCK
    scale = SB_HEAD_DIM ** -0.5

    def body(q_ref, k_ref, v_ref, do_ref, dq_ref, dk_ref, dv_ref):
        i = pl.program_id(1)

        @pl.when(i == 0)
        def _():
            dk_ref[...] = jnp.zeros_like(dk_ref)
            dv_ref[...] = jnp.zeros_like(dv_ref)

        later, from_here, first = _sb_consts()
        q = q_ref[...] * scale
        do = do_ref[...]
        qs = (jnp.where(first, q, 0.0).astype(BF16), jnp.where(first, 0.0, q).astype(BF16))
        dos = (jnp.where(first, do, 0.0).astype(BF16), jnp.where(first, 0.0, do).astype(BF16))

        def total(j, carry, causal):
            start = pl.multiple_of(j * tk, tk)
            kb = k_ref[pl.ds(start, tk), :].astype(BF16)
            vb = v_ref[pl.ds(start, tk), :].astype(BF16)
            _, w, tot = _sb_weights(qs, kb, [carry[h][0] for h in range(2)], later, causal)
            dw = [lax.dot_general(dos[h], vb, (((1,), (1,)), ((), ())), preferred_element_type=F32) for h in range(2)]
            tde = [_sums(dw[h] * w[h], from_here)[:, tk:] for h in range(2)]
            return tuple((carry[h][0] + tot[h], carry[h][1] + tde[h]) for h in range(2))

        zero = jnp.zeros((t, 128), F32)
        sums, stop = _sb_sweep(i, total, ((zero, zero), (zero, zero)), lambda cr: (cr[0][0], cr[1][0]))
        deltas = (sums[0][1], sums[1][1])

        def block(j, carry, causal):
            start = pl.multiple_of(j * tk, tk)
            kb = k_ref[pl.ds(start, tk), :].astype(BF16)
            vb = v_ref[pl.ds(start, tk), :].astype(BF16)
            both = range(2)
            tn = (((0,), (0,)), ((), ()))
            lk, w, tot = _sb_weights(qs, kb, [carry[h][1] for h in both], later, causal)
            dw = [lax.dot_general(dos[h], vb, (((1,), (1,)), ((), ())), preferred_element_type=F32) for h in both]
            de = [dw[h] * w[h] for h in both]
            cs = [_sums(de[h], from_here) for h in both]
            keep = [jnp.exp(lk[h]) for h in both]
            dz = [de[h] * keep[h] - (deltas[h] - (cs[h][:, :tk] + carry[h][2])) * (1.0 - keep[h]) for h in both]
            if causal is not None:
                dz = [jnp.where(causal, v, 0.0) for v in dz]
            dzb = [v.astype(BF16) for v in dz]
            dq = [jnp.dot(dzb[h], kb, preferred_element_type=F32) for h in both]
            dk = [lax.dot_general(dzb[h], qs[h], tn, preferred_element_type=F32) for h in both]
            dv = [lax.dot_general(w[h].astype(BF16), dos[h], tn, preferred_element_type=F32) for h in both]
            dk_ref[pl.ds(start, tk), :] += dk[0] + dk[1]
            dv_ref[pl.ds(start, tk), :] += dv[0] + dv[1]
            return tuple((carry[h][0] + dq[h], carry[h][1] + tot[h], carry[h][2] + cs[h][:, tk:]) for h in both)

        carry = _sb_sweep(i, block, ((zero, zero, zero), (zero, zero, zero)), None, stop=stop)
        dq_ref[...] = jnp.where(first, carry[0][0], carry[1][0]) * scale

    qblk = lambda off: pl.BlockSpec((t, 128), lambda pr, i: (i, off // 128 + pr))
    full = lambda off: pl.BlockSpec((s, 128), lambda pr, i: (0, off // 128 + pr))
    res = pl.pallas_call(
        body, name=name, grid=(BRANCH_WIDTH // 128, s // t),
        in_specs=[qblk(OFF_CQ), full(OFF_CK), full(OFF_CV), qblk(0)],
        out_specs=[qblk(0), full(0), full(0)],
        out_shape=[jax.ShapeDtypeStruct((s, BRANCH_WIDTH), F32)] * 3,
        compiler_params=_params(("arbitrary", "arbitrary")),
    )(p, p, p, do)
    return list(res)


def _loss_head(y, target, *, name, ts=512):
    s, d = y.shape
    ts = min(ts, s)

    def body(y_ref, t_ref, sq_ref, dy_ref):
        @pl.when(pl.program_id(0) == 0)
        def _():
            sq_ref[...] = jnp.zeros_like(sq_ref)

        err = y_ref[...] - t_ref[...]
        dy_ref[...] = err * (1.0 / d)
        tot = jnp.sum(jnp.sum(err * err, axis=1, keepdims=True), axis=0, keepdims=True)
        sq_ref[...] += jnp.broadcast_to(tot, sq_ref.shape)

    blk = pl.BlockSpec((ts, d), lambda i: (i, 0))
    return pl.pallas_call(
        body, name=name, grid=(s // ts,), in_specs=[blk, blk],
        out_specs=[pl.BlockSpec((1, 128), lambda i: (0, 0)), blk],
        out_shape=[jax.ShapeDtypeStruct((1, 128), F32), jax.ShapeDtypeStruct((s, d), F32)],
        compiler_params=_params(("arbitrary",)),
    )(y, target)


def _row_tile(r, limit=512):
    return max(t for t in range(8, limit + 1, 8) if r % t == 0)


def _adamw(w, g, m, v, *, name):
    shape = w.shape
    lanes = shape[-1]
    w, g, m, v = (a.reshape(-1, lanes) for a in (w, g, m, v))
    r = w.shape[0]
    ts = r if r <= 256 else _row_tile(r, 256 if lanes > LANES else 512)

    def body(w_ref, g_ref, m_ref, v_ref, d_ref, nm_ref, nv_ref):
        gv = g_ref[...]
        m_new = ADAM_B1 * m_ref[...] + (1.0 - ADAM_B1) * gv
        v_new = ADAM_B2 * v_ref[...] + (1.0 - ADAM_B2) * jnp.square(gv)
        m_hat = m_new / (1.0 - ADAM_B1 ** ADAM_STEP)
        v_hat = v_new / (1.0 - ADAM_B2 ** ADAM_STEP)
        d_ref[...] = -ADAM_LR * (m_hat / (jnp.sqrt(v_hat) + ADAM_EPS) + ADAM_WD * w_ref[...])
        nm_ref[...] = m_new
        nv_ref[...] = v_new

    blk = pl.BlockSpec((ts, lanes), lambda i: (i, 0))
    res = pl.pallas_call(
        body, name=name, grid=(r // ts,), in_specs=[blk] * 4, out_specs=[blk] * 3,
        out_shape=[jax.ShapeDtypeStruct((r, lanes), F32)] * 3,
        compiler_params=_params(("parallel",)),
    )(w, g, m, v)
    return [a.reshape(shape) for a in res]


def _add_rows(terms, *, name, out_dtype=F32):
    r = terms[0].shape[0]
    ts = _row_tile(r)

    def body(*refs):
        acc = refs[0][...].astype(F32)
        for ref in refs[1:-1]:
            acc = acc + ref[...].astype(F32)
        refs[-1][...] = acc.astype(out_dtype)

    blk = pl.BlockSpec((ts, LANES), lambda i: (i, 0))
    return pl.pallas_call(
        body, name=name, grid=(r // ts,), in_specs=[blk] * len(terms), out_specs=blk,
        out_shape=jax.ShapeDtypeStruct((r, LANES), out_dtype), compiler_params=_params(("parallel",)),
    )(*terms)


def _col_sums(a, *, name, ts=256):
    s, n = a.shape
    ts = min(ts, s)

    def body(a_ref, o_ref):
        @pl.when(pl.program_id(0) == 0)
        def _():
            o_ref[...] = jnp.zeros_like(o_ref)

        o_ref[...] += jnp.sum(a_ref[...], axis=0, keepdims=True)

    return pl.pallas_call(
        body, name=name, grid=(s // ts,), in_specs=[pl.BlockSpec((ts, n), lambda i: (i, 0))],
        out_specs=pl.BlockSpec((1, n), lambda i: (0, 0)), out_shape=jax.ShapeDtypeStruct((1, n), F32),
        compiler_params=_params(("arbitrary",)),
    )(a)


_HBM = pl.BlockSpec(memory_space=pltpu.HBM)
_MESH = pl.DeviceIdType.MESH


def _other_chips(x, y):
    return [(1 - x, y), (x, 1 - y), (1 - x, 1 - y)]


def _gather_chips(shard, *, name):
    r, lanes = shard.shape
    half = r // 2
    assert half * 2 == r

    def body(in_ref, out_ref, send_sems, recv_sems):
        x, y, c = lax.axis_index("x"), lax.axis_index("y"), lax.axis_index("c")
        me = 2 * x + y
        sibling = (x, y, 1 - c)
        chips = _other_chips(x, y)

        def copy(sem, chip, core_half, to):
            rows = out_ref.at[chip, pl.ds(core_half * half, half)]
            return pltpu.make_async_remote_copy(src_ref=rows, dst_ref=rows, send_sem=send_sems.at[sem],
                                                recv_sem=recv_sems.at[sem], device_id=to, device_id_type=_MESH)

        first = []
        for kk, (px, py) in enumerate(chips):
            cp = pltpu.make_async_remote_copy(
                src_ref=in_ref.at[pl.ds(c * half, half)], dst_ref=out_ref.at[me, pl.ds(c * half, half)],
                send_sem=send_sems.at[kk], recv_sem=recv_sems.at[kk], device_id=(px, py, c), device_id_type=_MESH)
            cp.start()
            first.append(cp)
        passed = [copy(3 + kk, 2 * px + py, c, sibling) for kk, (px, py) in enumerate(chips)]
        for kk, (px, py) in enumerate(chips):
            copy(kk, 2 * px + py, c, (px, py, c)).wait_recv()
            passed[kk].start()
        for kk, (px, py) in enumerate(chips):
            copy(3 + kk, 2 * px + py, 1 - c, sibling).wait_recv()
        for cp in first + passed:
            cp.wait_send()

    gathered = pl.pallas_call(
        body, name=name, in_specs=[_HBM], out_specs=_HBM,
        out_shape=jax.ShapeDtypeStruct((4, r, lanes), shard.dtype),
        scratch_shapes=[pltpu.SemaphoreType.DMA((6,)), pltpu.SemaphoreType.DMA((6,))],
    )(shard)
    me = 2 * lax.axis_index("x") + lax.axis_index("y")
    return jnp.where(lax.broadcasted_iota(jnp.int32, (4, 1, 1), 0) == me, shard[None], gathered)


def _to_sibling(block, *, name):
    def body(in_ref, out_ref, send_sem, recv_sem):
        x, y, c = lax.axis_index("x"), lax.axis_index("y"), lax.axis_index("c")
        cp = pltpu.make_async_remote_copy(src_ref=in_ref, dst_ref=out_ref, send_sem=send_sem, recv_sem=recv_sem,
                                          device_id=(x, y, 1 - c), device_id_type=_MESH)
        cp.start()
        cp.wait()

    return pl.pallas_call(
        body, name=name, in_specs=[_HBM], out_specs=_HBM, out_shape=jax.ShapeDtypeStruct(block.shape, block.dtype),
        scratch_shapes=[pltpu.SemaphoreType.DMA, pltpu.SemaphoreType.DMA],
    )(block)


def _scatter_chips(blocks, *, name):
    _, r, lanes = blocks.shape

    def body(in_ref, out_ref, send_sems, recv_sems, local_sem):
        x, y, c = lax.axis_index("x"), lax.axis_index("y"), lax.axis_index("c")
        me = 2 * x + y
        mine = pltpu.make_async_copy(in_ref.at[me], out_ref.at[me], local_sem)
        mine.start()
        copies = []
        for kk, (px, py) in enumerate(_other_chips(x, y)):
            cp = pltpu.make_async_remote_copy(src_ref=in_ref.at[2 * px + py], dst_ref=out_ref.at[me],
                                              send_sem=send_sems.at[kk], recv_sem=recv_sems.at[kk],
                                              device_id=(px, py, c), device_id_type=_MESH)
            cp.start()
            copies.append(cp)
        for kk, (px, py) in enumerate(_other_chips(x, y)):
            pltpu.make_async_remote_copy(src_ref=in_ref.at[me], dst_ref=out_ref.at[2 * px + py],
                                         send_sem=send_sems.at[kk], recv_sem=recv_sems.at[kk], device_id=(px, py, c),
                                         device_id_type=_MESH).wait_recv()
        for cp in copies:
            cp.wait_send()
        mine.wait()

    return pl.pallas_call(
        body, name=name, in_specs=[_HBM], out_specs=_HBM, out_shape=jax.ShapeDtypeStruct((4, r, lanes), blocks.dtype),
        scratch_shapes=[pltpu.SemaphoreType.DMA((3,)), pltpu.SemaphoreType.DMA((3,)), pltpu.SemaphoreType.DMA],
    )(blocks)


SHARDED = (("norm_g", 2), ("w_in", 2), ("conv_w", 2), ("w_branch", 3), ("w_out", 1), ("w_ff1", 2), ("w_ff2", 1))
MATMUL_WEIGHTS = ("w_in", "w_branch", "w_out", "w_ff1", "w_ff2")
VECTOR_WEIGHTS = ("norm_g", "conv_w")
REPLICATED = ("b_in", "sgu_ln_g", "sgu_ln_b", "w_spatial", "b_spatial", "a_log", "dt_bias", "dn_norm_g")
WEIGHT_ORDER = ("norm_g", "w_in", "b_in", "sgu_ln_g", "sgu_ln_b", "w_spatial", "b_spatial", "conv_w", "a_log",
                "dt_bias", "dn_norm_g", "w_branch", "w_out", "w_ff1", "w_ff2")
PACK_ROW_MULTIPLE = 32


def _rows_of(shape):
    n = 1
    for dim in shape:
        n *= dim
    return -(-n // LANES)


def _pack(arrays):
    parts = []
    for a in arrays:
        flat = a.reshape(-1)
        pad = _rows_of(a.shape) * LANES - flat.shape[0]
        if pad:
            flat = jnp.concatenate([flat, jnp.zeros((pad,), flat.dtype)])
        parts.append(flat.reshape(-1, LANES))
    rows = sum(p.shape[0] for p in parts)
    pad = -rows % PACK_ROW_MULTIPLE
    if pad:
        parts.append(jnp.zeros((pad, LANES), parts[0].dtype))
    return jnp.concatenate(parts, axis=0)


def _unpack(buf, shapes):
    out, row = [], 0
    for shape in shapes:
        n = 1
        for dim in shape:
            n *= dim
        rows = _rows_of(shape)
        out.append(buf[row:row + rows].reshape(-1)[:n].reshape(shape))
        row += rows
    return out


def _chip_slice(a, axis, k):
    size = a.shape[axis] // 4
    return lax.slice_in_dim(a, k * size, (k + 1) * size, axis=axis)


def _rearrange_w_in(w):
    pad = jnp.zeros(w.shape[:-1] + (P_PAD - P_IN,), w.dtype)
    return jnp.concatenate([w[..., 0:3072], w[..., 4616:7688], w[..., 3080:4616], w[..., 3072:3080], pad], axis=-1)


def _restore_w_in(w):
    return jnp.concatenate([w[..., 0:3072], w[..., 7680:7688], w[..., 6144:7680], w[..., 3072:6144]], axis=-1)


def _shift_rows(a, n):
    if n == 0:
        return a
    return jnp.concatenate([jnp.zeros((n, a.shape[1]), a.dtype), a[:-n]], axis=0)


def _unshift_rows(a, n):
    if n == 0:
        return a
    return jnp.concatenate([a[n:], jnp.zeros((n, a.shape[1]), a.dtype)], axis=0)


def _layer_params(wl):
    row = lambda v: v.reshape(1, -1)
    pad128 = lambda v, at: jnp.pad(v, (at, 128 - at - v.shape[0])).reshape(1, 128)
    return dict(
        g=[row(wl["norm_g"][i]) for i in range(4)],
        gmlp=[row(wl["sgu_ln_g"]), row(wl["sgu_ln_b"]), wl["w_spatial"],
              jnp.pad(wl["b_spatial"].T, ((0, 0), (0, 128 - GM_GROUPS)))],
        dn_in=[wl["conv_w"], pad128(wl["a_log"], DN_HEADS), pad128(wl["dt_bias"], DN_HEADS)],
        dn_g=[row(wl["dn_norm_g"])],
    )


def _layer_fwd(x0, wl, l):
    tag = lambda s: f"{s}_l{l}"
    pr = _layer_params(wl)
    w = BRANCH_WIDTH
    h0 = _row_fwd(_fn_rms, [(x0, 0, D_MODEL)], [pr["g"][0]], [(D_MODEL, BF16)], ts=512, name=tag("rms0"))[0]
    p = _mm(h0, wl["w_in"], bias=wl["b_in"].reshape(1, -1), name=tag("proj_in"))
    ya = _row_fwd(_fn_gmlp, [(p, OFF_AU, w), (p, OFF_AV, w)], pr["gmlp"], [(w, BF16)], ts=256, name=tag("gmlp"))[0]
    xq = p[:, OFF_BQ:OFF_BQ + 3 * w]
    shifted = [_shift_rows(xq, 3 - j) for j in range(4)]
    dn_rows = [(a, 0, 3 * w) for a in shifted] + [(p, OFF_BD, 128)]
    q, k, v, beta, g = _row_fwd(_fn_dn_in, dn_rows, pr["dn_in"], [(w, F32)] * 5, ts=256, name=tag("dn_in"))
    o, states = _delta_fwd(q, k, v, beta, g, name=tag("delta"))
    yb = _row_fwd(_fn_dn_out, [(o, 0, w), (p, OFF_BZ, w)], pr["dn_g"], [(w, BF16)], ts=512, name=tag("dn_out"))[0]
    yc = _sb_fwd(p, name=tag("sb"))
    ys = [ya, yb, yc]
    proj = [_mm(ys[i], wl["w_branch"][i], name=tag(f"branch{i}")) for i in range(3)]
    merge_rows = [(a, 0, D_MODEL) for a in proj] + [(p, OFF_GATE + i * D_MODEL, D_MODEL) for i in range(3)]
    m = _row_fwd(_fn_merge, merge_rows, [], [(D_MODEL, BF16)], ts=256, name=tag("merge"))[0]
    mixed = _mm(m, wl["w_out"], name=tag("out"))
    x1 = _row_fwd(_fn_resid_rms, [(x0, 0, D_MODEL), (mixed, 0, D_MODEL)], [pr["g"][1]], [(D_MODEL, F32)], ts=512,
                  name=tag("resid1"))[0]
    h2 = _row_fwd(_fn_rms, [(x1, 0, D_MODEL)], [pr["g"][2]], [(D_MODEL, BF16)], ts=512, name=tag("rms2"))[0]
    a = _mm(h2, wl["w_ff1"], name=tag("ff1"))
    r = _row_fwd(_fn_relu2, [(a, 0, D_FF)], [], [(D_FF, BF16)], ts=256, name=tag("relu2"))[0]
    f = _mm(r, wl["w_ff2"], name=tag("ff2"))
    x2 = _row_fwd(_fn_resid_rms, [(x1, 0, D_MODEL), (f, 0, D_MODEL)], [pr["g"][3]], [(D_MODEL, F32)], ts=512,
                  name=tag("resid2"))[0]
    saved = dict(x0=x0, h0=h0, p=p, shifted=shifted, q=q, k=k, v=v, beta=beta, g=g, states=states, o=o, ys=ys,
                 proj=proj, m=m, mixed=mixed, x1=x1, h2=h2, a=a, r=r, f=f)
    return x2, saved


def _layer_bwd(dx2, sv, wl, l):
    tag = lambda s: f"{s}_l{l}"
    pr = _layer_params(wl)
    w = BRANCH_WIDTH
    full = lambda a: (a, 0, a.shape[1])
    p = sv["p"]
    (df,), (dg3,) = _row_bwd(_fn_rms_branch, [full(sv["f"])], [pr["g"][3]], [full(dx2)], ts=512, name=tag("resid2_b"))
    dr = _mm(df, wl["w_ff2"], trans_b=True, name=tag("ff2_dx"))
    dw_ff2 = _mm_tn(sv["r"], df, name=tag("ff2_dw"))
    (da,), _ = _row_bwd(_fn_relu2, [full(sv["a"])], [], [full(dr)], ts=256, name=tag("relu2_b"))
    dh2 = _mm(da, wl["w_ff1"], trans_b=True, name=tag("ff1_dx"))
    dw_ff1 = _mm_tn(sv["h2"], da, name=tag("ff1_dw"))
    (dx1,), (dg2,) = _row_bwd(_fn_rms_keep, [full(sv["x1"])], [pr["g"][2]], [full(dh2), full(dx2)], ts=512,
                              name=tag("rms2_b"))
    (dmixed,), (dg1,) = _row_bwd(_fn_rms_branch, [full(sv["mixed"])], [pr["g"][1]], [full(dx1)], ts=512,
                                 name=tag("resid1_b"))
    dm = _mm(dmixed, wl["w_out"], trans_b=True, name=tag("out_dx"))
    dw_out = _mm_tn(sv["m"], dmixed, name=tag("out_dw"))
    merge_rows = [full(a) for a in sv["proj"]] + [(p, OFF_GATE + i * D_MODEL, D_MODEL) for i in range(3)]
    dmerge, _ = _row_bwd(_fn_merge, merge_rows, [], [full(dm)], ts=256, name=tag("merge_b"))
    dproj, dgates = dmerge[:3], dmerge[3:]
    dys = [_mm(dproj[i], wl["w_branch"][i], trans_b=True, name=tag(f"branch{i}_dx")) for i in range(3)]
    dw_branch = jnp.stack([_mm_tn(sv["ys"][i], dproj[i], name=tag(f"branch{i}_dw")) for i in range(3)])
    (du, dv_a), dgm = _row_bwd(_fn_gmlp, [(p, OFF_AU, w), (p, OFF_AV, w)], pr["gmlp"], [full(dys[0])], ts=256,
                               name=tag("gmlp_b"))
    (do, dz), (d_dn_g,) = _row_bwd(_fn_dn_out, [full(sv["o"]), (p, OFF_BZ, w)], pr["dn_g"], [full(dys[1])], ts=512,
                                   name=tag("dn_out_b"))
    dqkvbg = _delta_bwd(sv["q"], sv["k"], sv["v"], sv["beta"], sv["g"], sv["states"], do, name=tag("delta_b"))
    dn_rows = [full(a) for a in sv["shifted"]] + [(p, OFF_BD, 128)]
    d_in, d_dn_in = _row_bwd(_fn_dn_in, dn_rows, pr["dn_in"], [full(a) for a in dqkvbg], ts=256, name=tag("dn_in_b"))
    dxq = sum(_unshift_rows(d_in[j], 3 - j) for j in range(4))
    dcq, dck, dcv = _sb_bwd(p, dys[2], name=tag("sb_b"))
    s = p.shape[0]
    dp = jnp.concatenate([du, dv_a, dxq, dz] + dgates + [dcq, dck, dcv, d_in[4],
                                                         jnp.zeros((s, P_PAD - OFF_BD - 128), F32)], axis=1)
    dh0 = _mm(dp, wl["w_in"], trans_b=True, name=tag("proj_in_dx"))
    dw_in = _mm_tn(sv["h0"], dp, name=tag("proj_in_dw"))
    db_in = _col_sums(dp, name=tag("bias_b"))
    (dx0,), (dg0,) = _row_bwd(_fn_rms_keep, [full(sv["x0"])], [pr["g"][0]], [full(dh0), full(dx1)], ts=512,
                              name=tag("rms0_b"))
    grads = dict(
        norm_g=jnp.concatenate([dg0, dg1, dg2, dg3], axis=0), w_in=dw_in, b_in=db_in.reshape(-1),
        sgu_ln_g=dgm[0].reshape(-1), sgu_ln_b=dgm[1].reshape(-1), w_spatial=dgm[2],
        b_spatial=dgm[3][:, :GM_GROUPS].T, conv_w=d_dn_in[0], a_log=d_dn_in[1][0, DN_HEADS:2 * DN_HEADS],
        dt_bias=d_dn_in[2][0, DN_HEADS:2 * DN_HEADS], dn_norm_g=d_dn_g.reshape(-1), w_branch=dw_branch,
        w_out=dw_out, w_ff1=dw_ff1, w_ff2=dw_ff2)
    return dx0, grads


def _local_step(x, target, weights):
    saved = []
    h = x
    layers = []
    for l in range(DEPTH):
        wl = {n: weights[n][l] for n in WEIGHT_ORDER}
        layers.append(wl)
        h, sv = _layer_fwd(h, wl, l)
        saved.append(sv)
    sq, dh = _loss_head(h, target, name="loss_head")
    grads = [None] * DEPTH
    for l in reversed(range(DEPTH)):
        dh, grads[l] = _layer_bwd(dh, saved[l], layers[l], l)
    stacked = {n: jnp.stack([grads[l][n] for l in range(DEPTH)]) for n in WEIGHT_ORDER}
    return sq[0, 0], dh, stacked


def kernel(x, norm_g, w_in, b_in, sgu_ln_g, sgu_ln_b, w_spatial, b_spatial, conv_w, a_log, dt_bias, dn_norm_g, w_branch, w_out, w_ff1, w_ff2, loss_target, m_norm_g, m_w_in, m_b_in, m_sgu_ln_g, m_sgu_ln_b, m_w_spatial, m_b_spatial, m_conv_w, m_a_log, m_dt_bias, m_dn_norm_g, m_w_branch, m_w_out, m_w_ff1, m_w_ff2, v_norm_g, v_w_in, v_b_in, v_sgu_ln_g, v_sgu_ln_b, v_w_spatial, v_b_spatial, v_conv_w, v_a_log, v_dt_bias, v_dn_norm_g, v_w_branch, v_w_out, v_w_ff1, v_w_ff2):
    local = dict(norm_g=norm_g, w_in=w_in, b_in=b_in, sgu_ln_g=sgu_ln_g, sgu_ln_b=sgu_ln_b, w_spatial=w_spatial,
                 b_spatial=b_spatial, conv_w=conv_w, a_log=a_log, dt_bias=dt_bias, dn_norm_g=dn_norm_g,
                 w_branch=w_branch, w_out=w_out, w_ff1=w_ff1, w_ff2=w_ff2)
    mom1 = dict(norm_g=m_norm_g, w_in=m_w_in, b_in=m_b_in, sgu_ln_g=m_sgu_ln_g, sgu_ln_b=m_sgu_ln_b,
                w_spatial=m_w_spatial, b_spatial=m_b_spatial, conv_w=m_conv_w, a_log=m_a_log, dt_bias=m_dt_bias,
                dn_norm_g=m_dn_norm_g, w_branch=m_w_branch, w_out=m_w_out, w_ff1=m_w_ff1, w_ff2=m_w_ff2)
    mom2 = dict(norm_g=v_norm_g, w_in=v_w_in, b_in=v_b_in, sgu_ln_g=v_sgu_ln_g, sgu_ln_b=v_sgu_ln_b,
                w_spatial=v_w_spatial, b_spatial=v_b_spatial, conv_w=v_conv_w, a_log=v_a_log, dt_bias=v_dt_bias,
                dn_norm_g=v_dn_norm_g, w_branch=v_w_branch, w_out=v_w_out, w_ff1=v_w_ff1, w_ff2=v_w_ff2)
    shard_names = [n for n, _ in SHARDED]
    shard_shapes = [local[n].shape for n in shard_names]
    repl_shapes = [local[n].shape for n in REPLICATED]

    weights = {n: local[n] for n in REPLICATED}
    for names, dtype, call in ((MATMUL_WEIGHTS, BF16, "gather_matmul_weights"), (VECTOR_WEIGHTS, F32, "gather_vectors")):
        gathered = _gather_chips(_pack([local[n] for n in names]).astype(dtype), name=call)
        per_chip = [_unpack(gathered[k], [local[n].shape for n in names]) for k in range(4)]
        for i, n in enumerate(names):
            weights[n] = jnp.concatenate([per_chip[k][i] for k in range(4)], axis=dict(SHARDED)[n])
    weights["w_in"] = _rearrange_w_in(weights["w_in"])
    weights["b_in"] = _rearrange_w_in(weights["b_in"])

    sq, dx, grads = _local_step(x[0], loss_target[0], weights)
    loss = lax.psum(0.5 * sq / D_MODEL, ("x", "y", "c"))
    grads["w_in"] = _restore_w_in(grads["w_in"])
    grads["b_in"] = _restore_w_in(grads["b_in"])

    blocks = jnp.stack([_pack([_chip_slice(grads[n], axis, k) for n, axis in SHARDED] + [grads[n] for n in REPLICATED])
                        for k in range(4)])
    rows = blocks.shape[1]
    half = rows // 2
    c = lax.axis_index("c")
    halves = blocks.reshape(4, 2, half, LANES)
    keep = lax.dynamic_index_in_dim(halves, c, axis=1, keepdims=False)
    give = lax.dynamic_index_in_dim(halves, 1 - c, axis=1, keepdims=False)
    got = _to_sibling(give, name="grads_to_sibling")
    chip_sum = _add_rows([keep.reshape(4 * half, LANES), got.reshape(4 * half, LANES)], name="grads_chip_sum",
                         out_dtype=BF16)
    by_chip = _scatter_chips(chip_sum.reshape(4, half, LANES), name="grads_scatter")
    my_half = _add_rows([by_chip[k] for k in range(4)], name="grads_sum")
    other_half = _to_sibling(my_half, name="grads_half_swap")
    total = jnp.concatenate([jnp.where(c == 0, my_half, other_half), jnp.where(c == 0, other_half, my_half)], axis=0)
    g_out = dict(zip(shard_names + list(REPLICATED), _unpack(total, shard_shapes + repl_shapes)))

    d_out, m_out, v_out = {}, {}, {}
    for n in WEIGHT_ORDER:
        d_out[n], m_out[n], v_out[n] = _adamw(local[n], g_out[n], mom1[n], mom2[n], name=f"adamw_{n}")
    return (loss, dx[None], *[g_out[n] for n in WEIGHT_ORDER], *[d_out[n] for n in WEIGHT_ORDER],
            *[m_out[n] for n in WEIGHT_ORDER], *[v_out[n] for n in WEIGHT_ORDER])
```

```python
import functools

import jax
import jax.numpy as jnp
from jax import lax
from jax.experimental import pallas as pl
from jax.experimental.pallas import tpu as pltpu

F32 = jnp.float32
BF16 = jnp.bfloat16

D_MODEL = 1024
DEPTH = 4
BRANCH_WIDTH = 512
GM_CHUNK = 128
GM_GROUPS = 8
DN_HEADS = 4
DN_HEAD_DIM = 128
DN_BLOCK = 128
DN_SOLVE_PASSES, DN_STATE_PASSES, DN_OUT_PASSES = 3, 1, 1
SB_HEAD_DIM = 64
SB_BLOCK = 128
SB_QUERY_ROWS = 256
SB_DEAD_LOG = -88.0
D_FF = 4096
P_IN = 7688
P_PAD = 8192
NORM_EPS = 1e-6
ADAM_LR, ADAM_B1, ADAM_B2, ADAM_EPS, ADAM_WD, ADAM_STEP = 0.001, 0.9, 0.999, 1e-08, 0.01, 10

OFF_AU, OFF_AV, OFF_BQ, OFF_BK, OFF_BV, OFF_BZ = 0, 512, 1024, 1536, 2048, 2560
OFF_GATE = 3072
OFF_CQ, OFF_CK, OFF_CV = 6144, 6656, 7168
OFF_BD = 7680

LANES = 1024
VMEM_LIMIT_BYTES = 56 * 1024 * 1024

_HI = lax.Precision.HIGHEST


def _params(sem):
    return pltpu.CompilerParams(dimension_semantics=sem, vmem_limit_bytes=VMEM_LIMIT_BYTES)


def _mm(a, b, *, name, out_dtype=F32, bias=None, trans_b=False, beside=(), finish=None, tm=1024, tn=1024, tk=1024):
    m, k = a.shape
    n = b.shape[0] if trans_b else b.shape[1]
    tm, tn, tk = min(tm, m), min(tn, n), min(tk, k)
    assert m % tm == 0 and n % tn == 0 and k % tk == 0, (a.shape, b.shape)
    nk = k // tk
    dn = (((1,), (1,)), ((), ())) if trans_b else (((1,), (0,)), ((), ()))
    several = isinstance(out_dtype, tuple)
    out_dtypes = out_dtype if several else (out_dtype,)
    n_in = 2 + (bias is not None) + len(beside)

    def body(*refs):
        a_ref, b_ref = refs[:2]
        o_refs, acc = refs[n_in:-1], refs[-1]
        kk = pl.program_id(2)
        part = lax.dot_general(a_ref[...].astype(BF16), b_ref[...].astype(BF16), dn, preferred_element_type=F32)

        @pl.when(kk == 0)
        def _():
            acc[...] = part

        @pl.when(kk > 0)
        def _():
            acc[...] += part

        @pl.when(kk == nk - 1)
        def _():
            r = acc[...]
            if bias is not None:
                r = r + refs[2][...]
            tiles = (r,) if finish is None else finish(r, *[t[...] for t in refs[n_in - len(beside):n_in]])
            for o_ref, tile in zip(o_refs, tiles if isinstance(tiles, tuple) else (tiles,)):
                o_ref[...] = tile.astype(o_ref.dtype)

    in_specs = [pl.BlockSpec((tm, tk), lambda i, j, kk: (i, kk))]
    if trans_b:
        in_specs.append(pl.BlockSpec((tn, tk), lambda i, j, kk: (j, kk)))
    else:
        in_specs.append(pl.BlockSpec((tk, tn), lambda i, j, kk: (kk, j)))
    args = [a, b]
    if bias is not None:
        in_specs.append(pl.BlockSpec((1, tn), lambda i, j, kk: (0, j)))
        args.append(bias)
    for t in beside:
        assert t.shape == (m, n)
        in_specs.append(pl.BlockSpec((tm, tn), lambda i, j, kk: (i, j)))
        args.append(t)
    res = pl.pallas_call(
        body, name=name, grid=(m // tm, n // tn, nk),
        in_specs=in_specs, out_specs=[pl.BlockSpec((tm, tn), lambda i, j, kk: (i, j)) for _ in out_dtypes],
        out_shape=[jax.ShapeDtypeStruct((m, n), dt) for dt in out_dtypes],
        scratch_shapes=[pltpu.VMEM((tm, tn), F32)],
        compiler_params=_params(("parallel", "parallel", "arbitrary")),
    )(*args)
    return list(res) if several else res[0]


def _mm_tn(a, b, *, name, tm=1024, tn=1024, ts=1024):
    s, ka = a.shape
    n = b.shape[1]
    tm, tn, ts = min(tm, ka), min(tn, n), min(ts, s)
    assert ka % tm == 0 and n % tn == 0 and s % ts == 0, (a.shape, b.shape)

    def body(a_ref, b_ref, o_ref):
        part = lax.dot_general(a_ref[...].astype(BF16), b_ref[...].astype(BF16), (((0,), (0,)), ((), ())),
                               preferred_element_type=F32)

        @pl.when(pl.program_id(2) == 0)
        def _():
            o_ref[...] = part

        @pl.when(pl.program_id(2) > 0)
        def _():
            o_ref[...] += part

    return pl.pallas_call(
        body, name=name, grid=(ka // tm, n // tn, s // ts),
        in_specs=[pl.BlockSpec((ts, tm), lambda i, j, r: (r, i)), pl.BlockSpec((ts, tn), lambda i, j, r: (r, j))],
        out_specs=pl.BlockSpec((tm, tn), lambda i, j, r: (i, j)),
        out_shape=jax.ShapeDtypeStruct((ka, n), F32),
        compiler_params=_params(("parallel", "parallel", "arbitrary")),
    )(a, b)


def _col_block(i, *, c):
    return (i, c)


def _whole(i, *, nd):
    return (0,) * nd


def _row_specs(rows, ts):
    specs = []
    for arr, off, w in rows:
        assert off % w == 0 and arr.shape[0] % ts == 0
        specs.append(pl.BlockSpec((ts, w), functools.partial(_col_block, c=off // w)))
    return specs


def _row_fwd(fn, rows, params, outs, *, ts, name):
    s = rows[0][0].shape[0]
    ts = min(ts, s)
    nr, npar = len(rows), len(params)

    def body(*refs):
        rv = [r[...].astype(F32) for r in refs[:nr]]
        pv = [p[...] for p in refs[nr:nr + npar]]
        for o_ref, val in zip(refs[nr + npar:], fn(pv, rv)):
            o_ref[...] = val.astype(o_ref.dtype)

    in_specs = _row_specs(rows, ts) + [pl.BlockSpec(p.shape, functools.partial(_whole, nd=p.ndim)) for p in params]
    res = pl.pallas_call(
        body, name=name, grid=(s // ts,), in_specs=in_specs,
        out_specs=[pl.BlockSpec((ts, w), lambda i: (i, 0)) for w, _ in outs],
        out_shape=[jax.ShapeDtypeStruct((s, w), dt) for w, dt in outs],
        compiler_params=_params(("parallel",)),
    )(*[r[0] for r in rows], *params)
    return list(res)


def _row_bwd(fn, rows, params, cts, *, ts, name, row_grads=True):
    s = rows[0][0].shape[0]
    ts = min(ts, s)
    nr, npar, nc = len(rows), len(params), len(cts)
    n_dr = nr if row_grads else 0

    def body(*refs):
        rv = [r[...].astype(F32) for r in refs[:nr]]
        pv = [p[...] for p in refs[nr:nr + npar]]
        cv = [c[...].astype(F32) for c in refs[nr + npar:nr + npar + nc]]
        out_refs = refs[nr + npar + nc:]
        _, vjp = jax.vjp(lambda p, r: tuple(fn(p, r)), pv, rv)
        dp, dr = vjp(tuple(cv))
        for o_ref, val in zip(out_refs[:n_dr], dr):
            o_ref[...] = val

        @pl.when(pl.program_id(0) == 0)
        def _():
            for o_ref in out_refs[n_dr:]:
                o_ref[...] = jnp.zeros_like(o_ref)

        for o_ref, val in zip(out_refs[n_dr:], dp):
            o_ref[...] += val

    in_specs = (_row_specs(rows, ts) + [pl.BlockSpec(p.shape, functools.partial(_whole, nd=p.ndim)) for p in params]
                + _row_specs(cts, ts))
    out_specs = [pl.BlockSpec((ts, w), lambda i: (i, 0)) for _, _, w in rows[:n_dr]]
    out_specs += [pl.BlockSpec(p.shape, functools.partial(_whole, nd=p.ndim)) for p in params]
    out_shape = [jax.ShapeDtypeStruct((s, w), F32) for _, _, w in rows[:n_dr]]
    out_shape += [jax.ShapeDtypeStruct(p.shape, F32) for p in params]
    res = pl.pallas_call(
        body, name=name, grid=(s // ts,), in_specs=in_specs, out_specs=out_specs, out_shape=out_shape,
        compiler_params=_params(("arbitrary",)),
    )(*[r[0] for r in rows], *params, *[c[0] for c in cts])
    res = list(res)
    return res[:n_dr], res[n_dr:]


def _rms(x, g):
    return x * lax.rsqrt(jnp.mean(x * x, axis=-1, keepdims=True) + NORM_EPS) * g


def _gelu(x):
    return 0.5 * x * (1.0 + lax.erf(x * (2.0 ** -0.5)))


def _softplus(x):
    return jnp.maximum(x, 0.0) + jnp.log1p(jnp.exp(-jnp.abs(x)))


def _iota2(shape, dim):
    return lax.broadcasted_iota(jnp.int32, shape, dim)


def _fn_rms(pv, rv):
    return [_rms(rv[0], pv[0])]


def _fn_rms_keep(pv, rv):
    return [_rms(rv[0], pv[0]), rv[0]]


def _fn_resid_rms(pv, rv):
    return [rv[0] + _rms(rv[1], pv[0])]


def _fn_rms_branch(pv, rv):
    return [_rms(rv[0], pv[0])]


def _fn_merge(pv, rv):
    return [sum(jax.nn.sigmoid(rv[3 + i]) * rv[i] for i in range(3))]


def _fn_gmlp(pv, rv):
    ln_g, ln_b, w_sp, b_t = pv
    u = _gelu(rv[0])
    v = _gelu(rv[1])
    vc = v - jnp.mean(v, axis=-1, keepdims=True)
    v = vc * lax.rsqrt(jnp.mean(vc * vc, axis=-1, keepdims=True) + NORM_EPS) * ln_g + ln_b
    t = GM_CHUNK
    causal = _iota2((t, t), 1) <= _iota2((t, t), 0)
    first = _iota2((t, 128), 1) < 64
    expand = (_iota2((128, BRANCH_WIDTH), 0) == _iota2((128, BRANCH_WIDTH), 1) // 64).astype(F32)
    b_full = jnp.dot(b_t, expand, precision=_HI, preferred_element_type=F32)
    w_bf = [jnp.where(causal, w_sp[g], 0.0).astype(BF16) for g in range(GM_GROUPS)]
    chunks = []
    for c in range(rv[0].shape[0] // t):
        pairs = []
        for p in range(GM_GROUPS // 2):
            vp = v[c * t:(c + 1) * t, 128 * p:128 * (p + 1)].astype(BF16)
            m0 = jnp.dot(w_bf[2 * p], vp, preferred_element_type=F32)
            m1 = jnp.dot(w_bf[2 * p + 1], vp, preferred_element_type=F32)
            pairs.append(jnp.where(first, m0, m1))
        chunks.append(jnp.concatenate(pairs, axis=1) + b_full)
    return [u * jnp.concatenate(chunks, axis=0)]


def _head_expand(col0):
    return (_iota2((128, BRANCH_WIDTH), 0) == _iota2((128, BRANCH_WIDTH), 1) // DN_HEAD_DIM + col0).astype(F32)


def _fn_dn_in(pv, rv):
    conv_w, a_log, dt_b = pv
    c = sum(conv_w[j:j + 1, :] * rv[j] for j in range(4))
    a = c * jax.nn.sigmoid(c)
    outs = []
    for part in range(3):
        heads = []
        for h in range(DN_HEADS):
            lo = part * BRANCH_WIDTH + h * DN_HEAD_DIM
            xh = a[:, lo:lo + DN_HEAD_DIM]
            if part < 2:
                xh = xh * lax.rsqrt(jnp.sum(xh * xh, axis=-1, keepdims=True) + NORM_EPS)
            heads.append(xh)
        outs.append(jnp.concatenate(heads, axis=1))
    bd = rv[4]
    beta = jax.nn.sigmoid(bd)
    g = -jnp.exp(a_log) * _softplus(bd + dt_b)
    outs.append(jnp.dot(beta, _head_expand(0), precision=_HI, preferred_element_type=F32))
    outs.append(jnp.dot(g, _head_expand(DN_HEADS), precision=_HI, preferred_element_type=F32))
    return outs


def _fn_dn_out(pv, rv):
    heads = []
    for h in range(DN_HEADS):
        sl = slice(h * DN_HEAD_DIM, (h + 1) * DN_HEAD_DIM)
        z = rv[1][:, sl]
        heads.append(_rms(rv[0][:, sl], pv[0]) * (z * jax.nn.sigmoid(z)))
    return [jnp.concatenate(heads, axis=1)]


_DIMS = {"nn": (((1,), (0,)), ((), ())), "nt": (((1,), (1,)), ((), ())), "tn": (((0,), (0,)), ((), ()))}
_DIMS_BWD = {"nn": (("nt", "c", "b"), ("tn", "a", "c")), "nt": (("nn", "c", "b"), ("tn", "c", "a")),
             "tn": (("nt", "b", "c"), ("nn", "a", "c"))}


def _bf16_dot(a, b, kind):
    return lax.dot_general(a.astype(BF16), b.astype(BF16), _DIMS[kind], preferred_element_type=F32)


def _pdot_raw(a, b, kind, mode):
    if mode == 1:
        return _bf16_dot(a, b, kind)
    if mode == 6:
        return lax.dot_general(a, b, _DIMS[kind], precision=_HI, preferred_element_type=F32)
    b_hi, b_lo = _split_bf16(b)
    if mode == 3:
        a_hi, a_lo = _split_bf16(a)
        return _bf16_dot(a_hi, b_hi, kind) + (_bf16_dot(a_hi, b_lo, kind) + _bf16_dot(a_lo, b_hi, kind))
    b_rest = (b - b_hi.astype(F32) - b_lo.astype(F32)).astype(BF16)
    return _bf16_dot(a, b_hi, kind) + (_bf16_dot(a, b_lo, kind) + _bf16_dot(a, b_rest, kind))


@functools.partial(jax.custom_vjp, nondiff_argnums=(2, 3))
def _pdot(a, b, kind, mode):
    return _pdot_raw(a, b, kind, mode)


def _pdot_fwd(a, b, kind, mode):
    return _pdot_raw(a, b, kind, mode), (a, b)


def _pdot_bwd(kind, mode, res, ct):
    ops = {"a": res[0], "b": res[1], "c": ct}
    (ka, a1, a2), (kb, b1, b2) = _DIMS_BWD[kind]
    if mode == "count":
        return jnp.zeros_like(res[0]), _pdot(ops[b1], ops[b2], kb, mode)
    return _pdot(ops[a1], ops[a2], ka, mode), _pdot(ops[b1], ops[b2], kb, mode)


_pdot.defvjp(_pdot_fwd, _pdot_bwd)


@jax.custom_vjp
def _unit_lower_inverses(mats):
    c = mats[0].shape[0]
    row, col = _iota2((c, c), 0), _iota2((c, c), 1)
    x = [(row == col).astype(F32) for _ in mats]
    shift = 0
    while (1 << shift) < c:
        pair = jnp.right_shift(row, shift + 1) == jnp.right_shift(col, shift + 1)
        between = pair & (jnp.right_shift(row, shift) != jnp.right_shift(col, shift))
        q = [jnp.where(between, a, 0.0) for a in mats]
        qd = [_pdot_raw(qi, xi, "nn", DN_SOLVE_PASSES) for qi, xi in zip(q, x)]
        x = [xi - _pdot_raw(xi, m, "nn", DN_SOLVE_PASSES) for xi, m in zip(x, qd)]
        shift += 1
    return tuple(x)


def _unit_lower_inverses_fwd(mats):
    x = _unit_lower_inverses(mats)
    return x, x


def _unit_lower_inverses_bwd(x, dx):
    inner = [_pdot_raw(d, xi, "nt", DN_SOLVE_PASSES) for d, xi in zip(dx, x)]
    return (tuple(-_pdot_raw(xi, m, "tn", DN_SOLVE_PASSES) for xi, m in zip(x, inner)),)


_unit_lower_inverses.defvjp(_unit_lower_inverses_fwd, _unit_lower_inverses_bwd)


def _head(h):
    return slice(h * DN_HEAD_DIM, (h + 1) * DN_HEAD_DIM)


def _delta_chunk(states, q, k, v, beta, g):
    c = DN_BLOCK
    heads = range(DN_HEADS)
    row, col = _iota2((c, c), 0), _iota2((c, c), 1)
    tri, strict = col <= row, col < row
    counts = jnp.concatenate([tri.astype(BF16), jnp.ones((c, c), BF16)], axis=0)
    sums = _pdot(counts, g, "nn", "count")
    gc = [sums[:c, _head(h)] for h in heads]
    gl = [sums[c:, _head(h)] for h in heads]
    qh, kh, vh, bh = ([t[:, _head(h)] for h in heads] for t in (q * (DN_HEAD_DIM ** -0.5), k, v, beta))
    decay = [jnp.where(tri, jnp.exp(jnp.where(tri, gc[h] - gc[h].T, 0.0)), 0.0) for h in heads]
    solve, carry, out = DN_SOLVE_PASSES, DN_STATE_PASSES, DN_OUT_PASSES
    kk = [_pdot(kh[h], kh[h], "nt", solve) for h in heads]
    x = _unit_lower_inverses(tuple(jnp.where(strict, bh[h] * kk[h] * decay[h], 0.0) for h in heads))
    eg = [jnp.exp(gc[h]) for h in heads]
    u = [_pdot(x[h], vh[h] * bh[h], "nn", solve) for h in heads]
    wk = [_pdot(x[h], kh[h] * (bh[h] * eg[h]), "nn", solve) for h in heads]
    qk = [jnp.where(tri, _pdot(qh[h], kh[h], "nt", out) * decay[h], 0.0) for h in heads]
    v_new = [u[h] - _pdot(wk[h], states[h], "nn", carry) for h in heads]
    o = [_pdot(qh[h] * eg[h], states[h], "nn", out) + _pdot(qk[h], v_new[h], "nn", out) for h in heads]
    nxt = [states[h] * jnp.exp(gl[h]) + _pdot(kh[h] * jnp.exp(gl[h] - gc[h]), v_new[h], "tn", carry)
           for h in heads]
    return jnp.concatenate(o, axis=1), tuple(nxt)


def _delta_fwd(q, k, v, beta, g, *, name):
    s = q.shape[0]
    c = DN_BLOCK
    nc = s // c

    def body(q_ref, k_ref, v_ref, b_ref, g_ref, o_ref, sp_ref, st):
        @pl.when(pl.program_id(0) == 0)
        def _():
            st[...] = jnp.zeros_like(st)

        states = tuple(st[h] for h in range(DN_HEADS))
        for h in range(DN_HEADS):
            sp_ref[0, h] = states[h]
        o, nxt = _delta_chunk(states, q_ref[...], k_ref[...], v_ref[...], b_ref[...], g_ref[...])
        for h in range(DN_HEADS):
            st[h] = nxt[h]
        o_ref[...] = o

    blk = pl.BlockSpec((c, BRANCH_WIDTH), lambda n: (n, 0))
    return pl.pallas_call(
        body, name=name, grid=(nc,), in_specs=[blk] * 5,
        out_specs=[blk, pl.BlockSpec((1, DN_HEADS, DN_HEAD_DIM, DN_HEAD_DIM), lambda n: (n, 0, 0, 0))],
        out_shape=[jax.ShapeDtypeStruct((s, BRANCH_WIDTH), F32),
                   jax.ShapeDtypeStruct((nc, DN_HEADS, DN_HEAD_DIM, DN_HEAD_DIM), F32)],
        scratch_shapes=[pltpu.VMEM((DN_HEADS, DN_HEAD_DIM, DN_HEAD_DIM), F32)],
        compiler_params=_params(("arbitrary",)),
    )(q, k, v, beta, g)


def _delta_bwd(q, k, v, beta, g, states, do, *, name):
    s = q.shape[0]
    c = DN_BLOCK
    nc = s // c

    def body(q_ref, k_ref, v_ref, b_ref, g_ref, sp_ref, do_ref, dq_ref, dk_ref, dv_ref, db_ref, dg_ref, dst):
        @pl.when(pl.program_id(0) == 0)
        def _():
            dst[...] = jnp.zeros_like(dst)

        states = tuple(sp_ref[0, h] for h in range(DN_HEADS))
        _, vjp = jax.vjp(_delta_chunk, states, q_ref[...], k_ref[...], v_ref[...], b_ref[...], g_ref[...])
        d = vjp((do_ref[...], tuple(dst[h] for h in range(DN_HEADS))))
        for h in range(DN_HEADS):
            dst[h] = d[0][h]
        for o_ref, val in zip((dq_ref, dk_ref, dv_ref, db_ref, dg_ref), d[1:]):
            o_ref[...] = val

    blk = pl.BlockSpec((c, BRANCH_WIDTH), lambda n: (nc - 1 - n, 0))
    res = pl.pallas_call(
        body, name=name, grid=(nc,),
        in_specs=[blk] * 5 + [pl.BlockSpec((1, DN_HEADS, DN_HEAD_DIM, DN_HEAD_DIM), lambda n: (nc - 1 - n, 0, 0, 0)), blk],
        out_specs=[blk] * 5, out_shape=[jax.ShapeDtypeStruct((s, BRANCH_WIDTH), F32)] * 5,
        scratch_shapes=[pltpu.VMEM((DN_HEADS, DN_HEAD_DIM, DN_HEAD_DIM), F32)],
        compiler_params=_params(("arbitrary",)),
    )(q, k, v, beta, g, states, do)
    return list(res)


def _split_bf16(x):
    hi = x.astype(BF16)
    return hi, (x - hi.astype(F32)).astype(BF16)


def _sb_consts():
    tq, tk = SB_QUERY_ROWS, SB_BLOCK
    row, col = _iota2((tk, tk), 0), _iota2((tk, tk), 1)
    ones = jnp.ones((tk, tk), BF16)
    later = jnp.concatenate([(row > col).astype(BF16), ones], axis=1)
    from_here = jnp.concatenate([(row >= col).astype(BF16), ones], axis=1)
    first = _iota2((tq, 128), 1) < SB_HEAD_DIM
    return later, from_here, first


def _sb_causal(d):
    tq, tk = SB_QUERY_ROWS, SB_BLOCK
    return _iota2((tq, tk), 1) + d * tk < _iota2((tq, tk), 0)


def _sums(x, mat):
    hi, lo = _split_bf16(x)
    return jnp.dot(hi, mat, preferred_element_type=F32) + jnp.dot(lo, mat, preferred_element_type=F32)


def _sb_weights(qs, kb, accs, later, causal):
    tk = SB_BLOCK
    z = [lax.dot_general(qh, kb, (((1,), (1,)), ((), ())), preferred_element_type=F32) for qh in qs]
    lk = [-_softplus(zh) for zh in z]
    if causal is not None:
        lk = [jnp.where(causal, v, 0.0) for v in lk]
    cs = [_sums(v, later) for v in lk]
    e = [z[h] + lk[h] + cs[h][:, :tk] + accs[h] for h in range(2)]
    if causal is not None:
        e = [jnp.where(causal, v, -1e30) for v in e]
    return lk, [jnp.exp(v) for v in e], [v[:, tk:] for v in cs]


def _sb_alive(accs):
    return jnp.max(jnp.maximum(accs[0], accs[1])) > SB_DEAD_LOG


def _sb_sweep(i, block, carry, accs_of, stop=None):
    per = SB_QUERY_ROWS // SB_BLOCK
    for d in reversed(range(per)):
        carry = block(i * per + d, carry, _sb_causal(d))
    if stop is not None:
        return lax.fori_loop(0, i * per - 1 - stop, lambda jj, cr: block(i * per - 1 - jj, cr, None), carry)

    def step(state):
        j, _, cr = state
        cr = block(j, cr, None)
        return j - 1, _sb_alive(accs_of(cr)), cr

    j, _, carry = lax.while_loop(lambda st: jnp.logical_and(st[0] >= 0, st[1]), step,
                                 (i * per - 1, _sb_alive(accs_of(carry)), carry))
    return carry, j


def _sb_fwd(p, *, name):
    s = p.shape[0]
    t, tk = SB_QUERY_ROWS, SB_BLOCK
    scale = SB_HEAD_DIM ** -0.5

    def body(q_ref, k_ref, v_ref, o_ref):
        i = pl.program_id(1)
        later, _, first = _sb_consts()
        q = q_ref[...] * scale
        qs = (jnp.where(first, q, 0.0).astype(BF16), jnp.where(first, 0.0, q).astype(BF16))

        def block(j, carry, causal):
            start = pl.multiple_of(j * tk, tk)
            kb = k_ref[pl.ds(start, tk), :].astype(BF16)
            vb = v_ref[pl.ds(start, tk), :].astype(BF16)
            _, w, tot = _sb_weights(qs, kb, [carry[h][1] for h in range(2)], later, causal)
            out = [jnp.dot(w[h].astype(BF16), vb, preferred_element_type=F32) for h in range(2)]
            return tuple((carry[h][0] + out[h], carry[h][1] + tot[h]) for h in range(2))

        zero = jnp.zeros((t, 128), F32)
        carry, _ = _sb_sweep(i, block, ((zero, zero), (zero, zero)), lambda cr: (cr[0][1], cr[1][1]))
        o_ref[...] = jnp.where(first, carry[0][0], carry[1][0])

    return pl.pallas_call(
        body, name=name, grid=(BRANCH_WIDTH // 128, s // t),
        in_specs=[pl.BlockSpec((t, 128), lambda pr, i: (i, OFF_CQ // 128 + pr)),
                  pl.BlockSpec((s, 128), lambda pr, i: (0, OFF_CK // 128 + pr)),
                  pl.BlockSpec((s, 128), lambda pr, i: (0, OFF_CV // 128 + pr))],
        out_specs=pl.BlockSpec((t, 128), lambda pr, i: (i, pr)),
        out_shape=jax.ShapeDtypeStruct((s, BRANCH_WIDTH), F32),
        compiler_params=_params(("arbitrary", "arbitrary")),
    )(p, p, p)


def _sb_bwd(p, do, *, name):
    s = p.shape[0]
    t, tk = SB_QUERY_ROWS, SB_BLOCK
    scale = SB_HEAD_DIM ** -0.5

    def body(q_ref, k_ref, v_ref, do_ref, dq_ref, dk_ref, dv_ref):
        i = pl.program_id(1)

        @pl.when(i == 0)
        def _():
            dk_ref[...] = jnp.zeros_like(dk_ref)
            dv_ref[...] = jnp.zeros_like(dv_ref)

        later, from_here, first = _sb_consts()
        q = q_ref[...] * scale
        do = do_ref[...]
        qs = (jnp.where(first, q, 0.0).astype(BF16), jnp.where(first, 0.0, q).astype(BF16))
        dos = (jnp.where(first, do, 0.0).astype(BF16), jnp.where(first, 0.0, do).astype(BF16))

        def total(j, carry, causal):
            start = pl.multiple_of(j * tk, tk)
            kb = k_ref[pl.ds(start, tk), :].astype(BF16)
            vb = v_ref[pl.ds(start, tk), :].astype(BF16)
            _, w, tot = _sb_weights(qs, kb, [carry[h][0] for h in range(2)], later, causal)
            dw = [lax.dot_general(dos[h], vb, (((1,), (1,)), ((), ())), preferred_element_type=F32) for h in range(2)]
            tde = [_sums(dw[h] * w[h], from_here)[:, tk:] for h in range(2)]
            return tuple((carry[h][0] + tot[h], carry[h][1] + tde[h]) for h in range(2))

        zero = jnp.zeros((t, 128), F32)
        sums, stop = _sb_sweep(i, total, ((zero, zero), (zero, zero)), lambda cr: (cr[0][0], cr[1][0]))
        deltas = (sums[0][1], sums[1][1])

        def block(j, carry, causal):
            start = pl.multiple_of(j * tk, tk)
            kb = k_ref[pl.ds(start, tk), :].astype(BF16)
            vb = v_ref[pl.ds(start, tk), :].astype(BF16)
            both = range(2)
            tn = (((0,), (0,)), ((), ()))
            lk, w, tot = _sb_weights(qs, kb, [carry[h][1] for h in both], later, causal)
            dw = [lax.dot_general(dos[h], vb, (((1,), (1,)), ((), ())), preferred_element_type=F32) for h in both]
            de = [dw[h] * w[h] for h in both]
            cs = [_sums(de[h], from_here) for h in both]
            keep = [jnp.exp(lk[h]) for h in both]
            dz = [de[h] * keep[h] - (deltas[h] - (cs[h][:, :tk] + carry[h][2])) * (1.0 - keep[h]) for h in both]
            if causal is not None:
                dz = [jnp.where(causal, v, 0.0) for v in dz]
            dzb = [v.astype(BF16) for v in dz]
            dq = [jnp.dot(dzb[h], kb, preferred_element_type=F32) for h in both]
            dk = [lax.dot_general(dzb[h], qs[h], tn, preferred_element_type=F32) for h in both]
            dv = [lax.dot_general(w[h].astype(BF16), dos[h], tn, preferred_element_type=F32) for h in both]
            dk_ref[pl.ds(start, tk), :] += dk[0] + dk[1]
            dv_ref[pl.ds(start, tk), :] += dv[0] + dv[1]
            return tuple((carry[h][0] + dq[h], carry[h][1] + tot[h], carry[h][2] + cs[h][:, tk:]) for h in both)

        carry = _sb_sweep(i, block, ((zero, zero, zero), (zero, zero, zero)), None, stop=stop)
        dq_ref[...] = jnp.where(first, carry[0][0], carry[1][0]) * scale

    qblk = lambda off: pl.BlockSpec((t, 128), lambda pr, i: (i, off // 128 + pr))
    full = lambda off: pl.BlockSpec((s, 128), lambda pr, i: (0, off // 128 + pr))
    res = pl.pallas_call(
        body, name=name, grid=(BRANCH_WIDTH // 128, s // t),
        in_specs=[qblk(OFF_CQ), full(OFF_CK), full(OFF_CV), qblk(0)],
        out_specs=[qblk(0), full(0), full(0)],
        out_shape=[jax.ShapeDtypeStruct((s, BRANCH_WIDTH), F32)] * 3,
        compiler_params=_params(("arbitrary", "arbitrary")),
    )(p, p, p, do)
    return list(res)


def _loss_head(y, target, *, name, ts=512):
    s, d = y.shape
    ts = min(ts, s)

    def body(y_ref, t_ref, sq_ref, dy_ref):
        @pl.when(pl.program_id(0) == 0)
        def _():
            sq_ref[...] = jnp.zeros_like(sq_ref)

        err = y_ref[...] - t_ref[...]
        dy_ref[...] = err * (1.0 / d)
        tot = jnp.sum(jnp.sum(err * err, axis=1, keepdims=True), axis=0, keepdims=True)
        sq_ref[...] += jnp.broadcast_to(tot, sq_ref.shape)

    blk = pl.BlockSpec((ts, d), lambda i: (i, 0))
    return pl.pallas_call(
        body, name=name, grid=(s // ts,), in_specs=[blk, blk],
        out_specs=[pl.BlockSpec((1, 128), lambda i: (0, 0)), blk],
        out_shape=[jax.ShapeDtypeStruct((1, 128), F32), jax.ShapeDtypeStruct((s, d), F32)],
        compiler_params=_params(("arbitrary",)),
    )(y, target)


def _row_tile(r, limit=512):
    return max(t for t in range(8, limit + 1, 8) if r % t == 0)


def _adamw(w, g, m, v, *, name):
    shape = w.shape
    lanes = shape[-1]
    w, g, m, v = (a.reshape(-1, lanes) for a in (w, g, m, v))
    r = w.shape[0]
    ts = r if r <= 256 else _row_tile(r, 256 if lanes > LANES else 512)

    def body(w_ref, g_ref, m_ref, v_ref, d_ref, nm_ref, nv_ref):
        gv = g_ref[...]
        m_new = ADAM_B1 * m_ref[...] + (1.0 - ADAM_B1) * gv
        v_new = ADAM_B2 * v_ref[...] + (1.0 - ADAM_B2) * jnp.square(gv)
        m_hat = m_new / (1.0 - ADAM_B1 ** ADAM_STEP)
        v_hat = v_new / (1.0 - ADAM_B2 ** ADAM_STEP)
        d_ref[...] = -ADAM_LR * (m_hat / (jnp.sqrt(v_hat) + ADAM_EPS) + ADAM_WD * w_ref[...])
        nm_ref[...] = m_new
        nv_ref[...] = v_new

    blk = pl.BlockSpec((ts, lanes), lambda i: (i, 0))
    res = pl.pallas_call(
        body, name=name, grid=(r // ts,), in_specs=[blk] * 4, out_specs=[blk] * 3,
        out_shape=[jax.ShapeDtypeStruct((r, lanes), F32)] * 3,
        compiler_params=_params(("parallel",)),
    )(w, g, m, v)
    return [a.reshape(shape) for a in res]


def _add_rows(terms, *, name, out_dtype=F32):
    r = terms[0].shape[0]
    ts = _row_tile(r)

    def body(*refs):
        acc = refs[0][...].astype(F32)
        for ref in refs[1:-1]:
            acc = acc + ref[...].astype(F32)
        refs[-1][...] = acc.astype(out_dtype)

    blk = pl.BlockSpec((ts, LANES), lambda i: (i, 0))
    return pl.pallas_call(
        body, name=name, grid=(r // ts,), in_specs=[blk] * len(terms), out_specs=blk,
        out_shape=jax.ShapeDtypeStruct((r, LANES), out_dtype), compiler_params=_params(("parallel",)),
    )(*terms)


def _col_sums(a, *, name, ts=256):
    s, n = a.shape
    ts = min(ts, s)

    def body(a_ref, o_ref):
        @pl.when(pl.program_id(0) == 0)
        def _():
            o_ref[...] = jnp.zeros_like(o_ref)

        o_ref[...] += jnp.sum(a_ref[...], axis=0, keepdims=True)

    return pl.pallas_call(
        body, name=name, grid=(s // ts,), in_specs=[pl.BlockSpec((ts, n), lambda i: (i, 0))],
        out_specs=pl.BlockSpec((1, n), lambda i: (0, 0)), out_shape=jax.ShapeDtypeStruct((1, n), F32),
        compiler_params=_params(("arbitrary",)),
    )(a)


_HBM = pl.BlockSpec(memory_space=pltpu.HBM)
_MESH = pl.DeviceIdType.MESH


def _other_chips(x, y):
    return [(1 - x, y), (x, 1 - y), (1 - x, 1 - y)]


def _gather_chips(shard, *, name):
    r, lanes = shard.shape
    half = r // 2
    assert half * 2 == r

    def body(in_ref, out_ref, send_sems, recv_sems):
        x, y, c = lax.axis_index("x"), lax.axis_index("y"), lax.axis_index("c")
        me = 2 * x + y
        sibling = (x, y, 1 - c)
        chips = _other_chips(x, y)

        def copy(sem, chip, core_half, to):
            rows = out_ref.at[chip, pl.ds(core_half * half, half)]
            return pltpu.make_async_remote_copy(src_ref=rows, dst_ref=rows, send_sem=send_sems.at[sem],
                                                recv_sem=recv_sems.at[sem], device_id=to, device_id_type=_MESH)

        first = []
        for kk, (px, py) in enumerate(chips):
            cp = pltpu.make_async_remote_copy(
                src_ref=in_ref.at[pl.ds(c * half, half)], dst_ref=out_ref.at[me, pl.ds(c * half, half)],
                send_sem=send_sems.at[kk], recv_sem=recv_sems.at[kk], device_id=(px, py, c), device_id_type=_MESH)
            cp.start()
            first.append(cp)
        passed = [copy(3 + kk, 2 * px + py, c, sibling) for kk, (px, py) in enumerate(chips)]
        for kk, (px, py) in enumerate(chips):
            copy(kk, 2 * px + py, c, (px, py, c)).wait_recv()
            passed[kk].start()
        for kk, (px, py) in enumerate(chips):
            copy(3 + kk, 2 * px + py, 1 - c, sibling).wait_recv()
        for cp in first + passed:
            cp.wait_send()

    gathered = pl.pallas_call(
        body, name=name, in_specs=[_HBM], out_specs=_HBM,
        out_shape=jax.ShapeDtypeStruct((4, r, lanes), shard.dtype),
        scratch_shapes=[pltpu.SemaphoreType.DMA((6,)), pltpu.SemaphoreType.DMA((6,))],
    )(shard)
    me = 2 * lax.axis_index("x") + lax.axis_index("y")
    return jnp.where(lax.broadcasted_iota(jnp.int32, (4, 1, 1), 0) == me, shard[None], gathered)


def _to_sibling(block, *, name):
    def body(in_ref, out_ref, send_sem, recv_sem):
        x, y, c = lax.axis_index("x"), lax.axis_index("y"), lax.axis_index("c")
        cp = pltpu.make_async_remote_copy(src_ref=in_ref, dst_ref=out_ref, send_sem=send_sem, recv_sem=recv_sem,
                                          device_id=(x, y, 1 - c), device_id_type=_MESH)
        cp.start()
        cp.wait()

    return pl.pallas_call(
        body, name=name, in_specs=[_HBM], out_specs=_HBM, out_shape=jax.ShapeDtypeStruct(block.shape, block.dtype),
        scratch_shapes=[pltpu.SemaphoreType.DMA, pltpu.SemaphoreType.DMA],
    )(block)


def _halves_to_sibling(blocks, *, name):
    n, r, lanes = blocks.shape
    half = r // 2

    def body(in_ref, out_ref, send_sem, recv_sem):
        x, y, c = lax.axis_index("x"), lax.axis_index("y"), lax.axis_index("c")
        cp = pltpu.make_async_remote_copy(src_ref=in_ref.at[pl.ds(0, n), pl.ds((1 - c) * half, half)], dst_ref=out_ref,
                                          send_sem=send_sem, recv_sem=recv_sem, device_id=(x, y, 1 - c),
                                          device_id_type=_MESH)
        cp.start()
        cp.wait()

    return pl.pallas_call(
        body, name=name, in_specs=[_HBM], out_specs=_HBM, out_shape=jax.ShapeDtypeStruct((n, half, lanes), blocks.dtype),
        scratch_shapes=[pltpu.SemaphoreType.DMA, pltpu.SemaphoreType.DMA],
    )(blocks)


def _add_own_half(blocks, got, *, name, out_dtype):
    n, r, lanes = blocks.shape
    half = r // 2
    ts = _row_tile(half)
    steps = half // ts
    core = lax.axis_index("c").astype(jnp.int32).reshape(1)

    def body(core_ref, a_ref, b_ref, o_ref):
        o_ref[...] = (a_ref[...] + b_ref[...]).astype(out_dtype)

    return pl.pallas_call(
        body, name=name,
        grid_spec=pltpu.PrefetchScalarGridSpec(
            num_scalar_prefetch=1, grid=(n, steps),
            in_specs=[pl.BlockSpec((1, ts, lanes), lambda k, i, core_ref: (k, core_ref[0] * steps + i, 0)),
                      pl.BlockSpec((1, ts, lanes), lambda k, i, core_ref: (k, i, 0))],
            out_specs=pl.BlockSpec((1, ts, lanes), lambda k, i, core_ref: (k, i, 0))),
        out_shape=jax.ShapeDtypeStruct((n, half, lanes), out_dtype),
        compiler_params=_params(("parallel", "parallel")),
    )(core, blocks, got)


def _scatter_chips(blocks, *, name):
    _, r, lanes = blocks.shape

    def body(in_ref, out_ref, send_sems, recv_sems, local_sem):
        x, y, c = lax.axis_index("x"), lax.axis_index("y"), lax.axis_index("c")
        me = 2 * x + y
        mine = pltpu.make_async_copy(in_ref.at[me], out_ref.at[me], local_sem)
        mine.start()
        copies = []
        for kk, (px, py) in enumerate(_other_chips(x, y)):
            cp = pltpu.make_async_remote_copy(src_ref=in_ref.at[2 * px + py], dst_ref=out_ref.at[me],
                                              send_sem=send_sems.at[kk], recv_sem=recv_sems.at[kk],
                                              device_id=(px, py, c), device_id_type=_MESH)
            cp.start()
            copies.append(cp)
        for kk, (px, py) in enumerate(_other_chips(x, y)):
            pltpu.make_async_remote_copy(src_ref=in_ref.at[me], dst_ref=out_ref.at[2 * px + py],
                                         send_sem=send_sems.at[kk], recv_sem=recv_sems.at[kk], device_id=(px, py, c),
                                         device_id_type=_MESH).wait_recv()
        for cp in copies:
            cp.wait_send()
        mine.wait()

    return pl.pallas_call(
        body, name=name, in_specs=[_HBM], out_specs=_HBM, out_shape=jax.ShapeDtypeStruct((4, r, lanes), blocks.dtype),
        scratch_shapes=[pltpu.SemaphoreType.DMA((3,)), pltpu.SemaphoreType.DMA((3,)), pltpu.SemaphoreType.DMA],
    )(blocks)


SHARDED = (("norm_g", 2), ("w_in", 2), ("conv_w", 2), ("w_branch", 3), ("w_out", 1), ("w_ff1", 2), ("w_ff2", 1))
MATMUL_WEIGHTS = ("w_in", "w_branch", "w_out", "w_ff1", "w_ff2")
VECTOR_WEIGHTS = ("norm_g", "conv_w")
REPLICATED = ("b_in", "sgu_ln_g", "sgu_ln_b", "w_spatial", "b_spatial", "a_log", "dt_bias", "dn_norm_g")
WEIGHT_ORDER = ("norm_g", "w_in", "b_in", "sgu_ln_g", "sgu_ln_b", "w_spatial", "b_spatial", "conv_w", "a_log",
                "dt_bias", "dn_norm_g", "w_branch", "w_out", "w_ff1", "w_ff2")
PACK_ROW_MULTIPLE = 32


def _rows_of(shape):
    n = 1
    for dim in shape:
        n *= dim
    return -(-n // LANES)


def _pack(arrays):
    parts = []
    for a in arrays:
        flat = a.reshape(-1)
        pad = _rows_of(a.shape) * LANES - flat.shape[0]
        if pad:
            flat = jnp.concatenate([flat, jnp.zeros((pad,), flat.dtype)])
        parts.append(flat.reshape(-1, LANES))
    rows = sum(p.shape[0] for p in parts)
    pad = -rows % PACK_ROW_MULTIPLE
    if pad:
        parts.append(jnp.zeros((pad, LANES), parts[0].dtype))
    return jnp.concatenate(parts, axis=0)


def _unpack(buf, shapes):
    out, row = [], 0
    for shape in shapes:
        n = 1
        for dim in shape:
            n *= dim
        rows = _rows_of(shape)
        out.append(buf[row:row + rows].reshape(-1)[:n].reshape(shape))
        row += rows
    return out


def _chip_slice(a, axis, k):
    size = a.shape[axis] // 4
    return lax.slice_in_dim(a, k * size, (k + 1) * size, axis=axis)


def _rearrange_w_in(w):
    pad = jnp.zeros(w.shape[:-1] + (P_PAD - P_IN,), w.dtype)
    return jnp.concatenate([w[..., 0:3072], w[..., 4616:7688], w[..., 3080:4616], w[..., 3072:3080], pad], axis=-1)


def _restore_w_in(w):
    return jnp.concatenate([w[..., 0:3072], w[..., 7680:7688], w[..., 6144:7680], w[..., 3072:6144]], axis=-1)


def _shift_rows(a, n):
    if n == 0:
        return a
    return jnp.concatenate([jnp.zeros((n, a.shape[1]), a.dtype), a[:-n]], axis=0)


def _unshift_rows(a, n):
    if n == 0:
        return a
    return jnp.concatenate([a[n:], jnp.zeros((n, a.shape[1]), a.dtype)], axis=0)


def _layer_params(wl):
    row = lambda v: v.reshape(1, -1)
    pad128 = lambda v, at: jnp.pad(v, (at, 128 - at - v.shape[0])).reshape(1, 128)
    return dict(
        g=[row(wl["norm_g"][i]) for i in range(4)],
        gmlp=[row(wl["sgu_ln_g"]), row(wl["sgu_ln_b"]), wl["w_spatial"],
              jnp.pad(wl["b_spatial"].T, ((0, 0), (0, 128 - GM_GROUPS)))],
        dn_in=[wl["conv_w"], pad128(wl["a_log"], DN_HEADS), pad128(wl["dt_bias"], DN_HEADS)],
        dn_g=[row(wl["dn_norm_g"])],
    )


def _layer_fwd(x0, wl, l):
    tag = lambda s: f"{s}_l{l}"
    pr = _layer_params(wl)
    w = BRANCH_WIDTH
    h0 = _row_fwd(_fn_rms, [(x0, 0, D_MODEL)], [pr["g"][0]], [(D_MODEL, BF16)], ts=512, name=tag("rms0"))[0]
    p = _mm(h0, wl["w_in"], bias=wl["b_in"].reshape(1, -1), name=tag("proj_in"))
    ya = _row_fwd(_fn_gmlp, [(p, OFF_AU, w), (p, OFF_AV, w)], pr["gmlp"], [(w, BF16)], ts=256, name=tag("gmlp"))[0]
    xq = p[:, OFF_BQ:OFF_BQ + 3 * w]
    shifted = [_shift_rows(xq, 3 - j) for j in range(4)]
    dn_rows = [(a, 0, 3 * w) for a in shifted] + [(p, OFF_BD, 128)]
    q, k, v, beta, g = _row_fwd(_fn_dn_in, dn_rows, pr["dn_in"], [(w, F32)] * 5, ts=256, name=tag("dn_in"))
    o, states = _delta_fwd(q, k, v, beta, g, name=tag("delta"))
    yb = _row_fwd(_fn_dn_out, [(o, 0, w), (p, OFF_BZ, w)], pr["dn_g"], [(w, BF16)], ts=512, name=tag("dn_out"))[0]
    yc = _sb_fwd(p, name=tag("sb"))
    ys = [ya, yb, yc]
    proj = [_mm(ys[i], wl["w_branch"][i], name=tag(f"branch{i}")) for i in range(3)]
    merge_rows = [(a, 0, D_MODEL) for a in proj] + [(p, OFF_GATE + i * D_MODEL, D_MODEL) for i in range(3)]
    m = _row_fwd(_fn_merge, merge_rows, [], [(D_MODEL, BF16)], ts=256, name=tag("merge"))[0]
    mixed = _mm(m, wl["w_out"], name=tag("out"))
    x1 = _row_fwd(_fn_resid_rms, [(x0, 0, D_MODEL), (mixed, 0, D_MODEL)], [pr["g"][1]], [(D_MODEL, F32)], ts=512,
                  name=tag("resid1"))[0]
    h2 = _row_fwd(_fn_rms, [(x1, 0, D_MODEL)], [pr["g"][2]], [(D_MODEL, BF16)], ts=512, name=tag("rms2"))[0]
    a, r = _mm(h2, wl["w_ff1"], name=tag("ff1"), out_dtype=(BF16, BF16),
               finish=lambda t: (jnp.maximum(t, 0.0), jnp.square(jnp.maximum(t, 0.0))))
    f = _mm(r, wl["w_ff2"], name=tag("ff2"))
    x2 = _row_fwd(_fn_resid_rms, [(x1, 0, D_MODEL), (f, 0, D_MODEL)], [pr["g"][3]], [(D_MODEL, F32)], ts=512,
                  name=tag("resid2"))[0]
    saved = dict(x0=x0, h0=h0, p=p, shifted=shifted, q=q, k=k, v=v, beta=beta, g=g, states=states, o=o, ys=ys,
                 proj=proj, m=m, mixed=mixed, x1=x1, h2=h2, a=a, r=r, f=f)
    return x2, saved


def _layer_bwd(dx2, sv, wl, l):
    tag = lambda s: f"{s}_l{l}"
    pr = _layer_params(wl)
    w = BRANCH_WIDTH
    full = lambda a: (a, 0, a.shape[1])
    p = sv["p"]
    (df,), (dg3,) = _row_bwd(_fn_rms_branch, [full(sv["f"])], [pr["g"][3]], [full(dx2)], ts=512, name=tag("resid2_b"))
    da = _mm(df, wl["w_ff2"], trans_b=True, name=tag("ff2_dx"), out_dtype=BF16, beside=(sv["a"],),
             finish=lambda t, relu_a: 2.0 * relu_a.astype(F32) * t)
    dw_ff2 = _mm_tn(sv["r"], df, name=tag("ff2_dw"))
    dh2 = _mm(da, wl["w_ff1"], trans_b=True, name=tag("ff1_dx"))
    dw_ff1 = _mm_tn(sv["h2"], da, name=tag("ff1_dw"))
    (dx1,), (dg2,) = _row_bwd(_fn_rms_keep, [full(sv["x1"])], [pr["g"][2]], [full(dh2), full(dx2)], ts=512,
                              name=tag("rms2_b"))
    (dmixed,), (dg1,) = _row_bwd(_fn_rms_branch, [full(sv["mixed"])], [pr["g"][1]], [full(dx1)], ts=512,
                                 name=tag("resid1_b"))
    dm = _mm(dmixed, wl["w_out"], trans_b=True, name=tag("out_dx"))
    dw_out = _mm_tn(sv["m"], dmixed, name=tag("out_dw"))
    merge_rows = [full(a) for a in sv["proj"]] + [(p, OFF_GATE + i * D_MODEL, D_MODEL) for i in range(3)]
    dmerge, _ = _row_bwd(_fn_merge, merge_rows, [], [full(dm)], ts=256, name=tag("merge_b"))
    dproj, dgates = dmerge[:3], dmerge[3:]
    dys = [_mm(dproj[i], wl["w_branch"][i], trans_b=True, name=tag(f"branch{i}_dx")) for i in range(3)]
    dw_branch = jnp.stack([_mm_tn(sv["ys"][i], dproj[i], name=tag(f"branch{i}_dw")) for i in range(3)])
    (du, dv_a), dgm = _row_bwd(_fn_gmlp, [(p, OFF_AU, w), (p, OFF_AV, w)], pr["gmlp"], [full(dys[0])], ts=256,
                               name=tag("gmlp_b"))
    (do, dz), (d_dn_g,) = _row_bwd(_fn_dn_out, [full(sv["o"]), (p, OFF_BZ, w)], pr["dn_g"], [full(dys[1])], ts=512,
                                   name=tag("dn_out_b"))
    dqkvbg = _delta_bwd(sv["q"], sv["k"], sv["v"], sv["beta"], sv["g"], sv["states"], do, name=tag("delta_b"))
    dn_rows = [full(a) for a in sv["shifted"]] + [(p, OFF_BD, 128)]
    d_in, d_dn_in = _row_bwd(_fn_dn_in, dn_rows, pr["dn_in"], [full(a) for a in dqkvbg], ts=256, name=tag("dn_in_b"))
    dxq = sum(_unshift_rows(d_in[j], 3 - j) for j in range(4))
    dcq, dck, dcv = _sb_bwd(p, dys[2], name=tag("sb_b"))
    s = p.shape[0]
    dp = jnp.concatenate([du, dv_a, dxq, dz] + dgates + [dcq, dck, dcv, d_in[4],
                                                         jnp.zeros((s, P_PAD - OFF_BD - 128), F32)], axis=1)
    dh0 = _mm(dp, wl["w_in"], trans_b=True, name=tag("proj_in_dx"))
    dw_in = _mm_tn(sv["h0"], dp, name=tag("proj_in_dw"))
    db_in = _col_sums(dp, name=tag("bias_b"))
    (dx0,), (dg0,) = _row_bwd(_fn_rms_keep, [full(sv["x0"])], [pr["g"][0]], [full(dh0), full(dx1)], ts=512,
                              name=tag("rms0_b"))
    grads = dict(
        norm_g=jnp.concatenate([dg0, dg1, dg2, dg3], axis=0), w_in=dw_in, b_in=db_in.reshape(-1),
        sgu_ln_g=dgm[0].reshape(-1), sgu_ln_b=dgm[1].reshape(-1), w_spatial=dgm[2],
        b_spatial=dgm[3][:, :GM_GROUPS].T, conv_w=d_dn_in[0], a_log=d_dn_in[1][0, DN_HEADS:2 * DN_HEADS],
        dt_bias=d_dn_in[2][0, DN_HEADS:2 * DN_HEADS], dn_norm_g=d_dn_g.reshape(-1), w_branch=dw_branch,
        w_out=dw_out, w_ff1=dw_ff1, w_ff2=dw_ff2)
    return dx0, grads


def _local_step(x, target, weights):
    saved = []
    h = x
    layers = []
    for l in range(DEPTH):
        wl = {n: weights[n][l] for n in WEIGHT_ORDER}
        layers.append(wl)
        h, sv = _layer_fwd(h, wl, l)
        saved.append(sv)
    sq, dh = _loss_head(h, target, name="loss_head")
    grads = [None] * DEPTH
    for l in reversed(range(DEPTH)):
        dh, grads[l] = _layer_bwd(dh, saved[l], layers[l], l)
    stacked = {n: jnp.stack([grads[l][n] for l in range(DEPTH)]) for n in WEIGHT_ORDER}
    return sq[0, 0], dh, stacked


def kernel(x, norm_g, w_in, b_in, sgu_ln_g, sgu_ln_b, w_spatial, b_spatial, conv_w, a_log, dt_bias, dn_norm_g, w_branch, w_out, w_ff1, w_ff2, loss_target, m_norm_g, m_w_in, m_b_in, m_sgu_ln_g, m_sgu_ln_b, m_w_spatial, m_b_spatial, m_conv_w, m_a_log, m_dt_bias, m_dn_norm_g, m_w_branch, m_w_out, m_w_ff1, m_w_ff2, v_norm_g, v_w_in, v_b_in, v_sgu_ln_g, v_sgu_ln_b, v_w_spatial, v_b_spatial, v_conv_w, v_a_log, v_dt_bias, v_dn_norm_g, v_w_branch, v_w_out, v_w_ff1, v_w_ff2):
    local = dict(norm_g=norm_g, w_in=w_in, b_in=b_in, sgu_ln_g=sgu_ln_g, sgu_ln_b=sgu_ln_b, w_spatial=w_spatial,
                 b_spatial=b_spatial, conv_w=conv_w, a_log=a_log, dt_bias=dt_bias, dn_norm_g=dn_norm_g,
                 w_branch=w_branch, w_out=w_out, w_ff1=w_ff1, w_ff2=w_ff2)
    mom1 = dict(norm_g=m_norm_g, w_in=m_w_in, b_in=m_b_in, sgu_ln_g=m_sgu_ln_g, sgu_ln_b=m_sgu_ln_b,
                w_spatial=m_w_spatial, b_spatial=m_b_spatial, conv_w=m_conv_w, a_log=m_a_log, dt_bias=m_dt_bias,
                dn_norm_g=m_dn_norm_g, w_branch=m_w_branch, w_out=m_w_out, w_ff1=m_w_ff1, w_ff2=m_w_ff2)
    mom2 = dict(norm_g=v_norm_g, w_in=v_w_in, b_in=v_b_in, sgu_ln_g=v_sgu_ln_g, sgu_ln_b=v_sgu_ln_b,
                w_spatial=v_w_spatial, b_spatial=v_b_spatial, conv_w=v_conv_w, a_log=v_a_log, dt_bias=v_dt_bias,
                dn_norm_g=v_dn_norm_g, w_branch=v_w_branch, w_out=v_w_out, w_ff1=v_w_ff1, w_ff2=v_w_ff2)
    shard_names = [n for n, _ in SHARDED]
    shard_shapes = [local[n].shape for n in shard_names]
    repl_shapes = [local[n].shape for n in REPLICATED]

    weights = {n: local[n] for n in REPLICATED}
    for names, dtype, call in ((MATMUL_WEIGHTS, BF16, "gather_matmul_weights"), (VECTOR_WEIGHTS, F32, "gather_vectors")):
        gathered = _gather_chips(_pack([local[n] for n in names]).astype(dtype), name=call)
        per_chip = [_unpack(gathered[k], [local[n].shape for n in names]) for k in range(4)]
        for i, n in enumerate(names):
            weights[n] = jnp.concatenate([per_chip[k][i] for k in range(4)], axis=dict(SHARDED)[n])
    weights["w_in"] = _rearrange_w_in(weights["w_in"])
    weights["b_in"] = _rearrange_w_in(weights["b_in"])

    sq, dx, grads = _local_step(x[0], loss_target[0], weights)
    loss = lax.psum(0.5 * sq / D_MODEL, ("x", "y", "c"))
    grads["w_in"] = _restore_w_in(grads["w_in"])
    grads["b_in"] = _restore_w_in(grads["b_in"])

    blocks = jnp.stack([_pack([_chip_slice(grads[n], axis, k) for n, axis in SHARDED] + [grads[n] for n in REPLICATED])
                        for k in range(4)])
    rows = blocks.shape[1]
    half = rows // 2
    c = lax.axis_index("c")
    got = _halves_to_sibling(blocks, name="grads_to_sibling")
    chip_sum = _add_own_half(blocks, got, name="grads_chip_sum", out_dtype=BF16)
    by_chip = _scatter_chips(chip_sum, name="grads_scatter")
    my_half = _add_rows([by_chip[k] for k in range(4)], name="grads_sum")
    other_half = _to_sibling(my_half, name="grads_half_swap")
    total = jnp.concatenate([jnp.where(c == 0, my_half, other_half), jnp.where(c == 0, other_half, my_half)], axis=0)
    g_out = dict(zip(shard_names + list(REPLICATED), _unpack(total, shard_shapes + repl_shapes)))

    d_out, m_out, v_out = {}, {}, {}
    for n in WEIGHT_ORDER:
        d_out[n], m_out[n], v_out[n] = _adamw(local[n], g_out[n], mom1[n], mom2[n], name=f"adamw_{n}")
    return (loss, dx[None], *[g_out[n] for n in WEIGHT_ORDER], *[d_out[n] for n in WEIGHT_ORDER],
            *[m_out[n] for n in WEIGHT_ORDER], *[v_out[n] for n in WEIGHT_ORDER])
```

```python
import functools

import jax
import jax.numpy as jnp
from jax import lax
from jax.experimental import pallas as pl
from jax.experimental.pallas import tpu as pltpu

F32 = jnp.float32
BF16 = jnp.bfloat16
GRAD_DTYPE = BF16

D_MODEL = 1024
DEPTH = 4
BRANCH_WIDTH = 512
GM_CHUNK = 128
GM_GROUPS = 8
DN_HEADS = 4
DN_HEAD_DIM = 128
DN_BLOCK = 128
DN_SOLVE_PASSES, DN_STATE_PASSES, DN_OUT_PASSES = 3, 1, 1
SB_HEAD_DIM = 64
SB_BLOCK = 128
SB_QUERY_ROWS = 256
SB_DEAD_LOG = -88.0
D_FF = 4096
P_IN = 7688
P_PAD = 8192
NORM_EPS = 1e-6
ADAM_LR, ADAM_B1, ADAM_B2, ADAM_EPS, ADAM_WD, ADAM_STEP = 0.001, 0.9, 0.999, 1e-08, 0.01, 10

OFF_AU, OFF_AV, OFF_BQ, OFF_BK, OFF_BV, OFF_BZ = 0, 512, 1024, 1536, 2048, 2560
OFF_GATE = 3072
OFF_CQ, OFF_CK, OFF_CV = 6144, 6656, 7168
OFF_BD = 7680

LANES = 1024
VMEM_LIMIT_BYTES = 56 * 1024 * 1024

_HI = lax.Precision.HIGHEST


def _params(sem):
    return pltpu.CompilerParams(dimension_semantics=sem, vmem_limit_bytes=VMEM_LIMIT_BYTES)


def _mm(a, b, *, name, out_dtype=F32, bias=None, trans_b=False, beside=(), finish=None, tm=1024, tn=1024, tk=1024):
    m, k = a.shape
    n = b.shape[0] if trans_b else b.shape[1]
    tm, tn, tk = min(tm, m), min(tn, n), min(tk, k)
    assert m % tm == 0 and n % tn == 0 and k % tk == 0, (a.shape, b.shape)
    nk = k // tk
    dn = (((1,), (1,)), ((), ())) if trans_b else (((1,), (0,)), ((), ()))
    several = isinstance(out_dtype, tuple)
    out_dtypes = out_dtype if several else (out_dtype,)
    n_in = 2 + (bias is not None) + len(beside)

    def body(*refs):
        a_ref, b_ref = refs[:2]
        o_refs, acc = refs[n_in:-1], refs[-1]
        kk = pl.program_id(2)
        part = lax.dot_general(a_ref[...].astype(BF16), b_ref[...].astype(BF16), dn, preferred_element_type=F32)

        @pl.when(kk == 0)
        def _():
            acc[...] = part

        @pl.when(kk > 0)
        def _():
            acc[...] += part

        @pl.when(kk == nk - 1)
        def _():
            r = acc[...]
            if bias is not None:
                r = r + refs[2][...]
            tiles = (r,) if finish is None else finish(r, *[t[...] for t in refs[n_in - len(beside):n_in]])
            for o_ref, tile in zip(o_refs, tiles if isinstance(tiles, tuple) else (tiles,)):
                o_ref[...] = tile.astype(o_ref.dtype)

    in_specs = [pl.BlockSpec((tm, tk), lambda i, j, kk: (i, kk))]
    if trans_b:
        in_specs.append(pl.BlockSpec((tn, tk), lambda i, j, kk: (j, kk)))
    else:
        in_specs.append(pl.BlockSpec((tk, tn), lambda i, j, kk: (kk, j)))
    args = [a, b]
    if bias is not None:
        in_specs.append(pl.BlockSpec((1, tn), lambda i, j, kk: (0, j)))
        args.append(bias)
    for t in beside:
        assert t.shape == (m, n)
        in_specs.append(pl.BlockSpec((tm, tn), lambda i, j, kk: (i, j)))
        args.append(t)
    res = pl.pallas_call(
        body, name=name, grid=(m // tm, n // tn, nk),
        in_specs=in_specs, out_specs=[pl.BlockSpec((tm, tn), lambda i, j, kk: (i, j)) for _ in out_dtypes],
        out_shape=[jax.ShapeDtypeStruct((m, n), dt) for dt in out_dtypes],
        scratch_shapes=[pltpu.VMEM((tm, tn), F32)],
        compiler_params=_params(("parallel", "parallel", "arbitrary")),
    )(*args)
    return list(res) if several else res[0]


def _mm_tn(a, b, *, name, tm=1024, tn=1024, ts=1024):
    out_dtype = GRAD_DTYPE
    s, ka = a.shape
    n = b.shape[1]
    tm, tn, ts = min(tm, ka), min(tn, n), min(ts, s)
    assert ka % tm == 0 and n % tn == 0 and s % ts == 0, (a.shape, b.shape)
    steps = s // ts

    def body(a_ref, b_ref, o_ref, acc):
        part = lax.dot_general(a_ref[...].astype(BF16), b_ref[...].astype(BF16), (((0,), (0,)), ((), ())),
                               preferred_element_type=F32)

        @pl.when(pl.program_id(2) == 0)
        def _():
            acc[...] = part

        @pl.when(pl.program_id(2) > 0)
        def _():
            acc[...] += part

        @pl.when(pl.program_id(2) == steps - 1)
        def _():
            o_ref[...] = acc[...].astype(out_dtype)

    return pl.pallas_call(
        body, name=name, grid=(ka // tm, n // tn, steps),
        in_specs=[pl.BlockSpec((ts, tm), lambda i, j, r: (r, i)), pl.BlockSpec((ts, tn), lambda i, j, r: (r, j))],
        out_specs=pl.BlockSpec((tm, tn), lambda i, j, r: (i, j)),
        out_shape=jax.ShapeDtypeStruct((ka, n), out_dtype),
        scratch_shapes=[pltpu.VMEM((tm, tn), F32)],
        compiler_params=_params(("parallel", "parallel", "arbitrary")),
    )(a, b)


def _col_block(i, *, c):
    return (i, c)


def _whole(i, *, nd):
    return (0,) * nd


def _row_specs(rows, ts):
    specs = []
    for arr, off, w in rows:
        assert off % w == 0 and arr.shape[0] % ts == 0
        specs.append(pl.BlockSpec((ts, w), functools.partial(_col_block, c=off // w)))
    return specs


def _row_fwd(fn, rows, params, outs, *, ts, name):
    s = rows[0][0].shape[0]
    ts = min(ts, s)
    nr, npar = len(rows), len(params)

    def body(*refs):
        rv = [r[...].astype(F32) for r in refs[:nr]]
        pv = [p[...] for p in refs[nr:nr + npar]]
        for o_ref, val in zip(refs[nr + npar:], fn(pv, rv)):
            o_ref[...] = val.astype(o_ref.dtype)

    in_specs = _row_specs(rows, ts) + [pl.BlockSpec(p.shape, functools.partial(_whole, nd=p.ndim)) for p in params]
    res = pl.pallas_call(
        body, name=name, grid=(s // ts,), in_specs=in_specs,
        out_specs=[pl.BlockSpec((ts, w), lambda i: (i, 0)) for w, _ in outs],
        out_shape=[jax.ShapeDtypeStruct((s, w), dt) for w, dt in outs],
        compiler_params=_params(("parallel",)),
    )(*[r[0] for r in rows], *params)
    return list(res)


def _row_bwd(fn, rows, params, cts, *, ts, name, row_grads=True):
    s = rows[0][0].shape[0]
    ts = min(ts, s)
    nr, npar, nc = len(rows), len(params), len(cts)
    n_dr = nr if row_grads else 0

    def body(*refs):
        rv = [r[...].astype(F32) for r in refs[:nr]]
        pv = [p[...] for p in refs[nr:nr + npar]]
        cv = [c[...].astype(F32) for c in refs[nr + npar:nr + npar + nc]]
        out_refs = refs[nr + npar + nc:]
        _, vjp = jax.vjp(lambda p, r: tuple(fn(p, r)), pv, rv)
        dp, dr = vjp(tuple(cv))
        for o_ref, val in zip(out_refs[:n_dr], dr):
            o_ref[...] = val

        @pl.when(pl.program_id(0) == 0)
        def _():
            for o_ref in out_refs[n_dr:]:
                o_ref[...] = jnp.zeros_like(o_ref)

        for o_ref, val in zip(out_refs[n_dr:], dp):
            o_ref[...] += val

    in_specs = (_row_specs(rows, ts) + [pl.BlockSpec(p.shape, functools.partial(_whole, nd=p.ndim)) for p in params]
                + _row_specs(cts, ts))
    out_specs = [pl.BlockSpec((ts, w), lambda i: (i, 0)) for _, _, w in rows[:n_dr]]
    out_specs += [pl.BlockSpec(p.shape, functools.partial(_whole, nd=p.ndim)) for p in params]
    out_shape = [jax.ShapeDtypeStruct((s, w), F32) for _, _, w in rows[:n_dr]]
    out_shape += [jax.ShapeDtypeStruct(p.shape, F32) for p in params]
    res = pl.pallas_call(
        body, name=name, grid=(s // ts,), in_specs=in_specs, out_specs=out_specs, out_shape=out_shape,
        compiler_params=_params(("arbitrary",)),
    )(*[r[0] for r in rows], *params, *[c[0] for c in cts])
    res = list(res)
    return res[:n_dr], res[n_dr:]


def _rms(x, g):
    return x * lax.rsqrt(jnp.mean(x * x, axis=-1, keepdims=True) + NORM_EPS) * g


def _gelu(x):
    return 0.5 * x * (1.0 + lax.erf(x * (2.0 ** -0.5)))


def _softplus(x):
    return jnp.maximum(x, 0.0) + jnp.log1p(jnp.exp(-jnp.abs(x)))


def _iota2(shape, dim):
    return lax.broadcasted_iota(jnp.int32, shape, dim)


def _fn_rms(pv, rv):
    return [_rms(rv[0], pv[0])]


def _fn_rms_keep(pv, rv):
    return [_rms(rv[0], pv[0]), rv[0]]


def _fn_resid_rms(pv, rv):
    return [rv[0] + _rms(rv[1], pv[0])]


def _fn_rms_branch(pv, rv):
    return [_rms(rv[0], pv[0])]


def _fn_merge(pv, rv):
    return [sum(jax.nn.sigmoid(rv[3 + i]) * rv[i] for i in range(3))]


def _fn_gmlp(pv, rv):
    ln_g, ln_b, w_sp, b_t = pv
    u = _gelu(rv[0])
    v = _gelu(rv[1])
    vc = v - jnp.mean(v, axis=-1, keepdims=True)
    v = vc * lax.rsqrt(jnp.mean(vc * vc, axis=-1, keepdims=True) + NORM_EPS) * ln_g + ln_b
    t = GM_CHUNK
    causal = _iota2((t, t), 1) <= _iota2((t, t), 0)
    first = _iota2((t, 128), 1) < 64
    expand = (_iota2((128, BRANCH_WIDTH), 0) == _iota2((128, BRANCH_WIDTH), 1) // 64).astype(F32)
    b_full = jnp.dot(b_t, expand, precision=_HI, preferred_element_type=F32)
    w_bf = [jnp.where(causal, w_sp[g], 0.0).astype(BF16) for g in range(GM_GROUPS)]
    chunks = []
    for c in range(rv[0].shape[0] // t):
        pairs = []
        for p in range(GM_GROUPS // 2):
            vp = v[c * t:(c + 1) * t, 128 * p:128 * (p + 1)].astype(BF16)
            m0 = jnp.dot(w_bf[2 * p], vp, preferred_element_type=F32)
            m1 = jnp.dot(w_bf[2 * p + 1], vp, preferred_element_type=F32)
            pairs.append(jnp.where(first, m0, m1))
        chunks.append(jnp.concatenate(pairs, axis=1) + b_full)
    return [u * jnp.concatenate(chunks, axis=0)]


def _head_expand(col0):
    return (_iota2((128, BRANCH_WIDTH), 0) == _iota2((128, BRANCH_WIDTH), 1) // DN_HEAD_DIM + col0).astype(F32)


def _fn_dn_in(pv, rv):
    conv_w, a_log, dt_b = pv
    c = sum(conv_w[j:j + 1, :] * rv[j] for j in range(4))
    a = c * jax.nn.sigmoid(c)
    outs = []
    for part in range(3):
        heads = []
        for h in range(DN_HEADS):
            lo = part * BRANCH_WIDTH + h * DN_HEAD_DIM
            xh = a[:, lo:lo + DN_HEAD_DIM]
            if part < 2:
                xh = xh * lax.rsqrt(jnp.sum(xh * xh, axis=-1, keepdims=True) + NORM_EPS)
            heads.append(xh)
        outs.append(jnp.concatenate(heads, axis=1))
    bd = rv[4]
    beta = jax.nn.sigmoid(bd)
    g = -jnp.exp(a_log) * _softplus(bd + dt_b)
    outs.append(jnp.dot(beta, _head_expand(0), precision=_HI, preferred_element_type=F32))
    outs.append(jnp.dot(g, _head_expand(DN_HEADS), precision=_HI, preferred_element_type=F32))
    return outs


def _fn_dn_out(pv, rv):
    heads = []
    for h in range(DN_HEADS):
        sl = slice(h * DN_HEAD_DIM, (h + 1) * DN_HEAD_DIM)
        z = rv[1][:, sl]
        heads.append(_rms(rv[0][:, sl], pv[0]) * (z * jax.nn.sigmoid(z)))
    return [jnp.concatenate(heads, axis=1)]


_DIMS = {"nn": (((1,), (0,)), ((), ())), "nt": (((1,), (1,)), ((), ())), "tn": (((0,), (0,)), ((), ()))}
_DIMS_BWD = {"nn": (("nt", "c", "b"), ("tn", "a", "c")), "nt": (("nn", "c", "b"), ("tn", "c", "a")),
             "tn": (("nt", "b", "c"), ("nn", "a", "c"))}


def _bf16_dot(a, b, kind):
    return lax.dot_general(a.astype(BF16), b.astype(BF16), _DIMS[kind], preferred_element_type=F32)


def _pdot_raw(a, b, kind, mode):
    if mode == 1:
        return _bf16_dot(a, b, kind)
    if mode == 6:
        return lax.dot_general(a, b, _DIMS[kind], precision=_HI, preferred_element_type=F32)
    b_hi, b_lo = _split_bf16(b)
    if mode == 3:
        a_hi, a_lo = _split_bf16(a)
        return _bf16_dot(a_hi, b_hi, kind) + (_bf16_dot(a_hi, b_lo, kind) + _bf16_dot(a_lo, b_hi, kind))
    b_rest = (b - b_hi.astype(F32) - b_lo.astype(F32)).astype(BF16)
    return _bf16_dot(a, b_hi, kind) + (_bf16_dot(a, b_lo, kind) + _bf16_dot(a, b_rest, kind))


@functools.partial(jax.custom_vjp, nondiff_argnums=(2, 3))
def _pdot(a, b, kind, mode):
    return _pdot_raw(a, b, kind, mode)


def _pdot_fwd(a, b, kind, mode):
    return _pdot_raw(a, b, kind, mode), (a, b)


def _pdot_bwd(kind, mode, res, ct):
    ops = {"a": res[0], "b": res[1], "c": ct}
    (ka, a1, a2), (kb, b1, b2) = _DIMS_BWD[kind]
    if mode == "count":
        return jnp.zeros_like(res[0]), _pdot(ops[b1], ops[b2], kb, mode)
    return _pdot(ops[a1], ops[a2], ka, mode), _pdot(ops[b1], ops[b2], kb, mode)


_pdot.defvjp(_pdot_fwd, _pdot_bwd)


@jax.custom_vjp
def _unit_lower_inverses(mats):
    c = mats[0].shape[0]
    row, col = _iota2((c, c), 0), _iota2((c, c), 1)
    x = [(row == col).astype(F32) for _ in mats]
    shift = 0
    while (1 << shift) < c:
        pair = jnp.right_shift(row, shift + 1) == jnp.right_shift(col, shift + 1)
        between = pair & (jnp.right_shift(row, shift) != jnp.right_shift(col, shift))
        q = [jnp.where(between, a, 0.0) for a in mats]
        qd = [_pdot_raw(qi, xi, "nn", DN_SOLVE_PASSES) for qi, xi in zip(q, x)]
        x = [xi - _pdot_raw(xi, m, "nn", DN_SOLVE_PASSES) for xi, m in zip(x, qd)]
        shift += 1
    return tuple(x)


def _unit_lower_inverses_fwd(mats):
    x = _unit_lower_inverses(mats)
    return x, x


def _unit_lower_inverses_bwd(x, dx):
    inner = [_pdot_raw(d, xi, "nt", DN_SOLVE_PASSES) for d, xi in zip(dx, x)]
    return (tuple(-_pdot_raw(xi, m, "tn", DN_SOLVE_PASSES) for xi, m in zip(x, inner)),)


_unit_lower_inverses.defvjp(_unit_lower_inverses_fwd, _unit_lower_inverses_bwd)


def _head(h):
    return slice(h * DN_HEAD_DIM, (h + 1) * DN_HEAD_DIM)


def _delta_chunk(states, q, k, v, beta, g):
    c = DN_BLOCK
    heads = range(DN_HEADS)
    row, col = _iota2((c, c), 0), _iota2((c, c), 1)
    tri, strict = col <= row, col < row
    counts = jnp.concatenate([tri.astype(BF16), jnp.ones((c, c), BF16)], axis=0)
    sums = _pdot(counts, g, "nn", "count")
    gc = [sums[:c, _head(h)] for h in heads]
    gl = [sums[c:, _head(h)] for h in heads]
    qh, kh, vh, bh = ([t[:, _head(h)] for h in heads] for t in (q * (DN_HEAD_DIM ** -0.5), k, v, beta))
    decay = [jnp.where(tri, jnp.exp(jnp.where(tri, gc[h] - gc[h].T, 0.0)), 0.0) for h in heads]
    solve, carry, out = DN_SOLVE_PASSES, DN_STATE_PASSES, DN_OUT_PASSES
    kk = [_pdot(kh[h], kh[h], "nt", solve) for h in heads]
    x = _unit_lower_inverses(tuple(jnp.where(strict, bh[h] * kk[h] * decay[h], 0.0) for h in heads))
    eg = [jnp.exp(gc[h]) for h in heads]
    u = [_pdot(x[h], vh[h] * bh[h], "nn", solve) for h in heads]
    wk = [_pdot(x[h], kh[h] * (bh[h] * eg[h]), "nn", solve) for h in heads]
    qk = [jnp.where(tri, _pdot(qh[h], kh[h], "nt", out) * decay[h], 0.0) for h in heads]
    v_new = [u[h] - _pdot(wk[h], states[h], "nn", carry) for h in heads]
    o = [_pdot(qh[h] * eg[h], states[h], "nn", out) + _pdot(qk[h], v_new[h], "nn", out) for h in heads]
    nxt = [states[h] * jnp.exp(gl[h]) + _pdot(kh[h] * jnp.exp(gl[h] - gc[h]), v_new[h], "tn", carry)
           for h in heads]
    return jnp.concatenate(o, axis=1), tuple(nxt)


def _delta_fwd(q, k, v, beta, g, *, name):
    s = q.shape[0]
    c = DN_BLOCK
    nc = s // c

    def body(q_ref, k_ref, v_ref, b_ref, g_ref, o_ref, sp_ref, st):
        @pl.when(pl.program_id(0) == 0)
        def _():
            st[...] = jnp.zeros_like(st)

        states = tuple(st[h] for h in range(DN_HEADS))
        for h in range(DN_HEADS):
            sp_ref[0, h] = states[h]
        o, nxt = _delta_chunk(states, q_ref[...], k_ref[...], v_ref[...], b_ref[...], g_ref[...])
        for h in range(DN_HEADS):
            st[h] = nxt[h]
        o_ref[...] = o

    blk = pl.BlockSpec((c, BRANCH_WIDTH), lambda n: (n, 0))
    return pl.pallas_call(
        body, name=name, grid=(nc,), in_specs=[blk] * 5,
        out_specs=[blk, pl.BlockSpec((1, DN_HEADS, DN_HEAD_DIM, DN_HEAD_DIM), lambda n: (n, 0, 0, 0))],
        out_shape=[jax.ShapeDtypeStruct((s, BRANCH_WIDTH), F32),
                   jax.ShapeDtypeStruct((nc, DN_HEADS, DN_HEAD_DIM, DN_HEAD_DIM), F32)],
        scratch_shapes=[pltpu.VMEM((DN_HEADS, DN_HEAD_DIM, DN_HEAD_DIM), F32)],
        compiler_params=_params(("arbitrary",)),
    )(q, k, v, beta, g)


def _delta_bwd(q, k, v, beta, g, states, do, *, name):
    s = q.shape[0]
    c = DN_BLOCK
    nc = s // c

    def body(q_ref, k_ref, v_ref, b_ref, g_ref, sp_ref, do_ref, dq_ref, dk_ref, dv_ref, db_ref, dg_ref, dst):
        @pl.when(pl.program_id(0) == 0)
        def _():
            dst[...] = jnp.zeros_like(dst)

        states = tuple(sp_ref[0, h] for h in range(DN_HEADS))
        _, vjp = jax.vjp(_delta_chunk, states, q_ref[...], k_ref[...], v_ref[...], b_ref[...], g_ref[...])
        d = vjp((do_ref[...], tuple(dst[h] for h in range(DN_HEADS))))
        for h in range(DN_HEADS):
            dst[h] = d[0][h]
        for o_ref, val in zip((dq_ref, dk_ref, dv_ref, db_ref, dg_ref), d[1:]):
            o_ref[...] = val

    blk = pl.BlockSpec((c, BRANCH_WIDTH), lambda n: (nc - 1 - n, 0))
    res = pl.pallas_call(
        body, name=name, grid=(nc,),
        in_specs=[blk] * 5 + [pl.BlockSpec((1, DN_HEADS, DN_HEAD_DIM, DN_HEAD_DIM), lambda n: (nc - 1 - n, 0, 0, 0)), blk],
        out_specs=[blk] * 5, out_shape=[jax.ShapeDtypeStruct((s, BRANCH_WIDTH), F32)] * 5,
        scratch_shapes=[pltpu.VMEM((DN_HEADS, DN_HEAD_DIM, DN_HEAD_DIM), F32)],
        compiler_params=_params(("arbitrary",)),
    )(q, k, v, beta, g, states, do)
    return list(res)


def _split_bf16(x):
    hi = x.astype(BF16)
    return hi, (x - hi.astype(F32)).astype(BF16)


def _sb_consts():
    tq, tk = SB_QUERY_ROWS, SB_BLOCK
    row, col = _iota2((tk, tk), 0), _iota2((tk, tk), 1)
    ones = jnp.ones((tk, tk), BF16)
    later = jnp.concatenate([(row > col).astype(BF16), ones], axis=1)
    from_here = jnp.concatenate([(row >= col).astype(BF16), ones], axis=1)
    first = _iota2((tq, 128), 1) < SB_HEAD_DIM
    return later, from_here, first


def _sb_causal(d):
    tq, tk = SB_QUERY_ROWS, SB_BLOCK
    return _iota2((tq, tk), 1) + d * tk < _iota2((tq, tk), 0)


def _sums(x, mat):
    hi, lo = _split_bf16(x)
    return jnp.dot(hi, mat, preferred_element_type=F32) + jnp.dot(lo, mat, preferred_element_type=F32)


def _sb_weights(qs, kb, accs, later, causal):
    tk = SB_BLOCK
    z = [lax.dot_general(qh, kb, (((1,), (1,)), ((), ())), preferred_element_type=F32) for qh in qs]
    lk = [-_softplus(zh) for zh in z]
    if causal is not None:
        lk = [jnp.where(causal, v, 0.0) for v in lk]
    cs = [_sums(v, later) for v in lk]
    e = [z[h] + lk[h] + cs[h][:, :tk] + accs[h] for h in range(2)]
    if causal is not None:
        e = [jnp.where(causal, v, -1e30) for v in e]
    return lk, [jnp.exp(v) for v in e], [v[:, tk:] for v in cs]


def _sb_alive(accs):
    return jnp.max(jnp.maximum(accs[0], accs[1])) > SB_DEAD_LOG


def _sb_sweep(i, block, carry, accs_of, stop=None):
    per = SB_QUERY_ROWS // SB_BLOCK
    for d in reversed(range(per)):
        carry = block(i * per + d, carry, _sb_causal(d))
    if stop is not None:
        return lax.fori_loop(0, i * per - 1 - stop, lambda jj, cr: block(i * per - 1 - jj, cr, None), carry)

    def step(state):
        j, _, cr = state
        cr = block(j, cr, None)
        return j - 1, _sb_alive(accs_of(cr)), cr

    j, _, carry = lax.while_loop(lambda st: jnp.logical_and(st[0] >= 0, st[1]), step,
                                 (i * per - 1, _sb_alive(accs_of(carry)), carry))
    return carry, j


def _sb_fwd(p, *, name):
    s = p.shape[0]
    t, tk = SB_QUERY_ROWS, SB_BLOCK
    scale = SB_HEAD_DIM ** -0.5

    def body(q_ref, k_ref, v_ref, o_ref):
        i = pl.program_id(1)
        later, _, first = _sb_consts()
        q = q_ref[...] * scale
        qs = (jnp.where(first, q, 0.0).astype(BF16), jnp.where(first, 0.0, q).astype(BF16))

        def block(j, carry, causal):
            start = pl.multiple_of(j * tk, tk)
            kb = k_ref[pl.ds(start, tk), :].astype(BF16)
            vb = v_ref[pl.ds(start, tk), :].astype(BF16)
            _, w, tot = _sb_weights(qs, kb, [carry[h][1] for h in range(2)], later, causal)
            out = [jnp.dot(w[h].astype(BF16), vb, preferred_element_type=F32) for h in range(2)]
            return tuple((carry[h][0] + out[h], carry[h][1] + tot[h]) for h in range(2))

        zero = jnp.zeros((t, 128), F32)
        carry, _ = _sb_sweep(i, block, ((zero, zero), (zero, zero)), lambda cr: (cr[0][1], cr[1][1]))
        o_ref[...] = jnp.where(first, carry[0][0], carry[1][0])

    return pl.pallas_call(
        body, name=name, grid=(BRANCH_WIDTH // 128, s // t),
        in_specs=[pl.BlockSpec((t, 128), lambda pr, i: (i, OFF_CQ // 128 + pr)),
                  pl.BlockSpec((s, 128), lambda pr, i: (0, OFF_CK // 128 + pr)),
                  pl.BlockSpec((s, 128), lambda pr, i: (0, OFF_CV // 128 + pr))],
        out_specs=pl.BlockSpec((t, 128), lambda pr, i: (i, pr)),
        out_shape=jax.ShapeDtypeStruct((s, BRANCH_WIDTH), F32),
        compiler_params=_params(("arbitrary", "arbitrary")),
    )(p, p, p)


def _sb_bwd(p, do, *, name):
    s = p.shape[0]
    t, tk = SB_QUERY_ROWS, SB_BLOCK
    scale = SB_HEAD_DIM ** -0.5

    def body(q_ref, k_ref, v_ref, do_ref, dq_ref, dk_ref, dv_ref):
        i = pl.program_id(1)

        @pl.when(i == 0)
        def _():
            dk_ref[...] = jnp.zeros_like(dk_ref)
            dv_ref[...] = jnp.zeros_like(dv_ref)

        later, from_here, first = _sb_consts()
        q = q_ref[...] * scale
        do = do_ref[...]
        qs = (jnp.where(first, q, 0.0).astype(BF16), jnp.where(first, 0.0, q).astype(BF16))
        dos = (jnp.where(first, do, 0.0).astype(BF16), jnp.where(first, 0.0, do).astype(BF16))

        def total(j, carry, causal):
            start = pl.multiple_of(j * tk, tk)
            kb = k_ref[pl.ds(start, tk), :].astype(BF16)
            vb = v_ref[pl.ds(start, tk), :].astype(BF16)
            _, w, tot = _sb_weights(qs, kb, [carry[h][0] for h in range(2)], later, causal)
            dw = [lax.dot_general(dos[h], vb, (((1,), (1,)), ((), ())), preferred_element_type=F32) for h in range(2)]
            tde = [_sums(dw[h] * w[h], from_here)[:, tk:] for h in range(2)]
            return tuple((carry[h][0] + tot[h], carry[h][1] + tde[h]) for h in range(2))

        zero = jnp.zeros((t, 128), F32)
        sums, stop = _sb_sweep(i, total, ((zero, zero), (zero, zero)), lambda cr: (cr[0][0], cr[1][0]))
        deltas = (sums[0][1], sums[1][1])

        def block(j, carry, causal):
            start = pl.multiple_of(j * tk, tk)
            kb = k_ref[pl.ds(start, tk), :].astype(BF16)
            vb = v_ref[pl.ds(start, tk), :].astype(BF16)
            both = range(2)
            tn = (((0,), (0,)), ((), ()))
            lk, w, tot = _sb_weights(qs, kb, [carry[h][1] for h in both], later, causal)
            dw = [lax.dot_general(dos[h], vb, (((1,), (1,)), ((), ())), preferred_element_type=F32) for h in both]
            de = [dw[h] * w[h] for h in both]
            cs = [_sums(de[h], from_here) for h in both]
            keep = [jnp.exp(lk[h]) for h in both]
            dz = [de[h] * keep[h] - (deltas[h] - (cs[h][:, :tk] + carry[h][2])) * (1.0 - keep[h]) for h in both]
            if causal is not None:
                dz = [jnp.where(causal, v, 0.0) for v in dz]
            dzb = [v.astype(BF16) for v in dz]
            dq = [jnp.dot(dzb[h], kb, preferred_element_type=F32) for h in both]
            dk = [lax.dot_general(dzb[h], qs[h], tn, preferred_element_type=F32) for h in both]
            dv = [lax.dot_general(w[h].astype(BF16), dos[h], tn, preferred_element_type=F32) for h in both]
            dk_ref[pl.ds(start, tk), :] += dk[0] + dk[1]
            dv_ref[pl.ds(start, tk), :] += dv[0] + dv[1]
            return tuple((carry[h][0] + dq[h], carry[h][1] + tot[h], carry[h][2] + cs[h][:, tk:]) for h in both)

        carry = _sb_sweep(i, block, ((zero, zero, zero), (zero, zero, zero)), None, stop=stop)
        dq_ref[...] = jnp.where(first, carry[0][0], carry[1][0]) * scale

    qblk = lambda off: pl.BlockSpec((t, 128), lambda pr, i: (i, off // 128 + pr))
    full = lambda off: pl.BlockSpec((s, 128), lambda pr, i: (0, off // 128 + pr))
    res = pl.pallas_call(
        body, name=name, grid=(BRANCH_WIDTH // 128, s // t),
        in_specs=[qblk(OFF_CQ), full(OFF_CK), full(OFF_CV), qblk(0)],
        out_specs=[qblk(0), full(0), full(0)],
        out_shape=[jax.ShapeDtypeStruct((s, BRANCH_WIDTH), F32)] * 3,
        compiler_params=_params(("arbitrary", "arbitrary")),
    )(p, p, p, do)
    return list(res)


def _loss_head(y, target, *, name, ts=512):
    s, d = y.shape
    ts = min(ts, s)

    def body(y_ref, t_ref, sq_ref, dy_ref):
        @pl.when(pl.program_id(0) == 0)
        def _():
            sq_ref[...] = jnp.zeros_like(sq_ref)

        err = y_ref[...] - t_ref[...]
        dy_ref[...] = err * (1.0 / d)
        tot = jnp.sum(jnp.sum(err * err, axis=1, keepdims=True), axis=0, keepdims=True)
        sq_ref[...] += jnp.broadcast_to(tot, sq_ref.shape)

    blk = pl.BlockSpec((ts, d), lambda i: (i, 0))
    return pl.pallas_call(
        body, name=name, grid=(s // ts,), in_specs=[blk, blk],
        out_specs=[pl.BlockSpec((1, 128), lambda i: (0, 0)), blk],
        out_shape=[jax.ShapeDtypeStruct((1, 128), F32), jax.ShapeDtypeStruct((s, d), F32)],
        compiler_params=_params(("arbitrary",)),
    )(y, target)


def _row_tile(r, limit=512):
    return max(t for t in range(8, limit + 1, 8) if r % t == 0)


def _adamw(w, g, m, v, *, name):
    shape = w.shape
    lanes = shape[-1]
    w, g, m, v = (a.reshape(-1, lanes) for a in (w, g, m, v))
    r = w.shape[0]
    ts = r if r <= 256 else _row_tile(r, 256 if lanes > LANES else 512)

    def body(w_ref, g_ref, m_ref, v_ref, d_ref, nm_ref, nv_ref):
        gv = g_ref[...]
        m_new = ADAM_B1 * m_ref[...] + (1.0 - ADAM_B1) * gv
        v_new = ADAM_B2 * v_ref[...] + (1.0 - ADAM_B2) * jnp.square(gv)
        m_hat = m_new / (1.0 - ADAM_B1 ** ADAM_STEP)
        v_hat = v_new / (1.0 - ADAM_B2 ** ADAM_STEP)
        d_ref[...] = -ADAM_LR * (m_hat / (jnp.sqrt(v_hat) + ADAM_EPS) + ADAM_WD * w_ref[...])
        nm_ref[...] = m_new
        nv_ref[...] = v_new

    blk = pl.BlockSpec((ts, lanes), lambda i: (i, 0))
    res = pl.pallas_call(
        body, name=name, grid=(r // ts,), in_specs=[blk] * 4, out_specs=[blk] * 3,
        out_shape=[jax.ShapeDtypeStruct((r, lanes), F32)] * 3,
        compiler_params=_params(("parallel",)),
    )(w, g, m, v)
    return [a.reshape(shape) for a in res]


def _add_rows(terms, *, name, out_dtype=F32):
    r = terms[0].shape[0]
    ts = _row_tile(r)

    def body(*refs):
        acc = refs[0][...].astype(F32)
        for ref in refs[1:-1]:
            acc = acc + ref[...].astype(F32)
        refs[-1][...] = acc.astype(out_dtype)

    blk = pl.BlockSpec((ts, LANES), lambda i: (i, 0))
    return pl.pallas_call(
        body, name=name, grid=(r // ts,), in_specs=[blk] * len(terms), out_specs=blk,
        out_shape=jax.ShapeDtypeStruct((r, LANES), out_dtype), compiler_params=_params(("parallel",)),
    )(*terms)


def _col_sums(a, *, name, ts=256):
    s, n = a.shape
    ts = min(ts, s)

    def body(a_ref, o_ref):
        @pl.when(pl.program_id(0) == 0)
        def _():
            o_ref[...] = jnp.zeros_like(o_ref)

        o_ref[...] += jnp.sum(a_ref[...], axis=0, keepdims=True)

    return pl.pallas_call(
        body, name=name, grid=(s // ts,), in_specs=[pl.BlockSpec((ts, n), lambda i: (i, 0))],
        out_specs=pl.BlockSpec((1, n), lambda i: (0, 0)), out_shape=jax.ShapeDtypeStruct((1, n), F32),
        compiler_params=_params(("arbitrary",)),
    )(a)


_HBM = pl.BlockSpec(memory_space=pltpu.HBM)
_MESH = pl.DeviceIdType.MESH


def _other_chips(x, y):
    return [(1 - x, y), (x, 1 - y), (1 - x, 1 - y)]


def _gather_chips(shard, *, name):
    r, lanes = shard.shape
    half = r // 2
    assert half * 2 == r

    def body(in_ref, out_ref, send_sems, recv_sems):
        x, y, c = lax.axis_index("x"), lax.axis_index("y"), lax.axis_index("c")
        me = 2 * x + y
        sibling = (x, y, 1 - c)
        chips = _other_chips(x, y)

        def copy(sem, chip, core_half, to):
            rows = out_ref.at[chip, pl.ds(core_half * half, half)]
            return pltpu.make_async_remote_copy(src_ref=rows, dst_ref=rows, send_sem=send_sems.at[sem],
                                                recv_sem=recv_sems.at[sem], device_id=to, device_id_type=_MESH)

        first = []
        for kk, (px, py) in enumerate(chips):
            cp = pltpu.make_async_remote_copy(
                src_ref=in_ref.at[pl.ds(c * half, half)], dst_ref=out_ref.at[me, pl.ds(c * half, half)],
                send_sem=send_sems.at[kk], recv_sem=recv_sems.at[kk], device_id=(px, py, c), device_id_type=_MESH)
            cp.start()
            first.append(cp)
        passed = [copy(3 + kk, 2 * px + py, c, sibling) for kk, (px, py) in enumerate(chips)]
        for kk, (px, py) in enumerate(chips):
            copy(kk, 2 * px + py, c, (px, py, c)).wait_recv()
            passed[kk].start()
        for kk, (px, py) in enumerate(chips):
            copy(3 + kk, 2 * px + py, 1 - c, sibling).wait_recv()
        for cp in first + passed:
            cp.wait_send()

    gathered = pl.pallas_call(
        body, name=name, in_specs=[_HBM], out_specs=_HBM,
        out_shape=jax.ShapeDtypeStruct((4, r, lanes), shard.dtype),
        scratch_shapes=[pltpu.SemaphoreType.DMA((6,)), pltpu.SemaphoreType.DMA((6,))],
    )(shard)
    me = 2 * lax.axis_index("x") + lax.axis_index("y")
    return jnp.where(lax.broadcasted_iota(jnp.int32, (4, 1, 1), 0) == me, shard[None], gathered)


def _to_sibling(block, *, name):
    def body(in_ref, out_ref, send_sem, recv_sem):
        x, y, c = lax.axis_index("x"), lax.axis_index("y"), lax.axis_index("c")
        cp = pltpu.make_async_remote_copy(src_ref=in_ref, dst_ref=out_ref, send_sem=send_sem, recv_sem=recv_sem,
                                          device_id=(x, y, 1 - c), device_id_type=_MESH)
        cp.start()
        cp.wait()

    return pl.pallas_call(
        body, name=name, in_specs=[_HBM], out_specs=_HBM, out_shape=jax.ShapeDtypeStruct(block.shape, block.dtype),
        scratch_shapes=[pltpu.SemaphoreType.DMA, pltpu.SemaphoreType.DMA],
    )(block)


def _halves_to_sibling(blocks, *, name):
    n, r, lanes = blocks.shape
    half = r // 2

    def body(in_ref, out_ref, send_sem, recv_sem):
        x, y, c = lax.axis_index("x"), lax.axis_index("y"), lax.axis_index("c")
        cp = pltpu.make_async_remote_copy(src_ref=in_ref.at[pl.ds(0, n), pl.ds((1 - c) * half, half)], dst_ref=out_ref,
                                          send_sem=send_sem, recv_sem=recv_sem, device_id=(x, y, 1 - c),
                                          device_id_type=_MESH)
        cp.start()
        cp.wait()

    return pl.pallas_call(
        body, name=name, in_specs=[_HBM], out_specs=_HBM, out_shape=jax.ShapeDtypeStruct((n, half, lanes), blocks.dtype),
        scratch_shapes=[pltpu.SemaphoreType.DMA, pltpu.SemaphoreType.DMA],
    )(blocks)


def _add_own_half(blocks, got, *, name, out_dtype):
    n, r, lanes = blocks.shape
    half = r // 2
    ts = _row_tile(half)
    steps = half // ts
    core = lax.axis_index("c").astype(jnp.int32).reshape(1)

    def body(core_ref, a_ref, b_ref, o_ref):
        o_ref[...] = (a_ref[...].astype(F32) + b_ref[...].astype(F32)).astype(out_dtype)

    return pl.pallas_call(
        body, name=name,
        grid_spec=pltpu.PrefetchScalarGridSpec(
            num_scalar_prefetch=1, grid=(n, steps),
            in_specs=[pl.BlockSpec((1, ts, lanes), lambda k, i, core_ref: (k, core_ref[0] * steps + i, 0)),
                      pl.BlockSpec((1, ts, lanes), lambda k, i, core_ref: (k, i, 0))],
            out_specs=pl.BlockSpec((1, ts, lanes), lambda k, i, core_ref: (k, i, 0))),
        out_shape=jax.ShapeDtypeStruct((n, half, lanes), out_dtype),
        compiler_params=_params(("parallel", "parallel")),
    )(core, blocks, got)


def _scatter_chips(blocks, *, name):
    _, r, lanes = blocks.shape

    def body(in_ref, out_ref, send_sems, recv_sems, local_sem):
        x, y, c = lax.axis_index("x"), lax.axis_index("y"), lax.axis_index("c")
        me = 2 * x + y
        mine = pltpu.make_async_copy(in_ref.at[me], out_ref.at[me], local_sem)
        mine.start()
        copies = []
        for kk, (px, py) in enumerate(_other_chips(x, y)):
            cp = pltpu.make_async_remote_copy(src_ref=in_ref.at[2 * px + py], dst_ref=out_ref.at[me],
                                              send_sem=send_sems.at[kk], recv_sem=recv_sems.at[kk],
                                              device_id=(px, py, c), device_id_type=_MESH)
            cp.start()
            copies.append(cp)
        for kk, (px, py) in enumerate(_other_chips(x, y)):
            pltpu.make_async_remote_copy(src_ref=in_ref.at[me], dst_ref=out_ref.at[2 * px + py],
                                         send_sem=send_sems.at[kk], recv_sem=recv_sems.at[kk], device_id=(px, py, c),
                                         device_id_type=_MESH).wait_recv()
        for cp in copies:
            cp.wait_send()
        mine.wait()

    return pl.pallas_call(
        body, name=name, in_specs=[_HBM], out_specs=_HBM, out_shape=jax.ShapeDtypeStruct((4, r, lanes), blocks.dtype),
        scratch_shapes=[pltpu.SemaphoreType.DMA((3,)), pltpu.SemaphoreType.DMA((3,)), pltpu.SemaphoreType.DMA],
    )(blocks)


SHARDED = (("norm_g", 2), ("w_in", 2), ("conv_w", 2), ("w_branch", 3), ("w_out", 1), ("w_ff1", 2), ("w_ff2", 1))
MATMUL_WEIGHTS = ("w_in", "w_branch", "w_out", "w_ff1", "w_ff2")
VECTOR_WEIGHTS = ("norm_g", "conv_w")
REPLICATED = ("b_in", "sgu_ln_g", "sgu_ln_b", "w_spatial", "b_spatial", "a_log", "dt_bias", "dn_norm_g")
WEIGHT_ORDER = ("norm_g", "w_in", "b_in", "sgu_ln_g", "sgu_ln_b", "w_spatial", "b_spatial", "conv_w", "a_log",
                "dt_bias", "dn_norm_g", "w_branch", "w_out", "w_ff1", "w_ff2")
PACK_ROW_MULTIPLE = 32


def _rows_of(shape):
    n = 1
    for dim in shape:
        n *= dim
    return -(-n // LANES)


def _pack(arrays):
    parts = []
    for a in arrays:
        flat = a.reshape(-1)
        pad = _rows_of(a.shape) * LANES - flat.shape[0]
        if pad:
            flat = jnp.concatenate([flat, jnp.zeros((pad,), flat.dtype)])
        parts.append(flat.reshape(-1, LANES))
    rows = sum(p.shape[0] for p in parts)
    pad = -rows % PACK_ROW_MULTIPLE
    if pad:
        parts.append(jnp.zeros((pad, LANES), parts[0].dtype))
    return jnp.concatenate(parts, axis=0)


def _unpack(buf, shapes):
    out, row = [], 0
    for shape in shapes:
        n = 1
        for dim in shape:
            n *= dim
        rows = _rows_of(shape)
        out.append(buf[row:row + rows].reshape(-1)[:n].reshape(shape))
        row += rows
    return out


def _chip_slice(a, axis, k):
    size = a.shape[axis] // 4
    return lax.slice_in_dim(a, k * size, (k + 1) * size, axis=axis)


def _rearrange_w_in(w):
    pad = jnp.zeros(w.shape[:-1] + (P_PAD - P_IN,), w.dtype)
    return jnp.concatenate([w[..., 0:3072], w[..., 4616:7688], w[..., 3080:4616], w[..., 3072:3080], pad], axis=-1)


def _restore_w_in(w):
    return jnp.concatenate([w[..., 0:3072], w[..., 7680:7688], w[..., 6144:7680], w[..., 3072:6144]], axis=-1)


def _shift_rows(a, n):
    if n == 0:
        return a
    return jnp.concatenate([jnp.zeros((n, a.shape[1]), a.dtype), a[:-n]], axis=0)


def _unshift_rows(a, n):
    if n == 0:
        return a
    return jnp.concatenate([a[n:], jnp.zeros((n, a.shape[1]), a.dtype)], axis=0)


def _layer_params(wl):
    row = lambda v: v.reshape(1, -1)
    pad128 = lambda v, at: jnp.pad(v, (at, 128 - at - v.shape[0])).reshape(1, 128)
    return dict(
        g=[row(wl["norm_g"][i]) for i in range(4)],
        gmlp=[row(wl["sgu_ln_g"]), row(wl["sgu_ln_b"]), wl["w_spatial"],
              jnp.pad(wl["b_spatial"].T, ((0, 0), (0, 128 - GM_GROUPS)))],
        dn_in=[wl["conv_w"], pad128(wl["a_log"], DN_HEADS), pad128(wl["dt_bias"], DN_HEADS)],
        dn_g=[row(wl["dn_norm_g"])],
    )


def _layer_fwd(x0, wl, l):
    tag = lambda s: f"{s}_l{l}"
    pr = _layer_params(wl)
    w = BRANCH_WIDTH
    h0 = _row_fwd(_fn_rms, [(x0, 0, D_MODEL)], [pr["g"][0]], [(D_MODEL, BF16)], ts=512, name=tag("rms0"))[0]
    p = _mm(h0, wl["w_in"], bias=wl["b_in"].reshape(1, -1), name=tag("proj_in"))
    ya = _row_fwd(_fn_gmlp, [(p, OFF_AU, w), (p, OFF_AV, w)], pr["gmlp"], [(w, BF16)], ts=256, name=tag("gmlp"))[0]
    xq = p[:, OFF_BQ:OFF_BQ + 3 * w]
    shifted = [_shift_rows(xq, 3 - j) for j in range(4)]
    dn_rows = [(a, 0, 3 * w) for a in shifted] + [(p, OFF_BD, 128)]
    q, k, v, beta, g = _row_fwd(_fn_dn_in, dn_rows, pr["dn_in"], [(w, F32)] * 5, ts=256, name=tag("dn_in"))
    o, states = _delta_fwd(q, k, v, beta, g, name=tag("delta"))
    yb = _row_fwd(_fn_dn_out, [(o, 0, w), (p, OFF_BZ, w)], pr["dn_g"], [(w, BF16)], ts=512, name=tag("dn_out"))[0]
    yc = _sb_fwd(p, name=tag("sb"))
    ys = [ya, yb, yc]
    proj = [_mm(ys[i], wl["w_branch"][i], name=tag(f"branch{i}")) for i in range(3)]
    merge_rows = [(a, 0, D_MODEL) for a in proj] + [(p, OFF_GATE + i * D_MODEL, D_MODEL) for i in range(3)]
    m = _row_fwd(_fn_merge, merge_rows, [], [(D_MODEL, BF16)], ts=256, name=tag("merge"))[0]
    mixed = _mm(m, wl["w_out"], name=tag("out"))
    x1 = _row_fwd(_fn_resid_rms, [(x0, 0, D_MODEL), (mixed, 0, D_MODEL)], [pr["g"][1]], [(D_MODEL, F32)], ts=512,
                  name=tag("resid1"))[0]
    h2 = _row_fwd(_fn_rms, [(x1, 0, D_MODEL)], [pr["g"][2]], [(D_MODEL, BF16)], ts=512, name=tag("rms2"))[0]
    a, r = _mm(h2, wl["w_ff1"], name=tag("ff1"), out_dtype=(BF16, BF16),
               finish=lambda t: (jnp.maximum(t, 0.0), jnp.square(jnp.maximum(t, 0.0))))
    f = _mm(r, wl["w_ff2"], name=tag("ff2"))
    x2 = _row_fwd(_fn_resid_rms, [(x1, 0, D_MODEL), (f, 0, D_MODEL)], [pr["g"][3]], [(D_MODEL, F32)], ts=512,
                  name=tag("resid2"))[0]
    saved = dict(x0=x0, h0=h0, p=p, shifted=shifted, q=q, k=k, v=v, beta=beta, g=g, states=states, o=o, ys=ys,
                 proj=proj, m=m, mixed=mixed, x1=x1, h2=h2, a=a, r=r, f=f)
    return x2, saved


def _layer_bwd(dx2, sv, wl, l):
    tag = lambda s: f"{s}_l{l}"
    pr = _layer_params(wl)
    w = BRANCH_WIDTH
    full = lambda a: (a, 0, a.shape[1])
    p = sv["p"]
    (df,), (dg3,) = _row_bwd(_fn_rms_branch, [full(sv["f"])], [pr["g"][3]], [full(dx2)], ts=512, name=tag("resid2_b"))
    da = _mm(df, wl["w_ff2"], trans_b=True, name=tag("ff2_dx"), out_dtype=BF16, beside=(sv["a"],),
             finish=lambda t, relu_a: 2.0 * relu_a.astype(F32) * t)
    dw_ff2 = _mm_tn(sv["r"], df, name=tag("ff2_dw"))
    dh2 = _mm(da, wl["w_ff1"], trans_b=True, name=tag("ff1_dx"))
    dw_ff1 = _mm_tn(sv["h2"], da, name=tag("ff1_dw"))
    (dx1,), (dg2,) = _row_bwd(_fn_rms_keep, [full(sv["x1"])], [pr["g"][2]], [full(dh2), full(dx2)], ts=512,
                              name=tag("rms2_b"))
    (dmixed,), (dg1,) = _row_bwd(_fn_rms_branch, [full(sv["mixed"])], [pr["g"][1]], [full(dx1)], ts=512,
                                 name=tag("resid1_b"))
    dm = _mm(dmixed, wl["w_out"], trans_b=True, name=tag("out_dx"))
    dw_out = _mm_tn(sv["m"], dmixed, name=tag("out_dw"))
    merge_rows = [full(a) for a in sv["proj"]] + [(p, OFF_GATE + i * D_MODEL, D_MODEL) for i in range(3)]
    dmerge, _ = _row_bwd(_fn_merge, merge_rows, [], [full(dm)], ts=256, name=tag("merge_b"))
    dproj, dgates = dmerge[:3], dmerge[3:]
    dys = [_mm(dproj[i], wl["w_branch"][i], trans_b=True, name=tag(f"branch{i}_dx")) for i in range(3)]
    dw_branch = jnp.stack([_mm_tn(sv["ys"][i], dproj[i], name=tag(f"branch{i}_dw")) for i in range(3)])
    (du, dv_a), dgm = _row_bwd(_fn_gmlp, [(p, OFF_AU, w), (p, OFF_AV, w)], pr["gmlp"], [full(dys[0])], ts=256,
                               name=tag("gmlp_b"))
    (do, dz), (d_dn_g,) = _row_bwd(_fn_dn_out, [full(sv["o"]), (p, OFF_BZ, w)], pr["dn_g"], [full(dys[1])], ts=512,
                                   name=tag("dn_out_b"))
    dqkvbg = _delta_bwd(sv["q"], sv["k"], sv["v"], sv["beta"], sv["g"], sv["states"], do, name=tag("delta_b"))
    dn_rows = [full(a) for a in sv["shifted"]] + [(p, OFF_BD, 128)]
    d_in, d_dn_in = _row_bwd(_fn_dn_in, dn_rows, pr["dn_in"], [full(a) for a in dqkvbg], ts=256, name=tag("dn_in_b"))
    dxq = sum(_unshift_rows(d_in[j], 3 - j) for j in range(4))
    dcq, dck, dcv = _sb_bwd(p, dys[2], name=tag("sb_b"))
    s = p.shape[0]
    dp = jnp.concatenate([du, dv_a, dxq, dz] + dgates + [dcq, dck, dcv, d_in[4],
                                                         jnp.zeros((s, P_PAD - OFF_BD - 128), F32)], axis=1)
    dh0 = _mm(dp, wl["w_in"], trans_b=True, name=tag("proj_in_dx"))
    dw_in = _mm_tn(sv["h0"], dp, name=tag("proj_in_dw"))
    db_in = _col_sums(dp, name=tag("bias_b"))
    (dx0,), (dg0,) = _row_bwd(_fn_rms_keep, [full(sv["x0"])], [pr["g"][0]], [full(dh0), full(dx1)], ts=512,
                              name=tag("rms0_b"))
    grads = dict(
        norm_g=jnp.concatenate([dg0, dg1, dg2, dg3], axis=0), w_in=dw_in, b_in=db_in.reshape(-1),
        sgu_ln_g=dgm[0].reshape(-1), sgu_ln_b=dgm[1].reshape(-1), w_spatial=dgm[2],
        b_spatial=dgm[3][:, :GM_GROUPS].T, conv_w=d_dn_in[0], a_log=d_dn_in[1][0, DN_HEADS:2 * DN_HEADS],
        dt_bias=d_dn_in[2][0, DN_HEADS:2 * DN_HEADS], dn_norm_g=d_dn_g.reshape(-1), w_branch=dw_branch,
        w_out=dw_out, w_ff1=dw_ff1, w_ff2=dw_ff2)
    return dx0, grads


def _local_step(x, target, weights):
    saved = []
    h = x
    layers = []
    for l in range(DEPTH):
        wl = {n: weights[n][l] for n in WEIGHT_ORDER}
        layers.append(wl)
        h, sv = _layer_fwd(h, wl, l)
        saved.append(sv)
    sq, dh = _loss_head(h, target, name="loss_head")
    grads = [None] * DEPTH
    for l in reversed(range(DEPTH)):
        dh, grads[l] = _layer_bwd(dh, saved[l], layers[l], l)
    stacked = {n: jnp.stack([grads[l][n].astype(GRAD_DTYPE) for l in range(DEPTH)]) for n in WEIGHT_ORDER}
    return sq[0, 0], dh, stacked


def kernel(x, norm_g, w_in, b_in, sgu_ln_g, sgu_ln_b, w_spatial, b_spatial, conv_w, a_log, dt_bias, dn_norm_g, w_branch, w_out, w_ff1, w_ff2, loss_target, m_norm_g, m_w_in, m_b_in, m_sgu_ln_g, m_sgu_ln_b, m_w_spatial, m_b_spatial, m_conv_w, m_a_log, m_dt_bias, m_dn_norm_g, m_w_branch, m_w_out, m_w_ff1, m_w_ff2, v_norm_g, v_w_in, v_b_in, v_sgu_ln_g, v_sgu_ln_b, v_w_spatial, v_b_spatial, v_conv_w, v_a_log, v_dt_bias, v_dn_norm_g, v_w_branch, v_w_out, v_w_ff1, v_w_ff2):
    local = dict(norm_g=norm_g, w_in=w_in, b_in=b_in, sgu_ln_g=sgu_ln_g, sgu_ln_b=sgu_ln_b, w_spatial=w_spatial,
                 b_spatial=b_spatial, conv_w=conv_w, a_log=a_log, dt_bias=dt_bias, dn_norm_g=dn_norm_g,
                 w_branch=w_branch, w_out=w_out, w_ff1=w_ff1, w_ff2=w_ff2)
    mom1 = dict(norm_g=m_norm_g, w_in=m_w_in, b_in=m_b_in, sgu_ln_g=m_sgu_ln_g, sgu_ln_b=m_sgu_ln_b,
                w_spatial=m_w_spatial, b_spatial=m_b_spatial, conv_w=m_conv_w, a_log=m_a_log, dt_bias=m_dt_bias,
                dn_norm_g=m_dn_norm_g, w_branch=m_w_branch, w_out=m_w_out, w_ff1=m_w_ff1, w_ff2=m_w_ff2)
    mom2 = dict(norm_g=v_norm_g, w_in=v_w_in, b_in=v_b_in, sgu_ln_g=v_sgu_ln_g, sgu_ln_b=v_sgu_ln_b,
                w_spatial=v_w_spatial, b_spatial=v_b_spatial, conv_w=v_conv_w, a_log=v_a_log, dt_bias=v_dt_bias,
                dn_norm_g=v_dn_norm_g, w_branch=v_w_branch, w_out=v_w_out, w_ff1=v_w_ff1, w_ff2=v_w_ff2)
    shard_names = [n for n, _ in SHARDED]
    shard_shapes = [local[n].shape for n in shard_names]
    repl_shapes = [local[n].shape for n in REPLICATED]

    weights = {n: local[n] for n in REPLICATED}
    for names, dtype, call in ((MATMUL_WEIGHTS, BF16, "gather_matmul_weights"), (VECTOR_WEIGHTS, F32, "gather_vectors")):
        gathered = _gather_chips(_pack([local[n] for n in names]).astype(dtype), name=call)
        per_chip = [_unpack(gathered[k], [local[n].shape for n in names]) for k in range(4)]
        for i, n in enumerate(names):
            weights[n] = jnp.concatenate([per_chip[k][i] for k in range(4)], axis=dict(SHARDED)[n])
    weights["w_in"] = _rearrange_w_in(weights["w_in"])
    weights["b_in"] = _rearrange_w_in(weights["b_in"])

    sq, dx, grads = _local_step(x[0], loss_target[0], weights)
    loss = lax.psum(0.5 * sq / D_MODEL, ("x", "y", "c"))
    grads["w_in"] = _restore_w_in(grads["w_in"])
    grads["b_in"] = _restore_w_in(grads["b_in"])

    blocks = jnp.stack([_pack([_chip_slice(grads[n], axis, k) for n, axis in SHARDED] + [grads[n] for n in REPLICATED])
                        for k in range(4)])
    rows = blocks.shape[1]
    half = rows // 2
    c = lax.axis_index("c")
    got = _halves_to_sibling(blocks, name="grads_to_sibling")
    chip_sum = _add_own_half(blocks, got, name="grads_chip_sum", out_dtype=BF16)
    by_chip = _scatter_chips(chip_sum, name="grads_scatter")
    my_half = _add_rows([by_chip[k] for k in range(4)], name="grads_sum")
    other_half = _to_sibling(my_half, name="grads_half_swap")
    total = jnp.concatenate([jnp.where(c == 0, my_half, other_half), jnp.where(c == 0, other_half, my_half)], axis=0)
    g_out = dict(zip(shard_names + list(REPLICATED), _unpack(total, shard_shapes + repl_shapes)))

    d_out, m_out, v_out = {}, {}, {}
    for n in WEIGHT_ORDER:
        d_out[n], m_out[n], v_out[n] = _adamw(local[n], g_out[n], mom1[n], mom2[n], name=f"adamw_{n}")
    return (loss, dx[None], *[g_out[n] for n in WEIGHT_ORDER], *[d_out[n] for n in WEIGHT_ORDER],
            *[m_out[n] for n in WEIGHT_ORDER], *[v_out[n] for n in WEIGHT_ORDER])
```

```python
import functools

import jax
import jax.numpy as jnp
from jax import lax
from jax.experimental import pallas as pl
from jax.experimental.pallas import tpu as pltpu

F32 = jnp.float32
BF16 = jnp.bfloat16
GRAD_DTYPE = BF16

D_MODEL = 1024
DEPTH = 4
BRANCH_WIDTH = 512
GM_CHUNK = 128
GM_GROUPS = 8
DN_HEADS = 4
DN_HEAD_DIM = 128
CONV_WIDTH = 4
DN_HALO = 8
DN_BLOCK = 128
DN_SOLVE_PASSES, DN_STATE_PASSES, DN_OUT_PASSES = 3, 1, 1
SB_HEAD_DIM = 64
SB_BLOCK = 128
SB_QUERY_ROWS = 256
SB_DEAD_LOG = -88.0
D_FF = 4096
P_IN = 7688
P_PAD = 8192
NORM_EPS = 1e-6
ADAM_LR, ADAM_B1, ADAM_B2, ADAM_EPS, ADAM_WD, ADAM_STEP = 0.001, 0.9, 0.999, 1e-08, 0.01, 10

OFF_DN, DN_IN_WIDTH = 0, 2048
OFF_BD = 1536
OFF_GM = 2048
OFF_GATE = 3072
OFF_BZ = 6144
OFF_CQ, OFF_CK, OFF_CV = 6656, 7168, 7680

LANES = 1024
VMEM_LIMIT_BYTES = 56 * 1024 * 1024

_HI = lax.Precision.HIGHEST


def _params(sem):
    return pltpu.CompilerParams(dimension_semantics=sem, vmem_limit_bytes=VMEM_LIMIT_BYTES)


def _mm(a, b, *, name, out_dtype=F32, bias=None, trans_b=False, beside=(), finish=None, tm=1024, tn=1024, tk=1024):
    m, k = a.shape
    n = b.shape[0] if trans_b else b.shape[1]
    tm, tn, tk = min(tm, m), min(tn, n), min(tk, k)
    assert m % tm == 0 and n % tn == 0 and k % tk == 0, (a.shape, b.shape)
    nk = k // tk
    dn = (((1,), (1,)), ((), ())) if trans_b else (((1,), (0,)), ((), ()))
    several = isinstance(out_dtype, tuple)
    out_dtypes = out_dtype if several else (out_dtype,)
    n_in = 2 + (bias is not None) + len(beside)

    def body(*refs):
        a_ref, b_ref = refs[:2]
        o_refs, acc = refs[n_in:-1], refs[-1]
        kk = pl.program_id(2)
        part = lax.dot_general(a_ref[...].astype(BF16), b_ref[...].astype(BF16), dn, preferred_element_type=F32)

        @pl.when(kk == 0)
        def _():
            acc[...] = part

        @pl.when(kk > 0)
        def _():
            acc[...] += part

        @pl.when(kk == nk - 1)
        def _():
            r = acc[...]
            if bias is not None:
                r = r + refs[2][...]
            tiles = (r,) if finish is None else finish(r, *[t[...] for t in refs[n_in - len(beside):n_in]])
            for o_ref, tile in zip(o_refs, tiles if isinstance(tiles, tuple) else (tiles,)):
                o_ref[...] = tile.astype(o_ref.dtype)

    in_specs = [pl.BlockSpec((tm, tk), lambda i, j, kk: (i, kk))]
    if trans_b:
        in_specs.append(pl.BlockSpec((tn, tk), lambda i, j, kk: (j, kk)))
    else:
        in_specs.append(pl.BlockSpec((tk, tn), lambda i, j, kk: (kk, j)))
    args = [a, b]
    if bias is not None:
        in_specs.append(pl.BlockSpec((1, tn), lambda i, j, kk: (0, j)))
        args.append(bias)
    for t in beside:
        assert t.shape == (m, n)
        in_specs.append(pl.BlockSpec((tm, tn), lambda i, j, kk: (i, j)))
        args.append(t)
    res = pl.pallas_call(
        body, name=name, grid=(m // tm, n // tn, nk),
        in_specs=in_specs, out_specs=[pl.BlockSpec((tm, tn), lambda i, j, kk: (i, j)) for _ in out_dtypes],
        out_shape=[jax.ShapeDtypeStruct((m, n), dt) for dt in out_dtypes],
        scratch_shapes=[pltpu.VMEM((tm, tn), F32)],
        compiler_params=_params(("parallel", "parallel", "arbitrary")),
    )(*args)
    return list(res) if several else res[0]


def _mm_tn(a, b, *, name, tm=1024, tn=1024, ts=1024):
    out_dtype = GRAD_DTYPE
    s, ka = a.shape
    n = b.shape[1]
    tm, tn, ts = min(tm, ka), min(tn, n), min(ts, s)
    assert ka % tm == 0 and n % tn == 0 and s % ts == 0, (a.shape, b.shape)
    steps = s // ts

    def body(a_ref, b_ref, o_ref, acc):
        part = lax.dot_general(a_ref[...].astype(BF16), b_ref[...].astype(BF16), (((0,), (0,)), ((), ())),
                               preferred_element_type=F32)

        @pl.when(pl.program_id(2) == 0)
        def _():
            acc[...] = part

        @pl.when(pl.program_id(2) > 0)
        def _():
            acc[...] += part

        @pl.when(pl.program_id(2) == steps - 1)
        def _():
            o_ref[...] = acc[...].astype(out_dtype)

    return pl.pallas_call(
        body, name=name, grid=(ka // tm, n // tn, steps),
        in_specs=[pl.BlockSpec((ts, tm), lambda i, j, r: (r, i)), pl.BlockSpec((ts, tn), lambda i, j, r: (r, j))],
        out_specs=pl.BlockSpec((tm, tn), lambda i, j, r: (i, j)),
        out_shape=jax.ShapeDtypeStruct((ka, n), out_dtype),
        scratch_shapes=[pltpu.VMEM((tm, tn), F32)],
        compiler_params=_params(("parallel", "parallel", "arbitrary")),
    )(a, b)


def _col_block(i, *, c):
    return (i, c)


def _whole(i, *, nd):
    return (0,) * nd


def _row_specs(rows, ts):
    specs = []
    for arr, off, w in rows:
        assert off % w == 0 and arr.shape[0] % ts == 0
        specs.append(pl.BlockSpec((ts, w), functools.partial(_col_block, c=off // w)))
    return specs


def _row_fwd(fn, rows, params, outs, *, ts, name):
    s = rows[0][0].shape[0]
    ts = min(ts, s)
    nr, npar = len(rows), len(params)

    def body(*refs):
        rv = [r[...].astype(F32) for r in refs[:nr]]
        pv = [p[...] for p in refs[nr:nr + npar]]
        for o_ref, val in zip(refs[nr + npar:], fn(pv, rv)):
            o_ref[...] = val.astype(o_ref.dtype)

    in_specs = _row_specs(rows, ts) + [pl.BlockSpec(p.shape, functools.partial(_whole, nd=p.ndim)) for p in params]
    res = pl.pallas_call(
        body, name=name, grid=(s // ts,), in_specs=in_specs,
        out_specs=[pl.BlockSpec((ts, w), lambda i: (i, 0)) for w, _ in outs],
        out_shape=[jax.ShapeDtypeStruct((s, w), dt) for w, dt in outs],
        compiler_params=_params(("parallel",)),
    )(*[r[0] for r in rows], *params)
    return list(res)


def _row_bwd(fn, rows, params, cts, *, ts, name, into=None):
    s = rows[0][0].shape[0]
    ts = min(ts, s)
    nr, npar, nc = len(rows), len(params), len(cts)
    n_in = nr + npar + nc + (into is not None)

    def body(*refs):
        rv = [r[...].astype(F32) for r in refs[:nr]]
        pv = [p[...] for p in refs[nr:nr + npar]]
        cv = [c[...].astype(F32) for c in refs[nr + npar:nr + npar + nc]]
        out_refs = refs[n_in:]
        _, vjp = jax.vjp(lambda p, r: tuple(fn(p, r)), pv, rv)
        dp, dr = vjp(tuple(cv))
        for o_ref, val in zip(out_refs[:nr], dr):
            o_ref[...] = val

        @pl.when(pl.program_id(0) == 0)
        def _():
            for o_ref in out_refs[nr:]:
                o_ref[...] = jnp.zeros_like(o_ref)

        for o_ref, val in zip(out_refs[nr:], dp):
            o_ref[...] += val

    in_specs = (_row_specs(rows, ts) + [pl.BlockSpec(p.shape, functools.partial(_whole, nd=p.ndim)) for p in params]
                + _row_specs(cts, ts))
    out_specs = [pl.BlockSpec((ts, w), lambda i: (i, 0)) for _, _, w in rows]
    out_shape = [jax.ShapeDtypeStruct((s, w), F32) for _, _, w in rows]
    args = [r[0] for r in rows] + list(params) + [c[0] for c in cts]
    aliases = {}
    if into is not None:
        buffer, at = into
        in_specs.append(pl.BlockSpec(memory_space=pl.ANY))
        args.append(buffer)
        out_specs[at] = _row_specs([(buffer,) + tuple(rows[at][1:])], ts)[0]
        out_shape[at] = jax.ShapeDtypeStruct(buffer.shape, buffer.dtype)
        aliases = {n_in - 1: at}
    out_specs += [pl.BlockSpec(p.shape, functools.partial(_whole, nd=p.ndim)) for p in params]
    out_shape += [jax.ShapeDtypeStruct(p.shape, F32) for p in params]
    res = pl.pallas_call(
        body, name=name, grid=(s // ts,), in_specs=in_specs, out_specs=out_specs, out_shape=out_shape,
        input_output_aliases=aliases, compiler_params=_params(("arbitrary",)),
    )(*args)
    res = list(res)
    return res[:nr], res[nr:]


def _rms(x, g):
    return x * lax.rsqrt(jnp.mean(x * x, axis=-1, keepdims=True) + NORM_EPS) * g


def _gelu(x):
    return 0.5 * x * (1.0 + lax.erf(x * (2.0 ** -0.5)))


def _softplus(x):
    return jnp.maximum(x, 0.0) + jnp.log1p(jnp.exp(-jnp.abs(x)))


def _iota2(shape, dim):
    return lax.broadcasted_iota(jnp.int32, shape, dim)


def _fn_rms(pv, rv):
    return [_rms(rv[0], pv[0])]


def _fn_rms_keep(pv, rv):
    return [_rms(rv[0], pv[0]), rv[0]]


def _fn_resid_rms(pv, rv):
    return [rv[0] + _rms(rv[1], pv[0])]


def _fn_rms_branch(pv, rv):
    return [_rms(rv[0], pv[0])]


def _fn_merge(pv, rv):
    return [sum(jax.nn.sigmoid(rv[3][:, i * D_MODEL:(i + 1) * D_MODEL]) * rv[i] for i in range(3))]


def _fn_gmlp(pv, rv):
    ln_g, ln_b, w_sp, b_t = pv
    u = _gelu(rv[0][:, :BRANCH_WIDTH])
    v = _gelu(rv[0][:, BRANCH_WIDTH:])
    vc = v - jnp.mean(v, axis=-1, keepdims=True)
    v = vc * lax.rsqrt(jnp.mean(vc * vc, axis=-1, keepdims=True) + NORM_EPS) * ln_g + ln_b
    t = GM_CHUNK
    causal = _iota2((t, t), 1) <= _iota2((t, t), 0)
    first = _iota2((t, 128), 1) < 64
    expand = (_iota2((128, BRANCH_WIDTH), 0) == _iota2((128, BRANCH_WIDTH), 1) // 64).astype(F32)
    b_full = jnp.dot(b_t, expand, precision=_HI, preferred_element_type=F32)
    w_bf = [jnp.where(causal, w_sp[g], 0.0).astype(BF16) for g in range(GM_GROUPS)]
    chunks = []
    for c in range(rv[0].shape[0] // t):
        pairs = []
        for p in range(GM_GROUPS // 2):
            vp = v[c * t:(c + 1) * t, 128 * p:128 * (p + 1)].astype(BF16)
            m0 = jnp.dot(w_bf[2 * p], vp, preferred_element_type=F32)
            m1 = jnp.dot(w_bf[2 * p + 1], vp, preferred_element_type=F32)
            pairs.append(jnp.where(first, m0, m1))
        chunks.append(jnp.concatenate(pairs, axis=1) + b_full)
    return [u * jnp.concatenate(chunks, axis=0)]


def _head_expand(col0):
    return (_iota2((128, BRANCH_WIDTH), 0) == _iota2((128, BRANCH_WIDTH), 1) // DN_HEAD_DIM + col0).astype(F32)


@functools.partial(jax.custom_vjp, nondiff_argnums=(1,))
def _roll_rows(x, n):
    return pltpu.roll(x, n, 0)


def _roll_rows_fwd(x, n):
    return pltpu.roll(x, n, 0), None


def _roll_rows_bwd(n, _, ct):
    return (pltpu.roll(ct, ct.shape[0] - n, 0),)


_roll_rows.defvjp(_roll_rows_fwd, _roll_rows_bwd)


def _fn_dn_in(pv, rv):
    conv_w, a_log, dt_b = pv
    cur, before = rv
    x = cur[:, :3 * BRANCH_WIDTH]
    ext = jnp.concatenate([before[:, :3 * BRANCH_WIDTH], x], axis=0)
    c = conv_w[CONV_WIDTH - 1:CONV_WIDTH, :] * x
    for j in range(CONV_WIDTH - 1):
        c = c + conv_w[j:j + 1, :] * _roll_rows(ext, CONV_WIDTH - 1 - j)[DN_HALO:]
    a = c * jax.nn.sigmoid(c)
    outs = []
    for part in range(3):
        heads = []
        for h in range(DN_HEADS):
            lo = part * BRANCH_WIDTH + h * DN_HEAD_DIM
            xh = a[:, lo:lo + DN_HEAD_DIM]
            if part < 2:
                xh = xh * lax.rsqrt(jnp.sum(xh * xh, axis=-1, keepdims=True) + NORM_EPS)
            heads.append(xh)
        outs.append(jnp.concatenate(heads, axis=1))
    bd = cur[:, OFF_BD:OFF_BD + 128]
    beta = jax.nn.sigmoid(bd)
    g = -jnp.exp(a_log) * _softplus(bd + dt_b)
    outs.append(jnp.dot(beta, _head_expand(0), precision=_HI, preferred_element_type=F32))
    outs.append(jnp.dot(g, _head_expand(DN_HEADS), precision=_HI, preferred_element_type=F32))
    return outs


def _dn_in_specs(ts, first_block):
    per = ts // DN_HALO
    return [pl.BlockSpec((ts, DN_IN_WIDTH), lambda i: (first_block(i), OFF_DN // DN_IN_WIDTH)),
            pl.BlockSpec((DN_HALO, DN_IN_WIDTH),
                         lambda i: (jnp.maximum(first_block(i) * per - 1, 0), OFF_DN // DN_IN_WIDTH))]


def _dn_in_fwd(p, params, *, ts, name):
    s = p.shape[0]
    ts = min(ts, s)

    def body(cur_ref, before_ref, cw_ref, al_ref, db_ref, *o_refs):
        before = jnp.where(pl.program_id(0) == 0, 0.0, before_ref[...])
        outs = _fn_dn_in([cw_ref[...], al_ref[...], db_ref[...]], [cur_ref[...], before])
        for o_ref, val in zip(o_refs, outs):
            o_ref[...] = val

    whole = [pl.BlockSpec(a.shape, functools.partial(_whole, nd=a.ndim)) for a in params]
    res = pl.pallas_call(
        body, name=name, grid=(s // ts,), in_specs=_dn_in_specs(ts, lambda i: i) + whole,
        out_specs=[pl.BlockSpec((ts, BRANCH_WIDTH), lambda i: (i, 0))] * 5,
        out_shape=[jax.ShapeDtypeStruct((s, BRANCH_WIDTH), F32)] * 5,
        compiler_params=_params(("parallel",)),
    )(p, p, *params)
    return list(res)


def _dn_in_bwd(p, params, cts, dp, *, ts, name):
    s = p.shape[0]
    ts = min(ts, s)
    nb = s // ts
    block = lambda i: nb - 1 - i

    def body(cur_ref, before_ref, cw_ref, al_ref, db_ref, *rest):
        ct_refs, dx_ref, dpar_refs, halo = rest[:5], rest[6], rest[7:10], rest[10]
        first = pl.program_id(0) == 0

        @pl.when(first)
        def _():
            halo[...] = jnp.zeros_like(halo)
            for o_ref in dpar_refs:
                o_ref[...] = jnp.zeros_like(o_ref)

        before = jnp.where(pl.program_id(0) == nb - 1, 0.0, before_ref[...])
        _, vjp = jax.vjp(lambda pv, cur, bef: tuple(_fn_dn_in(pv, [cur, bef])),
                         [cw_ref[...], al_ref[...], db_ref[...]], cur_ref[...], before)
        dpar, dcur, dbefore = vjp(tuple(c[...] for c in ct_refs))
        dx_ref[...] = jnp.concatenate([dcur[:ts - DN_HALO], dcur[ts - DN_HALO:] + halo[...]], axis=0)
        halo[...] = dbefore
        for o_ref, val in zip(dpar_refs, dpar):
            o_ref[...] += val

    whole = [pl.BlockSpec(a.shape, functools.partial(_whole, nd=a.ndim)) for a in params]
    tile = pl.BlockSpec((ts, BRANCH_WIDTH), lambda i: (block(i), 0))
    res = pl.pallas_call(
        body, name=name, grid=(nb,),
        in_specs=_dn_in_specs(ts, block) + whole + [tile] * 5 + [pl.BlockSpec(memory_space=pl.ANY)],
        out_specs=[pl.BlockSpec((ts, DN_IN_WIDTH), lambda i: (block(i), OFF_DN // DN_IN_WIDTH))] + whole,
        out_shape=[jax.ShapeDtypeStruct(dp.shape, dp.dtype)] + [jax.ShapeDtypeStruct(a.shape, F32) for a in params],
        scratch_shapes=[pltpu.VMEM((DN_HALO, DN_IN_WIDTH), F32)],
        input_output_aliases={10: 0}, compiler_params=_params(("arbitrary",)),
    )(p, p, *params, *cts, dp)
    return res[0], list(res[1:])


def _fn_dn_out(pv, rv):
    heads = []
    for h in range(DN_HEADS):
        sl = slice(h * DN_HEAD_DIM, (h + 1) * DN_HEAD_DIM)
        z = rv[1][:, sl]
        heads.append(_rms(rv[0][:, sl], pv[0]) * (z * jax.nn.sigmoid(z)))
    return [jnp.concatenate(heads, axis=1)]


_DIMS = {"nn": (((1,), (0,)), ((), ())), "nt": (((1,), (1,)), ((), ())), "tn": (((0,), (0,)), ((), ()))}
_DIMS_BWD = {"nn": (("nt", "c", "b"), ("tn", "a", "c")), "nt": (("nn", "c", "b"), ("tn", "c", "a")),
             "tn": (("nt", "b", "c"), ("nn", "a", "c"))}


def _bf16_dot(a, b, kind):
    return lax.dot_general(a.astype(BF16), b.astype(BF16), _DIMS[kind], preferred_element_type=F32)


def _pdot_raw(a, b, kind, mode):
    if mode == 1:
        return _bf16_dot(a, b, kind)
    if mode == 6:
        return lax.dot_general(a, b, _DIMS[kind], precision=_HI, preferred_element_type=F32)
    b_hi, b_lo = _split_bf16(b)
    if mode == 3:
        a_hi, a_lo = _split_bf16(a)
        return _bf16_dot(a_hi, b_hi, kind) + (_bf16_dot(a_hi, b_lo, kind) + _bf16_dot(a_lo, b_hi, kind))
    b_rest = (b - b_hi.astype(F32) - b_lo.astype(F32)).astype(BF16)
    return _bf16_dot(a, b_hi, kind) + (_bf16_dot(a, b_lo, kind) + _bf16_dot(a, b_rest, kind))


@functools.partial(jax.custom_vjp, nondiff_argnums=(2, 3))
def _pdot(a, b, kind, mode):
    return _pdot_raw(a, b, kind, mode)


def _pdot_fwd(a, b, kind, mode):
    return _pdot_raw(a, b, kind, mode), (a, b)


def _pdot_bwd(kind, mode, res, ct):
    ops = {"a": res[0], "b": res[1], "c": ct}
    (ka, a1, a2), (kb, b1, b2) = _DIMS_BWD[kind]
    if mode == "count":
        return jnp.zeros_like(res[0]), _pdot(ops[b1], ops[b2], kb, mode)
    return _pdot(ops[a1], ops[a2], ka, mode), _pdot(ops[b1], ops[b2], kb, mode)


_pdot.defvjp(_pdot_fwd, _pdot_bwd)


@jax.custom_vjp
def _unit_lower_inverses(mats):
    c = mats[0].shape[0]
    row, col = _iota2((c, c), 0), _iota2((c, c), 1)
    x = [(row == col).astype(F32) for _ in mats]
    shift = 0
    while (1 << shift) < c:
        pair = jnp.right_shift(row, shift + 1) == jnp.right_shift(col, shift + 1)
        between = pair & (jnp.right_shift(row, shift) != jnp.right_shift(col, shift))
        q = [jnp.where(between, a, 0.0) for a in mats]
        qd = [_pdot_raw(qi, xi, "nn", DN_SOLVE_PASSES) for qi, xi in zip(q, x)]
        x = [xi - _pdot_raw(xi, m, "nn", DN_SOLVE_PASSES) for xi, m in zip(x, qd)]
        shift += 1
    return tuple(x)


def _unit_lower_inverses_fwd(mats):
    x = _unit_lower_inverses(mats)
    return x, x


def _unit_lower_inverses_bwd(x, dx):
    inner = [_pdot_raw(d, xi, "nt", DN_SOLVE_PASSES) for d, xi in zip(dx, x)]
    return (tuple(-_pdot_raw(xi, m, "tn", DN_SOLVE_PASSES) for xi, m in zip(x, inner)),)


_unit_lower_inverses.defvjp(_unit_lower_inverses_fwd, _unit_lower_inverses_bwd)


def _head(h):
    return slice(h * DN_HEAD_DIM, (h + 1) * DN_HEAD_DIM)


def _delta_chunk(states, q, k, v, beta, g):
    c = DN_BLOCK
    heads = range(DN_HEADS)
    row, col = _iota2((c, c), 0), _iota2((c, c), 1)
    tri, strict = col <= row, col < row
    counts = jnp.concatenate([tri.astype(BF16), jnp.ones((c, c), BF16)], axis=0)
    sums = _pdot(counts, g, "nn", "count")
    gc = [sums[:c, _head(h)] for h in heads]
    gl = [sums[c:, _head(h)] for h in heads]
    qh, kh, vh, bh = ([t[:, _head(h)] for h in heads] for t in (q * (DN_HEAD_DIM ** -0.5), k, v, beta))
    decay = [jnp.where(tri, jnp.exp(jnp.where(tri, gc[h] - gc[h].T, 0.0)), 0.0) for h in heads]
    solve, carry, out = DN_SOLVE_PASSES, DN_STATE_PASSES, DN_OUT_PASSES
    kk = [_pdot(kh[h], kh[h], "nt", solve) for h in heads]
    x = _unit_lower_inverses(tuple(jnp.where(strict, bh[h] * kk[h] * decay[h], 0.0) for h in heads))
    eg = [jnp.exp(gc[h]) for h in heads]
    u = [_pdot(x[h], vh[h] * bh[h], "nn", solve) for h in heads]
    wk = [_pdot(x[h], kh[h] * (bh[h] * eg[h]), "nn", solve) for h in heads]
    qk = [jnp.where(tri, _pdot(qh[h], kh[h], "nt", out) * decay[h], 0.0) for h in heads]
    v_new = [u[h] - _pdot(wk[h], states[h], "nn", carry) for h in heads]
    o = [_pdot(qh[h] * eg[h], states[h], "nn", out) + _pdot(qk[h], v_new[h], "nn", out) for h in heads]
    nxt = [states[h] * jnp.exp(gl[h]) + _pdot(kh[h] * jnp.exp(gl[h] - gc[h]), v_new[h], "tn", carry)
           for h in heads]
    return jnp.concatenate(o, axis=1), tuple(nxt)


def _delta_fwd(q, k, v, beta, g, *, name):
    s = q.shape[0]
    c = DN_BLOCK
    nc = s // c

    def body(q_ref, k_ref, v_ref, b_ref, g_ref, o_ref, sp_ref, st):
        @pl.when(pl.program_id(0) == 0)
        def _():
            st[...] = jnp.zeros_like(st)

        states = tuple(st[h] for h in range(DN_HEADS))
        for h in range(DN_HEADS):
            sp_ref[0, h] = states[h]
        o, nxt = _delta_chunk(states, q_ref[...], k_ref[...], v_ref[...], b_ref[...], g_ref[...])
        for h in range(DN_HEADS):
            st[h] = nxt[h]
        o_ref[...] = o

    blk = pl.BlockSpec((c, BRANCH_WIDTH), lambda n: (n, 0))
    return pl.pallas_call(
        body, name=name, grid=(nc,), in_specs=[blk] * 5,
        out_specs=[blk, pl.BlockSpec((1, DN_HEADS, DN_HEAD_DIM, DN_HEAD_DIM), lambda n: (n, 0, 0, 0))],
        out_shape=[jax.ShapeDtypeStruct((s, BRANCH_WIDTH), F32),
                   jax.ShapeDtypeStruct((nc, DN_HEADS, DN_HEAD_DIM, DN_HEAD_DIM), F32)],
        scratch_shapes=[pltpu.VMEM((DN_HEADS, DN_HEAD_DIM, DN_HEAD_DIM), F32)],
        compiler_params=_params(("arbitrary",)),
    )(q, k, v, beta, g)


def _delta_bwd(q, k, v, beta, g, states, do, *, name):
    s = q.shape[0]
    c = DN_BLOCK
    nc = s // c

    def body(q_ref, k_ref, v_ref, b_ref, g_ref, sp_ref, do_ref, dq_ref, dk_ref, dv_ref, db_ref, dg_ref, dst):
        @pl.when(pl.program_id(0) == 0)
        def _():
            dst[...] = jnp.zeros_like(dst)

        states = tuple(sp_ref[0, h] for h in range(DN_HEADS))
        _, vjp = jax.vjp(_delta_chunk, states, q_ref[...], k_ref[...], v_ref[...], b_ref[...], g_ref[...])
        d = vjp((do_ref[...], tuple(dst[h] for h in range(DN_HEADS))))
        for h in range(DN_HEADS):
            dst[h] = d[0][h]
        for o_ref, val in zip((dq_ref, dk_ref, dv_ref, db_ref, dg_ref), d[1:]):
            o_ref[...] = val

    blk = pl.BlockSpec((c, BRANCH_WIDTH), lambda n: (nc - 1 - n, 0))
    res = pl.pallas_call(
        body, name=name, grid=(nc,),
        in_specs=[blk] * 5 + [pl.BlockSpec((1, DN_HEADS, DN_HEAD_DIM, DN_HEAD_DIM), lambda n: (nc - 1 - n, 0, 0, 0)), blk],
        out_specs=[blk] * 5, out_shape=[jax.ShapeDtypeStruct((s, BRANCH_WIDTH), F32)] * 5,
        scratch_shapes=[pltpu.VMEM((DN_HEADS, DN_HEAD_DIM, DN_HEAD_DIM), F32)],
        compiler_params=_params(("arbitrary",)),
    )(q, k, v, beta, g, states, do)
    return list(res)


def _split_bf16(x):
    hi = x.astype(BF16)
    return hi, (x - hi.astype(F32)).astype(BF16)


def _sb_consts():
    tq, tk = SB_QUERY_ROWS, SB_BLOCK
    row, col = _iota2((tk, tk), 0), _iota2((tk, tk), 1)
    ones = jnp.ones((tk, tk), BF16)
    later = jnp.concatenate([(row > col).astype(BF16), ones], axis=1)
    from_here = jnp.concatenate([(row >= col).astype(BF16), ones], axis=1)
    first = _iota2((tq, 128), 1) < SB_HEAD_DIM
    return later, from_here, first


def _sb_causal(d):
    tq, tk = SB_QUERY_ROWS, SB_BLOCK
    return _iota2((tq, tk), 1) + d * tk < _iota2((tq, tk), 0)


def _sums(x, mat):
    hi, lo = _split_bf16(x)
    return jnp.dot(hi, mat, preferred_element_type=F32) + jnp.dot(lo, mat, preferred_element_type=F32)


def _sb_weights(qs, kb, accs, later, causal):
    tk = SB_BLOCK
    z = [lax.dot_general(qh, kb, (((1,), (1,)), ((), ())), preferred_element_type=F32) for qh in qs]
    lk = [-_softplus(zh) for zh in z]
    if causal is not None:
        lk = [jnp.where(causal, v, 0.0) for v in lk]
    cs = [_sums(v, later) for v in lk]
    e = [z[h] + lk[h] + cs[h][:, :tk] + accs[h] for h in range(2)]
    if causal is not None:
        e = [jnp.where(causal, v, -1e30) for v in e]
    return lk, [jnp.exp(v) for v in e], [v[:, tk:] for v in cs]


def _sb_alive(accs):
    return jnp.max(jnp.maximum(accs[0], accs[1])) > SB_DEAD_LOG


def _sb_sweep(i, block, carry, accs_of, stop=None):
    per = SB_QUERY_ROWS // SB_BLOCK
    for d in reversed(range(per)):
        carry = block(i * per + d, carry, _sb_causal(d))
    if stop is not None:
        return lax.fori_loop(0, i * per - 1 - stop, lambda jj, cr: block(i * per - 1 - jj, cr, None), carry)

    def step(state):
        j, _, cr = state
        cr = block(j, cr, None)
        return j - 1, _sb_alive(accs_of(cr)), cr

    j, _, carry = lax.while_loop(lambda st: jnp.logical_and(st[0] >= 0, st[1]), step,
                                 (i * per - 1, _sb_alive(accs_of(carry)), carry))
    return carry, j


def _sb_fwd(p, *, name):
    s = p.shape[0]
    t, tk = SB_QUERY_ROWS, SB_BLOCK
    scale = SB_HEAD_DIM ** -0.5

    def body(q_ref, k_ref, v_ref, o_ref):
        i = pl.program_id(1)
        later, _, first = _sb_consts()
        q = q_ref[...] * scale
        qs = (jnp.where(first, q, 0.0).astype(BF16), jnp.where(first, 0.0, q).astype(BF16))

        def block(j, carry, causal):
            start = pl.multiple_of(j * tk, tk)
            kb = k_ref[pl.ds(start, tk), :].astype(BF16)
            vb = v_ref[pl.ds(start, tk), :].astype(BF16)
            _, w, tot = _sb_weights(qs, kb, [carry[h][1] for h in range(2)], later, causal)
            out = [jnp.dot(w[h].astype(BF16), vb, preferred_element_type=F32) for h in range(2)]
            return tuple((carry[h][0] + out[h], carry[h][1] + tot[h]) for h in range(2))

        zero = jnp.zeros((t, 128), F32)
        carry, _ = _sb_sweep(i, block, ((zero, zero), (zero, zero)), lambda cr: (cr[0][1], cr[1][1]))
        o_ref[...] = jnp.where(first, carry[0][0], carry[1][0])

    return pl.pallas_call(
        body, name=name, grid=(BRANCH_WIDTH // 128, s // t),
        in_specs=[pl.BlockSpec((t, 128), lambda pr, i: (i, OFF_CQ // 128 + pr)),
                  pl.BlockSpec((s, 128), lambda pr, i: (0, OFF_CK // 128 + pr)),
                  pl.BlockSpec((s, 128), lambda pr, i: (0, OFF_CV // 128 + pr))],
        out_specs=pl.BlockSpec((t, 128), lambda pr, i: (i, pr)),
        out_shape=jax.ShapeDtypeStruct((s, BRANCH_WIDTH), F32),
        compiler_params=_params(("arbitrary", "arbitrary")),
    )(p, p, p)


def _sb_bwd(p, do, *, name):
    s = p.shape[0]
    t, tk = SB_QUERY_ROWS, SB_BLOCK
    scale = SB_HEAD_DIM ** -0.5

    def body(q_ref, k_ref, v_ref, do_ref, dq_ref, dk_ref, dv_ref):
        i = pl.program_id(1)

        @pl.when(i == 0)
        def _():
            dk_ref[...] = jnp.zeros_like(dk_ref)
            dv_ref[...] = jnp.zeros_like(dv_ref)

        later, from_here, first = _sb_consts()
        q = q_ref[...] * scale
        do = do_ref[...]
        qs = (jnp.where(first, q, 0.0).astype(BF16), jnp.where(first, 0.0, q).astype(BF16))
        dos = (jnp.where(first, do, 0.0).astype(BF16), jnp.where(first, 0.0, do).astype(BF16))

        def total(j, carry, causal):
            start = pl.multiple_of(j * tk, tk)
            kb = k_ref[pl.ds(start, tk), :].astype(BF16)
            vb = v_ref[pl.ds(start, tk), :].astype(BF16)
            _, w, tot = _sb_weights(qs, kb, [carry[h][0] for h in range(2)], later, causal)
            dw = [lax.dot_general(dos[h], vb, (((1,), (1,)), ((), ())), preferred_element_type=F32) for h in range(2)]
            tde = [_sums(dw[h] * w[h], from_here)[:, tk:] for h in range(2)]
            return tuple((carry[h][0] + tot[h], carry[h][1] + tde[h]) for h in range(2))

        zero = jnp.zeros((t, 128), F32)
        sums, stop = _sb_sweep(i, total, ((zero, zero), (zero, zero)), lambda cr: (cr[0][0], cr[1][0]))
        deltas = (sums[0][1], sums[1][1])

        def block(j, carry, causal):
            start = pl.multiple_of(j * tk, tk)
            kb = k_ref[pl.ds(start, tk), :].astype(BF16)
            vb = v_ref[pl.ds(start, tk), :].astype(BF16)
            both = range(2)
            tn = (((0,), (0,)), ((), ()))
            lk, w, tot = _sb_weights(qs, kb, [carry[h][1] for h in both], later, causal)
            dw = [lax.dot_general(dos[h], vb, (((1,), (1,)), ((), ())), preferred_element_type=F32) for h in both]
            de = [dw[h] * w[h] for h in both]
            cs = [_sums(de[h], from_here) for h in both]
            keep = [jnp.exp(lk[h]) for h in both]
            dz = [de[h] * keep[h] - (deltas[h] - (cs[h][:, :tk] + carry[h][2])) * (1.0 - keep[h]) for h in both]
            if causal is not None:
                dz = [jnp.where(causal, v, 0.0) for v in dz]
            dzb = [v.astype(BF16) for v in dz]
            dq = [jnp.dot(dzb[h], kb, preferred_element_type=F32) for h in both]
            dk = [lax.dot_general(dzb[h], qs[h], tn, preferred_element_type=F32) for h in both]
            dv = [lax.dot_general(w[h].astype(BF16), dos[h], tn, preferred_element_type=F32) for h in both]
            dk_ref[pl.ds(start, tk), :] += dk[0] + dk[1]
            dv_ref[pl.ds(start, tk), :] += dv[0] + dv[1]
            return tuple((carry[h][0] + dq[h], carry[h][1] + tot[h], carry[h][2] + cs[h][:, tk:]) for h in both)

        carry = _sb_sweep(i, block, ((zero, zero, zero), (zero, zero, zero)), None, stop=stop)
        dq_ref[...] = jnp.where(first, carry[0][0], carry[1][0]) * scale

    qblk = lambda off: pl.BlockSpec((t, 128), lambda pr, i: (i, off // 128 + pr))
    full = lambda off: pl.BlockSpec((s, 128), lambda pr, i: (0, off // 128 + pr))
    res = pl.pallas_call(
        body, name=name, grid=(BRANCH_WIDTH // 128, s // t),
        in_specs=[qblk(OFF_CQ), full(OFF_CK), full(OFF_CV), qblk(0)],
        out_specs=[qblk(0), full(0), full(0)],
        out_shape=[jax.ShapeDtypeStruct((s, BRANCH_WIDTH), F32)] * 3,
        compiler_params=_params(("arbitrary", "arbitrary")),
    )(p, p, p, do)
    return list(res)


def _loss_head(y, target, *, name, ts=512):
    s, d = y.shape
    ts = min(ts, s)

    def body(y_ref, t_ref, sq_ref, dy_ref):
        @pl.when(pl.program_id(0) == 0)
        def _():
            sq_ref[...] = jnp.zeros_like(sq_ref)

        err = y_ref[...] - t_ref[...]
        dy_ref[...] = err * (1.0 / d)
        tot = jnp.sum(jnp.sum(err * err, axis=1, keepdims=True), axis=0, keepdims=True)
        sq_ref[...] += jnp.broadcast_to(tot, sq_ref.shape)

    blk = pl.BlockSpec((ts, d), lambda i: (i, 0))
    return pl.pallas_call(
        body, name=name, grid=(s // ts,), in_specs=[blk, blk],
        out_specs=[pl.BlockSpec((1, 128), lambda i: (0, 0)), blk],
        out_shape=[jax.ShapeDtypeStruct((1, 128), F32), jax.ShapeDtypeStruct((s, d), F32)],
        compiler_params=_params(("arbitrary",)),
    )(y, target)


def _row_tile(r, limit=512):
    return max(t for t in range(8, limit + 1, 8) if r % t == 0)


def _adamw(w, g, m, v, *, name):
    shape = w.shape
    lanes = shape[-1]
    w, g, m, v = (a.reshape(-1, lanes) for a in (w, g, m, v))
    r = w.shape[0]
    ts = r if r <= 256 else _row_tile(r, 256 if lanes > LANES else 512)

    def body(w_ref, g_ref, m_ref, v_ref, d_ref, nm_ref, nv_ref):
        gv = g_ref[...]
        m_new = ADAM_B1 * m_ref[...] + (1.0 - ADAM_B1) * gv
        v_new = ADAM_B2 * v_ref[...] + (1.0 - ADAM_B2) * jnp.square(gv)
        m_hat = m_new / (1.0 - ADAM_B1 ** ADAM_STEP)
        v_hat = v_new / (1.0 - ADAM_B2 ** ADAM_STEP)
        d_ref[...] = -ADAM_LR * (m_hat / (jnp.sqrt(v_hat) + ADAM_EPS) + ADAM_WD * w_ref[...])
        nm_ref[...] = m_new
        nv_ref[...] = v_new

    blk = pl.BlockSpec((ts, lanes), lambda i: (i, 0))
    res = pl.pallas_call(
        body, name=name, grid=(r // ts,), in_specs=[blk] * 4, out_specs=[blk] * 3,
        out_shape=[jax.ShapeDtypeStruct((r, lanes), F32)] * 3,
        compiler_params=_params(("parallel",)),
    )(w, g, m, v)
    return [a.reshape(shape) for a in res]


def _add_rows(terms, *, name, out_dtype=F32):
    r = terms[0].shape[0]
    ts = _row_tile(r)

    def body(*refs):
        acc = refs[0][...].astype(F32)
        for ref in refs[1:-1]:
            acc = acc + ref[...].astype(F32)
        refs[-1][...] = acc.astype(out_dtype)

    blk = pl.BlockSpec((ts, LANES), lambda i: (i, 0))
    return pl.pallas_call(
        body, name=name, grid=(r // ts,), in_specs=[blk] * len(terms), out_specs=blk,
        out_shape=jax.ShapeDtypeStruct((r, LANES), out_dtype), compiler_params=_params(("parallel",)),
    )(*terms)


def _col_sums(a, *, name, ts=256):
    s, n = a.shape
    ts = min(ts, s)

    def body(a_ref, o_ref):
        @pl.when(pl.program_id(0) == 0)
        def _():
            o_ref[...] = jnp.zeros_like(o_ref)

        o_ref[...] += jnp.sum(a_ref[...], axis=0, keepdims=True)

    return pl.pallas_call(
        body, name=name, grid=(s // ts,), in_specs=[pl.BlockSpec((ts, n), lambda i: (i, 0))],
        out_specs=pl.BlockSpec((1, n), lambda i: (0, 0)), out_shape=jax.ShapeDtypeStruct((1, n), F32),
        compiler_params=_params(("arbitrary",)),
    )(a)


_HBM = pl.BlockSpec(memory_space=pltpu.HBM)
_MESH = pl.DeviceIdType.MESH


def _other_chips(x, y):
    return [(1 - x, y), (x, 1 - y), (1 - x, 1 - y)]


def _gather_chips(shard, *, name):
    r, lanes = shard.shape
    half = r // 2
    assert half * 2 == r

    def body(in_ref, out_ref, send_sems, recv_sems):
        x, y, c = lax.axis_index("x"), lax.axis_index("y"), lax.axis_index("c")
        me = 2 * x + y
        sibling = (x, y, 1 - c)
        chips = _other_chips(x, y)

        def copy(sem, chip, core_half, to):
            rows = out_ref.at[chip, pl.ds(core_half * half, half)]
            return pltpu.make_async_remote_copy(src_ref=rows, dst_ref=rows, send_sem=send_sems.at[sem],
                                                recv_sem=recv_sems.at[sem], device_id=to, device_id_type=_MESH)

        first = []
        for kk, (px, py) in enumerate(chips):
            cp = pltpu.make_async_remote_copy(
                src_ref=in_ref.at[pl.ds(c * half, half)], dst_ref=out_ref.at[me, pl.ds(c * half, half)],
                send_sem=send_sems.at[kk], recv_sem=recv_sems.at[kk], device_id=(px, py, c), device_id_type=_MESH)
            cp.start()
            first.append(cp)
        passed = [copy(3 + kk, 2 * px + py, c, sibling) for kk, (px, py) in enumerate(chips)]
        for kk, (px, py) in enumerate(chips):
            copy(kk, 2 * px + py, c, (px, py, c)).wait_recv()
            passed[kk].start()
        for kk, (px, py) in enumerate(chips):
            copy(3 + kk, 2 * px + py, 1 - c, sibling).wait_recv()
        for cp in first + passed:
            cp.wait_send()

    gathered = pl.pallas_call(
        body, name=name, in_specs=[_HBM], out_specs=_HBM,
        out_shape=jax.ShapeDtypeStruct((4, r, lanes), shard.dtype),
        scratch_shapes=[pltpu.SemaphoreType.DMA((6,)), pltpu.SemaphoreType.DMA((6,))],
    )(shard)
    me = 2 * lax.axis_index("x") + lax.axis_index("y")
    return jnp.where(lax.broadcasted_iota(jnp.int32, (4, 1, 1), 0) == me, shard[None], gathered)


def _to_sibling(block, *, name):
    def body(in_ref, out_ref, send_sem, recv_sem):
        x, y, c = lax.axis_index("x"), lax.axis_index("y"), lax.axis_index("c")
        cp = pltpu.make_async_remote_copy(src_ref=in_ref, dst_ref=out_ref, send_sem=send_sem, recv_sem=recv_sem,
                                          device_id=(x, y, 1 - c), device_id_type=_MESH)
        cp.start()
        cp.wait()

    return pl.pallas_call(
        body, name=name, in_specs=[_HBM], out_specs=_HBM, out_shape=jax.ShapeDtypeStruct(block.shape, block.dtype),
        scratch_shapes=[pltpu.SemaphoreType.DMA, pltpu.SemaphoreType.DMA],
    )(block)


def _halves_to_sibling(blocks, *, name):
    n, r, lanes = blocks.shape
    half = r // 2

    def body(in_ref, out_ref, send_sem, recv_sem):
        x, y, c = lax.axis_index("x"), lax.axis_index("y"), lax.axis_index("c")
        cp = pltpu.make_async_remote_copy(src_ref=in_ref.at[pl.ds(0, n), pl.ds((1 - c) * half, half)], dst_ref=out_ref,
                                          send_sem=send_sem, recv_sem=recv_sem, device_id=(x, y, 1 - c),
                                          device_id_type=_MESH)
        cp.start()
        cp.wait()

    return pl.pallas_call(
        body, name=name, in_specs=[_HBM], out_specs=_HBM, out_shape=jax.ShapeDtypeStruct((n, half, lanes), blocks.dtype),
        scratch_shapes=[pltpu.SemaphoreType.DMA, pltpu.SemaphoreType.DMA],
    )(blocks)


def _add_own_half(blocks, got, *, name, out_dtype):
    n, r, lanes = blocks.shape
    half = r // 2
    ts = _row_tile(half)
    steps = half // ts
    core = lax.axis_index("c").astype(jnp.int32).reshape(1)

    def body(core_ref, a_ref, b_ref, o_ref):
        o_ref[...] = (a_ref[...].astype(F32) + b_ref[...].astype(F32)).astype(out_dtype)

    return pl.pallas_call(
        body, name=name,
        grid_spec=pltpu.PrefetchScalarGridSpec(
            num_scalar_prefetch=1, grid=(n, steps),
            in_specs=[pl.BlockSpec((1, ts, lanes), lambda k, i, core_ref: (k, core_ref[0] * steps + i, 0)),
                      pl.BlockSpec((1, ts, lanes), lambda k, i, core_ref: (k, i, 0))],
            out_specs=pl.BlockSpec((1, ts, lanes), lambda k, i, core_ref: (k, i, 0))),
        out_shape=jax.ShapeDtypeStruct((n, half, lanes), out_dtype),
        compiler_params=_params(("parallel", "parallel")),
    )(core, blocks, got)


def _scatter_chips(blocks, *, name):
    _, r, lanes = blocks.shape

    def body(in_ref, out_ref, send_sems, recv_sems, local_sem):
        x, y, c = lax.axis_index("x"), lax.axis_index("y"), lax.axis_index("c")
        me = 2 * x + y
        mine = pltpu.make_async_copy(in_ref.at[me], out_ref.at[me], local_sem)
        mine.start()
        copies = []
        for kk, (px, py) in enumerate(_other_chips(x, y)):
            cp = pltpu.make_async_remote_copy(src_ref=in_ref.at[2 * px + py], dst_ref=out_ref.at[me],
                                              send_sem=send_sems.at[kk], recv_sem=recv_sems.at[kk],
                                              device_id=(px, py, c), device_id_type=_MESH)
            cp.start()
            copies.append(cp)
        for kk, (px, py) in enumerate(_other_chips(x, y)):
            pltpu.make_async_remote_copy(src_ref=in_ref.at[me], dst_ref=out_ref.at[2 * px + py],
                                         send_sem=send_sems.at[kk], recv_sem=recv_sems.at[kk], device_id=(px, py, c),
                                         device_id_type=_MESH).wait_recv()
        for cp in copies:
            cp.wait_send()
        mine.wait()

    return pl.pallas_call(
        body, name=name, in_specs=[_HBM], out_specs=_HBM, out_shape=jax.ShapeDtypeStruct((4, r, lanes), blocks.dtype),
        scratch_shapes=[pltpu.SemaphoreType.DMA((3,)), pltpu.SemaphoreType.DMA((3,)), pltpu.SemaphoreType.DMA],
    )(blocks)


SHARDED = (("norm_g", 2), ("w_in", 2), ("conv_w", 2), ("w_branch", 3), ("w_out", 1), ("w_ff1", 2), ("w_ff2", 1))
MATMUL_WEIGHTS = ("w_in", "w_branch", "w_out", "w_ff1", "w_ff2")
VECTOR_WEIGHTS = ("norm_g", "conv_w")
REPLICATED = ("b_in", "sgu_ln_g", "sgu_ln_b", "w_spatial", "b_spatial", "a_log", "dt_bias", "dn_norm_g")
WEIGHT_ORDER = ("norm_g", "w_in", "b_in", "sgu_ln_g", "sgu_ln_b", "w_spatial", "b_spatial", "conv_w", "a_log",
                "dt_bias", "dn_norm_g", "w_branch", "w_out", "w_ff1", "w_ff2")
PACK_ROW_MULTIPLE = 32


def _rows_of(shape):
    n = 1
    for dim in shape:
        n *= dim
    return -(-n // LANES)


def _pack(arrays):
    parts = []
    for a in arrays:
        flat = a.reshape(-1)
        pad = _rows_of(a.shape) * LANES - flat.shape[0]
        if pad:
            flat = jnp.concatenate([flat, jnp.zeros((pad,), flat.dtype)])
        parts.append(flat.reshape(-1, LANES))
    rows = sum(p.shape[0] for p in parts)
    pad = -rows % PACK_ROW_MULTIPLE
    if pad:
        parts.append(jnp.zeros((pad, LANES), parts[0].dtype))
    return jnp.concatenate(parts, axis=0)


def _unpack(buf, shapes):
    out, row = [], 0
    for shape in shapes:
        n = 1
        for dim in shape:
            n *= dim
        rows = _rows_of(shape)
        out.append(buf[row:row + rows].reshape(-1)[:n].reshape(shape))
        row += rows
    return out


def _chip_slice(a, axis, k):
    size = a.shape[axis] // 4
    return lax.slice_in_dim(a, k * size, (k + 1) * size, axis=axis)


def _rearrange_w_in(w):
    pad = jnp.zeros(w.shape[:-1] + (P_PAD - P_IN,), w.dtype)
    return jnp.concatenate([w[..., 1024:2560], w[..., 3072:3080], pad, w[..., 0:1024], w[..., 4616:7688],
                            w[..., 2560:3072], w[..., 3080:4616]], axis=-1)


def _restore_w_in(w):
    return jnp.concatenate([w[..., 2048:3072], w[..., 0:1536], w[..., 6144:6656], w[..., 1536:1544],
                            w[..., 6656:8192], w[..., 3072:6144]], axis=-1)


def _layer_params(wl):
    row = lambda v: v.reshape(1, -1)
    pad128 = lambda v, at: jnp.pad(v, (at, 128 - at - v.shape[0])).reshape(1, 128)
    return dict(
        g=[row(wl["norm_g"][i]) for i in range(4)],
        gmlp=[row(wl["sgu_ln_g"]), row(wl["sgu_ln_b"]), wl["w_spatial"],
              jnp.pad(wl["b_spatial"].T, ((0, 0), (0, 128 - GM_GROUPS)))],
        dn_in=[wl["conv_w"], pad128(wl["a_log"], DN_HEADS), pad128(wl["dt_bias"], DN_HEADS)],
        dn_g=[row(wl["dn_norm_g"])],
    )


def _layer_fwd(x0, wl, l):
    tag = lambda s: f"{s}_l{l}"
    pr = _layer_params(wl)
    w = BRANCH_WIDTH
    h0 = _row_fwd(_fn_rms, [(x0, 0, D_MODEL)], [pr["g"][0]], [(D_MODEL, BF16)], ts=512, name=tag("rms0"))[0]
    p = _mm(h0, wl["w_in"], bias=wl["b_in"].reshape(1, -1), name=tag("proj_in"))
    ya = _row_fwd(_fn_gmlp, [(p, OFF_GM, 2 * w)], pr["gmlp"], [(w, BF16)], ts=256, name=tag("gmlp"))[0]
    q, k, v, beta, g = _dn_in_fwd(p, pr["dn_in"], ts=256, name=tag("dn_in"))
    o, states = _delta_fwd(q, k, v, beta, g, name=tag("delta"))
    yb = _row_fwd(_fn_dn_out, [(o, 0, w), (p, OFF_BZ, w)], pr["dn_g"], [(w, BF16)], ts=512, name=tag("dn_out"))[0]
    yc = _sb_fwd(p, name=tag("sb"))
    ys = [ya, yb, yc]
    proj = [_mm(ys[i], wl["w_branch"][i], name=tag(f"branch{i}")) for i in range(3)]
    merge_rows = [(a, 0, D_MODEL) for a in proj] + [(p, OFF_GATE, 3 * D_MODEL)]
    m = _row_fwd(_fn_merge, merge_rows, [], [(D_MODEL, BF16)], ts=256, name=tag("merge"))[0]
    mixed = _mm(m, wl["w_out"], name=tag("out"))
    x1 = _row_fwd(_fn_resid_rms, [(x0, 0, D_MODEL), (mixed, 0, D_MODEL)], [pr["g"][1]], [(D_MODEL, F32)], ts=512,
                  name=tag("resid1"))[0]
    h2 = _row_fwd(_fn_rms, [(x1, 0, D_MODEL)], [pr["g"][2]], [(D_MODEL, BF16)], ts=512, name=tag("rms2"))[0]
    a, r = _mm(h2, wl["w_ff1"], name=tag("ff1"), out_dtype=(BF16, BF16),
               finish=lambda t: (jnp.maximum(t, 0.0), jnp.square(jnp.maximum(t, 0.0))))
    f = _mm(r, wl["w_ff2"], name=tag("ff2"))
    x2 = _row_fwd(_fn_resid_rms, [(x1, 0, D_MODEL), (f, 0, D_MODEL)], [pr["g"][3]], [(D_MODEL, F32)], ts=512,
                  name=tag("resid2"))[0]
    saved = dict(x0=x0, h0=h0, p=p, q=q, k=k, v=v, beta=beta, g=g, states=states, o=o, ys=ys,
                 proj=proj, m=m, mixed=mixed, x1=x1, h2=h2, a=a, r=r, f=f)
    return x2, saved


def _layer_bwd(dx2, sv, wl, l):
    tag = lambda s: f"{s}_l{l}"
    pr = _layer_params(wl)
    w = BRANCH_WIDTH
    full = lambda a: (a, 0, a.shape[1])
    p = sv["p"]
    (df,), (dg3,) = _row_bwd(_fn_rms_branch, [full(sv["f"])], [pr["g"][3]], [full(dx2)], ts=512, name=tag("resid2_b"))
    da = _mm(df, wl["w_ff2"], trans_b=True, name=tag("ff2_dx"), out_dtype=BF16, beside=(sv["a"],),
             finish=lambda t, relu_a: 2.0 * relu_a.astype(F32) * t)
    dw_ff2 = _mm_tn(sv["r"], df, name=tag("ff2_dw"))
    dh2 = _mm(da, wl["w_ff1"], trans_b=True, name=tag("ff1_dx"))
    dw_ff1 = _mm_tn(sv["h2"], da, name=tag("ff1_dw"))
    (dx1,), (dg2,) = _row_bwd(_fn_rms_keep, [full(sv["x1"])], [pr["g"][2]], [full(dh2), full(dx2)], ts=512,
                              name=tag("rms2_b"))
    (dmixed,), (dg1,) = _row_bwd(_fn_rms_branch, [full(sv["mixed"])], [pr["g"][1]], [full(dx1)], ts=512,
                                 name=tag("resid1_b"))
    dm = _mm(dmixed, wl["w_out"], trans_b=True, name=tag("out_dx"))
    dw_out = _mm_tn(sv["m"], dmixed, name=tag("out_dw"))
    dp = lax.empty(p.shape, F32)
    merge_rows = [full(a) for a in sv["proj"]] + [(p, OFF_GATE, 3 * D_MODEL)]
    dmerge, _ = _row_bwd(_fn_merge, merge_rows, [], [full(dm)], ts=256, name=tag("merge_b"), into=(dp, 3))
    dproj, dp = dmerge[:3], dmerge[3]
    dys = [_mm(dproj[i], wl["w_branch"][i], trans_b=True, name=tag(f"branch{i}_dx")) for i in range(3)]
    dw_branch = jnp.stack([_mm_tn(sv["ys"][i], dproj[i], name=tag(f"branch{i}_dw")) for i in range(3)])
    (dp,), dgm = _row_bwd(_fn_gmlp, [(p, OFF_GM, 2 * w)], pr["gmlp"], [full(dys[0])], ts=256, name=tag("gmlp_b"),
                          into=(dp, 0))
    (do, dp), (d_dn_g,) = _row_bwd(_fn_dn_out, [full(sv["o"]), (p, OFF_BZ, w)], pr["dn_g"], [full(dys[1])], ts=512,
                                   name=tag("dn_out_b"), into=(dp, 1))
    dqkvbg = _delta_bwd(sv["q"], sv["k"], sv["v"], sv["beta"], sv["g"], sv["states"], do, name=tag("delta_b"))
    dp, d_dn_in = _dn_in_bwd(p, pr["dn_in"], dqkvbg, dp, ts=256, name=tag("dn_in_b"))
    for off, part in zip((OFF_CQ, OFF_CK, OFF_CV), _sb_bwd(p, dys[2], name=tag("sb_b"))):
        dp = lax.dynamic_update_slice(dp, part, (0, off))
    dh0 = _mm(dp, wl["w_in"], trans_b=True, name=tag("proj_in_dx"))
    dw_in = _mm_tn(sv["h0"], dp, name=tag("proj_in_dw"))
    db_in = _col_sums(dp, name=tag("bias_b"))
    (dx0,), (dg0,) = _row_bwd(_fn_rms_keep, [full(sv["x0"])], [pr["g"][0]], [full(dh0), full(dx1)], ts=512,
                              name=tag("rms0_b"))
    grads = dict(
        norm_g=jnp.concatenate([dg0, dg1, dg2, dg3], axis=0), w_in=dw_in, b_in=db_in.reshape(-1),
        sgu_ln_g=dgm[0].reshape(-1), sgu_ln_b=dgm[1].reshape(-1), w_spatial=dgm[2],
        b_spatial=dgm[3][:, :GM_GROUPS].T, conv_w=d_dn_in[0], a_log=d_dn_in[1][0, DN_HEADS:2 * DN_HEADS],
        dt_bias=d_dn_in[2][0, DN_HEADS:2 * DN_HEADS], dn_norm_g=d_dn_g.reshape(-1), w_branch=dw_branch,
        w_out=dw_out, w_ff1=dw_ff1, w_ff2=dw_ff2)
    return dx0, grads


def _local_step(x, target, weights):
    saved = []
    h = x
    layers = []
    for l in range(DEPTH):
        wl = {n: weights[n][l] for n in WEIGHT_ORDER}
        layers.append(wl)
        h, sv = _layer_fwd(h, wl, l)
        saved.append(sv)
    sq, dh = _loss_head(h, target, name="loss_head")
    grads = [None] * DEPTH
    for l in reversed(range(DEPTH)):
        dh, grads[l] = _layer_bwd(dh, saved[l], layers[l], l)
    stacked = {n: jnp.stack([grads[l][n].astype(GRAD_DTYPE) for l in range(DEPTH)]) for n in WEIGHT_ORDER}
    return sq[0, 0], dh, stacked


def kernel(x, norm_g, w_in, b_in, sgu_ln_g, sgu_ln_b, w_spatial, b_spatial, conv_w, a_log, dt_bias, dn_norm_g, w_branch, w_out, w_ff1, w_ff2, loss_target, m_norm_g, m_w_in, m_b_in, m_sgu_ln_g, m_sgu_ln_b, m_w_spatial, m_b_spatial, m_conv_w, m_a_log, m_dt_bias, m_dn_norm_g, m_w_branch, m_w_out, m_w_ff1, m_w_ff2, v_norm_g, v_w_in, v_b_in, v_sgu_ln_g, v_sgu_ln_b, v_w_spatial, v_b_spatial, v_conv_w, v_a_log, v_dt_bias, v_dn_norm_g, v_w_branch, v_w_out, v_w_ff1, v_w_ff2):
    local = dict(norm_g=norm_g, w_in=w_in, b_in=b_in, sgu_ln_g=sgu_ln_g, sgu_ln_b=sgu_ln_b, w_spatial=w_spatial,
                 b_spatial=b_spatial, conv_w=conv_w, a_log=a_log, dt_bias=dt_bias, dn_norm_g=dn_norm_g,
                 w_branch=w_branch, w_out=w_out, w_ff1=w_ff1, w_ff2=w_ff2)
    mom1 = dict(norm_g=m_norm_g, w_in=m_w_in, b_in=m_b_in, sgu_ln_g=m_sgu_ln_g, sgu_ln_b=m_sgu_ln_b,
                w_spatial=m_w_spatial, b_spatial=m_b_spatial, conv_w=m_conv_w, a_log=m_a_log, dt_bias=m_dt_bias,
                dn_norm_g=m_dn_norm_g, w_branch=m_w_branch, w_out=m_w_out, w_ff1=m_w_ff1, w_ff2=m_w_ff2)
    mom2 = dict(norm_g=v_norm_g, w_in=v_w_in, b_in=v_b_in, sgu_ln_g=v_sgu_ln_g, sgu_ln_b=v_sgu_ln_b,
                w_spatial=v_w_spatial, b_spatial=v_b_spatial, conv_w=v_conv_w, a_log=v_a_log, dt_bias=v_dt_bias,
                dn_norm_g=v_dn_norm_g, w_branch=v_w_branch, w_out=v_w_out, w_ff1=v_w_ff1, w_ff2=v_w_ff2)
    shard_names = [n for n, _ in SHARDED]
    shard_shapes = [local[n].shape for n in shard_names]
    repl_shapes = [local[n].shape for n in REPLICATED]

    weights = {n: local[n] for n in REPLICATED}
    for names, dtype, call in ((MATMUL_WEIGHTS, BF16, "gather_matmul_weights"), (VECTOR_WEIGHTS, F32, "gather_vectors")):
        gathered = _gather_chips(_pack([local[n] for n in names]).astype(dtype), name=call)
        per_chip = [_unpack(gathered[k], [local[n].shape for n in names]) for k in range(4)]
        for i, n in enumerate(names):
            weights[n] = jnp.concatenate([per_chip[k][i] for k in range(4)], axis=dict(SHARDED)[n])
    weights["w_in"] = _rearrange_w_in(weights["w_in"])
    weights["b_in"] = _rearrange_w_in(weights["b_in"])

    sq, dx, grads = _local_step(x[0], loss_target[0], weights)
    loss = lax.psum(0.5 * sq / D_MODEL, ("x", "y", "c"))
    grads["w_in"] = _restore_w_in(grads["w_in"])
    grads["b_in"] = _restore_w_in(grads["b_in"])

    blocks = jnp.stack([_pack([_chip_slice(grads[n], axis, k) for n, axis in SHARDED] + [grads[n] for n in REPLICATED])
                        for k in range(4)])
    rows = blocks.shape[1]
    half = rows // 2
    c = lax.axis_index("c")
    got = _halves_to_sibling(blocks, name="grads_to_sibling")
    chip_sum = _add_own_half(blocks, got, name="grads_chip_sum", out_dtype=BF16)
    by_chip = _scatter_chips(chip_sum, name="grads_scatter")
    my_half = _add_rows([by_chip[k] for k in range(4)], name="grads_sum")
    other_half = _to_sibling(my_half, name="grads_half_swap")
    total = jnp.concatenate([jnp.where(c == 0, my_half, other_half), jnp.where(c == 0, other_half, my_half)], axis=0)
    g_out = dict(zip(shard_names + list(REPLICATED), _unpack(total, shard_shapes + repl_shapes)))

    d_out, m_out, v_out = {}, {}, {}
    for n in WEIGHT_ORDER:
        d_out[n], m_out[n], v_out[n] = _adamw(local[n], g_out[n], mom1[n], mom2[n], name=f"adamw_{n}")
    return (loss, dx[None], *[g_out[n] for n in WEIGHT_ORDER], *[d_out[n] for n in WEIGHT_ORDER],
            *[m_out[n] for n in WEIGHT_ORDER], *[v_out[n] for n in WEIGHT_ORDER])
```

```python
import functools

import jax
import jax.numpy as jnp
from jax import lax
from jax.experimental import pallas as pl
from jax.experimental.pallas import tpu as pltpu

F32 = jnp.float32
BF16 = jnp.bfloat16
GRAD_DTYPE = BF16

D_MODEL = 1024
DEPTH = 4
BRANCH_WIDTH = 512
GM_CHUNK = 128
GM_GROUPS = 8
DN_HEADS = 4
DN_HEAD_DIM = 128
CONV_WIDTH = 4
DN_HALO = 8
DN_BLOCK = 128
DN_SOLVE_PASSES, DN_STATE_PASSES, DN_OUT_PASSES = 3, 1, 1
SB_HEAD_DIM = 64
SB_BLOCK = 128
SB_QUERY_ROWS = 256
SB_DEAD_LOG = -88.0
D_FF = 4096
P_IN = 7688
P_PAD = 8192
NORM_EPS = 1e-6
ADAM_LR, ADAM_B1, ADAM_B2, ADAM_EPS, ADAM_WD, ADAM_STEP = 0.001, 0.9, 0.999, 1e-08, 0.01, 10

OFF_DN, DN_IN_WIDTH = 0, 2048
OFF_BD = 1536
OFF_GM = 2048
OFF_GATE = 3072
OFF_BZ = 6144
OFF_CQ, OFF_CK, OFF_CV = 6656, 7168, 7680

LANES = 1024
VMEM_LIMIT_BYTES = 56 * 1024 * 1024

_HI = lax.Precision.HIGHEST


def _params(sem):
    return pltpu.CompilerParams(dimension_semantics=sem, vmem_limit_bytes=VMEM_LIMIT_BYTES)


def _mm(a, b, *, name, out_dtype=F32, bias=None, trans_b=False, beside=(), finish=None, tm=1024, tn=1024, tk=1024):
    m, k = a.shape
    n = b.shape[0] if trans_b else b.shape[1]
    tm, tn, tk = min(tm, m), min(tn, n), min(tk, k)
    assert m % tm == 0 and n % tn == 0 and k % tk == 0, (a.shape, b.shape)
    nk = k // tk
    dn = (((1,), (1,)), ((), ())) if trans_b else (((1,), (0,)), ((), ()))
    several = isinstance(out_dtype, tuple)
    out_dtypes = out_dtype if several else (out_dtype,)
    n_in = 2 + (bias is not None) + len(beside)

    def body(*refs):
        a_ref, b_ref = refs[:2]
        o_refs, acc = refs[n_in:-1], refs[-1]
        kk = pl.program_id(2)
        part = lax.dot_general(a_ref[...].astype(BF16), b_ref[...].astype(BF16), dn, preferred_element_type=F32)

        @pl.when(kk == 0)
        def _():
            acc[...] = part

        @pl.when(kk > 0)
        def _():
            acc[...] += part

        @pl.when(kk == nk - 1)
        def _():
            r = acc[...]
            if bias is not None:
                r = r + refs[2][...]
            tiles = (r,) if finish is None else finish(r, *[t[...] for t in refs[n_in - len(beside):n_in]])
            for o_ref, tile in zip(o_refs, tiles if isinstance(tiles, tuple) else (tiles,)):
                o_ref[...] = tile.astype(o_ref.dtype)

    in_specs = [pl.BlockSpec((tm, tk), lambda i, j, kk: (i, kk))]
    if trans_b:
        in_specs.append(pl.BlockSpec((tn, tk), lambda i, j, kk: (j, kk)))
    else:
        in_specs.append(pl.BlockSpec((tk, tn), lambda i, j, kk: (kk, j)))
    args = [a, b]
    if bias is not None:
        in_specs.append(pl.BlockSpec((1, tn), lambda i, j, kk: (0, j)))
        args.append(bias)
    for t in beside:
        assert t.shape == (m, n)
        in_specs.append(pl.BlockSpec((tm, tn), lambda i, j, kk: (i, j)))
        args.append(t)
    res = pl.pallas_call(
        body, name=name, grid=(m // tm, n // tn, nk),
        in_specs=in_specs, out_specs=[pl.BlockSpec((tm, tn), lambda i, j, kk: (i, j)) for _ in out_dtypes],
        out_shape=[jax.ShapeDtypeStruct((m, n), dt) for dt in out_dtypes],
        scratch_shapes=[pltpu.VMEM((tm, tn), F32)],
        compiler_params=_params(("parallel", "parallel", "arbitrary")),
    )(*args)
    return list(res) if several else res[0]


def _mm_tn(a, b, *, name, col_sums=False, tm=1024, tn=1024, ts=1024):
    out_dtype = GRAD_DTYPE
    s, ka = a.shape
    n = b.shape[1]
    tm, tn, ts = min(tm, ka), min(tn, n), min(ts, s)
    assert ka % tm == 0 and n % tn == 0 and s % ts == 0, (a.shape, b.shape)
    steps = s // ts

    def body(a_ref, b_ref, o_ref, *rest):
        acc = rest[-1]
        step = pl.program_id(2)
        part = lax.dot_general(a_ref[...].astype(BF16), b_ref[...].astype(BF16), (((0,), (0,)), ((), ())),
                               preferred_element_type=F32)

        @pl.when(step == 0)
        def _():
            acc[...] = part

        @pl.when(step > 0)
        def _():
            acc[...] += part

        @pl.when(step == steps - 1)
        def _():
            o_ref[...] = acc[...].astype(out_dtype)

        if col_sums:
            sums_ref = rest[0]
            first_rows = jnp.logical_and(pl.program_id(1) == 0, step == 0)

            @pl.when(first_rows)
            def _():
                sums_ref[...] = jnp.zeros_like(sums_ref)

            @pl.when(pl.program_id(1) == 0)
            def _():
                sums_ref[...] += jnp.sum(b_ref[...].astype(F32), axis=0, keepdims=True)

    out_specs = [pl.BlockSpec((tm, tn), lambda j, i, r: (i, j))]
    out_shape = [jax.ShapeDtypeStruct((ka, n), out_dtype)]
    if col_sums:
        out_specs.append(pl.BlockSpec((1, tn), lambda j, i, r: (0, j)))
        out_shape.append(jax.ShapeDtypeStruct((1, n), F32))
    res = pl.pallas_call(
        body, name=name, grid=(n // tn, ka // tm, steps),
        in_specs=[pl.BlockSpec((ts, tm), lambda j, i, r: (r, i)), pl.BlockSpec((ts, tn), lambda j, i, r: (r, j))],
        out_specs=out_specs, out_shape=out_shape,
        scratch_shapes=[pltpu.VMEM((tm, tn), F32)],
        compiler_params=_params(("parallel", "arbitrary", "arbitrary")),
    )(a, b)
    return list(res) if col_sums else res[0]


def _col_block(i, *, c):
    return (i, c)


def _whole(i, *, nd):
    return (0,) * nd


def _row_specs(rows, ts):
    specs = []
    for arr, off, w in rows:
        assert off % w == 0 and arr.shape[0] % ts == 0
        specs.append(pl.BlockSpec((ts, w), functools.partial(_col_block, c=off // w)))
    return specs


def _row_fwd(fn, rows, params, outs, *, ts, name):
    s = rows[0][0].shape[0]
    ts = min(ts, s)
    nr, npar = len(rows), len(params)

    def body(*refs):
        rv = [r[...].astype(F32) for r in refs[:nr]]
        pv = [p[...] for p in refs[nr:nr + npar]]
        for o_ref, val in zip(refs[nr + npar:], fn(pv, rv)):
            o_ref[...] = val.astype(o_ref.dtype)

    in_specs = _row_specs(rows, ts) + [pl.BlockSpec(p.shape, functools.partial(_whole, nd=p.ndim)) for p in params]
    res = pl.pallas_call(
        body, name=name, grid=(s // ts,), in_specs=in_specs,
        out_specs=[pl.BlockSpec((ts, w), lambda i: (i, 0)) for w, _ in outs],
        out_shape=[jax.ShapeDtypeStruct((s, w), dt) for w, dt in outs],
        compiler_params=_params(("parallel",)),
    )(*[r[0] for r in rows], *params)
    return list(res)


def _row_bwd(fn, rows, params, cts, *, ts, name, into=None):
    s = rows[0][0].shape[0]
    ts = min(ts, s)
    nr, npar, nc = len(rows), len(params), len(cts)
    n_in = nr + npar + nc + (into is not None)

    def body(*refs):
        rv = [r[...].astype(F32) for r in refs[:nr]]
        pv = [p[...] for p in refs[nr:nr + npar]]
        cv = [c[...].astype(F32) for c in refs[nr + npar:nr + npar + nc]]
        out_refs = refs[n_in:]
        _, vjp = jax.vjp(lambda p, r: tuple(fn(p, r)), pv, rv)
        dp, dr = vjp(tuple(cv))
        for o_ref, val in zip(out_refs[:nr], dr):
            o_ref[...] = val

        @pl.when(pl.program_id(0) == 0)
        def _():
            for o_ref in out_refs[nr:]:
                o_ref[...] = jnp.zeros_like(o_ref)

        for o_ref, val in zip(out_refs[nr:], dp):
            o_ref[...] += val

    in_specs = (_row_specs(rows, ts) + [pl.BlockSpec(p.shape, functools.partial(_whole, nd=p.ndim)) for p in params]
                + _row_specs(cts, ts))
    out_specs = [pl.BlockSpec((ts, w), lambda i: (i, 0)) for _, _, w in rows]
    out_shape = [jax.ShapeDtypeStruct((s, w), F32) for _, _, w in rows]
    args = [r[0] for r in rows] + list(params) + [c[0] for c in cts]
    aliases = {}
    if into is not None:
        buffer, at = into
        in_specs.append(pl.BlockSpec(memory_space=pl.ANY))
        args.append(buffer)
        out_specs[at] = _row_specs([(buffer,) + tuple(rows[at][1:])], ts)[0]
        out_shape[at] = jax.ShapeDtypeStruct(buffer.shape, buffer.dtype)
        aliases = {n_in - 1: at}
    out_specs += [pl.BlockSpec(p.shape, functools.partial(_whole, nd=p.ndim)) for p in params]
    out_shape += [jax.ShapeDtypeStruct(p.shape, F32) for p in params]
    res = pl.pallas_call(
        body, name=name, grid=(s // ts,), in_specs=in_specs, out_specs=out_specs, out_shape=out_shape,
        input_output_aliases=aliases, compiler_params=_params(("arbitrary",)),
    )(*args)
    res = list(res)
    return res[:nr], res[nr:]


def _rms(x, g):
    return x * lax.rsqrt(jnp.mean(x * x, axis=-1, keepdims=True) + NORM_EPS) * g


def _gelu(x):
    return 0.5 * x * (1.0 + lax.erf(x * (2.0 ** -0.5)))


def _softplus(x):
    return jnp.maximum(x, 0.0) + jnp.log1p(jnp.exp(-jnp.abs(x)))


def _iota2(shape, dim):
    return lax.broadcasted_iota(jnp.int32, shape, dim)


def _fn_rms(pv, rv):
    return [_rms(rv[0], pv[0])]


def _fn_rms_keep(pv, rv):
    return [_rms(rv[0], pv[0]), rv[0]]


def _fn_resid_rms(pv, rv):
    return [rv[0] + _rms(rv[1], pv[0])]


def _fn_rms_branch(pv, rv):
    return [_rms(rv[0], pv[0])]


def _fn_merge(pv, rv):
    return [sum(jax.nn.sigmoid(rv[3][:, i * D_MODEL:(i + 1) * D_MODEL]) * rv[i] for i in range(3))]


def _fn_gmlp(pv, rv):
    ln_g, ln_b, w_sp, b_t = pv
    u = _gelu(rv[0][:, :BRANCH_WIDTH])
    v = _gelu(rv[0][:, BRANCH_WIDTH:])
    vc = v - jnp.mean(v, axis=-1, keepdims=True)
    v = vc * lax.rsqrt(jnp.mean(vc * vc, axis=-1, keepdims=True) + NORM_EPS) * ln_g + ln_b
    t = GM_CHUNK
    causal = _iota2((t, t), 1) <= _iota2((t, t), 0)
    first = _iota2((t, 128), 1) < 64
    expand = (_iota2((128, BRANCH_WIDTH), 0) == _iota2((128, BRANCH_WIDTH), 1) // 64).astype(F32)
    b_full = jnp.dot(b_t, expand, precision=_HI, preferred_element_type=F32)
    w_bf = [jnp.where(causal, w_sp[g], 0.0).astype(BF16) for g in range(GM_GROUPS)]
    chunks = []
    for c in range(rv[0].shape[0] // t):
        pairs = []
        for p in range(GM_GROUPS // 2):
            vp = v[c * t:(c + 1) * t, 128 * p:128 * (p + 1)].astype(BF16)
            m0 = jnp.dot(w_bf[2 * p], vp, preferred_element_type=F32)
            m1 = jnp.dot(w_bf[2 * p + 1], vp, preferred_element_type=F32)
            pairs.append(jnp.where(first, m0, m1))
        chunks.append(jnp.concatenate(pairs, axis=1) + b_full)
    return [u * jnp.concatenate(chunks, axis=0)]


def _head_expand(col0):
    return (_iota2((128, BRANCH_WIDTH), 0) == _iota2((128, BRANCH_WIDTH), 1) // DN_HEAD_DIM + col0).astype(F32)


@functools.partial(jax.custom_vjp, nondiff_argnums=(1,))
def _roll_rows(x, n):
    return pltpu.roll(x, n, 0)


def _roll_rows_fwd(x, n):
    return pltpu.roll(x, n, 0), None


def _roll_rows_bwd(n, _, ct):
    return (pltpu.roll(ct, ct.shape[0] - n, 0),)


_roll_rows.defvjp(_roll_rows_fwd, _roll_rows_bwd)


def _fn_dn_in(pv, rv):
    conv_w, a_log, dt_b = pv
    cur, before = rv
    x = cur[:, :3 * BRANCH_WIDTH]
    ext = jnp.concatenate([before[:, :3 * BRANCH_WIDTH], x], axis=0)
    c = conv_w[CONV_WIDTH - 1:CONV_WIDTH, :] * x
    for j in range(CONV_WIDTH - 1):
        c = c + conv_w[j:j + 1, :] * _roll_rows(ext, CONV_WIDTH - 1 - j)[DN_HALO:]
    a = c * jax.nn.sigmoid(c)
    outs = []
    for part in range(3):
        heads = []
        for h in range(DN_HEADS):
            lo = part * BRANCH_WIDTH + h * DN_HEAD_DIM
            xh = a[:, lo:lo + DN_HEAD_DIM]
            if part < 2:
                xh = xh * lax.rsqrt(jnp.sum(xh * xh, axis=-1, keepdims=True) + NORM_EPS)
            heads.append(xh)
        outs.append(jnp.concatenate(heads, axis=1))
    bd = cur[:, OFF_BD:OFF_BD + 128]
    beta = jax.nn.sigmoid(bd)
    g = -jnp.exp(a_log) * _softplus(bd + dt_b)
    outs.append(jnp.dot(beta, _head_expand(0), precision=_HI, preferred_element_type=F32))
    outs.append(jnp.dot(g, _head_expand(DN_HEADS), precision=_HI, preferred_element_type=F32))
    return outs


def _dn_in_specs(ts, first_block):
    per = ts // DN_HALO
    return [pl.BlockSpec((ts, DN_IN_WIDTH), lambda i: (first_block(i), OFF_DN // DN_IN_WIDTH)),
            pl.BlockSpec((DN_HALO, DN_IN_WIDTH),
                         lambda i: (jnp.maximum(first_block(i) * per - 1, 0), OFF_DN // DN_IN_WIDTH))]


def _dn_in_fwd(p, params, *, ts, name):
    s = p.shape[0]
    ts = min(ts, s)

    def body(cur_ref, before_ref, cw_ref, al_ref, db_ref, *o_refs):
        before = jnp.where(pl.program_id(0) == 0, 0.0, before_ref[...])
        outs = _fn_dn_in([cw_ref[...], al_ref[...], db_ref[...]], [cur_ref[...], before])
        for o_ref, val in zip(o_refs, outs):
            o_ref[...] = val

    whole = [pl.BlockSpec(a.shape, functools.partial(_whole, nd=a.ndim)) for a in params]
    res = pl.pallas_call(
        body, name=name, grid=(s // ts,), in_specs=_dn_in_specs(ts, lambda i: i) + whole,
        out_specs=[pl.BlockSpec((ts, BRANCH_WIDTH), lambda i: (i, 0))] * 5,
        out_shape=[jax.ShapeDtypeStruct((s, BRANCH_WIDTH), F32)] * 5,
        compiler_params=_params(("parallel",)),
    )(p, p, *params)
    return list(res)


def _dn_in_bwd(p, params, cts, dp, *, ts, name):
    s = p.shape[0]
    ts = min(ts, s)
    nb = s // ts
    block = lambda i: nb - 1 - i

    def body(cur_ref, before_ref, cw_ref, al_ref, db_ref, *rest):
        ct_refs, dx_ref, dpar_refs, halo = rest[:5], rest[6], rest[7:10], rest[10]
        first = pl.program_id(0) == 0

        @pl.when(first)
        def _():
            halo[...] = jnp.zeros_like(halo)
            for o_ref in dpar_refs:
                o_ref[...] = jnp.zeros_like(o_ref)

        before = jnp.where(pl.program_id(0) == nb - 1, 0.0, before_ref[...])
        _, vjp = jax.vjp(lambda pv, cur, bef: tuple(_fn_dn_in(pv, [cur, bef])),
                         [cw_ref[...], al_ref[...], db_ref[...]], cur_ref[...], before)
        dpar, dcur, dbefore = vjp(tuple(c[...] for c in ct_refs))
        dx_ref[...] = jnp.concatenate([dcur[:ts - DN_HALO], dcur[ts - DN_HALO:] + halo[...]], axis=0)
        halo[...] = dbefore
        for o_ref, val in zip(dpar_refs, dpar):
            o_ref[...] += val

    whole = [pl.BlockSpec(a.shape, functools.partial(_whole, nd=a.ndim)) for a in params]
    tile = pl.BlockSpec((ts, BRANCH_WIDTH), lambda i: (block(i), 0))
    res = pl.pallas_call(
        body, name=name, grid=(nb,),
        in_specs=_dn_in_specs(ts, block) + whole + [tile] * 5 + [pl.BlockSpec(memory_space=pl.ANY)],
        out_specs=[pl.BlockSpec((ts, DN_IN_WIDTH), lambda i: (block(i), OFF_DN // DN_IN_WIDTH))] + whole,
        out_shape=[jax.ShapeDtypeStruct(dp.shape, dp.dtype)] + [jax.ShapeDtypeStruct(a.shape, F32) for a in params],
        scratch_shapes=[pltpu.VMEM((DN_HALO, DN_IN_WIDTH), F32)],
        input_output_aliases={10: 0}, compiler_params=_params(("arbitrary",)),
    )(p, p, *params, *cts, dp)
    return res[0], list(res[1:])


def _fn_dn_out(pv, rv):
    heads = []
    for h in range(DN_HEADS):
        sl = slice(h * DN_HEAD_DIM, (h + 1) * DN_HEAD_DIM)
        z = rv[1][:, sl]
        heads.append(_rms(rv[0][:, sl], pv[0]) * (z * jax.nn.sigmoid(z)))
    return [jnp.concatenate(heads, axis=1)]


_DIMS = {"nn": (((1,), (0,)), ((), ())), "nt": (((1,), (1,)), ((), ())), "tn": (((0,), (0,)), ((), ()))}
_DIMS_BWD = {"nn": (("nt", "c", "b"), ("tn", "a", "c")), "nt": (("nn", "c", "b"), ("tn", "c", "a")),
             "tn": (("nt", "b", "c"), ("nn", "a", "c"))}


def _bf16_dot(a, b, kind):
    return lax.dot_general(a.astype(BF16), b.astype(BF16), _DIMS[kind], preferred_element_type=F32)


def _pdot_raw(a, b, kind, mode):
    if mode == 1:
        return _bf16_dot(a, b, kind)
    if mode == 6:
        return lax.dot_general(a, b, _DIMS[kind], precision=_HI, preferred_element_type=F32)
    b_hi, b_lo = _split_bf16(b)
    if mode == 3:
        a_hi, a_lo = _split_bf16(a)
        return _bf16_dot(a_hi, b_hi, kind) + (_bf16_dot(a_hi, b_lo, kind) + _bf16_dot(a_lo, b_hi, kind))
    b_rest = (b - b_hi.astype(F32) - b_lo.astype(F32)).astype(BF16)
    return _bf16_dot(a, b_hi, kind) + (_bf16_dot(a, b_lo, kind) + _bf16_dot(a, b_rest, kind))


@functools.partial(jax.custom_vjp, nondiff_argnums=(2, 3))
def _pdot(a, b, kind, mode):
    return _pdot_raw(a, b, kind, mode)


def _pdot_fwd(a, b, kind, mode):
    return _pdot_raw(a, b, kind, mode), (a, b)


def _pdot_bwd(kind, mode, res, ct):
    ops = {"a": res[0], "b": res[1], "c": ct}
    (ka, a1, a2), (kb, b1, b2) = _DIMS_BWD[kind]
    if mode == "count":
        return jnp.zeros_like(res[0]), _pdot(ops[b1], ops[b2], kb, mode)
    return _pdot(ops[a1], ops[a2], ka, mode), _pdot(ops[b1], ops[b2], kb, mode)


_pdot.defvjp(_pdot_fwd, _pdot_bwd)


@jax.custom_vjp
def _unit_lower_inverses(mats):
    c = mats[0].shape[0]
    row, col = _iota2((c, c), 0), _iota2((c, c), 1)
    x = [(row == col).astype(F32) for _ in mats]
    shift = 0
    while (1 << shift) < c:
        pair = jnp.right_shift(row, shift + 1) == jnp.right_shift(col, shift + 1)
        between = pair & (jnp.right_shift(row, shift) != jnp.right_shift(col, shift))
        q = [jnp.where(between, a, 0.0) for a in mats]
        qd = [_pdot_raw(qi, xi, "nn", DN_SOLVE_PASSES) for qi, xi in zip(q, x)]
        x = [xi - _pdot_raw(xi, m, "nn", DN_SOLVE_PASSES) for xi, m in zip(x, qd)]
        shift += 1
    return tuple(x)


def _unit_lower_inverses_fwd(mats):
    x = _unit_lower_inverses(mats)
    return x, x


def _unit_lower_inverses_bwd(x, dx):
    inner = [_pdot_raw(d, xi, "nt", DN_SOLVE_PASSES) for d, xi in zip(dx, x)]
    return (tuple(-_pdot_raw(xi, m, "tn", DN_SOLVE_PASSES) for xi, m in zip(x, inner)),)


_unit_lower_inverses.defvjp(_unit_lower_inverses_fwd, _unit_lower_inverses_bwd)


def _head(h):
    return slice(h * DN_HEAD_DIM, (h + 1) * DN_HEAD_DIM)


def _delta_chunk(states, q, k, v, beta, g):
    c = DN_BLOCK
    heads = range(DN_HEADS)
    row, col = _iota2((c, c), 0), _iota2((c, c), 1)
    tri, strict = col <= row, col < row
    counts = jnp.concatenate([tri.astype(BF16), jnp.ones((c, c), BF16)], axis=0)
    sums = _pdot(counts, g, "nn", "count")
    gc = [sums[:c, _head(h)] for h in heads]
    gl = [sums[c:, _head(h)] for h in heads]
    qh, kh, vh, bh = ([t[:, _head(h)] for h in heads] for t in (q * (DN_HEAD_DIM ** -0.5), k, v, beta))
    decay = [jnp.where(tri, jnp.exp(jnp.where(tri, gc[h] - gc[h].T, 0.0)), 0.0) for h in heads]
    solve, carry, out = DN_SOLVE_PASSES, DN_STATE_PASSES, DN_OUT_PASSES
    kk = [_pdot(kh[h], kh[h], "nt", solve) for h in heads]
    x = _unit_lower_inverses(tuple(jnp.where(strict, bh[h] * kk[h] * decay[h], 0.0) for h in heads))
    eg = [jnp.exp(gc[h]) for h in heads]
    u = [_pdot(x[h], vh[h] * bh[h], "nn", solve) for h in heads]
    wk = [_pdot(x[h], kh[h] * (bh[h] * eg[h]), "nn", solve) for h in heads]
    qk = [jnp.where(tri, _pdot(qh[h], kh[h], "nt", out) * decay[h], 0.0) for h in heads]
    v_new = [u[h] - _pdot(wk[h], states[h], "nn", carry) for h in heads]
    o = [_pdot(qh[h] * eg[h], states[h], "nn", out) + _pdot(qk[h], v_new[h], "nn", out) for h in heads]
    nxt = [states[h] * jnp.exp(gl[h]) + _pdot(kh[h] * jnp.exp(gl[h] - gc[h]), v_new[h], "tn", carry)
           for h in heads]
    return jnp.concatenate(o, axis=1), tuple(nxt)


def _delta_fwd(q, k, v, beta, g, *, name):
    s = q.shape[0]
    c = DN_BLOCK
    nc = s // c

    def body(q_ref, k_ref, v_ref, b_ref, g_ref, o_ref, sp_ref, st):
        @pl.when(pl.program_id(0) == 0)
        def _():
            st[...] = jnp.zeros_like(st)

        states = tuple(st[h] for h in range(DN_HEADS))
        for h in range(DN_HEADS):
            sp_ref[0, h] = states[h]
        o, nxt = _delta_chunk(states, q_ref[...], k_ref[...], v_ref[...], b_ref[...], g_ref[...])
        for h in range(DN_HEADS):
            st[h] = nxt[h]
        o_ref[...] = o

    blk = pl.BlockSpec((c, BRANCH_WIDTH), lambda n: (n, 0))
    return pl.pallas_call(
        body, name=name, grid=(nc,), in_specs=[blk] * 5,
        out_specs=[blk, pl.BlockSpec((1, DN_HEADS, DN_HEAD_DIM, DN_HEAD_DIM), lambda n: (n, 0, 0, 0))],
        out_shape=[jax.ShapeDtypeStruct((s, BRANCH_WIDTH), F32),
                   jax.ShapeDtypeStruct((nc, DN_HEADS, DN_HEAD_DIM, DN_HEAD_DIM), F32)],
        scratch_shapes=[pltpu.VMEM((DN_HEADS, DN_HEAD_DIM, DN_HEAD_DIM), F32)],
        compiler_params=_params(("arbitrary",)),
    )(q, k, v, beta, g)


def _delta_bwd(q, k, v, beta, g, states, do, *, name):
    s = q.shape[0]
    c = DN_BLOCK
    nc = s // c

    def body(q_ref, k_ref, v_ref, b_ref, g_ref, sp_ref, do_ref, dq_ref, dk_ref, dv_ref, db_ref, dg_ref, dst):
        @pl.when(pl.program_id(0) == 0)
        def _():
            dst[...] = jnp.zeros_like(dst)

        states = tuple(sp_ref[0, h] for h in range(DN_HEADS))
        _, vjp = jax.vjp(_delta_chunk, states, q_ref[...], k_ref[...], v_ref[...], b_ref[...], g_ref[...])
        d = vjp((do_ref[...], tuple(dst[h] for h in range(DN_HEADS))))
        for h in range(DN_HEADS):
            dst[h] = d[0][h]
        for o_ref, val in zip((dq_ref, dk_ref, dv_ref, db_ref, dg_ref), d[1:]):
            o_ref[...] = val

    blk = pl.BlockSpec((c, BRANCH_WIDTH), lambda n: (nc - 1 - n, 0))
    res = pl.pallas_call(
        body, name=name, grid=(nc,),
        in_specs=[blk] * 5 + [pl.BlockSpec((1, DN_HEADS, DN_HEAD_DIM, DN_HEAD_DIM), lambda n: (nc - 1 - n, 0, 0, 0)), blk],
        out_specs=[blk] * 5, out_shape=[jax.ShapeDtypeStruct((s, BRANCH_WIDTH), F32)] * 5,
        scratch_shapes=[pltpu.VMEM((DN_HEADS, DN_HEAD_DIM, DN_HEAD_DIM), F32)],
        compiler_params=_params(("arbitrary",)),
    )(q, k, v, beta, g, states, do)
    return list(res)


def _split_bf16(x):
    hi = x.astype(BF16)
    return hi, (x - hi.astype(F32)).astype(BF16)


def _sb_consts():
    tq, tk = SB_QUERY_ROWS, SB_BLOCK
    row, col = _iota2((tk, tk), 0), _iota2((tk, tk), 1)
    ones = jnp.ones((tk, tk), BF16)
    later = jnp.concatenate([(row > col).astype(BF16), ones], axis=1)
    from_here = jnp.concatenate([(row >= col).astype(BF16), ones], axis=1)
    first = _iota2((tq, 128), 1) < SB_HEAD_DIM
    return later, from_here, first


def _sb_causal(d):
    tq, tk = SB_QUERY_ROWS, SB_BLOCK
    return _iota2((tq - d * tk, tk), 1) < _iota2((tq - d * tk, tk), 0)


def _sums(x, mat):
    hi, lo = _split_bf16(x)
    return jnp.dot(hi, mat, preferred_element_type=F32) + jnp.dot(lo, mat, preferred_element_type=F32)


def _sb_weights(qs, kb, accs, later, causal):
    tk = SB_BLOCK
    z = [lax.dot_general(qh, kb, (((1,), (1,)), ((), ())), preferred_element_type=F32) for qh in qs]
    lk = [-_softplus(zh) for zh in z]
    if causal is not None:
        lk = [jnp.where(causal, v, 0.0) for v in lk]
    cs = [_sums(v, later) for v in lk]
    e = [z[h] + lk[h] + cs[h][:, :tk] + accs[h] for h in range(2)]
    if causal is not None:
        e = [jnp.where(causal, v, -1e30) for v in e]
    return lk, [jnp.exp(v) for v in e], [v[:, tk:] for v in cs]


def _sb_alive(accs):
    return jnp.max(jnp.maximum(accs[0], accs[1])) > SB_DEAD_LOG


def _sb_sweep(i, block, carry, accs_of, stop=None):
    per = SB_QUERY_ROWS // SB_BLOCK
    for d in reversed(range(per)):
        r0 = d * SB_BLOCK
        seen = block(i * per + d, jax.tree.map(lambda a: a[r0:], carry), _sb_causal(d), r0)
        carry = seen if r0 == 0 else jax.tree.map(lambda old, new: jnp.concatenate([old[:r0], new], axis=0), carry, seen)
    if stop is not None:
        return lax.fori_loop(0, i * per - 1 - stop, lambda jj, cr: block(i * per - 1 - jj, cr, None), carry)

    def step(state):
        j, _, cr = state
        cr = block(j, cr, None)
        return j - 1, _sb_alive(accs_of(cr)), cr

    j, _, carry = lax.while_loop(lambda st: jnp.logical_and(st[0] >= 0, st[1]), step,
                                 (i * per - 1, _sb_alive(accs_of(carry)), carry))
    return carry, j


def _sb_fwd(p, *, name):
    s = p.shape[0]
    t, tk = SB_QUERY_ROWS, SB_BLOCK
    scale = SB_HEAD_DIM ** -0.5

    def body(q_ref, k_ref, v_ref, o_ref):
        i = pl.program_id(1)
        later, _, first = _sb_consts()
        q = q_ref[...] * scale
        qs = (jnp.where(first, q, 0.0).astype(BF16), jnp.where(first, 0.0, q).astype(BF16))

        def block(j, carry, causal, r0=0):
            start = pl.multiple_of(j * tk, tk)
            kb = k_ref[pl.ds(start, tk), :].astype(BF16)
            vb = v_ref[pl.ds(start, tk), :].astype(BF16)
            _, w, tot = _sb_weights([qh[r0:] for qh in qs], kb, [carry[h][1] for h in range(2)], later, causal)
            out = [jnp.dot(w[h].astype(BF16), vb, preferred_element_type=F32) for h in range(2)]
            return tuple((carry[h][0] + out[h], carry[h][1] + tot[h]) for h in range(2))

        zero = jnp.zeros((t, 128), F32)
        carry, _ = _sb_sweep(i, block, ((zero, zero), (zero, zero)), lambda cr: (cr[0][1], cr[1][1]))
        o_ref[...] = jnp.where(first, carry[0][0], carry[1][0])

    return pl.pallas_call(
        body, name=name, grid=(BRANCH_WIDTH // 128, s // t),
        in_specs=[pl.BlockSpec((t, 128), lambda pr, i: (i, OFF_CQ // 128 + pr)),
                  pl.BlockSpec((s, 128), lambda pr, i: (0, OFF_CK // 128 + pr)),
                  pl.BlockSpec((s, 128), lambda pr, i: (0, OFF_CV // 128 + pr))],
        out_specs=pl.BlockSpec((t, 128), lambda pr, i: (i, pr)),
        out_shape=jax.ShapeDtypeStruct((s, BRANCH_WIDTH), F32),
        compiler_params=_params(("arbitrary", "arbitrary")),
    )(p, p, p)


def _sb_bwd(p, do, *, name):
    s = p.shape[0]
    t, tk = SB_QUERY_ROWS, SB_BLOCK
    scale = SB_HEAD_DIM ** -0.5

    def body(q_ref, k_ref, v_ref, do_ref, dq_ref, dk_ref, dv_ref):
        i = pl.program_id(1)

        @pl.when(i == 0)
        def _():
            dk_ref[...] = jnp.zeros_like(dk_ref)
            dv_ref[...] = jnp.zeros_like(dv_ref)

        later, from_here, first = _sb_consts()
        q = q_ref[...] * scale
        do = do_ref[...]
        qs = (jnp.where(first, q, 0.0).astype(BF16), jnp.where(first, 0.0, q).astype(BF16))
        dos = (jnp.where(first, do, 0.0).astype(BF16), jnp.where(first, 0.0, do).astype(BF16))

        def total(j, carry, causal, r0=0):
            start = pl.multiple_of(j * tk, tk)
            kb = k_ref[pl.ds(start, tk), :].astype(BF16)
            vb = v_ref[pl.ds(start, tk), :].astype(BF16)
            _, w, tot = _sb_weights([qh[r0:] for qh in qs], kb, [carry[h][0] for h in range(2)], later, causal)
            dw = [lax.dot_general(dos[h][r0:], vb, (((1,), (1,)), ((), ())), preferred_element_type=F32)
                  for h in range(2)]
            tde = [_sums(dw[h] * w[h], from_here)[:, tk:] for h in range(2)]
            return tuple((carry[h][0] + tot[h], carry[h][1] + tde[h]) for h in range(2))

        zero = jnp.zeros((t, 128), F32)
        sums, stop = _sb_sweep(i, total, ((zero, zero), (zero, zero)), lambda cr: (cr[0][0], cr[1][0]))
        deltas = (sums[0][1], sums[1][1])

        def block(j, carry, causal, r0=0):
            start = pl.multiple_of(j * tk, tk)
            kb = k_ref[pl.ds(start, tk), :].astype(BF16)
            vb = v_ref[pl.ds(start, tk), :].astype(BF16)
            both = range(2)
            tn = (((0,), (0,)), ((), ()))
            qr, dor, total_de = [x[r0:] for x in qs], [x[r0:] for x in dos], [x[r0:] for x in deltas]
            lk, w, tot = _sb_weights(qr, kb, [carry[h][1] for h in both], later, causal)
            dw = [lax.dot_general(dor[h], vb, (((1,), (1,)), ((), ())), preferred_element_type=F32) for h in both]
            de = [dw[h] * w[h] for h in both]
            cs = [_sums(de[h], from_here) for h in both]
            keep = [jnp.exp(lk[h]) for h in both]
            dz = [de[h] * keep[h] - (total_de[h] - (cs[h][:, :tk] + carry[h][2])) * (1.0 - keep[h]) for h in both]
            if causal is not None:
                dz = [jnp.where(causal, v, 0.0) for v in dz]
            dzb = [v.astype(BF16) for v in dz]
            dq = [jnp.dot(dzb[h], kb, preferred_element_type=F32) for h in both]
            dk = [lax.dot_general(dzb[h], qr[h], tn, preferred_element_type=F32) for h in both]
            dv = [lax.dot_general(w[h].astype(BF16), dor[h], tn, preferred_element_type=F32) for h in both]
            dk_ref[pl.ds(start, tk), :] += dk[0] + dk[1]
            dv_ref[pl.ds(start, tk), :] += dv[0] + dv[1]
            return tuple((carry[h][0] + dq[h], carry[h][1] + tot[h], carry[h][2] + cs[h][:, tk:]) for h in both)

        carry = _sb_sweep(i, block, ((zero, zero, zero), (zero, zero, zero)), None, stop=stop)
        dq_ref[...] = jnp.where(first, carry[0][0], carry[1][0]) * scale

    qblk = lambda off: pl.BlockSpec((t, 128), lambda pr, i: (i, off // 128 + pr))
    full = lambda off: pl.BlockSpec((s, 128), lambda pr, i: (0, off // 128 + pr))
    res = pl.pallas_call(
        body, name=name, grid=(BRANCH_WIDTH // 128, s // t),
        in_specs=[qblk(OFF_CQ), full(OFF_CK), full(OFF_CV), qblk(0)],
        out_specs=[qblk(0), full(0), full(0)],
        out_shape=[jax.ShapeDtypeStruct((s, BRANCH_WIDTH), F32)] * 3,
        compiler_params=_params(("arbitrary", "arbitrary")),
    )(p, p, p, do)
    return list(res)


def _loss_head(y, target, *, name, ts=512):
    s, d = y.shape
    ts = min(ts, s)

    def body(y_ref, t_ref, sq_ref, dy_ref):
        @pl.when(pl.program_id(0) == 0)
        def _():
            sq_ref[...] = jnp.zeros_like(sq_ref)

        err = y_ref[...] - t_ref[...]
        dy_ref[...] = err * (1.0 / d)
        tot = jnp.sum(jnp.sum(err * err, axis=1, keepdims=True), axis=0, keepdims=True)
        sq_ref[...] += jnp.broadcast_to(tot, sq_ref.shape)

    blk = pl.BlockSpec((ts, d), lambda i: (i, 0))
    return pl.pallas_call(
        body, name=name, grid=(s // ts,), in_specs=[blk, blk],
        out_specs=[pl.BlockSpec((1, 128), lambda i: (0, 0)), blk],
        out_shape=[jax.ShapeDtypeStruct((1, 128), F32), jax.ShapeDtypeStruct((s, d), F32)],
        compiler_params=_params(("arbitrary",)),
    )(y, target)


def _row_tile(r, limit=512):
    return max(t for t in range(16, limit + 1, 16) if r % t == 0)


def _adamw(w, g, m, v, *, name):
    shape = w.shape
    lanes = shape[-1]
    w, g, m, v = (a.reshape(-1, lanes) for a in (w, g, m, v))
    r = w.shape[0]
    ts = r if r <= 256 else _row_tile(r, 256 if lanes > LANES else 512)

    def body(w_ref, g_ref, m_ref, v_ref, d_ref, nm_ref, nv_ref):
        gv = g_ref[...]
        m_new = ADAM_B1 * m_ref[...] + (1.0 - ADAM_B1) * gv
        v_new = ADAM_B2 * v_ref[...] + (1.0 - ADAM_B2) * jnp.square(gv)
        m_hat = m_new / (1.0 - ADAM_B1 ** ADAM_STEP)
        v_hat = v_new / (1.0 - ADAM_B2 ** ADAM_STEP)
        d_ref[...] = -ADAM_LR * (m_hat / (jnp.sqrt(v_hat) + ADAM_EPS) + ADAM_WD * w_ref[...])
        nm_ref[...] = m_new
        nv_ref[...] = v_new

    blk = pl.BlockSpec((ts, lanes), lambda i: (i, 0))
    res = pl.pallas_call(
        body, name=name, grid=(r // ts,), in_specs=[blk] * 4, out_specs=[blk] * 3,
        out_shape=[jax.ShapeDtypeStruct((r, lanes), F32)] * 3,
        compiler_params=_params(("parallel",)),
    )(w, g, m, v)
    return [a.reshape(shape) for a in res]


def _add_rows(terms, *, name, out_dtype=F32):
    r = terms[0].shape[0]
    ts = _row_tile(r)

    def body(*refs):
        acc = refs[0][...].astype(F32)
        for ref in refs[1:-1]:
            acc = acc + ref[...].astype(F32)
        refs[-1][...] = acc.astype(out_dtype)

    blk = pl.BlockSpec((ts, LANES), lambda i: (i, 0))
    return pl.pallas_call(
        body, name=name, grid=(r // ts,), in_specs=[blk] * len(terms), out_specs=blk,
        out_shape=jax.ShapeDtypeStruct((r, LANES), out_dtype), compiler_params=_params(("parallel",)),
    )(*terms)


_HBM = pl.BlockSpec(memory_space=pltpu.HBM)
_MESH = pl.DeviceIdType.MESH


def _other_chips(x, y):
    return [(1 - x, y), (x, 1 - y), (1 - x, 1 - y)]


def _gather_chips(shard, *, name):
    r, lanes = shard.shape
    half = r // 2
    assert half * 2 == r

    def body(in_ref, out_ref, send_sems, recv_sems):
        x, y, c = lax.axis_index("x"), lax.axis_index("y"), lax.axis_index("c")
        me = 2 * x + y
        sibling = (x, y, 1 - c)
        chips = _other_chips(x, y)

        def copy(sem, chip, core_half, to):
            rows = out_ref.at[chip, pl.ds(core_half * half, half)]
            return pltpu.make_async_remote_copy(src_ref=rows, dst_ref=rows, send_sem=send_sems.at[sem],
                                                recv_sem=recv_sems.at[sem], device_id=to, device_id_type=_MESH)

        first = []
        for kk, (px, py) in enumerate(chips):
            cp = pltpu.make_async_remote_copy(
                src_ref=in_ref.at[pl.ds(c * half, half)], dst_ref=out_ref.at[me, pl.ds(c * half, half)],
                send_sem=send_sems.at[kk], recv_sem=recv_sems.at[kk], device_id=(px, py, c), device_id_type=_MESH)
            cp.start()
            first.append(cp)
        passed = [copy(3 + kk, 2 * px + py, c, sibling) for kk, (px, py) in enumerate(chips)]
        for kk, (px, py) in enumerate(chips):
            copy(kk, 2 * px + py, c, (px, py, c)).wait_recv()
            passed[kk].start()
        for kk, (px, py) in enumerate(chips):
            copy(3 + kk, 2 * px + py, 1 - c, sibling).wait_recv()
        for cp in first + passed:
            cp.wait_send()

    gathered = pl.pallas_call(
        body, name=name, in_specs=[_HBM], out_specs=_HBM,
        out_shape=jax.ShapeDtypeStruct((4, r, lanes), shard.dtype),
        scratch_shapes=[pltpu.SemaphoreType.DMA((6,)), pltpu.SemaphoreType.DMA((6,))],
    )(shard)
    me = 2 * lax.axis_index("x") + lax.axis_index("y")
    return jnp.where(lax.broadcasted_iota(jnp.int32, (4, 1, 1), 0) == me, shard[None], gathered)


def _to_sibling(block, *, name):
    def body(in_ref, out_ref, send_sem, recv_sem):
        x, y, c = lax.axis_index("x"), lax.axis_index("y"), lax.axis_index("c")
        cp = pltpu.make_async_remote_copy(src_ref=in_ref, dst_ref=out_ref, send_sem=send_sem, recv_sem=recv_sem,
                                          device_id=(x, y, 1 - c), device_id_type=_MESH)
        cp.start()
        cp.wait()

    return pl.pallas_call(
        body, name=name, in_specs=[_HBM], out_specs=_HBM, out_shape=jax.ShapeDtypeStruct(block.shape, block.dtype),
        scratch_shapes=[pltpu.SemaphoreType.DMA, pltpu.SemaphoreType.DMA],
    )(block)


def _halves_to_sibling(blocks, *, name):
    n, r, lanes = blocks.shape
    half = r // 2

    def body(in_ref, out_ref, send_sem, recv_sem):
        x, y, c = lax.axis_index("x"), lax.axis_index("y"), lax.axis_index("c")
        cp = pltpu.make_async_remote_copy(src_ref=in_ref.at[pl.ds(0, n), pl.ds((1 - c) * half, half)], dst_ref=out_ref,
                                          send_sem=send_sem, recv_sem=recv_sem, device_id=(x, y, 1 - c),
                                          device_id_type=_MESH)
        cp.start()
        cp.wait()

    return pl.pallas_call(
        body, name=name, in_specs=[_HBM], out_specs=_HBM, out_shape=jax.ShapeDtypeStruct((n, half, lanes), blocks.dtype),
        scratch_shapes=[pltpu.SemaphoreType.DMA, pltpu.SemaphoreType.DMA],
    )(blocks)


def _add_own_half(blocks, got, *, name, out_dtype):
    n, r, lanes = blocks.shape
    half = r // 2
    ts = _row_tile(half)
    steps = half // ts
    core = lax.axis_index("c").astype(jnp.int32).reshape(1)

    def body(core_ref, a_ref, b_ref, o_ref):
        o_ref[...] = (a_ref[...].astype(F32) + b_ref[...].astype(F32)).astype(out_dtype)

    return pl.pallas_call(
        body, name=name,
        grid_spec=pltpu.PrefetchScalarGridSpec(
            num_scalar_prefetch=1, grid=(n, steps),
            in_specs=[pl.BlockSpec((1, ts, lanes), lambda k, i, core_ref: (k, core_ref[0] * steps + i, 0)),
                      pl.BlockSpec((1, ts, lanes), lambda k, i, core_ref: (k, i, 0))],
            out_specs=pl.BlockSpec((1, ts, lanes), lambda k, i, core_ref: (k, i, 0))),
        out_shape=jax.ShapeDtypeStruct((n, half, lanes), out_dtype),
        compiler_params=_params(("parallel", "parallel")),
    )(core, blocks, got)


def _scatter_chips(blocks, *, name):
    _, r, lanes = blocks.shape

    def body(in_ref, out_ref, send_sems, recv_sems, local_sem):
        x, y, c = lax.axis_index("x"), lax.axis_index("y"), lax.axis_index("c")
        me = 2 * x + y
        mine = pltpu.make_async_copy(in_ref.at[me], out_ref.at[me], local_sem)
        mine.start()
        copies = []
        for kk, (px, py) in enumerate(_other_chips(x, y)):
            cp = pltpu.make_async_remote_copy(src_ref=in_ref.at[2 * px + py], dst_ref=out_ref.at[me],
                                              send_sem=send_sems.at[kk], recv_sem=recv_sems.at[kk],
                                              device_id=(px, py, c), device_id_type=_MESH)
            cp.start()
            copies.append(cp)
        for kk, (px, py) in enumerate(_other_chips(x, y)):
            pltpu.make_async_remote_copy(src_ref=in_ref.at[me], dst_ref=out_ref.at[2 * px + py],
                                         send_sem=send_sems.at[kk], recv_sem=recv_sems.at[kk], device_id=(px, py, c),
                                         device_id_type=_MESH).wait_recv()
        for cp in copies:
            cp.wait_send()
        mine.wait()

    return pl.pallas_call(
        body, name=name, in_specs=[_HBM], out_specs=_HBM, out_shape=jax.ShapeDtypeStruct((4, r, lanes), blocks.dtype),
        scratch_shapes=[pltpu.SemaphoreType.DMA((3,)), pltpu.SemaphoreType.DMA((3,)), pltpu.SemaphoreType.DMA],
    )(blocks)


SHARDED = (("norm_g", 2), ("w_in", 2), ("conv_w", 2), ("w_branch", 3), ("w_out", 1), ("w_ff1", 2), ("w_ff2", 1))
MATMUL_WEIGHTS = ("w_in", "w_branch", "w_out", "w_ff1", "w_ff2")
VECTOR_WEIGHTS = ("norm_g", "conv_w")
REPLICATED = ("b_in", "sgu_ln_g", "sgu_ln_b", "w_spatial", "b_spatial", "a_log", "dt_bias", "dn_norm_g")
WEIGHT_ORDER = ("norm_g", "w_in", "b_in", "sgu_ln_g", "sgu_ln_b", "w_spatial", "b_spatial", "conv_w", "a_log",
                "dt_bias", "dn_norm_g", "w_branch", "w_out", "w_ff1", "w_ff2")
PACK_ROW_MULTIPLE = 32


def _size(shape):
    n = 1
    for dim in shape:
        n *= dim
    return n


def _wide(shape):
    return len(shape) >= 2 and shape[-1] > LANES


def _rows_of(shape):
    if _wide(shape):
        return (_size(shape) // shape[-1]) * -(-shape[-1] // LANES)
    return -(-_size(shape) // LANES)


def _pack(arrays):
    parts = []
    for a in arrays:
        if _wide(a.shape):
            rows = a.reshape(-1, a.shape[-1])
            pad = -a.shape[-1] % LANES
            if pad:
                rows = jnp.concatenate([rows, jnp.zeros((rows.shape[0], pad), a.dtype)], axis=1)
            parts += [rows[:, c:c + LANES] for c in range(0, rows.shape[1], LANES)]
            continue
        flat = a.reshape(-1)
        pad = _rows_of(a.shape) * LANES - flat.shape[0]
        if pad:
            flat = jnp.concatenate([flat, jnp.zeros((pad,), flat.dtype)])
        parts.append(flat.reshape(-1, LANES))
    rows = sum(p.shape[0] for p in parts)
    pad = -rows % PACK_ROW_MULTIPLE
    if pad:
        parts.append(jnp.zeros((pad, LANES), parts[0].dtype))
    return jnp.concatenate(parts, axis=0)


def _unpack(buf, shapes):
    out, row = [], 0
    for shape in shapes:
        rows = _rows_of(shape)
        part = buf[row:row + rows]
        if _wide(shape):
            chunks = -(-shape[-1] // LANES)
            each = rows // chunks
            whole = jnp.concatenate([part[c * each:(c + 1) * each] for c in range(chunks)], axis=1)
            out.append(whole[:, :shape[-1]].reshape(shape))
        else:
            out.append(part.reshape(-1)[:_size(shape)].reshape(shape))
        row += rows
    return out


def _chip_slice(a, axis, k):
    size = a.shape[axis] // 4
    return lax.slice_in_dim(a, k * size, (k + 1) * size, axis=axis)


def _rearrange_w_in(w):
    pad = jnp.zeros(w.shape[:-1] + (P_PAD - P_IN,), w.dtype)
    return jnp.concatenate([w[..., 1024:2560], w[..., 3072:3080], pad, w[..., 0:1024], w[..., 4616:7688],
                            w[..., 2560:3072], w[..., 3080:4616]], axis=-1)


def _restore_w_in(w):
    return jnp.concatenate([w[..., 2048:3072], w[..., 0:1536], w[..., 6144:6656], w[..., 1536:1544],
                            w[..., 6656:8192], w[..., 3072:6144]], axis=-1)


def _layer_params(wl):
    row = lambda v: v.reshape(1, -1)
    pad128 = lambda v, at: jnp.pad(v, (at, 128 - at - v.shape[0])).reshape(1, 128)
    return dict(
        g=[row(wl["norm_g"][i]) for i in range(4)],
        gmlp=[row(wl["sgu_ln_g"]), row(wl["sgu_ln_b"]), wl["w_spatial"],
              jnp.pad(wl["b_spatial"].T, ((0, 0), (0, 128 - GM_GROUPS)))],
        dn_in=[wl["conv_w"], pad128(wl["a_log"], DN_HEADS), pad128(wl["dt_bias"], DN_HEADS)],
        dn_g=[row(wl["dn_norm_g"])],
    )


def _layer_fwd(x0, wl, l):
    tag = lambda s: f"{s}_l{l}"
    pr = _layer_params(wl)
    w = BRANCH_WIDTH
    h0 = _row_fwd(_fn_rms, [(x0, 0, D_MODEL)], [pr["g"][0]], [(D_MODEL, BF16)], ts=512, name=tag("rms0"))[0]
    p = _mm(h0, wl["w_in"], bias=wl["b_in"].reshape(1, -1), name=tag("proj_in"))
    ya = _row_fwd(_fn_gmlp, [(p, OFF_GM, 2 * w)], pr["gmlp"], [(w, BF16)], ts=256, name=tag("gmlp"))[0]
    q, k, v, beta, g = _dn_in_fwd(p, pr["dn_in"], ts=256, name=tag("dn_in"))
    o, states = _delta_fwd(q, k, v, beta, g, name=tag("delta"))
    yb = _row_fwd(_fn_dn_out, [(o, 0, w), (p, OFF_BZ, w)], pr["dn_g"], [(w, BF16)], ts=512, name=tag("dn_out"))[0]
    yc = _sb_fwd(p, name=tag("sb"))
    ys = [ya, yb, yc]
    proj = [_mm(ys[i], wl["w_branch"][i], name=tag(f"branch{i}")) for i in range(3)]
    merge_rows = [(a, 0, D_MODEL) for a in proj] + [(p, OFF_GATE, 3 * D_MODEL)]
    m = _row_fwd(_fn_merge, merge_rows, [], [(D_MODEL, BF16)], ts=256, name=tag("merge"))[0]
    mixed = _mm(m, wl["w_out"], name=tag("out"))
    x1 = _row_fwd(_fn_resid_rms, [(x0, 0, D_MODEL), (mixed, 0, D_MODEL)], [pr["g"][1]], [(D_MODEL, F32)], ts=512,
                  name=tag("resid1"))[0]
    h2 = _row_fwd(_fn_rms, [(x1, 0, D_MODEL)], [pr["g"][2]], [(D_MODEL, BF16)], ts=512, name=tag("rms2"))[0]
    a, r = _mm(h2, wl["w_ff1"], name=tag("ff1"), out_dtype=(BF16, BF16),
               finish=lambda t: (jnp.maximum(t, 0.0), jnp.square(jnp.maximum(t, 0.0))))
    f = _mm(r, wl["w_ff2"], name=tag("ff2"))
    x2 = _row_fwd(_fn_resid_rms, [(x1, 0, D_MODEL), (f, 0, D_MODEL)], [pr["g"][3]], [(D_MODEL, F32)], ts=512,
                  name=tag("resid2"))[0]
    saved = dict(x0=x0, h0=h0, p=p, q=q, k=k, v=v, beta=beta, g=g, states=states, o=o, ys=ys,
                 proj=proj, m=m, mixed=mixed, x1=x1, h2=h2, a=a, r=r, f=f)
    return x2, saved


def _layer_bwd(dx2, sv, wl, l):
    tag = lambda s: f"{s}_l{l}"
    pr = _layer_params(wl)
    w = BRANCH_WIDTH
    full = lambda a: (a, 0, a.shape[1])
    p = sv["p"]
    (df,), (dg3,) = _row_bwd(_fn_rms_branch, [full(sv["f"])], [pr["g"][3]], [full(dx2)], ts=512, name=tag("resid2_b"))
    da = _mm(df, wl["w_ff2"], trans_b=True, name=tag("ff2_dx"), out_dtype=BF16, beside=(sv["a"],),
             finish=lambda t, relu_a: 2.0 * relu_a.astype(F32) * t)
    dw_ff2 = _mm_tn(sv["r"], df, name=tag("ff2_dw"))
    dh2 = _mm(da, wl["w_ff1"], trans_b=True, name=tag("ff1_dx"))
    dw_ff1 = _mm_tn(sv["h2"], da, name=tag("ff1_dw"))
    (dx1,), (dg2,) = _row_bwd(_fn_rms_keep, [full(sv["x1"])], [pr["g"][2]], [full(dh2), full(dx2)], ts=512,
                              name=tag("rms2_b"))
    (dmixed,), (dg1,) = _row_bwd(_fn_rms_branch, [full(sv["mixed"])], [pr["g"][1]], [full(dx1)], ts=512,
                                 name=tag("resid1_b"))
    dm = _mm(dmixed, wl["w_out"], trans_b=True, name=tag("out_dx"))
    dw_out = _mm_tn(sv["m"], dmixed, name=tag("out_dw"))
    dp = lax.empty(p.shape, F32)
    merge_rows = [full(a) for a in sv["proj"]] + [(p, OFF_GATE, 3 * D_MODEL)]
    dmerge, _ = _row_bwd(_fn_merge, merge_rows, [], [full(dm)], ts=256, name=tag("merge_b"), into=(dp, 3))
    dproj, dp = dmerge[:3], dmerge[3]
    dys = [_mm(dproj[i], wl["w_branch"][i], trans_b=True, name=tag(f"branch{i}_dx")) for i in range(3)]
    dw_branch = jnp.stack([_mm_tn(sv["ys"][i], dproj[i], name=tag(f"branch{i}_dw")) for i in range(3)])
    (dp,), dgm = _row_bwd(_fn_gmlp, [(p, OFF_GM, 2 * w)], pr["gmlp"], [full(dys[0])], ts=256, name=tag("gmlp_b"),
                          into=(dp, 0))
    (do, dp), (d_dn_g,) = _row_bwd(_fn_dn_out, [full(sv["o"]), (p, OFF_BZ, w)], pr["dn_g"], [full(dys[1])], ts=512,
                                   name=tag("dn_out_b"), into=(dp, 1))
    dqkvbg = _delta_bwd(sv["q"], sv["k"], sv["v"], sv["beta"], sv["g"], sv["states"], do, name=tag("delta_b"))
    dp, d_dn_in = _dn_in_bwd(p, pr["dn_in"], dqkvbg, dp, ts=256, name=tag("dn_in_b"))
    for off, part in zip((OFF_CQ, OFF_CK, OFF_CV), _sb_bwd(p, dys[2], name=tag("sb_b"))):
        dp = lax.dynamic_update_slice(dp, part, (0, off))
    dh0 = _mm(dp, wl["w_in"], trans_b=True, name=tag("proj_in_dx"))
    dw_in, db_in = _mm_tn(sv["h0"], dp, name=tag("proj_in_dw"), col_sums=True)
    (dx0,), (dg0,) = _row_bwd(_fn_rms_keep, [full(sv["x0"])], [pr["g"][0]], [full(dh0), full(dx1)], ts=512,
                              name=tag("rms0_b"))
    grads = dict(
        norm_g=jnp.concatenate([dg0, dg1, dg2, dg3], axis=0), w_in=dw_in, b_in=db_in.reshape(-1),
        sgu_ln_g=dgm[0].reshape(-1), sgu_ln_b=dgm[1].reshape(-1), w_spatial=dgm[2],
        b_spatial=dgm[3][:, :GM_GROUPS].T, conv_w=d_dn_in[0], a_log=d_dn_in[1][0, DN_HEADS:2 * DN_HEADS],
        dt_bias=d_dn_in[2][0, DN_HEADS:2 * DN_HEADS], dn_norm_g=d_dn_g.reshape(-1), w_branch=dw_branch,
        w_out=dw_out, w_ff1=dw_ff1, w_ff2=dw_ff2)
    return dx0, grads


def _local_step(x, target, weights):
    saved = []
    h = x
    layers = []
    for l in range(DEPTH):
        wl = {n: weights[n][l] for n in WEIGHT_ORDER}
        layers.append(wl)
        h, sv = _layer_fwd(h, wl, l)
        saved.append(sv)
    sq, dh = _loss_head(h, target, name="loss_head")
    grads = [None] * DEPTH
    for l in reversed(range(DEPTH)):
        dh, grads[l] = _layer_bwd(dh, saved[l], layers[l], l)
    stacked = {n: jnp.stack([grads[l][n].astype(GRAD_DTYPE) for l in range(DEPTH)]) for n in WEIGHT_ORDER}
    return sq[0, 0], dh, stacked


def kernel(x, norm_g, w_in, b_in, sgu_ln_g, sgu_ln_b, w_spatial, b_spatial, conv_w, a_log, dt_bias, dn_norm_g, w_branch, w_out, w_ff1, w_ff2, loss_target, m_norm_g, m_w_in, m_b_in, m_sgu_ln_g, m_sgu_ln_b, m_w_spatial, m_b_spatial, m_conv_w, m_a_log, m_dt_bias, m_dn_norm_g, m_w_branch, m_w_out, m_w_ff1, m_w_ff2, v_norm_g, v_w_in, v_b_in, v_sgu_ln_g, v_sgu_ln_b, v_w_spatial, v_b_spatial, v_conv_w, v_a_log, v_dt_bias, v_dn_norm_g, v_w_branch, v_w_out, v_w_ff1, v_w_ff2):
    local = dict(norm_g=norm_g, w_in=w_in, b_in=b_in, sgu_ln_g=sgu_ln_g, sgu_ln_b=sgu_ln_b, w_spatial=w_spatial,
                 b_spatial=b_spatial, conv_w=conv_w, a_log=a_log, dt_bias=dt_bias, dn_norm_g=dn_norm_g,
                 w_branch=w_branch, w_out=w_out, w_ff1=w_ff1, w_ff2=w_ff2)
    mom1 = dict(norm_g=m_norm_g, w_in=m_w_in, b_in=m_b_in, sgu_ln_g=m_sgu_ln_g, sgu_ln_b=m_sgu_ln_b,
                w_spatial=m_w_spatial, b_spatial=m_b_spatial, conv_w=m_conv_w, a_log=m_a_log, dt_bias=m_dt_bias,
                dn_norm_g=m_dn_norm_g, w_branch=m_w_branch, w_out=m_w_out, w_ff1=m_w_ff1, w_ff2=m_w_ff2)
    mom2 = dict(norm_g=v_norm_g, w_in=v_w_in, b_in=v_b_in, sgu_ln_g=v_sgu_ln_g, sgu_ln_b=v_sgu_ln_b,
                w_spatial=v_w_spatial, b_spatial=v_b_spatial, conv_w=v_conv_w, a_log=v_a_log, dt_bias=v_dt_bias,
                dn_norm_g=v_dn_norm_g, w_branch=v_w_branch, w_out=v_w_out, w_ff1=v_w_ff1, w_ff2=v_w_ff2)
    shard_names = [n for n, _ in SHARDED]
    shard_shapes = [local[n].shape for n in shard_names]
    repl_shapes = [local[n].shape for n in REPLICATED]

    weights = {n: local[n] for n in REPLICATED}
    for names, dtype, call in ((MATMUL_WEIGHTS, BF16, "gather_matmul_weights"), (VECTOR_WEIGHTS, F32, "gather_vectors")):
        gathered = _gather_chips(_pack([local[n] for n in names]).astype(dtype), name=call)
        per_chip = [_unpack(gathered[k], [local[n].shape for n in names]) for k in range(4)]
        for i, n in enumerate(names):
            weights[n] = jnp.concatenate([per_chip[k][i] for k in range(4)], axis=dict(SHARDED)[n])
    weights["w_in"] = _rearrange_w_in(weights["w_in"])
    weights["b_in"] = _rearrange_w_in(weights["b_in"])

    sq, dx, grads = _local_step(x[0], loss_target[0], weights)
    loss = lax.psum(0.5 * sq / D_MODEL, ("x", "y", "c"))
    grads["w_in"] = _restore_w_in(grads["w_in"])
    grads["b_in"] = _restore_w_in(grads["b_in"])

    blocks = jnp.stack([_pack([_chip_slice(grads[n], axis, k) for n, axis in SHARDED] + [grads[n] for n in REPLICATED])
                        for k in range(4)])
    rows = blocks.shape[1]
    half = rows // 2
    c = lax.axis_index("c")
    got = _halves_to_sibling(blocks, name="grads_to_sibling")
    chip_sum = _add_own_half(blocks, got, name="grads_chip_sum", out_dtype=BF16)
    by_chip = _scatter_chips(chip_sum, name="grads_scatter")
    my_half = _add_rows([by_chip[k] for k in range(4)], name="grads_sum")
    other_half = _to_sibling(my_half, name="grads_half_swap")
    total = jnp.concatenate([jnp.where(c == 0, my_half, other_half), jnp.where(c == 0, other_half, my_half)], axis=0)
    g_out = dict(zip(shard_names + list(REPLICATED), _unpack(total, shard_shapes + repl_shapes)))

    d_out, m_out, v_out = {}, {}, {}
    for n in WEIGHT_ORDER:
        d_out[n], m_out[n], v_out[n] = _adamw(local[n], g_out[n], mom1[n], mom2[n], name=f"adamw_{n}")
    return (loss, dx[None], *[g_out[n] for n in WEIGHT_ORDER], *[d_out[n] for n in WEIGHT_ORDER],
            *[m_out[n] for n in WEIGHT_ORDER], *[v_out[n] for n in WEIGHT_ORDER])
```

```python
import functools

import jax
import jax.numpy as jnp
from jax import lax
from jax.experimental import pallas as pl
from jax.experimental.pallas import tpu as pltpu

F32 = jnp.float32
BF16 = jnp.bfloat16
GRAD_DTYPE = BF16

D_MODEL = 1024
DEPTH = 4
BRANCH_WIDTH = 512
GM_CHUNK = 128
GM_GROUPS = 8
DN_HEADS = 4
DN_HEAD_DIM = 128
CONV_WIDTH = 4
DN_HALO = 8
DN_BLOCK = 128
DN_SOLVE_PASSES, DN_STATE_PASSES, DN_OUT_PASSES = 1, 1, 1
SB_HEAD_DIM = 64
SB_BLOCK = 128
SB_QUERY_ROWS = 256
SB_DEAD_LOG = -88.0
D_FF = 4096
P_IN = 7688
P_PAD = 8192
NORM_EPS = 1e-6
ADAM_LR, ADAM_B1, ADAM_B2, ADAM_EPS, ADAM_WD, ADAM_STEP = 0.001, 0.9, 0.999, 1e-08, 0.01, 10

OFF_DN, DN_IN_WIDTH = 0, 2048
OFF_BD = 1536
OFF_GM = 2048
OFF_GATE = 3072
OFF_BZ = 6144
OFF_CQ, OFF_CK, OFF_CV = 6656, 7168, 7680

LANES = 1024
VMEM_LIMIT_BYTES = 56 * 1024 * 1024

_HI = lax.Precision.HIGHEST


def _params(sem):
    return pltpu.CompilerParams(dimension_semantics=sem, vmem_limit_bytes=VMEM_LIMIT_BYTES)


def _mm(a, b, *, name, out_dtype=F32, bias=None, trans_b=False, beside=(), finish=None, tm=1024, tn=1024, tk=1024):
    m, k = a.shape
    n = b.shape[0] if trans_b else b.shape[1]
    tm, tn, tk = min(tm, m), min(tn, n), min(tk, k)
    assert m % tm == 0 and n % tn == 0 and k % tk == 0, (a.shape, b.shape)
    nk = k // tk
    dn = (((1,), (1,)), ((), ())) if trans_b else (((1,), (0,)), ((), ()))
    several = isinstance(out_dtype, tuple)
    out_dtypes = out_dtype if several else (out_dtype,)
    n_in = 2 + (bias is not None) + len(beside)

    def body(*refs):
        a_ref, b_ref = refs[:2]
        o_refs, acc = refs[n_in:-1], refs[-1]
        kk = pl.program_id(2)
        part = lax.dot_general(a_ref[...].astype(BF16), b_ref[...].astype(BF16), dn, preferred_element_type=F32)

        @pl.when(kk == 0)
        def _():
            acc[...] = part

        @pl.when(kk > 0)
        def _():
            acc[...] += part

        @pl.when(kk == nk - 1)
        def _():
            r = acc[...]
            if bias is not None:
                r = r + refs[2][...]
            tiles = (r,) if finish is None else finish(r, *[t[...] for t in refs[n_in - len(beside):n_in]])
            for o_ref, tile in zip(o_refs, tiles if isinstance(tiles, tuple) else (tiles,)):
                o_ref[...] = tile.astype(o_ref.dtype)

    in_specs = [pl.BlockSpec((tm, tk), lambda i, j, kk: (i, kk))]
    if trans_b:
        in_specs.append(pl.BlockSpec((tn, tk), lambda i, j, kk: (j, kk)))
    else:
        in_specs.append(pl.BlockSpec((tk, tn), lambda i, j, kk: (kk, j)))
    args = [a, b]
    if bias is not None:
        in_specs.append(pl.BlockSpec((1, tn), lambda i, j, kk: (0, j)))
        args.append(bias)
    for t in beside:
        assert t.shape == (m, n)
        in_specs.append(pl.BlockSpec((tm, tn), lambda i, j, kk: (i, j)))
        args.append(t)
    res = pl.pallas_call(
        body, name=name, grid=(m // tm, n // tn, nk),
        in_specs=in_specs, out_specs=[pl.BlockSpec((tm, tn), lambda i, j, kk: (i, j)) for _ in out_dtypes],
        out_shape=[jax.ShapeDtypeStruct((m, n), dt) for dt in out_dtypes],
        scratch_shapes=[pltpu.VMEM((tm, tn), F32)],
        compiler_params=_params(("parallel", "parallel", "arbitrary")),
    )(*args)
    return list(res) if several else res[0]


def _mm_tn(a, b, *, name, col_sums=False, tm=1024, tn=1024, ts=1024):
    out_dtype = GRAD_DTYPE
    s, ka = a.shape
    n = b.shape[1]
    tm, tn, ts = min(tm, ka), min(tn, n), min(ts, s)
    assert ka % tm == 0 and n % tn == 0 and s % ts == 0, (a.shape, b.shape)
    steps = s // ts

    def body(a_ref, b_ref, o_ref, *rest):
        acc = rest[-1]
        step = pl.program_id(2)
        part = lax.dot_general(a_ref[...].astype(BF16), b_ref[...].astype(BF16), (((0,), (0,)), ((), ())),
                               preferred_element_type=F32)

        @pl.when(step == 0)
        def _():
            acc[...] = part

        @pl.when(step > 0)
        def _():
            acc[...] += part

        @pl.when(step == steps - 1)
        def _():
            o_ref[...] = acc[...].astype(out_dtype)

        if col_sums:
            sums_ref = rest[0]
            first_rows = jnp.logical_and(pl.program_id(1) == 0, step == 0)

            @pl.when(first_rows)
            def _():
                sums_ref[...] = jnp.zeros_like(sums_ref)

            @pl.when(pl.program_id(1) == 0)
            def _():
                sums_ref[...] += jnp.sum(b_ref[...].astype(F32), axis=0, keepdims=True)

    out_specs = [pl.BlockSpec((tm, tn), lambda j, i, r: (i, j))]
    out_shape = [jax.ShapeDtypeStruct((ka, n), out_dtype)]
    if col_sums:
        out_specs.append(pl.BlockSpec((1, tn), lambda j, i, r: (0, j)))
        out_shape.append(jax.ShapeDtypeStruct((1, n), F32))
    res = pl.pallas_call(
        body, name=name, grid=(n // tn, ka // tm, steps),
        in_specs=[pl.BlockSpec((ts, tm), lambda j, i, r: (r, i)), pl.BlockSpec((ts, tn), lambda j, i, r: (r, j))],
        out_specs=out_specs, out_shape=out_shape,
        scratch_shapes=[pltpu.VMEM((tm, tn), F32)],
        compiler_params=_params(("parallel", "arbitrary", "arbitrary")),
    )(a, b)
    return list(res) if col_sums else res[0]


def _col_block(i, *, c):
    return (i, c)


def _whole(i, *, nd):
    return (0,) * nd


def _row_specs(rows, ts):
    specs = []
    for arr, off, w in rows:
        assert off % w == 0 and arr.shape[0] % ts == 0
        specs.append(pl.BlockSpec((ts, w), functools.partial(_col_block, c=off // w)))
    return specs


def _row_fwd(fn, rows, params, outs, *, ts, name):
    s = rows[0][0].shape[0]
    ts = min(ts, s)
    nr, npar = len(rows), len(params)

    def body(*refs):
        rv = [r[...].astype(F32) for r in refs[:nr]]
        pv = [p[...] for p in refs[nr:nr + npar]]
        for o_ref, val in zip(refs[nr + npar:], fn(pv, rv)):
            o_ref[...] = val.astype(o_ref.dtype)

    in_specs = _row_specs(rows, ts) + [pl.BlockSpec(p.shape, functools.partial(_whole, nd=p.ndim)) for p in params]
    res = pl.pallas_call(
        body, name=name, grid=(s // ts,), in_specs=in_specs,
        out_specs=[pl.BlockSpec((ts, w), lambda i: (i, 0)) for w, _ in outs],
        out_shape=[jax.ShapeDtypeStruct((s, w), dt) for w, dt in outs],
        compiler_params=_params(("parallel",)),
    )(*[r[0] for r in rows], *params)
    return list(res)


def _row_bwd(fn, rows, params, cts, *, ts, name, into=None, matmul_only=()):
    s = rows[0][0].shape[0]
    ts = min(ts, s)
    nr, npar, nc = len(rows), len(params), len(cts)
    n_in = nr + npar + nc + (into is not None)

    def body(*refs):
        rv = [r[...].astype(F32) for r in refs[:nr]]
        pv = [p[...] for p in refs[nr:nr + npar]]
        cv = [c[...].astype(F32) for c in refs[nr + npar:nr + npar + nc]]
        out_refs = refs[n_in:]
        _, vjp = jax.vjp(lambda p, r: tuple(fn(p, r)), pv, rv)
        dp, dr = vjp(tuple(cv))
        for o_ref, val in zip(out_refs[:nr], dr):
            o_ref[...] = val.astype(o_ref.dtype)

        @pl.when(pl.program_id(0) == 0)
        def _():
            for o_ref in out_refs[nr:]:
                o_ref[...] = jnp.zeros_like(o_ref)

        for o_ref, val in zip(out_refs[nr:], dp):
            o_ref[...] += val

    in_specs = (_row_specs(rows, ts) + [pl.BlockSpec(p.shape, functools.partial(_whole, nd=p.ndim)) for p in params]
                + _row_specs(cts, ts))
    out_specs = [pl.BlockSpec((ts, w), lambda i: (i, 0)) for _, _, w in rows]
    out_shape = [jax.ShapeDtypeStruct((s, w), BF16 if k in matmul_only else F32) for k, (_, _, w) in enumerate(rows)]
    args = [r[0] for r in rows] + list(params) + [c[0] for c in cts]
    aliases = {}
    if into is not None:
        buffer, at = into
        in_specs.append(pl.BlockSpec(memory_space=pl.ANY))
        args.append(buffer)
        out_specs[at] = _row_specs([(buffer,) + tuple(rows[at][1:])], ts)[0]
        out_shape[at] = jax.ShapeDtypeStruct(buffer.shape, buffer.dtype)
        aliases = {n_in - 1: at}
    out_specs += [pl.BlockSpec(p.shape, functools.partial(_whole, nd=p.ndim)) for p in params]
    out_shape += [jax.ShapeDtypeStruct(p.shape, F32) for p in params]
    res = pl.pallas_call(
        body, name=name, grid=(s // ts,), in_specs=in_specs, out_specs=out_specs, out_shape=out_shape,
        input_output_aliases=aliases, compiler_params=_params(("arbitrary",)),
    )(*args)
    res = list(res)
    return res[:nr], res[nr:]


def _rms(x, g):
    return x * lax.rsqrt(jnp.mean(x * x, axis=-1, keepdims=True) + NORM_EPS) * g


def _gelu(x):
    return 0.5 * x * (1.0 + lax.erf(x * (2.0 ** -0.5)))


def _softplus(x):
    return jnp.maximum(x, 0.0) + jnp.log1p(jnp.exp(-jnp.abs(x)))


def _iota2(shape, dim):
    return lax.broadcasted_iota(jnp.int32, shape, dim)


def _fn_rms(pv, rv):
    return [_rms(rv[0], pv[0])]


def _fn_rms_keep(pv, rv):
    return [_rms(rv[0], pv[0]), rv[0]]


def _fn_resid_rms(pv, rv):
    return [rv[0] + _rms(rv[1], pv[0])]


def _fn_rms_branch(pv, rv):
    return [_rms(rv[0], pv[0])]


def _fn_merge(pv, rv):
    return [sum(jax.nn.sigmoid(rv[3][:, i * D_MODEL:(i + 1) * D_MODEL]) * rv[i] for i in range(3))]


def _fn_gmlp(pv, rv):
    ln_g, ln_b, w_sp, b_t = pv
    u = _gelu(rv[0][:, :BRANCH_WIDTH])
    v = _gelu(rv[0][:, BRANCH_WIDTH:])
    vc = v - jnp.mean(v, axis=-1, keepdims=True)
    v = vc * lax.rsqrt(jnp.mean(vc * vc, axis=-1, keepdims=True) + NORM_EPS) * ln_g + ln_b
    t = GM_CHUNK
    causal = _iota2((t, t), 1) <= _iota2((t, t), 0)
    first = _iota2((t, 128), 1) < 64
    expand = (_iota2((128, BRANCH_WIDTH), 0) == _iota2((128, BRANCH_WIDTH), 1) // 64).astype(F32)
    b_full = jnp.dot(b_t, expand, precision=_HI, preferred_element_type=F32)
    w_bf = [jnp.where(causal, w_sp[g], 0.0).astype(BF16) for g in range(GM_GROUPS)]
    chunks = []
    for c in range(rv[0].shape[0] // t):
        pairs = []
        for p in range(GM_GROUPS // 2):
            vp = v[c * t:(c + 1) * t, 128 * p:128 * (p + 1)].astype(BF16)
            m0 = jnp.dot(w_bf[2 * p], vp, preferred_element_type=F32)
            m1 = jnp.dot(w_bf[2 * p + 1], vp, preferred_element_type=F32)
            pairs.append(jnp.where(first, m0, m1))
        chunks.append(jnp.concatenate(pairs, axis=1) + b_full)
    return [u * jnp.concatenate(chunks, axis=0)]


def _head_expand(col0):
    return (_iota2((128, BRANCH_WIDTH), 0) == _iota2((128, BRANCH_WIDTH), 1) // DN_HEAD_DIM + col0).astype(F32)


@functools.partial(jax.custom_vjp, nondiff_argnums=(1,))
def _roll_rows(x, n):
    return pltpu.roll(x, n, 0)


def _roll_rows_fwd(x, n):
    return pltpu.roll(x, n, 0), None


def _roll_rows_bwd(n, _, ct):
    return (pltpu.roll(ct, ct.shape[0] - n, 0),)


_roll_rows.defvjp(_roll_rows_fwd, _roll_rows_bwd)


def _fn_dn_in(pv, rv):
    conv_w, a_log, dt_b = pv
    cur, before = rv
    x = cur[:, :3 * BRANCH_WIDTH]
    ext = jnp.concatenate([before[:, :3 * BRANCH_WIDTH], x], axis=0)
    c = conv_w[CONV_WIDTH - 1:CONV_WIDTH, :] * x
    for j in range(CONV_WIDTH - 1):
        c = c + conv_w[j:j + 1, :] * _roll_rows(ext, CONV_WIDTH - 1 - j)[DN_HALO:]
    a = c * jax.nn.sigmoid(c)
    outs = []
    for part in range(3):
        heads = []
        for h in range(DN_HEADS):
            lo = part * BRANCH_WIDTH + h * DN_HEAD_DIM
            xh = a[:, lo:lo + DN_HEAD_DIM]
            if part < 2:
                xh = xh * lax.rsqrt(jnp.sum(xh * xh, axis=-1, keepdims=True) + NORM_EPS)
            heads.append(xh)
        outs.append(jnp.concatenate(heads, axis=1))
    bd = cur[:, OFF_BD:OFF_BD + 128]
    beta = jax.nn.sigmoid(bd)
    g = -jnp.exp(a_log) * _softplus(bd + dt_b)
    outs.append(jnp.dot(beta, _head_expand(0), precision=_HI, preferred_element_type=F32))
    outs.append(jnp.dot(g, _head_expand(DN_HEADS), precision=_HI, preferred_element_type=F32))
    return outs


def _dn_in_specs(ts, first_block):
    per = ts // DN_HALO
    return [pl.BlockSpec((ts, DN_IN_WIDTH), lambda i: (first_block(i), OFF_DN // DN_IN_WIDTH)),
            pl.BlockSpec((DN_HALO, DN_IN_WIDTH),
                         lambda i: (jnp.maximum(first_block(i) * per - 1, 0), OFF_DN // DN_IN_WIDTH))]


def _dn_in_fwd(p, params, *, ts, name):
    s = p.shape[0]
    ts = min(ts, s)

    def body(cur_ref, before_ref, cw_ref, al_ref, db_ref, *o_refs):
        before = jnp.where(pl.program_id(0) == 0, 0.0, before_ref[...])
        outs = _fn_dn_in([cw_ref[...], al_ref[...], db_ref[...]], [cur_ref[...], before])
        for o_ref, val in zip(o_refs, outs):
            o_ref[...] = val

    whole = [pl.BlockSpec(a.shape, functools.partial(_whole, nd=a.ndim)) for a in params]
    res = pl.pallas_call(
        body, name=name, grid=(s // ts,), in_specs=_dn_in_specs(ts, lambda i: i) + whole,
        out_specs=[pl.BlockSpec((ts, BRANCH_WIDTH), lambda i: (i, 0))] * 5,
        out_shape=[jax.ShapeDtypeStruct((s, BRANCH_WIDTH), F32)] * 5,
        compiler_params=_params(("parallel",)),
    )(p, p, *params)
    return list(res)


def _dn_in_bwd(p, params, cts, dp, *, ts, name):
    s = p.shape[0]
    ts = min(ts, s)
    nb = s // ts
    block = lambda i: nb - 1 - i

    def body(cur_ref, before_ref, cw_ref, al_ref, db_ref, *rest):
        ct_refs, dx_ref, dpar_refs, halo = rest[:5], rest[6], rest[7:10], rest[10]
        first = pl.program_id(0) == 0

        @pl.when(first)
        def _():
            halo[...] = jnp.zeros_like(halo)
            for o_ref in dpar_refs:
                o_ref[...] = jnp.zeros_like(o_ref)

        before = jnp.where(pl.program_id(0) == nb - 1, 0.0, before_ref[...])
        _, vjp = jax.vjp(lambda pv, cur, bef: tuple(_fn_dn_in(pv, [cur, bef])),
                         [cw_ref[...], al_ref[...], db_ref[...]], cur_ref[...], before)
        dpar, dcur, dbefore = vjp(tuple(c[...] for c in ct_refs))
        dx_ref[...] = jnp.concatenate([dcur[:ts - DN_HALO], dcur[ts - DN_HALO:] + halo[...]], axis=0)
        halo[...] = dbefore
        for o_ref, val in zip(dpar_refs, dpar):
            o_ref[...] += val

    whole = [pl.BlockSpec(a.shape, functools.partial(_whole, nd=a.ndim)) for a in params]
    tile = pl.BlockSpec((ts, BRANCH_WIDTH), lambda i: (block(i), 0))
    res = pl.pallas_call(
        body, name=name, grid=(nb,),
        in_specs=_dn_in_specs(ts, block) + whole + [tile] * 5 + [pl.BlockSpec(memory_space=pl.ANY)],
        out_specs=[pl.BlockSpec((ts, DN_IN_WIDTH), lambda i: (block(i), OFF_DN // DN_IN_WIDTH))] + whole,
        out_shape=[jax.ShapeDtypeStruct(dp.shape, dp.dtype)] + [jax.ShapeDtypeStruct(a.shape, F32) for a in params],
        scratch_shapes=[pltpu.VMEM((DN_HALO, DN_IN_WIDTH), F32)],
        input_output_aliases={10: 0}, compiler_params=_params(("arbitrary",)),
    )(p, p, *params, *cts, dp)
    return res[0], list(res[1:])


def _fn_dn_out(pv, rv):
    heads = []
    for h in range(DN_HEADS):
        sl = slice(h * DN_HEAD_DIM, (h + 1) * DN_HEAD_DIM)
        z = rv[1][:, sl]
        heads.append(_rms(rv[0][:, sl], pv[0]) * (z * jax.nn.sigmoid(z)))
    return [jnp.concatenate(heads, axis=1)]


_DIMS = {"nn": (((1,), (0,)), ((), ())), "nt": (((1,), (1,)), ((), ())), "tn": (((0,), (0,)), ((), ()))}
_DIMS_BWD = {"nn": (("nt", "c", "b"), ("tn", "a", "c")), "nt": (("nn", "c", "b"), ("tn", "c", "a")),
             "tn": (("nt", "b", "c"), ("nn", "a", "c"))}


def _bf16_dot(a, b, kind):
    return lax.dot_general(a.astype(BF16), b.astype(BF16), _DIMS[kind], preferred_element_type=F32)


def _pdot_raw(a, b, kind, mode):
    if mode == 1:
        return _bf16_dot(a, b, kind)
    if mode == 6:
        return lax.dot_general(a, b, _DIMS[kind], precision=_HI, preferred_element_type=F32)
    b_hi, b_lo = _split_bf16(b)
    if mode == 3:
        a_hi, a_lo = _split_bf16(a)
        return _bf16_dot(a_hi, b_hi, kind) + (_bf16_dot(a_hi, b_lo, kind) + _bf16_dot(a_lo, b_hi, kind))
    b_rest = (b - b_hi.astype(F32) - b_lo.astype(F32)).astype(BF16)
    return _bf16_dot(a, b_hi, kind) + (_bf16_dot(a, b_lo, kind) + _bf16_dot(a, b_rest, kind))


@functools.partial(jax.custom_vjp, nondiff_argnums=(2, 3))
def _pdot(a, b, kind, mode):
    return _pdot_raw(a, b, kind, mode)


def _pdot_fwd(a, b, kind, mode):
    return _pdot_raw(a, b, kind, mode), (a, b)


def _pdot_bwd(kind, mode, res, ct):
    ops = {"a": res[0], "b": res[1], "c": ct}
    (ka, a1, a2), (kb, b1, b2) = _DIMS_BWD[kind]
    if mode == "count":
        return jnp.zeros_like(res[0]), _pdot(ops[b1], ops[b2], kb, mode)
    return _pdot(ops[a1], ops[a2], ka, mode), _pdot(ops[b1], ops[b2], kb, mode)


_pdot.defvjp(_pdot_fwd, _pdot_bwd)


@jax.custom_vjp
def _unit_lower_inverses(mats):
    c = mats[0].shape[0]
    row, col = _iota2((c, c), 0), _iota2((c, c), 1)
    x = [(row == col).astype(F32) for _ in mats]
    shift = 0
    while (1 << shift) < c:
        pair = jnp.right_shift(row, shift + 1) == jnp.right_shift(col, shift + 1)
        between = pair & (jnp.right_shift(row, shift) != jnp.right_shift(col, shift))
        q = [jnp.where(between, a, 0.0) for a in mats]
        qd = [_pdot_raw(qi, xi, "nn", DN_SOLVE_PASSES) for qi, xi in zip(q, x)]
        x = [xi - _pdot_raw(xi, m, "nn", DN_SOLVE_PASSES) for xi, m in zip(x, qd)]
        shift += 1
    return tuple(x)


def _unit_lower_inverses_fwd(mats):
    x = _unit_lower_inverses(mats)
    return x, x


def _unit_lower_inverses_bwd(x, dx):
    inner = [_pdot_raw(d, xi, "nt", DN_SOLVE_PASSES) for d, xi in zip(dx, x)]
    return (tuple(-_pdot_raw(xi, m, "tn", DN_SOLVE_PASSES) for xi, m in zip(x, inner)),)


_unit_lower_inverses.defvjp(_unit_lower_inverses_fwd, _unit_lower_inverses_bwd)


def _head(h):
    return slice(h * DN_HEAD_DIM, (h + 1) * DN_HEAD_DIM)


def _delta_chunk(states, q, k, v, beta, g):
    c = DN_BLOCK
    heads = range(DN_HEADS)
    row, col = _iota2((c, c), 0), _iota2((c, c), 1)
    tri, strict = col <= row, col < row
    counts = jnp.concatenate([tri.astype(BF16), jnp.ones((c, c), BF16)], axis=0)
    sums = _pdot(counts, g, "nn", "count")
    gc = [sums[:c, _head(h)] for h in heads]
    gl = [sums[c:, _head(h)] for h in heads]
    qh, kh, vh, bh = ([t[:, _head(h)] for h in heads] for t in (q * (DN_HEAD_DIM ** -0.5), k, v, beta))
    decay = [jnp.where(tri, jnp.exp(jnp.where(tri, gc[h] - gc[h].T, 0.0)), 0.0) for h in heads]
    solve, carry, out = DN_SOLVE_PASSES, DN_STATE_PASSES, DN_OUT_PASSES
    kk = [_pdot(kh[h], kh[h], "nt", solve) for h in heads]
    x = _unit_lower_inverses(tuple(jnp.where(strict, bh[h] * kk[h] * decay[h], 0.0) for h in heads))
    eg = [jnp.exp(gc[h]) for h in heads]
    u = [_pdot(x[h], vh[h] * bh[h], "nn", solve) for h in heads]
    wk = [_pdot(x[h], kh[h] * (bh[h] * eg[h]), "nn", solve) for h in heads]
    qk = [jnp.where(tri, _pdot(qh[h], kh[h], "nt", out) * decay[h], 0.0) for h in heads]
    v_new = [u[h] - _pdot(wk[h], states[h], "nn", carry) for h in heads]
    o = [_pdot(qh[h] * eg[h], states[h], "nn", out) + _pdot(qk[h], v_new[h], "nn", out) for h in heads]
    nxt = [states[h] * jnp.exp(gl[h]) + _pdot(kh[h] * jnp.exp(gl[h] - gc[h]), v_new[h], "tn", carry)
           for h in heads]
    return jnp.concatenate(o, axis=1), tuple(nxt)


def _delta_fwd(q, k, v, beta, g, *, name):
    s = q.shape[0]
    c = DN_BLOCK
    nc = s // c

    def body(q_ref, k_ref, v_ref, b_ref, g_ref, o_ref, sp_ref, st):
        @pl.when(pl.program_id(0) == 0)
        def _():
            st[...] = jnp.zeros_like(st)

        states = tuple(st[h] for h in range(DN_HEADS))
        for h in range(DN_HEADS):
            sp_ref[0, h] = states[h]
        o, nxt = _delta_chunk(states, q_ref[...], k_ref[...], v_ref[...], b_ref[...], g_ref[...])
        for h in range(DN_HEADS):
            st[h] = nxt[h]
        o_ref[...] = o

    blk = pl.BlockSpec((c, BRANCH_WIDTH), lambda n: (n, 0))
    return pl.pallas_call(
        body, name=name, grid=(nc,), in_specs=[blk] * 5,
        out_specs=[blk, pl.BlockSpec((1, DN_HEADS, DN_HEAD_DIM, DN_HEAD_DIM), lambda n: (n, 0, 0, 0))],
        out_shape=[jax.ShapeDtypeStruct((s, BRANCH_WIDTH), F32),
                   jax.ShapeDtypeStruct((nc, DN_HEADS, DN_HEAD_DIM, DN_HEAD_DIM), F32)],
        scratch_shapes=[pltpu.VMEM((DN_HEADS, DN_HEAD_DIM, DN_HEAD_DIM), F32)],
        compiler_params=_params(("arbitrary",)),
    )(q, k, v, beta, g)


def _delta_bwd(q, k, v, beta, g, states, do, *, name):
    s = q.shape[0]
    c = DN_BLOCK
    nc = s // c

    def body(q_ref, k_ref, v_ref, b_ref, g_ref, sp_ref, do_ref, dq_ref, dk_ref, dv_ref, db_ref, dg_ref, dst):
        @pl.when(pl.program_id(0) == 0)
        def _():
            dst[...] = jnp.zeros_like(dst)

        states = tuple(sp_ref[0, h] for h in range(DN_HEADS))
        _, vjp = jax.vjp(_delta_chunk, states, q_ref[...], k_ref[...], v_ref[...], b_ref[...], g_ref[...])
        d = vjp((do_ref[...], tuple(dst[h] for h in range(DN_HEADS))))
        for h in range(DN_HEADS):
            dst[h] = d[0][h]
        for o_ref, val in zip((dq_ref, dk_ref, dv_ref, db_ref, dg_ref), d[1:]):
            o_ref[...] = val

    blk = pl.BlockSpec((c, BRANCH_WIDTH), lambda n: (nc - 1 - n, 0))
    res = pl.pallas_call(
        body, name=name, grid=(nc,),
        in_specs=[blk] * 5 + [pl.BlockSpec((1, DN_HEADS, DN_HEAD_DIM, DN_HEAD_DIM), lambda n: (nc - 1 - n, 0, 0, 0)), blk],
        out_specs=[blk] * 5, out_shape=[jax.ShapeDtypeStruct((s, BRANCH_WIDTH), F32)] * 5,
        scratch_shapes=[pltpu.VMEM((DN_HEADS, DN_HEAD_DIM, DN_HEAD_DIM), F32)],
        compiler_params=_params(("arbitrary",)),
    )(q, k, v, beta, g, states, do)
    return list(res)


def _split_bf16(x):
    hi = x.astype(BF16)
    return hi, (x - hi.astype(F32)).astype(BF16)


def _sb_consts():
    tq, tk = SB_QUERY_ROWS, SB_BLOCK
    row, col = _iota2((tk, tk), 0), _iota2((tk, tk), 1)
    ones = jnp.ones((tk, tk), BF16)
    later = jnp.concatenate([(row > col).astype(BF16), ones], axis=1)
    from_here = jnp.concatenate([(row >= col).astype(BF16), ones], axis=1)
    first = _iota2((tq, 128), 1) < SB_HEAD_DIM
    return later, from_here, first


def _sb_causal(d):
    tq, tk = SB_QUERY_ROWS, SB_BLOCK
    return _iota2((tq - d * tk, tk), 1) < _iota2((tq - d * tk, tk), 0)


def _sums(x, mat):
    hi, lo = _split_bf16(x)
    return jnp.dot(hi, mat, preferred_element_type=F32) + jnp.dot(lo, mat, preferred_element_type=F32)


def _sb_weights(qs, kb, accs, later, causal):
    tk = SB_BLOCK
    z = [lax.dot_general(qh, kb, (((1,), (1,)), ((), ())), preferred_element_type=F32) for qh in qs]
    lk = [-_softplus(zh) for zh in z]
    if causal is not None:
        lk = [jnp.where(causal, v, 0.0) for v in lk]
    cs = [_sums(v, later) for v in lk]
    e = [z[h] + lk[h] + cs[h][:, :tk] + accs[h] for h in range(2)]
    if causal is not None:
        e = [jnp.where(causal, v, -1e30) for v in e]
    return lk, [jnp.exp(v) for v in e], [v[:, tk:] for v in cs]


def _sb_alive(accs):
    return jnp.max(jnp.maximum(accs[0], accs[1])) > SB_DEAD_LOG


def _sb_sweep(i, block, carry, accs_of, stop=None):
    per = SB_QUERY_ROWS // SB_BLOCK
    for d in reversed(range(per)):
        r0 = d * SB_BLOCK
        seen = block(i * per + d, jax.tree.map(lambda a: a[r0:], carry), _sb_causal(d), r0)
        carry = seen if r0 == 0 else jax.tree.map(lambda old, new: jnp.concatenate([old[:r0], new], axis=0), carry, seen)
    if stop is not None:
        return lax.fori_loop(0, i * per - 1 - stop, lambda jj, cr: block(i * per - 1 - jj, cr, None), carry)

    def step(state):
        j, _, cr = state
        cr = block(j, cr, None)
        return j - 1, _sb_alive(accs_of(cr)), cr

    j, _, carry = lax.while_loop(lambda st: jnp.logical_and(st[0] >= 0, st[1]), step,
                                 (i * per - 1, _sb_alive(accs_of(carry)), carry))
    return carry, j


def _sb_fwd(p, *, name):
    s = p.shape[0]
    t, tk = SB_QUERY_ROWS, SB_BLOCK
    scale = SB_HEAD_DIM ** -0.5

    def body(q_ref, k_ref, v_ref, o_ref):
        i = pl.program_id(1)
        later, _, first = _sb_consts()
        q = q_ref[...] * scale
        qs = (jnp.where(first, q, 0.0).astype(BF16), jnp.where(first, 0.0, q).astype(BF16))

        def block(j, carry, causal, r0=0):
            start = pl.multiple_of(j * tk, tk)
            kb = k_ref[pl.ds(start, tk), :].astype(BF16)
            vb = v_ref[pl.ds(start, tk), :].astype(BF16)
            _, w, tot = _sb_weights([qh[r0:] for qh in qs], kb, [carry[h][1] for h in range(2)], later, causal)
            out = [jnp.dot(w[h].astype(BF16), vb, preferred_element_type=F32) for h in range(2)]
            return tuple((carry[h][0] + out[h], carry[h][1] + tot[h]) for h in range(2))

        zero = jnp.zeros((t, 128), F32)
        carry, _ = _sb_sweep(i, block, ((zero, zero), (zero, zero)), lambda cr: (cr[0][1], cr[1][1]))
        o_ref[...] = jnp.where(first, carry[0][0], carry[1][0])

    return pl.pallas_call(
        body, name=name, grid=(BRANCH_WIDTH // 128, s // t),
        in_specs=[pl.BlockSpec((t, 128), lambda pr, i: (i, OFF_CQ // 128 + pr)),
                  pl.BlockSpec((s, 128), lambda pr, i: (0, OFF_CK // 128 + pr)),
                  pl.BlockSpec((s, 128), lambda pr, i: (0, OFF_CV // 128 + pr))],
        out_specs=pl.BlockSpec((t, 128), lambda pr, i: (i, pr)),
        out_shape=jax.ShapeDtypeStruct((s, BRANCH_WIDTH), F32),
        compiler_params=_params(("arbitrary", "arbitrary")),
    )(p, p, p)


def _sb_bwd(p, do, *, name):
    s = p.shape[0]
    t, tk = SB_QUERY_ROWS, SB_BLOCK
    scale = SB_HEAD_DIM ** -0.5

    def body(q_ref, k_ref, v_ref, do_ref, dq_ref, dk_ref, dv_ref):
        i = pl.program_id(1)

        @pl.when(i == 0)
        def _():
            dk_ref[...] = jnp.zeros_like(dk_ref)
            dv_ref[...] = jnp.zeros_like(dv_ref)

        later, from_here, first = _sb_consts()
        q = q_ref[...] * scale
        do = do_ref[...]
        qs = (jnp.where(first, q, 0.0).astype(BF16), jnp.where(first, 0.0, q).astype(BF16))
        dos = (jnp.where(first, do, 0.0).astype(BF16), jnp.where(first, 0.0, do).astype(BF16))

        def total(j, carry, causal, r0=0):
            start = pl.multiple_of(j * tk, tk)
            kb = k_ref[pl.ds(start, tk), :].astype(BF16)
            vb = v_ref[pl.ds(start, tk), :].astype(BF16)
            _, w, tot = _sb_weights([qh[r0:] for qh in qs], kb, [carry[h][0] for h in range(2)], later, causal)
            dw = [lax.dot_general(dos[h][r0:], vb, (((1,), (1,)), ((), ())), preferred_element_type=F32)
                  for h in range(2)]
            tde = [_sums(dw[h] * w[h], from_here)[:, tk:] for h in range(2)]
            return tuple((carry[h][0] + tot[h], carry[h][1] + tde[h]) for h in range(2))

        zero = jnp.zeros((t, 128), F32)
        sums, stop = _sb_sweep(i, total, ((zero, zero), (zero, zero)), lambda cr: (cr[0][0], cr[1][0]))
        deltas = (sums[0][1], sums[1][1])

        def block(j, carry, causal, r0=0):
            start = pl.multiple_of(j * tk, tk)
            kb = k_ref[pl.ds(start, tk), :].astype(BF16)
            vb = v_ref[pl.ds(start, tk), :].astype(BF16)
            both = range(2)
            tn = (((0,), (0,)), ((), ()))
            qr, dor, total_de = [x[r0:] for x in qs], [x[r0:] for x in dos], [x[r0:] for x in deltas]
            lk, w, tot = _sb_weights(qr, kb, [carry[h][1] for h in both], later, causal)
            dw = [lax.dot_general(dor[h], vb, (((1,), (1,)), ((), ())), preferred_element_type=F32) for h in both]
            de = [dw[h] * w[h] for h in both]
            cs = [_sums(de[h], from_here) for h in both]
            keep = [jnp.exp(lk[h]) for h in both]
            dz = [de[h] * keep[h] - (total_de[h] - (cs[h][:, :tk] + carry[h][2])) * (1.0 - keep[h]) for h in both]
            if causal is not None:
                dz = [jnp.where(causal, v, 0.0) for v in dz]
            dzb = [v.astype(BF16) for v in dz]
            dq = [jnp.dot(dzb[h], kb, preferred_element_type=F32) for h in both]
            dk = [lax.dot_general(dzb[h], qr[h], tn, preferred_element_type=F32) for h in both]
            dv = [lax.dot_general(w[h].astype(BF16), dor[h], tn, preferred_element_type=F32) for h in both]
            dk_ref[pl.ds(start, tk), :] += dk[0] + dk[1]
            dv_ref[pl.ds(start, tk), :] += dv[0] + dv[1]
            return tuple((carry[h][0] + dq[h], carry[h][1] + tot[h], carry[h][2] + cs[h][:, tk:]) for h in both)

        carry = _sb_sweep(i, block, ((zero, zero, zero), (zero, zero, zero)), None, stop=stop)
        dq_ref[...] = jnp.where(first, carry[0][0], carry[1][0]) * scale

    qblk = lambda off: pl.BlockSpec((t, 128), lambda pr, i: (i, off // 128 + pr))
    full = lambda off: pl.BlockSpec((s, 128), lambda pr, i: (0, off // 128 + pr))
    res = pl.pallas_call(
        body, name=name, grid=(BRANCH_WIDTH // 128, s // t),
        in_specs=[qblk(OFF_CQ), full(OFF_CK), full(OFF_CV), qblk(0)],
        out_specs=[qblk(0), full(0), full(0)],
        out_shape=[jax.ShapeDtypeStruct((s, BRANCH_WIDTH), F32)] * 3,
        compiler_params=_params(("arbitrary", "arbitrary")),
    )(p, p, p, do)
    return list(res)


def _loss_head(y, target, *, name, ts=512):
    s, d = y.shape
    ts = min(ts, s)

    def body(y_ref, t_ref, sq_ref, dy_ref):
        @pl.when(pl.program_id(0) == 0)
        def _():
            sq_ref[...] = jnp.zeros_like(sq_ref)

        err = y_ref[...] - t_ref[...]
        dy_ref[...] = err * (1.0 / d)
        tot = jnp.sum(jnp.sum(err * err, axis=1, keepdims=True), axis=0, keepdims=True)
        sq_ref[...] += jnp.broadcast_to(tot, sq_ref.shape)

    blk = pl.BlockSpec((ts, d), lambda i: (i, 0))
    return pl.pallas_call(
        body, name=name, grid=(s // ts,), in_specs=[blk, blk],
        out_specs=[pl.BlockSpec((1, 128), lambda i: (0, 0)), blk],
        out_shape=[jax.ShapeDtypeStruct((1, 128), F32), jax.ShapeDtypeStruct((s, d), F32)],
        compiler_params=_params(("arbitrary",)),
    )(y, target)


def _row_tile(r, limit=512):
    return max(t for t in range(16, limit + 1, 16) if r % t == 0)


def _adamw(w, g, m, v, *, name):
    shape = w.shape
    lanes = shape[-1]
    w, g, m, v = (a.reshape(-1, lanes) for a in (w, g, m, v))
    r = w.shape[0]
    ts = r if r <= 256 else _row_tile(r, 256 if lanes > LANES else 512)

    def body(w_ref, g_ref, m_ref, v_ref, d_ref, nm_ref, nv_ref):
        gv = g_ref[...]
        m_new = ADAM_B1 * m_ref[...] + (1.0 - ADAM_B1) * gv
        v_new = ADAM_B2 * v_ref[...] + (1.0 - ADAM_B2) * jnp.square(gv)
        m_hat = m_new / (1.0 - ADAM_B1 ** ADAM_STEP)
        v_hat = v_new / (1.0 - ADAM_B2 ** ADAM_STEP)
        d_ref[...] = -ADAM_LR * (m_hat / (jnp.sqrt(v_hat) + ADAM_EPS) + ADAM_WD * w_ref[...])
        nm_ref[...] = m_new
        nv_ref[...] = v_new

    blk = pl.BlockSpec((ts, lanes), lambda i: (i, 0))
    res = pl.pallas_call(
        body, name=name, grid=(r // ts,), in_specs=[blk] * 4, out_specs=[blk] * 3,
        out_shape=[jax.ShapeDtypeStruct((r, lanes), F32)] * 3,
        compiler_params=_params(("parallel",)),
    )(w, g, m, v)
    return [a.reshape(shape) for a in res]


def _add_rows(terms, *, name, out_dtype=F32):
    r = terms[0].shape[0]
    ts = _row_tile(r)

    def body(*refs):
        acc = refs[0][...].astype(F32)
        for ref in refs[1:-1]:
            acc = acc + ref[...].astype(F32)
        refs[-1][...] = acc.astype(out_dtype)

    blk = pl.BlockSpec((ts, LANES), lambda i: (i, 0))
    return pl.pallas_call(
        body, name=name, grid=(r // ts,), in_specs=[blk] * len(terms), out_specs=blk,
        out_shape=jax.ShapeDtypeStruct((r, LANES), out_dtype), compiler_params=_params(("parallel",)),
    )(*terms)


_HBM = pl.BlockSpec(memory_space=pltpu.HBM)
_MESH = pl.DeviceIdType.MESH


def _other_chips(x, y):
    return [(1 - x, y), (x, 1 - y), (1 - x, 1 - y)]


def _gather_chips(shard, *, name):
    r, lanes = shard.shape
    half = r // 2
    assert half * 2 == r

    def body(in_ref, out_ref, send_sems, recv_sems):
        x, y, c = lax.axis_index("x"), lax.axis_index("y"), lax.axis_index("c")
        me = 2 * x + y
        sibling = (x, y, 1 - c)
        chips = _other_chips(x, y)

        def copy(sem, chip, core_half, to):
            rows = out_ref.at[chip, pl.ds(core_half * half, half)]
            return pltpu.make_async_remote_copy(src_ref=rows, dst_ref=rows, send_sem=send_sems.at[sem],
                                                recv_sem=recv_sems.at[sem], device_id=to, device_id_type=_MESH)

        first = []
        for kk, (px, py) in enumerate(chips):
            cp = pltpu.make_async_remote_copy(
                src_ref=in_ref.at[pl.ds(c * half, half)], dst_ref=out_ref.at[me, pl.ds(c * half, half)],
                send_sem=send_sems.at[kk], recv_sem=recv_sems.at[kk], device_id=(px, py, c), device_id_type=_MESH)
            cp.start()
            first.append(cp)
        passed = [copy(3 + kk, 2 * px + py, c, sibling) for kk, (px, py) in enumerate(chips)]
        for kk, (px, py) in enumerate(chips):
            copy(kk, 2 * px + py, c, (px, py, c)).wait_recv()
            passed[kk].start()
        for kk, (px, py) in enumerate(chips):
            copy(3 + kk, 2 * px + py, 1 - c, sibling).wait_recv()
        for cp in first + passed:
            cp.wait_send()

    gathered = pl.pallas_call(
        body, name=name, in_specs=[_HBM], out_specs=_HBM,
        out_shape=jax.ShapeDtypeStruct((4, r, lanes), shard.dtype),
        scratch_shapes=[pltpu.SemaphoreType.DMA((6,)), pltpu.SemaphoreType.DMA((6,))],
    )(shard)
    me = 2 * lax.axis_index("x") + lax.axis_index("y")
    return jnp.where(lax.broadcasted_iota(jnp.int32, (4, 1, 1), 0) == me, shard[None], gathered)


def _to_sibling(block, *, name):
    def body(in_ref, out_ref, send_sem, recv_sem):
        x, y, c = lax.axis_index("x"), lax.axis_index("y"), lax.axis_index("c")
        cp = pltpu.make_async_remote_copy(src_ref=in_ref, dst_ref=out_ref, send_sem=send_sem, recv_sem=recv_sem,
                                          device_id=(x, y, 1 - c), device_id_type=_MESH)
        cp.start()
        cp.wait()

    return pl.pallas_call(
        body, name=name, in_specs=[_HBM], out_specs=_HBM, out_shape=jax.ShapeDtypeStruct(block.shape, block.dtype),
        scratch_shapes=[pltpu.SemaphoreType.DMA, pltpu.SemaphoreType.DMA],
    )(block)


def _halves_to_sibling(blocks, *, name):
    n, r, lanes = blocks.shape
    half = r // 2

    def body(in_ref, out_ref, send_sem, recv_sem):
        x, y, c = lax.axis_index("x"), lax.axis_index("y"), lax.axis_index("c")
        cp = pltpu.make_async_remote_copy(src_ref=in_ref.at[pl.ds(0, n), pl.ds((1 - c) * half, half)], dst_ref=out_ref,
                                          send_sem=send_sem, recv_sem=recv_sem, device_id=(x, y, 1 - c),
                                          device_id_type=_MESH)
        cp.start()
        cp.wait()

    return pl.pallas_call(
        body, name=name, in_specs=[_HBM], out_specs=_HBM, out_shape=jax.ShapeDtypeStruct((n, half, lanes), blocks.dtype),
        scratch_shapes=[pltpu.SemaphoreType.DMA, pltpu.SemaphoreType.DMA],
    )(blocks)


def _add_own_half(blocks, got, *, name, out_dtype):
    n, r, lanes = blocks.shape
    half = r // 2
    ts = _row_tile(half)
    steps = half // ts
    core = lax.axis_index("c").astype(jnp.int32).reshape(1)

    def body(core_ref, a_ref, b_ref, o_ref):
        o_ref[...] = (a_ref[...].astype(F32) + b_ref[...].astype(F32)).astype(out_dtype)

    return pl.pallas_call(
        body, name=name,
        grid_spec=pltpu.PrefetchScalarGridSpec(
            num_scalar_prefetch=1, grid=(n, steps),
            in_specs=[pl.BlockSpec((1, ts, lanes), lambda k, i, core_ref: (k, core_ref[0] * steps + i, 0)),
                      pl.BlockSpec((1, ts, lanes), lambda k, i, core_ref: (k, i, 0))],
            out_specs=pl.BlockSpec((1, ts, lanes), lambda k, i, core_ref: (k, i, 0))),
        out_shape=jax.ShapeDtypeStruct((n, half, lanes), out_dtype),
        compiler_params=_params(("parallel", "parallel")),
    )(core, blocks, got)


def _scatter_chips(blocks, *, name):
    _, r, lanes = blocks.shape

    def body(in_ref, out_ref, send_sems, recv_sems, local_sem):
        x, y, c = lax.axis_index("x"), lax.axis_index("y"), lax.axis_index("c")
        me = 2 * x + y
        mine = pltpu.make_async_copy(in_ref.at[me], out_ref.at[me], local_sem)
        mine.start()
        copies = []
        for kk, (px, py) in enumerate(_other_chips(x, y)):
            cp = pltpu.make_async_remote_copy(src_ref=in_ref.at[2 * px + py], dst_ref=out_ref.at[me],
                                              send_sem=send_sems.at[kk], recv_sem=recv_sems.at[kk],
                                              device_id=(px, py, c), device_id_type=_MESH)
            cp.start()
            copies.append(cp)
        for kk, (px, py) in enumerate(_other_chips(x, y)):
            pltpu.make_async_remote_copy(src_ref=in_ref.at[me], dst_ref=out_ref.at[2 * px + py],
                                         send_sem=send_sems.at[kk], recv_sem=recv_sems.at[kk], device_id=(px, py, c),
                                         device_id_type=_MESH).wait_recv()
        for cp in copies:
            cp.wait_send()
        mine.wait()

    return pl.pallas_call(
        body, name=name, in_specs=[_HBM], out_specs=_HBM, out_shape=jax.ShapeDtypeStruct((4, r, lanes), blocks.dtype),
        scratch_shapes=[pltpu.SemaphoreType.DMA((3,)), pltpu.SemaphoreType.DMA((3,)), pltpu.SemaphoreType.DMA],
    )(blocks)


SHARDED = (("norm_g", 2), ("w_in", 2), ("conv_w", 2), ("w_branch", 3), ("w_out", 1), ("w_ff1", 2), ("w_ff2", 1))
MATMUL_WEIGHTS = ("w_in", "w_branch", "w_out", "w_ff1", "w_ff2")
VECTOR_WEIGHTS = ("norm_g", "conv_w")
REPLICATED = ("b_in", "sgu_ln_g", "sgu_ln_b", "w_spatial", "b_spatial", "a_log", "dt_bias", "dn_norm_g")
WEIGHT_ORDER = ("norm_g", "w_in", "b_in", "sgu_ln_g", "sgu_ln_b", "w_spatial", "b_spatial", "conv_w", "a_log",
                "dt_bias", "dn_norm_g", "w_branch", "w_out", "w_ff1", "w_ff2")
PACK_ROW_MULTIPLE = 32


def _size(shape):
    n = 1
    for dim in shape:
        n *= dim
    return n


def _wide(shape):
    return len(shape) >= 2 and shape[-1] > LANES


def _rows_of(shape):
    if _wide(shape):
        return (_size(shape) // shape[-1]) * -(-shape[-1] // LANES)
    return -(-_size(shape) // LANES)


def _pack(arrays):
    parts = []
    for a in arrays:
        if _wide(a.shape):
            rows = a.reshape(-1, a.shape[-1])
            pad = -a.shape[-1] % LANES
            if pad:
                rows = jnp.concatenate([rows, jnp.zeros((rows.shape[0], pad), a.dtype)], axis=1)
            parts += [rows[:, c:c + LANES] for c in range(0, rows.shape[1], LANES)]
            continue
        flat = a.reshape(-1)
        pad = _rows_of(a.shape) * LANES - flat.shape[0]
        if pad:
            flat = jnp.concatenate([flat, jnp.zeros((pad,), flat.dtype)])
        parts.append(flat.reshape(-1, LANES))
    rows = sum(p.shape[0] for p in parts)
    pad = -rows % PACK_ROW_MULTIPLE
    if pad:
        parts.append(jnp.zeros((pad, LANES), parts[0].dtype))
    return jnp.concatenate(parts, axis=0)


def _unpack(buf, shapes):
    out, row = [], 0
    for shape in shapes:
        rows = _rows_of(shape)
        part = buf[row:row + rows]
        if _wide(shape):
            chunks = -(-shape[-1] // LANES)
            each = rows // chunks
            whole = jnp.concatenate([part[c * each:(c + 1) * each] for c in range(chunks)], axis=1)
            out.append(whole[:, :shape[-1]].reshape(shape))
        else:
            out.append(part.reshape(-1)[:_size(shape)].reshape(shape))
        row += rows
    return out


def _chip_slice(a, axis, k):
    size = a.shape[axis] // 4
    return lax.slice_in_dim(a, k * size, (k + 1) * size, axis=axis)


def _rearrange_w_in(w):
    pad = jnp.zeros(w.shape[:-1] + (P_PAD - P_IN,), w.dtype)
    return jnp.concatenate([w[..., 1024:2560], w[..., 3072:3080], pad, w[..., 0:1024], w[..., 4616:7688],
                            w[..., 2560:3072], w[..., 3080:4616]], axis=-1)


def _restore_w_in(w):
    return jnp.concatenate([w[..., 2048:3072], w[..., 0:1536], w[..., 6144:6656], w[..., 1536:1544],
                            w[..., 6656:8192], w[..., 3072:6144]], axis=-1)


def _layer_params(wl):
    row = lambda v: v.reshape(1, -1)
    pad128 = lambda v, at: jnp.pad(v, (at, 128 - at - v.shape[0])).reshape(1, 128)
    return dict(
        g=[row(wl["norm_g"][i]) for i in range(4)],
        gmlp=[row(wl["sgu_ln_g"]), row(wl["sgu_ln_b"]), wl["w_spatial"],
              jnp.pad(wl["b_spatial"].T, ((0, 0), (0, 128 - GM_GROUPS)))],
        dn_in=[wl["conv_w"], pad128(wl["a_log"], DN_HEADS), pad128(wl["dt_bias"], DN_HEADS)],
        dn_g=[row(wl["dn_norm_g"])],
    )


def _layer_fwd(x0, wl, l):
    tag = lambda s: f"{s}_l{l}"
    pr = _layer_params(wl)
    w = BRANCH_WIDTH
    h0 = _row_fwd(_fn_rms, [(x0, 0, D_MODEL)], [pr["g"][0]], [(D_MODEL, BF16)], ts=512, name=tag("rms0"))[0]
    p = _mm(h0, wl["w_in"], bias=wl["b_in"].reshape(1, -1), name=tag("proj_in"))
    ya = _row_fwd(_fn_gmlp, [(p, OFF_GM, 2 * w)], pr["gmlp"], [(w, BF16)], ts=256, name=tag("gmlp"))[0]
    q, k, v, beta, g = _dn_in_fwd(p, pr["dn_in"], ts=256, name=tag("dn_in"))
    o, states = _delta_fwd(q, k, v, beta, g, name=tag("delta"))
    yb = _row_fwd(_fn_dn_out, [(o, 0, w), (p, OFF_BZ, w)], pr["dn_g"], [(w, BF16)], ts=512, name=tag("dn_out"))[0]
    yc = _sb_fwd(p, name=tag("sb"))
    ys = [ya, yb, yc]
    proj = [_mm(ys[i], wl["w_branch"][i], out_dtype=BF16, name=tag(f"branch{i}")) for i in range(3)]
    merge_rows = [(a, 0, D_MODEL) for a in proj] + [(p, OFF_GATE, 3 * D_MODEL)]
    m = _row_fwd(_fn_merge, merge_rows, [], [(D_MODEL, BF16)], ts=256, name=tag("merge"))[0]
    mixed = _mm(m, wl["w_out"], name=tag("out"))
    x1 = _row_fwd(_fn_resid_rms, [(x0, 0, D_MODEL), (mixed, 0, D_MODEL)], [pr["g"][1]], [(D_MODEL, F32)], ts=512,
                  name=tag("resid1"))[0]
    h2 = _row_fwd(_fn_rms, [(x1, 0, D_MODEL)], [pr["g"][2]], [(D_MODEL, BF16)], ts=512, name=tag("rms2"))[0]
    a, r = _mm(h2, wl["w_ff1"], name=tag("ff1"), out_dtype=(BF16, BF16),
               finish=lambda t: (jnp.maximum(t, 0.0), jnp.square(jnp.maximum(t, 0.0))))
    f = _mm(r, wl["w_ff2"], name=tag("ff2"))
    x2 = _row_fwd(_fn_resid_rms, [(x1, 0, D_MODEL), (f, 0, D_MODEL)], [pr["g"][3]], [(D_MODEL, F32)], ts=512,
                  name=tag("resid2"))[0]
    saved = dict(x0=x0, h0=h0, p=p, q=q, k=k, v=v, beta=beta, g=g, states=states, o=o, ys=ys,
                 proj=proj, m=m, mixed=mixed, x1=x1, h2=h2, a=a, r=r, f=f)
    return x2, saved


def _layer_bwd(dx2, sv, wl, l):
    tag = lambda s: f"{s}_l{l}"
    pr = _layer_params(wl)
    w = BRANCH_WIDTH
    full = lambda a: (a, 0, a.shape[1])
    p = sv["p"]
    (df,), (dg3,) = _row_bwd(_fn_rms_branch, [full(sv["f"])], [pr["g"][3]], [full(dx2)], ts=512, name=tag("resid2_b"),
                             matmul_only=(0,))
    da = _mm(df, wl["w_ff2"], trans_b=True, name=tag("ff2_dx"), out_dtype=BF16, beside=(sv["a"],),
             finish=lambda t, relu_a: 2.0 * relu_a.astype(F32) * t)
    dw_ff2 = _mm_tn(sv["r"], df, name=tag("ff2_dw"))
    dh2 = _mm(da, wl["w_ff1"], trans_b=True, name=tag("ff1_dx"))
    dw_ff1 = _mm_tn(sv["h2"], da, name=tag("ff1_dw"))
    (dx1,), (dg2,) = _row_bwd(_fn_rms_keep, [full(sv["x1"])], [pr["g"][2]], [full(dh2), full(dx2)], ts=512,
                              name=tag("rms2_b"))
    (dmixed,), (dg1,) = _row_bwd(_fn_rms_branch, [full(sv["mixed"])], [pr["g"][1]], [full(dx1)], ts=512,
                                 name=tag("resid1_b"), matmul_only=(0,))
    dm = _mm(dmixed, wl["w_out"], trans_b=True, name=tag("out_dx"))
    dw_out = _mm_tn(sv["m"], dmixed, name=tag("out_dw"))
    dp = lax.empty(p.shape, F32)
    merge_rows = [full(a) for a in sv["proj"]] + [(p, OFF_GATE, 3 * D_MODEL)]
    dmerge, _ = _row_bwd(_fn_merge, merge_rows, [], [full(dm)], ts=256, name=tag("merge_b"), into=(dp, 3),
                         matmul_only=(0, 1, 2))
    dproj, dp = dmerge[:3], dmerge[3]
    dys = [_mm(dproj[i], wl["w_branch"][i], trans_b=True, name=tag(f"branch{i}_dx")) for i in range(3)]
    dw_branch = jnp.stack([_mm_tn(sv["ys"][i], dproj[i], name=tag(f"branch{i}_dw")) for i in range(3)])
    (dp,), dgm = _row_bwd(_fn_gmlp, [(p, OFF_GM, 2 * w)], pr["gmlp"], [full(dys[0])], ts=256, name=tag("gmlp_b"),
                          into=(dp, 0))
    (do, dp), (d_dn_g,) = _row_bwd(_fn_dn_out, [full(sv["o"]), (p, OFF_BZ, w)], pr["dn_g"], [full(dys[1])], ts=512,
                                   name=tag("dn_out_b"), into=(dp, 1))
    dqkvbg = _delta_bwd(sv["q"], sv["k"], sv["v"], sv["beta"], sv["g"], sv["states"], do, name=tag("delta_b"))
    dp, d_dn_in = _dn_in_bwd(p, pr["dn_in"], dqkvbg, dp, ts=256, name=tag("dn_in_b"))
    for off, part in zip((OFF_CQ, OFF_CK, OFF_CV), _sb_bwd(p, dys[2], name=tag("sb_b"))):
        dp = lax.dynamic_update_slice(dp, part, (0, off))
    dh0 = _mm(dp, wl["w_in"], trans_b=True, name=tag("proj_in_dx"))
    dw_in, db_in = _mm_tn(sv["h0"], dp, name=tag("proj_in_dw"), col_sums=True)
    (dx0,), (dg0,) = _row_bwd(_fn_rms_keep, [full(sv["x0"])], [pr["g"][0]], [full(dh0), full(dx1)], ts=512,
                              name=tag("rms0_b"))
    grads = dict(
        norm_g=jnp.concatenate([dg0, dg1, dg2, dg3], axis=0), w_in=dw_in, b_in=db_in.reshape(-1),
        sgu_ln_g=dgm[0].reshape(-1), sgu_ln_b=dgm[1].reshape(-1), w_spatial=dgm[2],
        b_spatial=dgm[3][:, :GM_GROUPS].T, conv_w=d_dn_in[0], a_log=d_dn_in[1][0, DN_HEADS:2 * DN_HEADS],
        dt_bias=d_dn_in[2][0, DN_HEADS:2 * DN_HEADS], dn_norm_g=d_dn_g.reshape(-1), w_branch=dw_branch,
        w_out=dw_out, w_ff1=dw_ff1, w_ff2=dw_ff2)
    return dx0, grads


def _local_step(x, target, weights):
    saved = []
    h = x
    layers = []
    for l in range(DEPTH):
        wl = {n: weights[n][l] for n in WEIGHT_ORDER}
        layers.append(wl)
        h, sv = _layer_fwd(h, wl, l)
        saved.append(sv)
    sq, dh = _loss_head(h, target, name="loss_head")
    grads = [None] * DEPTH
    for l in reversed(range(DEPTH)):
        dh, grads[l] = _layer_bwd(dh, saved[l], layers[l], l)
    stacked = {n: jnp.stack([grads[l][n].astype(GRAD_DTYPE) for l in range(DEPTH)]) for n in WEIGHT_ORDER}
    return sq[0, 0], dh, stacked


def kernel(x, norm_g, w_in, b_in, sgu_ln_g, sgu_ln_b, w_spatial, b_spatial, conv_w, a_log, dt_bias, dn_norm_g, w_branch, w_out, w_ff1, w_ff2, loss_target, m_norm_g, m_w_in, m_b_in, m_sgu_ln_g, m_sgu_ln_b, m_w_spatial, m_b_spatial, m_conv_w, m_a_log, m_dt_bias, m_dn_norm_g, m_w_branch, m_w_out, m_w_ff1, m_w_ff2, v_norm_g, v_w_in, v_b_in, v_sgu_ln_g, v_sgu_ln_b, v_w_spatial, v_b_spatial, v_conv_w, v_a_log, v_dt_bias, v_dn_norm_g, v_w_branch, v_w_out, v_w_ff1, v_w_ff2):
    local = dict(norm_g=norm_g, w_in=w_in, b_in=b_in, sgu_ln_g=sgu_ln_g, sgu_ln_b=sgu_ln_b, w_spatial=w_spatial,
                 b_spatial=b_spatial, conv_w=conv_w, a_log=a_log, dt_bias=dt_bias, dn_norm_g=dn_norm_g,
                 w_branch=w_branch, w_out=w_out, w_ff1=w_ff1, w_ff2=w_ff2)
    mom1 = dict(norm_g=m_norm_g, w_in=m_w_in, b_in=m_b_in, sgu_ln_g=m_sgu_ln_g, sgu_ln_b=m_sgu_ln_b,
                w_spatial=m_w_spatial, b_spatial=m_b_spatial, conv_w=m_conv_w, a_log=m_a_log, dt_bias=m_dt_bias,
                dn_norm_g=m_dn_norm_g, w_branch=m_w_branch, w_out=m_w_out, w_ff1=m_w_ff1, w_ff2=m_w_ff2)
    mom2 = dict(norm_g=v_norm_g, w_in=v_w_in, b_in=v_b_in, sgu_ln_g=v_sgu_ln_g, sgu_ln_b=v_sgu_ln_b,
                w_spatial=v_w_spatial, b_spatial=v_b_spatial, conv_w=v_conv_w, a_log=v_a_log, dt_bias=v_dt_bias,
                dn_norm_g=v_dn_norm_g, w_branch=v_w_branch, w_out=v_w_out, w_ff1=v_w_ff1, w_ff2=v_w_ff2)
    shard_names = [n for n, _ in SHARDED]
    shard_shapes = [local[n].shape for n in shard_names]
    repl_shapes = [local[n].shape for n in REPLICATED]

    weights = {n: local[n] for n in REPLICATED}
    for names, dtype, call in ((MATMUL_WEIGHTS, BF16, "gather_matmul_weights"), (VECTOR_WEIGHTS, F32, "gather_vectors")):
        gathered = _gather_chips(_pack([local[n] for n in names]).astype(dtype), name=call)
        per_chip = [_unpack(gathered[k], [local[n].shape for n in names]) for k in range(4)]
        for i, n in enumerate(names):
            weights[n] = jnp.concatenate([per_chip[k][i] for k in range(4)], axis=dict(SHARDED)[n])
    weights["w_in"] = _rearrange_w_in(weights["w_in"])
    weights["b_in"] = _rearrange_w_in(weights["b_in"])

    sq, dx, grads = _local_step(x[0], loss_target[0], weights)
    loss = lax.psum(0.5 * sq / D_MODEL, ("x", "y", "c"))
    grads["w_in"] = _restore_w_in(grads["w_in"])
    grads["b_in"] = _restore_w_in(grads["b_in"])

    blocks = jnp.stack([_pack([_chip_slice(grads[n], axis, k) for n, axis in SHARDED] + [grads[n] for n in REPLICATED])
                        for k in range(4)])
    rows = blocks.shape[1]
    half = rows // 2
    c = lax.axis_index("c")
    got = _halves_to_sibling(blocks, name="grads_to_sibling")
    chip_sum = _add_own_half(blocks, got, name="grads_chip_sum", out_dtype=BF16)
    by_chip = _scatter_chips(chip_sum, name="grads_scatter")
    my_half = _add_rows([by_chip[k] for k in range(4)], name="grads_sum")
    other_half = _to_sibling(my_half, name="grads_half_swap")
    total = jnp.concatenate([jnp.where(c == 0, my_half, other_half), jnp.where(c == 0, other_half, my_half)], axis=0)
    g_out = dict(zip(shard_names + list(REPLICATED), _unpack(total, shard_shapes + repl_shapes)))

    d_out, m_out, v_out = {}, {}, {}
    for n in WEIGHT_ORDER:
        d_out[n], m_out[n], v_out[n] = _adamw(local[n], g_out[n], mom1[n], mom2[n], name=f"adamw_{n}")
    return (loss, dx[None], *[g_out[n] for n in WEIGHT_ORDER], *[d_out[n] for n in WEIGHT_ORDER],
            *[m_out[n] for n in WEIGHT_ORDER], *[v_out[n] for n in WEIGHT_ORDER])
```

```python
import functools

import jax
import jax.numpy as jnp
from jax import lax
from jax.experimental import pallas as pl
from jax.experimental.pallas import tpu as pltpu

F32 = jnp.float32
BF16 = jnp.bfloat16
GRAD_DTYPE = BF16

D_MODEL = 1024
DEPTH = 4
BRANCH_WIDTH = 512
GM_CHUNK = 128
GM_GROUPS = 8
DN_HEADS = 4
DN_HEAD_DIM = 128
CONV_WIDTH = 4
DN_HALO = 8
DN_BLOCK = 128
DN_SOLVE_PASSES, DN_STATE_PASSES, DN_OUT_PASSES = 1, 1, 1
SB_HEAD_DIM = 64
SB_BLOCK = 128
SB_QUERY_ROWS = 256
SB_DEAD_LOG = -88.0
D_FF = 4096
P_IN = 7688
P_PAD = 8192
NORM_EPS = 1e-6
ADAM_LR, ADAM_B1, ADAM_B2, ADAM_EPS, ADAM_WD, ADAM_STEP = 0.001, 0.9, 0.999, 1e-08, 0.01, 10

OFF_DN, DN_IN_WIDTH = 0, 2048
OFF_BD = 1536
OFF_GM = 2048
OFF_GATE = 3072
OFF_BZ = 6144
OFF_CQ, OFF_CK, OFF_CV = 6656, 7168, 7680

LANES = 1024
VMEM_LIMIT_BYTES = 56 * 1024 * 1024

_HI = lax.Precision.HIGHEST


def _params(sem):
    return pltpu.CompilerParams(dimension_semantics=sem, vmem_limit_bytes=VMEM_LIMIT_BYTES)


def _mm(a, b, *, name, out_dtype=F32, bias=None, trans_b=False, beside=(), finish=None, tm=1024, tn=1024, tk=1024):
    m, k = a.shape
    n = b.shape[0] if trans_b else b.shape[1]
    tm, tn, tk = min(tm, m), min(tn, n), min(tk, k)
    assert m % tm == 0 and n % tn == 0 and k % tk == 0, (a.shape, b.shape)
    nk = k // tk
    dn = (((1,), (1,)), ((), ())) if trans_b else (((1,), (0,)), ((), ()))
    several = isinstance(out_dtype, tuple)
    out_dtypes = out_dtype if several else (out_dtype,)
    n_in = 2 + (bias is not None) + len(beside)

    def body(*refs):
        a_ref, b_ref = refs[:2]
        o_refs, acc = refs[n_in:-1], refs[-1]
        kk = pl.program_id(2)
        part = lax.dot_general(a_ref[...].astype(BF16), b_ref[...].astype(BF16), dn, preferred_element_type=F32)

        @pl.when(kk == 0)
        def _():
            acc[...] = part

        @pl.when(kk > 0)
        def _():
            acc[...] += part

        @pl.when(kk == nk - 1)
        def _():
            r = acc[...]
            if bias is not None:
                r = r + refs[2][...]
            tiles = (r,) if finish is None else finish(r, *[t[...] for t in refs[n_in - len(beside):n_in]])
            for o_ref, tile in zip(o_refs, tiles if isinstance(tiles, tuple) else (tiles,)):
                o_ref[...] = tile.astype(o_ref.dtype)

    in_specs = [pl.BlockSpec((tm, tk), lambda i, j, kk: (i, kk))]
    if trans_b:
        in_specs.append(pl.BlockSpec((tn, tk), lambda i, j, kk: (j, kk)))
    else:
        in_specs.append(pl.BlockSpec((tk, tn), lambda i, j, kk: (kk, j)))
    args = [a, b]
    if bias is not None:
        in_specs.append(pl.BlockSpec((1, tn), lambda i, j, kk: (0, j)))
        args.append(bias)
    for t in beside:
        assert t.shape == (m, n)
        in_specs.append(pl.BlockSpec((tm, tn), lambda i, j, kk: (i, j)))
        args.append(t)
    res = pl.pallas_call(
        body, name=name, grid=(m // tm, n // tn, nk),
        in_specs=in_specs, out_specs=[pl.BlockSpec((tm, tn), lambda i, j, kk: (i, j)) for _ in out_dtypes],
        out_shape=[jax.ShapeDtypeStruct((m, n), dt) for dt in out_dtypes],
        scratch_shapes=[pltpu.VMEM((tm, tn), F32)],
        compiler_params=_params(("parallel", "parallel", "arbitrary")),
    )(*args)
    return list(res) if several else res[0]


def _mm_tn(a, b, *, name, col_sums=False, tm=1024, tn=1024, ts=1024):
    out_dtype = GRAD_DTYPE
    s, ka = a.shape
    n = b.shape[1]
    tm, tn, ts = min(tm, ka), min(tn, n), min(ts, s)
    assert ka % tm == 0 and n % tn == 0 and s % ts == 0, (a.shape, b.shape)
    steps = s // ts

    def body(a_ref, b_ref, o_ref, *rest):
        acc = rest[-1]
        step = pl.program_id(2)
        part = lax.dot_general(a_ref[...].astype(BF16), b_ref[...].astype(BF16), (((0,), (0,)), ((), ())),
                               preferred_element_type=F32)

        @pl.when(step == 0)
        def _():
            acc[...] = part

        @pl.when(step > 0)
        def _():
            acc[...] += part

        @pl.when(step == steps - 1)
        def _():
            o_ref[...] = acc[...].astype(out_dtype)

        if col_sums:
            sums_ref = rest[0]
            first_rows = jnp.logical_and(pl.program_id(1) == 0, step == 0)

            @pl.when(first_rows)
            def _():
                sums_ref[...] = jnp.zeros_like(sums_ref)

            @pl.when(pl.program_id(1) == 0)
            def _():
                sums_ref[...] += jnp.sum(b_ref[...].astype(F32), axis=0, keepdims=True)

    out_specs = [pl.BlockSpec((tm, tn), lambda j, i, r: (i, j))]
    out_shape = [jax.ShapeDtypeStruct((ka, n), out_dtype)]
    if col_sums:
        out_specs.append(pl.BlockSpec((1, tn), lambda j, i, r: (0, j)))
        out_shape.append(jax.ShapeDtypeStruct((1, n), F32))
    res = pl.pallas_call(
        body, name=name, grid=(n // tn, ka // tm, steps),
        in_specs=[pl.BlockSpec((ts, tm), lambda j, i, r: (r, i)), pl.BlockSpec((ts, tn), lambda j, i, r: (r, j))],
        out_specs=out_specs, out_shape=out_shape,
        scratch_shapes=[pltpu.VMEM((tm, tn), F32)],
        compiler_params=_params(("parallel", "arbitrary", "arbitrary")),
    )(a, b)
    return list(res) if col_sums else res[0]


def _col_block(i, *, c):
    return (i, c)


def _whole(i, *, nd):
    return (0,) * nd


def _row_specs(rows, ts):
    specs = []
    for arr, off, w in rows:
        assert off % w == 0 and arr.shape[0] % ts == 0
        specs.append(pl.BlockSpec((ts, w), functools.partial(_col_block, c=off // w)))
    return specs


def _row_fwd(fn, rows, params, outs, *, ts, name):
    s = rows[0][0].shape[0]
    ts = min(ts, s)
    nr, npar = len(rows), len(params)

    def body(*refs):
        rv = [r[...].astype(F32) for r in refs[:nr]]
        pv = [p[...] for p in refs[nr:nr + npar]]
        for o_ref, val in zip(refs[nr + npar:], fn(pv, rv)):
            o_ref[...] = val.astype(o_ref.dtype)

    in_specs = _row_specs(rows, ts) + [pl.BlockSpec(p.shape, functools.partial(_whole, nd=p.ndim)) for p in params]
    res = pl.pallas_call(
        body, name=name, grid=(s // ts,), in_specs=in_specs,
        out_specs=[pl.BlockSpec((ts, w), lambda i: (i, 0)) for w, _ in outs],
        out_shape=[jax.ShapeDtypeStruct((s, w), dt) for w, dt in outs],
        compiler_params=_params(("parallel",)),
    )(*[r[0] for r in rows], *params)
    return list(res)


def _row_bwd(fn, rows, params, cts, *, ts, name, into=None, matmul_only=()):
    s = rows[0][0].shape[0]
    ts = min(ts, s)
    nr, npar, nc = len(rows), len(params), len(cts)
    n_in = nr + npar + nc + (into is not None)

    def body(*refs):
        rv = [r[...].astype(F32) for r in refs[:nr]]
        pv = [p[...] for p in refs[nr:nr + npar]]
        cv = [c[...].astype(F32) for c in refs[nr + npar:nr + npar + nc]]
        out_refs = refs[n_in:]
        _, vjp = jax.vjp(lambda p, r: tuple(fn(p, r)), pv, rv)
        dp, dr = vjp(tuple(cv))
        for o_ref, val in zip(out_refs[:nr], dr):
            o_ref[...] = val.astype(o_ref.dtype)

        @pl.when(pl.program_id(0) == 0)
        def _():
            for o_ref in out_refs[nr:]:
                o_ref[...] = jnp.zeros_like(o_ref)

        for o_ref, val in zip(out_refs[nr:], dp):
            o_ref[...] += val

    in_specs = (_row_specs(rows, ts) + [pl.BlockSpec(p.shape, functools.partial(_whole, nd=p.ndim)) for p in params]
                + _row_specs(cts, ts))
    out_specs = [pl.BlockSpec((ts, w), lambda i: (i, 0)) for _, _, w in rows]
    out_shape = [jax.ShapeDtypeStruct((s, w), BF16 if k in matmul_only else F32) for k, (_, _, w) in enumerate(rows)]
    args = [r[0] for r in rows] + list(params) + [c[0] for c in cts]
    aliases = {}
    if into is not None:
        buffer, at = into
        in_specs.append(pl.BlockSpec(memory_space=pl.ANY))
        args.append(buffer)
        out_specs[at] = _row_specs([(buffer,) + tuple(rows[at][1:])], ts)[0]
        out_shape[at] = jax.ShapeDtypeStruct(buffer.shape, buffer.dtype)
        aliases = {n_in - 1: at}
    out_specs += [pl.BlockSpec(p.shape, functools.partial(_whole, nd=p.ndim)) for p in params]
    out_shape += [jax.ShapeDtypeStruct(p.shape, F32) for p in params]
    res = pl.pallas_call(
        body, name=name, grid=(s // ts,), in_specs=in_specs, out_specs=out_specs, out_shape=out_shape,
        input_output_aliases=aliases, compiler_params=_params(("arbitrary",)),
    )(*args)
    res = list(res)
    return res[:nr], res[nr:]


def _rms(x, g):
    return x * lax.rsqrt(jnp.mean(x * x, axis=-1, keepdims=True) + NORM_EPS) * g


def _gelu(x):
    return 0.5 * x * (1.0 + lax.erf(x * (2.0 ** -0.5)))


def _softplus(x):
    return jnp.maximum(x, 0.0) + jnp.log1p(jnp.exp(-jnp.abs(x)))


def _iota2(shape, dim):
    return lax.broadcasted_iota(jnp.int32, shape, dim)


def _fn_rms(pv, rv):
    return [_rms(rv[0], pv[0])]


def _fn_rms_keep(pv, rv):
    return [_rms(rv[0], pv[0]), rv[0]]


def _fn_resid_rms(pv, rv):
    return [rv[0] + _rms(rv[1], pv[0])]


def _fn_rms_branch(pv, rv):
    return [_rms(rv[0], pv[0])]


def _fn_merge(pv, rv):
    return [sum(jax.nn.sigmoid(rv[3][:, i * D_MODEL:(i + 1) * D_MODEL]) * rv[i] for i in range(3))]


def _fn_gmlp(pv, rv):
    ln_g, ln_b, w_sp, b_t = pv
    u = _gelu(rv[0][:, :BRANCH_WIDTH])
    v = _gelu(rv[0][:, BRANCH_WIDTH:])
    vc = v - jnp.mean(v, axis=-1, keepdims=True)
    v = vc * lax.rsqrt(jnp.mean(vc * vc, axis=-1, keepdims=True) + NORM_EPS) * ln_g + ln_b
    t = GM_CHUNK
    causal = _iota2((t, t), 1) <= _iota2((t, t), 0)
    first = _iota2((t, 128), 1) < 64
    expand = (_iota2((128, BRANCH_WIDTH), 0) == _iota2((128, BRANCH_WIDTH), 1) // 64).astype(F32)
    b_full = jnp.dot(b_t, expand, precision=_HI, preferred_element_type=F32)
    w_bf = [jnp.where(causal, w_sp[g], 0.0).astype(BF16) for g in range(GM_GROUPS)]
    chunks = []
    for c in range(rv[0].shape[0] // t):
        pairs = []
        for p in range(GM_GROUPS // 2):
            vp = v[c * t:(c + 1) * t, 128 * p:128 * (p + 1)].astype(BF16)
            m0 = jnp.dot(w_bf[2 * p], vp, preferred_element_type=F32)
            m1 = jnp.dot(w_bf[2 * p + 1], vp, preferred_element_type=F32)
            pairs.append(jnp.where(first, m0, m1))
        chunks.append(jnp.concatenate(pairs, axis=1) + b_full)
    return [u * jnp.concatenate(chunks, axis=0)]


def _head_expand(col0):
    return (_iota2((128, BRANCH_WIDTH), 0) == _iota2((128, BRANCH_WIDTH), 1) // DN_HEAD_DIM + col0).astype(F32)


@functools.partial(jax.custom_vjp, nondiff_argnums=(1,))
def _roll_rows(x, n):
    return pltpu.roll(x, n, 0)


def _roll_rows_fwd(x, n):
    return pltpu.roll(x, n, 0), None


def _roll_rows_bwd(n, _, ct):
    return (pltpu.roll(ct, ct.shape[0] - n, 0),)


_roll_rows.defvjp(_roll_rows_fwd, _roll_rows_bwd)


def _fn_dn_in(pv, rv):
    conv_w, a_log, dt_b = pv
    cur, before = rv
    x = cur[:, :3 * BRANCH_WIDTH]
    ext = jnp.concatenate([before[:, :3 * BRANCH_WIDTH], x], axis=0)
    c = conv_w[CONV_WIDTH - 1:CONV_WIDTH, :] * x
    for j in range(CONV_WIDTH - 1):
        c = c + conv_w[j:j + 1, :] * _roll_rows(ext, CONV_WIDTH - 1 - j)[DN_HALO:]
    a = c * jax.nn.sigmoid(c)
    outs = []
    for part in range(3):
        heads = []
        for h in range(DN_HEADS):
            lo = part * BRANCH_WIDTH + h * DN_HEAD_DIM
            xh = a[:, lo:lo + DN_HEAD_DIM]
            if part < 2:
                xh = xh * lax.rsqrt(jnp.sum(xh * xh, axis=-1, keepdims=True) + NORM_EPS)
            heads.append(xh)
        outs.append(jnp.concatenate(heads, axis=1))
    bd = cur[:, OFF_BD:OFF_BD + 128]
    beta = jax.nn.sigmoid(bd)
    g = -jnp.exp(a_log) * _softplus(bd + dt_b)
    outs.append(jnp.dot(beta, _head_expand(0), precision=_HI, preferred_element_type=F32))
    outs.append(jnp.dot(g, _head_expand(DN_HEADS), precision=_HI, preferred_element_type=F32))
    return outs


def _dn_in_specs(ts, first_block):
    per = ts // DN_HALO
    return [pl.BlockSpec((ts, DN_IN_WIDTH), lambda i: (first_block(i), OFF_DN // DN_IN_WIDTH)),
            pl.BlockSpec((DN_HALO, DN_IN_WIDTH),
                         lambda i: (jnp.maximum(first_block(i) * per - 1, 0), OFF_DN // DN_IN_WIDTH))]


def _dn_in_fwd(p, params, *, ts, name):
    s = p.shape[0]
    ts = min(ts, s)

    def body(cur_ref, before_ref, cw_ref, al_ref, db_ref, *o_refs):
        before = jnp.where(pl.program_id(0) == 0, 0.0, before_ref[...])
        outs = _fn_dn_in([cw_ref[...], al_ref[...], db_ref[...]], [cur_ref[...], before])
        for o_ref, val in zip(o_refs, outs):
            o_ref[...] = val

    whole = [pl.BlockSpec(a.shape, functools.partial(_whole, nd=a.ndim)) for a in params]
    res = pl.pallas_call(
        body, name=name, grid=(s // ts,), in_specs=_dn_in_specs(ts, lambda i: i) + whole,
        out_specs=[pl.BlockSpec((ts, BRANCH_WIDTH), lambda i: (i, 0))] * 5,
        out_shape=[jax.ShapeDtypeStruct((s, BRANCH_WIDTH), F32)] * 5,
        compiler_params=_params(("parallel",)),
    )(p, p, *params)
    return list(res)


def _dn_in_bwd(p, params, cts, dp, *, ts, name):
    s = p.shape[0]
    ts = min(ts, s)
    nb = s // ts
    block = lambda i: nb - 1 - i

    def body(cur_ref, before_ref, cw_ref, al_ref, db_ref, *rest):
        ct_refs, dx_ref, dpar_refs, halo = rest[:5], rest[6], rest[7:10], rest[10]
        first = pl.program_id(0) == 0

        @pl.when(first)
        def _():
            halo[...] = jnp.zeros_like(halo)
            for o_ref in dpar_refs:
                o_ref[...] = jnp.zeros_like(o_ref)

        before = jnp.where(pl.program_id(0) == nb - 1, 0.0, before_ref[...])
        _, vjp = jax.vjp(lambda pv, cur, bef: tuple(_fn_dn_in(pv, [cur, bef])),
                         [cw_ref[...], al_ref[...], db_ref[...]], cur_ref[...], before)
        dpar, dcur, dbefore = vjp(tuple(c[...] for c in ct_refs))
        dx_ref[...] = jnp.concatenate([dcur[:ts - DN_HALO], dcur[ts - DN_HALO:] + halo[...]], axis=0)
        halo[...] = dbefore
        for o_ref, val in zip(dpar_refs, dpar):
            o_ref[...] += val

    whole = [pl.BlockSpec(a.shape, functools.partial(_whole, nd=a.ndim)) for a in params]
    tile = pl.BlockSpec((ts, BRANCH_WIDTH), lambda i: (block(i), 0))
    res = pl.pallas_call(
        body, name=name, grid=(nb,),
        in_specs=_dn_in_specs(ts, block) + whole + [tile] * 5 + [pl.BlockSpec(memory_space=pl.ANY)],
        out_specs=[pl.BlockSpec((ts, DN_IN_WIDTH), lambda i: (block(i), OFF_DN // DN_IN_WIDTH))] + whole,
        out_shape=[jax.ShapeDtypeStruct(dp.shape, dp.dtype)] + [jax.ShapeDtypeStruct(a.shape, F32) for a in params],
        scratch_shapes=[pltpu.VMEM((DN_HALO, DN_IN_WIDTH), F32)],
        input_output_aliases={10: 0}, compiler_params=_params(("arbitrary",)),
    )(p, p, *params, *cts, dp)
    return res[0], list(res[1:])


def _fn_dn_out(pv, rv):
    heads = []
    for h in range(DN_HEADS):
        sl = slice(h * DN_HEAD_DIM, (h + 1) * DN_HEAD_DIM)
        z = rv[1][:, sl]
        heads.append(_rms(rv[0][:, sl], pv[0]) * (z * jax.nn.sigmoid(z)))
    return [jnp.concatenate(heads, axis=1)]


_DIMS = {"nn": (((1,), (0,)), ((), ())), "nt": (((1,), (1,)), ((), ())), "tn": (((0,), (0,)), ((), ()))}
_DIMS_BWD = {"nn": (("nt", "c", "b"), ("tn", "a", "c")), "nt": (("nn", "c", "b"), ("tn", "c", "a")),
             "tn": (("nt", "b", "c"), ("nn", "a", "c"))}


def _bf16_dot(a, b, kind):
    return lax.dot_general(a.astype(BF16), b.astype(BF16), _DIMS[kind], preferred_element_type=F32)


def _pdot_raw(a, b, kind, mode):
    if mode == 1:
        return _bf16_dot(a, b, kind)
    if mode == 6:
        return lax.dot_general(a, b, _DIMS[kind], precision=_HI, preferred_element_type=F32)
    b_hi, b_lo = _split_bf16(b)
    if mode == 3:
        a_hi, a_lo = _split_bf16(a)
        return _bf16_dot(a_hi, b_hi, kind) + (_bf16_dot(a_hi, b_lo, kind) + _bf16_dot(a_lo, b_hi, kind))
    b_rest = (b - b_hi.astype(F32) - b_lo.astype(F32)).astype(BF16)
    return _bf16_dot(a, b_hi, kind) + (_bf16_dot(a, b_lo, kind) + _bf16_dot(a, b_rest, kind))


@functools.partial(jax.custom_vjp, nondiff_argnums=(2, 3))
def _pdot(a, b, kind, mode):
    return _pdot_raw(a, b, kind, mode)


def _pdot_fwd(a, b, kind, mode):
    return _pdot_raw(a, b, kind, mode), (a, b)


def _pdot_bwd(kind, mode, res, ct):
    ops = {"a": res[0], "b": res[1], "c": ct}
    (ka, a1, a2), (kb, b1, b2) = _DIMS_BWD[kind]
    if mode == "count":
        return jnp.zeros_like(res[0]), _pdot(ops[b1], ops[b2], kb, mode)
    return _pdot(ops[a1], ops[a2], ka, mode), _pdot(ops[b1], ops[b2], kb, mode)


_pdot.defvjp(_pdot_fwd, _pdot_bwd)


@jax.custom_vjp
def _unit_lower_inverses(mats):
    c = mats[0].shape[0]
    row, col = _iota2((c, c), 0), _iota2((c, c), 1)
    x = [(row == col).astype(F32) for _ in mats]
    shift = 0
    while (1 << shift) < c:
        pair = jnp.right_shift(row, shift + 1) == jnp.right_shift(col, shift + 1)
        between = pair & (jnp.right_shift(row, shift) != jnp.right_shift(col, shift))
        q = [jnp.where(between, a, 0.0) for a in mats]
        qd = [_pdot_raw(qi, xi, "nn", DN_SOLVE_PASSES) for qi, xi in zip(q, x)]
        x = [xi - _pdot_raw(xi, m, "nn", DN_SOLVE_PASSES) for xi, m in zip(x, qd)]
        shift += 1
    return tuple(x)


def _unit_lower_inverses_fwd(mats):
    x = _unit_lower_inverses(mats)
    return x, x


def _unit_lower_inverses_bwd(x, dx):
    inner = [_pdot_raw(d, xi, "nt", DN_SOLVE_PASSES) for d, xi in zip(dx, x)]
    return (tuple(-_pdot_raw(xi, m, "tn", DN_SOLVE_PASSES) for xi, m in zip(x, inner)),)


_unit_lower_inverses.defvjp(_unit_lower_inverses_fwd, _unit_lower_inverses_bwd)


def _head(h):
    return slice(h * DN_HEAD_DIM, (h + 1) * DN_HEAD_DIM)


def _delta_chunk(states, q, k, v, beta, g):
    c = DN_BLOCK
    heads = range(DN_HEADS)
    row, col = _iota2((c, c), 0), _iota2((c, c), 1)
    tri, strict = col <= row, col < row
    counts = jnp.concatenate([tri.astype(BF16), jnp.ones((c, c), BF16)], axis=0)
    sums = _pdot(counts, g, "nn", "count")
    gc = [sums[:c, _head(h)] for h in heads]
    gl = [sums[c:, _head(h)] for h in heads]
    qh, kh, vh, bh = ([t[:, _head(h)] for h in heads] for t in (q * (DN_HEAD_DIM ** -0.5), k, v, beta))
    decay = [jnp.where(tri, jnp.exp(jnp.where(tri, gc[h] - gc[h].T, 0.0)), 0.0) for h in heads]
    solve, carry, out = DN_SOLVE_PASSES, DN_STATE_PASSES, DN_OUT_PASSES
    kk = [_pdot(kh[h], kh[h], "nt", solve) for h in heads]
    x = _unit_lower_inverses(tuple(jnp.where(strict, bh[h] * kk[h] * decay[h], 0.0) for h in heads))
    eg = [jnp.exp(gc[h]) for h in heads]
    u = [_pdot(x[h], vh[h] * bh[h], "nn", solve) for h in heads]
    wk = [_pdot(x[h], kh[h] * (bh[h] * eg[h]), "nn", solve) for h in heads]
    qk = [jnp.where(tri, _pdot(qh[h], kh[h], "nt", out) * decay[h], 0.0) for h in heads]
    v_new = [u[h] - _pdot(wk[h], states[h], "nn", carry) for h in heads]
    o = [_pdot(qh[h] * eg[h], states[h], "nn", out) + _pdot(qk[h], v_new[h], "nn", out) for h in heads]
    nxt = [states[h] * jnp.exp(gl[h]) + _pdot(kh[h] * jnp.exp(gl[h] - gc[h]), v_new[h], "tn", carry)
           for h in heads]
    return jnp.concatenate(o, axis=1), tuple(nxt)


def _delta_fwd(q, k, v, beta, g, *, name):
    s = q.shape[0]
    c = DN_BLOCK
    nc = s // c

    def body(q_ref, k_ref, v_ref, b_ref, g_ref, o_ref, sp_ref, st):
        @pl.when(pl.program_id(0) == 0)
        def _():
            st[...] = jnp.zeros_like(st)

        states = tuple(st[h] for h in range(DN_HEADS))
        for h in range(DN_HEADS):
            sp_ref[0, h] = states[h]
        o, nxt = _delta_chunk(states, q_ref[...], k_ref[...], v_ref[...], b_ref[...], g_ref[...])
        for h in range(DN_HEADS):
            st[h] = nxt[h]
        o_ref[...] = o

    blk = pl.BlockSpec((c, BRANCH_WIDTH), lambda n: (n, 0))
    return pl.pallas_call(
        body, name=name, grid=(nc,), in_specs=[blk] * 5,
        out_specs=[blk, pl.BlockSpec((1, DN_HEADS, DN_HEAD_DIM, DN_HEAD_DIM), lambda n: (n, 0, 0, 0))],
        out_shape=[jax.ShapeDtypeStruct((s, BRANCH_WIDTH), F32),
                   jax.ShapeDtypeStruct((nc, DN_HEADS, DN_HEAD_DIM, DN_HEAD_DIM), F32)],
        scratch_shapes=[pltpu.VMEM((DN_HEADS, DN_HEAD_DIM, DN_HEAD_DIM), F32)],
        compiler_params=_params(("arbitrary",)),
    )(q, k, v, beta, g)


def _delta_bwd(q, k, v, beta, g, states, do, *, name):
    s = q.shape[0]
    c = DN_BLOCK
    nc = s // c

    def body(q_ref, k_ref, v_ref, b_ref, g_ref, sp_ref, do_ref, dq_ref, dk_ref, dv_ref, db_ref, dg_ref, dst):
        @pl.when(pl.program_id(0) == 0)
        def _():
            dst[...] = jnp.zeros_like(dst)

        states = tuple(sp_ref[0, h] for h in range(DN_HEADS))
        _, vjp = jax.vjp(_delta_chunk, states, q_ref[...], k_ref[...], v_ref[...], b_ref[...], g_ref[...])
        d = vjp((do_ref[...], tuple(dst[h] for h in range(DN_HEADS))))
        for h in range(DN_HEADS):
            dst[h] = d[0][h]
        for o_ref, val in zip((dq_ref, dk_ref, dv_ref, db_ref, dg_ref), d[1:]):
            o_ref[...] = val

    blk = pl.BlockSpec((c, BRANCH_WIDTH), lambda n: (nc - 1 - n, 0))
    res = pl.pallas_call(
        body, name=name, grid=(nc,),
        in_specs=[blk] * 5 + [pl.BlockSpec((1, DN_HEADS, DN_HEAD_DIM, DN_HEAD_DIM), lambda n: (nc - 1 - n, 0, 0, 0)), blk],
        out_specs=[blk] * 5, out_shape=[jax.ShapeDtypeStruct((s, BRANCH_WIDTH), F32)] * 5,
        scratch_shapes=[pltpu.VMEM((DN_HEADS, DN_HEAD_DIM, DN_HEAD_DIM), F32)],
        compiler_params=_params(("arbitrary",)),
    )(q, k, v, beta, g, states, do)
    return list(res)


def _split_bf16(x):
    hi = x.astype(BF16)
    return hi, (x - hi.astype(F32)).astype(BF16)


def _sb_consts():
    tq, tk = SB_QUERY_ROWS, SB_BLOCK
    row, col = _iota2((tk, tk), 0), _iota2((tk, tk), 1)
    ones = jnp.ones((tk, tk), BF16)
    later = jnp.concatenate([(row > col).astype(BF16), ones], axis=1)
    from_here = jnp.concatenate([(row >= col).astype(BF16), ones], axis=1)
    first = _iota2((tq, 128), 1) < SB_HEAD_DIM
    return later, from_here, first


def _sb_causal(d):
    tq, tk = SB_QUERY_ROWS, SB_BLOCK
    return _iota2((tq - d * tk, tk), 1) < _iota2((tq - d * tk, tk), 0)


def _sums(x, mat):
    hi, lo = _split_bf16(x)
    return jnp.dot(hi, mat, preferred_element_type=F32) + jnp.dot(lo, mat, preferred_element_type=F32)


def _sb_weights(qs, kb, accs, later, causal):
    tk = SB_BLOCK
    z = [lax.dot_general(qh, kb, (((1,), (1,)), ((), ())), preferred_element_type=F32) for qh in qs]
    lk = [-_softplus(zh) for zh in z]
    if causal is not None:
        lk = [jnp.where(causal, v, 0.0) for v in lk]
    cs = [_sums(v, later) for v in lk]
    e = [z[h] + lk[h] + cs[h][:, :tk] + accs[h] for h in range(2)]
    if causal is not None:
        e = [jnp.where(causal, v, -1e30) for v in e]
    return lk, [jnp.exp(v) for v in e], [v[:, tk:] for v in cs]


def _sb_alive(accs):
    return jnp.max(jnp.maximum(accs[0], accs[1])) > SB_DEAD_LOG


def _sb_sweep(i, block, carry, accs_of, stop=None):
    per = SB_QUERY_ROWS // SB_BLOCK
    for d in reversed(range(per)):
        r0 = d * SB_BLOCK
        seen = block(i * per + d, jax.tree.map(lambda a: a[r0:], carry), _sb_causal(d), r0)
        carry = seen if r0 == 0 else jax.tree.map(lambda old, new: jnp.concatenate([old[:r0], new], axis=0), carry, seen)
    if stop is not None:
        return lax.fori_loop(0, i * per - 1 - stop, lambda jj, cr: block(i * per - 1 - jj, cr, None), carry)

    def step(state):
        j, _, cr = state
        cr = block(j, cr, None)
        return j - 1, _sb_alive(accs_of(cr)), cr

    j, _, carry = lax.while_loop(lambda st: jnp.logical_and(st[0] >= 0, st[1]), step,
                                 (i * per - 1, _sb_alive(accs_of(carry)), carry))
    return carry, j


def _sb_fwd(p, *, name):
    s = p.shape[0]
    t, tk = SB_QUERY_ROWS, SB_BLOCK
    scale = SB_HEAD_DIM ** -0.5

    def body(q_ref, k_ref, v_ref, o_ref):
        i = pl.program_id(1)
        later, _, first = _sb_consts()
        q = q_ref[...] * scale
        qs = (jnp.where(first, q, 0.0).astype(BF16), jnp.where(first, 0.0, q).astype(BF16))

        def block(j, carry, causal, r0=0):
            start = pl.multiple_of(j * tk, tk)
            kb = k_ref[pl.ds(start, tk), :].astype(BF16)
            vb = v_ref[pl.ds(start, tk), :].astype(BF16)
            _, w, tot = _sb_weights([qh[r0:] for qh in qs], kb, [carry[h][1] for h in range(2)], later, causal)
            out = [jnp.dot(w[h].astype(BF16), vb, preferred_element_type=F32) for h in range(2)]
            return tuple((carry[h][0] + out[h], carry[h][1] + tot[h]) for h in range(2))

        zero = jnp.zeros((t, 128), F32)
        carry, _ = _sb_sweep(i, block, ((zero, zero), (zero, zero)), lambda cr: (cr[0][1], cr[1][1]))
        o_ref[...] = jnp.where(first, carry[0][0], carry[1][0])

    return pl.pallas_call(
        body, name=name, grid=(BRANCH_WIDTH // 128, s // t),
        in_specs=[pl.BlockSpec((t, 128), lambda pr, i: (i, OFF_CQ // 128 + pr)),
                  pl.BlockSpec((s, 128), lambda pr, i: (0, OFF_CK // 128 + pr)),
                  pl.BlockSpec((s, 128), lambda pr, i: (0, OFF_CV // 128 + pr))],
        out_specs=pl.BlockSpec((t, 128), lambda pr, i: (i, pr)),
        out_shape=jax.ShapeDtypeStruct((s, BRANCH_WIDTH), F32),
        compiler_params=_params(("arbitrary", "arbitrary")),
    )(p, p, p)


def _sb_bwd(p, do, *, name):
    s = p.shape[0]
    t, tk = SB_QUERY_ROWS, SB_BLOCK
    scale = SB_HEAD_DIM ** -0.5

    def body(q_ref, k_ref, v_ref, do_ref, dq_ref, dk_ref, dv_ref):
        i = pl.program_id(1)

        @pl.when(i == 0)
        def _():
            dk_ref[...] = jnp.zeros_like(dk_ref)
            dv_ref[...] = jnp.zeros_like(dv_ref)

        later, from_here, first = _sb_consts()
        q = q_ref[...] * scale
        do = do_ref[...]
        qs = (jnp.where(first, q, 0.0).astype(BF16), jnp.where(first, 0.0, q).astype(BF16))
        dos = (jnp.where(first, do, 0.0).astype(BF16), jnp.where(first, 0.0, do).astype(BF16))

        def total(j, carry, causal, r0=0):
            start = pl.multiple_of(j * tk, tk)
            kb = k_ref[pl.ds(start, tk), :].astype(BF16)
            vb = v_ref[pl.ds(start, tk), :].astype(BF16)
            _, w, tot = _sb_weights([qh[r0:] for qh in qs], kb, [carry[h][0] for h in range(2)], later, causal)
            dw = [lax.dot_general(dos[h][r0:], vb, (((1,), (1,)), ((), ())), preferred_element_type=F32)
                  for h in range(2)]
            tde = [_sums(dw[h] * w[h], from_here)[:, tk:] for h in range(2)]
            return tuple((carry[h][0] + tot[h], carry[h][1] + tde[h]) for h in range(2))

        zero = jnp.zeros((t, 128), F32)
        sums, stop = _sb_sweep(i, total, ((zero, zero), (zero, zero)), lambda cr: (cr[0][0], cr[1][0]))
        deltas = (sums[0][1], sums[1][1])

        def block(j, carry, causal, r0=0):
            start = pl.multiple_of(j * tk, tk)
            kb = k_ref[pl.ds(start, tk), :].astype(BF16)
            vb = v_ref[pl.ds(start, tk), :].astype(BF16)
            both = range(2)
            tn = (((0,), (0,)), ((), ()))
            qr, dor, total_de = [x[r0:] for x in qs], [x[r0:] for x in dos], [x[r0:] for x in deltas]
            lk, w, tot = _sb_weights(qr, kb, [carry[h][1] for h in both], later, causal)
            dw = [lax.dot_general(dor[h], vb, (((1,), (1,)), ((), ())), preferred_element_type=F32) for h in both]
            de = [dw[h] * w[h] for h in both]
            cs = [_sums(de[h], from_here) for h in both]
            keep = [jnp.exp(lk[h]) for h in both]
            dz = [de[h] * keep[h] - (total_de[h] - (cs[h][:, :tk] + carry[h][2])) * (1.0 - keep[h]) for h in both]
            if causal is not None:
                dz = [jnp.where(causal, v, 0.0) for v in dz]
            dzb = [v.astype(BF16) for v in dz]
            dq = [jnp.dot(dzb[h], kb, preferred_element_type=F32) for h in both]
            dk = [lax.dot_general(dzb[h], qr[h], tn, preferred_element_type=F32) for h in both]
            dv = [lax.dot_general(w[h].astype(BF16), dor[h], tn, preferred_element_type=F32) for h in both]
            dk_ref[pl.ds(start, tk), :] += dk[0] + dk[1]
            dv_ref[pl.ds(start, tk), :] += dv[0] + dv[1]
            return tuple((carry[h][0] + dq[h], carry[h][1] + tot[h], carry[h][2] + cs[h][:, tk:]) for h in both)

        carry = _sb_sweep(i, block, ((zero, zero, zero), (zero, zero, zero)), None, stop=stop)
        dq_ref[...] = jnp.where(first, carry[0][0], carry[1][0]) * scale

    qblk = lambda off: pl.BlockSpec((t, 128), lambda pr, i: (i, off // 128 + pr))
    full = lambda off: pl.BlockSpec((s, 128), lambda pr, i: (0, off // 128 + pr))
    res = pl.pallas_call(
        body, name=name, grid=(BRANCH_WIDTH // 128, s // t),
        in_specs=[qblk(OFF_CQ), full(OFF_CK), full(OFF_CV), qblk(0)],
        out_specs=[qblk(0), full(0), full(0)],
        out_shape=[jax.ShapeDtypeStruct((s, BRANCH_WIDTH), F32)] * 3,
        compiler_params=_params(("arbitrary", "arbitrary")),
    )(p, p, p, do)
    return list(res)


def _loss_head(y, target, *, name, ts=512):
    s, d = y.shape
    ts = min(ts, s)

    def body(y_ref, t_ref, sq_ref, dy_ref):
        @pl.when(pl.program_id(0) == 0)
        def _():
            sq_ref[...] = jnp.zeros_like(sq_ref)

        err = y_ref[...] - t_ref[...]
        dy_ref[...] = err * (1.0 / d)
        tot = jnp.sum(jnp.sum(err * err, axis=1, keepdims=True), axis=0, keepdims=True)
        sq_ref[...] += jnp.broadcast_to(tot, sq_ref.shape)

    blk = pl.BlockSpec((ts, d), lambda i: (i, 0))
    return pl.pallas_call(
        body, name=name, grid=(s // ts,), in_specs=[blk, blk],
        out_specs=[pl.BlockSpec((1, 128), lambda i: (0, 0)), blk],
        out_shape=[jax.ShapeDtypeStruct((1, 128), F32), jax.ShapeDtypeStruct((s, d), F32)],
        compiler_params=_params(("arbitrary",)),
    )(y, target)


def _row_tile(r, limit=512):
    return max(t for t in range(16, limit + 1, 16) if r % t == 0)


def _adamw(w, g, m, v, *, name):
    shape = w.shape
    lanes = shape[-1]
    w, g, m, v = (a.reshape(-1, lanes) for a in (w, g, m, v))
    r = w.shape[0]
    ts = r if r <= 256 else _row_tile(r, 256 if lanes > LANES else 512)

    def body(w_ref, g_ref, m_ref, v_ref, d_ref, nm_ref, nv_ref):
        gv = g_ref[...]
        m_new = ADAM_B1 * m_ref[...] + (1.0 - ADAM_B1) * gv
        v_new = ADAM_B2 * v_ref[...] + (1.0 - ADAM_B2) * jnp.square(gv)
        m_hat = m_new / (1.0 - ADAM_B1 ** ADAM_STEP)
        v_hat = v_new / (1.0 - ADAM_B2 ** ADAM_STEP)
        d_ref[...] = -ADAM_LR * (m_hat / (jnp.sqrt(v_hat) + ADAM_EPS) + ADAM_WD * w_ref[...])
        nm_ref[...] = m_new
        nv_ref[...] = v_new

    blk = pl.BlockSpec((ts, lanes), lambda i: (i, 0))
    res = pl.pallas_call(
        body, name=name, grid=(r // ts,), in_specs=[blk] * 4, out_specs=[blk] * 3,
        out_shape=[jax.ShapeDtypeStruct((r, lanes), F32)] * 3,
        compiler_params=_params(("parallel",)),
    )(w, g, m, v)
    return [a.reshape(shape) for a in res]


def _add_rows(terms, *, name, out_dtype=F32):
    r = terms[0].shape[0]
    ts = _row_tile(r)

    def body(*refs):
        acc = refs[0][...].astype(F32)
        for ref in refs[1:-1]:
            acc = acc + ref[...].astype(F32)
        refs[-1][...] = acc.astype(out_dtype)

    blk = pl.BlockSpec((ts, LANES), lambda i: (i, 0))
    return pl.pallas_call(
        body, name=name, grid=(r // ts,), in_specs=[blk] * len(terms), out_specs=blk,
        out_shape=jax.ShapeDtypeStruct((r, LANES), out_dtype), compiler_params=_params(("parallel",)),
    )(*terms)


_HBM = pl.BlockSpec(memory_space=pltpu.HBM)
_MESH = pl.DeviceIdType.MESH


def _other_chips(x, y):
    return [(1 - x, y), (x, 1 - y), (1 - x, 1 - y)]


def _gather_chips(shard, *, name):
    r, lanes = shard.shape
    half = r // 2
    assert half * 2 == r

    def body(in_ref, out_ref, send_sems, recv_sems):
        x, y, c = lax.axis_index("x"), lax.axis_index("y"), lax.axis_index("c")
        me = 2 * x + y
        sibling = (x, y, 1 - c)
        chips = _other_chips(x, y)

        def copy(sem, chip, core_half, to):
            rows = out_ref.at[chip, pl.ds(core_half * half, half)]
            return pltpu.make_async_remote_copy(src_ref=rows, dst_ref=rows, send_sem=send_sems.at[sem],
                                                recv_sem=recv_sems.at[sem], device_id=to, device_id_type=_MESH)

        first = []
        for kk, (px, py) in enumerate(chips):
            cp = pltpu.make_async_remote_copy(
                src_ref=in_ref.at[pl.ds(c * half, half)], dst_ref=out_ref.at[me, pl.ds(c * half, half)],
                send_sem=send_sems.at[kk], recv_sem=recv_sems.at[kk], device_id=(px, py, c), device_id_type=_MESH)
            cp.start()
            first.append(cp)
        passed = [copy(3 + kk, 2 * px + py, c, sibling) for kk, (px, py) in enumerate(chips)]
        for kk, (px, py) in enumerate(chips):
            copy(kk, 2 * px + py, c, (px, py, c)).wait_recv()
            passed[kk].start()
        for kk, (px, py) in enumerate(chips):
            copy(3 + kk, 2 * px + py, 1 - c, sibling).wait_recv()
        for cp in first + passed:
            cp.wait_send()

    gathered = pl.pallas_call(
        body, name=name, in_specs=[_HBM], out_specs=_HBM,
        out_shape=jax.ShapeDtypeStruct((4, r, lanes), shard.dtype),
        scratch_shapes=[pltpu.SemaphoreType.DMA((6,)), pltpu.SemaphoreType.DMA((6,))],
    )(shard)
    me = 2 * lax.axis_index("x") + lax.axis_index("y")
    return lax.dynamic_update_slice(gathered, shard[None], (me, 0, 0))


def _to_sibling(block, *, name):
    def body(in_ref, out_ref, send_sem, recv_sem):
        x, y, c = lax.axis_index("x"), lax.axis_index("y"), lax.axis_index("c")
        cp = pltpu.make_async_remote_copy(src_ref=in_ref, dst_ref=out_ref, send_sem=send_sem, recv_sem=recv_sem,
                                          device_id=(x, y, 1 - c), device_id_type=_MESH)
        cp.start()
        cp.wait()

    return pl.pallas_call(
        body, name=name, in_specs=[_HBM], out_specs=_HBM, out_shape=jax.ShapeDtypeStruct(block.shape, block.dtype),
        scratch_shapes=[pltpu.SemaphoreType.DMA, pltpu.SemaphoreType.DMA],
    )(block)


def _halves_to_sibling(blocks, *, name):
    n, r, lanes = blocks.shape
    half = r // 2

    def body(in_ref, out_ref, send_sem, recv_sem):
        x, y, c = lax.axis_index("x"), lax.axis_index("y"), lax.axis_index("c")
        cp = pltpu.make_async_remote_copy(src_ref=in_ref.at[pl.ds(0, n), pl.ds((1 - c) * half, half)], dst_ref=out_ref,
                                          send_sem=send_sem, recv_sem=recv_sem, device_id=(x, y, 1 - c),
                                          device_id_type=_MESH)
        cp.start()
        cp.wait()

    return pl.pallas_call(
        body, name=name, in_specs=[_HBM], out_specs=_HBM, out_shape=jax.ShapeDtypeStruct((n, half, lanes), blocks.dtype),
        scratch_shapes=[pltpu.SemaphoreType.DMA, pltpu.SemaphoreType.DMA],
    )(blocks)


def _add_own_half(blocks, got, *, name, out_dtype):
    n, r, lanes = blocks.shape
    half = r // 2
    ts = _row_tile(half)
    steps = half // ts
    core = lax.axis_index("c").astype(jnp.int32).reshape(1)

    def body(core_ref, a_ref, b_ref, o_ref):
        o_ref[...] = (a_ref[...].astype(F32) + b_ref[...].astype(F32)).astype(out_dtype)

    return pl.pallas_call(
        body, name=name,
        grid_spec=pltpu.PrefetchScalarGridSpec(
            num_scalar_prefetch=1, grid=(n, steps),
            in_specs=[pl.BlockSpec((1, ts, lanes), lambda k, i, core_ref: (k, core_ref[0] * steps + i, 0)),
                      pl.BlockSpec((1, ts, lanes), lambda k, i, core_ref: (k, i, 0))],
            out_specs=pl.BlockSpec((1, ts, lanes), lambda k, i, core_ref: (k, i, 0))),
        out_shape=jax.ShapeDtypeStruct((n, half, lanes), out_dtype),
        compiler_params=_params(("parallel", "parallel")),
    )(core, blocks, got)


def _scatter_chips(blocks, *, name):
    _, r, lanes = blocks.shape

    def body(in_ref, out_ref, send_sems, recv_sems, local_sem):
        x, y, c = lax.axis_index("x"), lax.axis_index("y"), lax.axis_index("c")
        me = 2 * x + y
        mine = pltpu.make_async_copy(in_ref.at[me], out_ref.at[me], local_sem)
        mine.start()
        copies = []
        for kk, (px, py) in enumerate(_other_chips(x, y)):
            cp = pltpu.make_async_remote_copy(src_ref=in_ref.at[2 * px + py], dst_ref=out_ref.at[me],
                                              send_sem=send_sems.at[kk], recv_sem=recv_sems.at[kk],
                                              device_id=(px, py, c), device_id_type=_MESH)
            cp.start()
            copies.append(cp)
        for kk, (px, py) in enumerate(_other_chips(x, y)):
            pltpu.make_async_remote_copy(src_ref=in_ref.at[me], dst_ref=out_ref.at[2 * px + py],
                                         send_sem=send_sems.at[kk], recv_sem=recv_sems.at[kk], device_id=(px, py, c),
                                         device_id_type=_MESH).wait_recv()
        for cp in copies:
            cp.wait_send()
        mine.wait()

    return pl.pallas_call(
        body, name=name, in_specs=[_HBM], out_specs=_HBM, out_shape=jax.ShapeDtypeStruct((4, r, lanes), blocks.dtype),
        scratch_shapes=[pltpu.SemaphoreType.DMA((3,)), pltpu.SemaphoreType.DMA((3,)), pltpu.SemaphoreType.DMA],
    )(blocks)


SHARDED = (("norm_g", 2), ("w_in", 2), ("conv_w", 2), ("w_branch", 3), ("w_out", 1), ("w_ff1", 2), ("w_ff2", 1))
MATMUL_WEIGHTS = ("w_in", "w_branch", "w_out", "w_ff1", "w_ff2")
VECTOR_WEIGHTS = ("norm_g", "conv_w")
REPLICATED = ("b_in", "sgu_ln_g", "sgu_ln_b", "w_spatial", "b_spatial", "a_log", "dt_bias", "dn_norm_g")
WEIGHT_ORDER = ("norm_g", "w_in", "b_in", "sgu_ln_g", "sgu_ln_b", "w_spatial", "b_spatial", "conv_w", "a_log",
                "dt_bias", "dn_norm_g", "w_branch", "w_out", "w_ff1", "w_ff2")
PACK_ROW_MULTIPLE = 32


def _size(shape):
    n = 1
    for dim in shape:
        n *= dim
    return n


def _wide(shape):
    return len(shape) >= 2 and shape[-1] > LANES


def _rows_of(shape):
    if _wide(shape):
        return (_size(shape) // shape[-1]) * -(-shape[-1] // LANES)
    return -(-_size(shape) // LANES)


def _pack(arrays):
    parts = []
    for a in arrays:
        if _wide(a.shape):
            rows = a.reshape(-1, a.shape[-1])
            pad = -a.shape[-1] % LANES
            if pad:
                rows = jnp.concatenate([rows, jnp.zeros((rows.shape[0], pad), a.dtype)], axis=1)
            parts += [rows[:, c:c + LANES] for c in range(0, rows.shape[1], LANES)]
            continue
        flat = a.reshape(-1)
        pad = _rows_of(a.shape) * LANES - flat.shape[0]
        if pad:
            flat = jnp.concatenate([flat, jnp.zeros((pad,), flat.dtype)])
        parts.append(flat.reshape(-1, LANES))
    rows = sum(p.shape[0] for p in parts)
    pad = -rows % PACK_ROW_MULTIPLE
    if pad:
        parts.append(jnp.zeros((pad, LANES), parts[0].dtype))
    return jnp.concatenate(parts, axis=0)


def _unpack(buf, shapes):
    out, row = [], 0
    for shape in shapes:
        rows = _rows_of(shape)
        part = buf[row:row + rows]
        if _wide(shape):
            chunks = -(-shape[-1] // LANES)
            each = rows // chunks
            whole = jnp.concatenate([part[c * each:(c + 1) * each] for c in range(chunks)], axis=1)
            out.append(whole[:, :shape[-1]].reshape(shape))
        else:
            out.append(part.reshape(-1)[:_size(shape)].reshape(shape))
        row += rows
    return out


def _chip_slice(a, axis, k):
    size = a.shape[axis] // 4
    return lax.slice_in_dim(a, k * size, (k + 1) * size, axis=axis)


def _rearrange_w_in(w):
    pad = jnp.zeros(w.shape[:-1] + (P_PAD - P_IN,), w.dtype)
    return jnp.concatenate([w[..., 1024:2560], w[..., 3072:3080], pad, w[..., 0:1024], w[..., 4616:7688],
                            w[..., 2560:3072], w[..., 3080:4616]], axis=-1)


def _restore_w_in(w):
    return jnp.concatenate([w[..., 2048:3072], w[..., 0:1536], w[..., 6144:6656], w[..., 1536:1544],
                            w[..., 6656:8192], w[..., 3072:6144]], axis=-1)


def _layer_params(wl):
    row = lambda v: v.reshape(1, -1)
    pad128 = lambda v, at: jnp.pad(v, (at, 128 - at - v.shape[0])).reshape(1, 128)
    return dict(
        g=[row(wl["norm_g"][i]) for i in range(4)],
        gmlp=[row(wl["sgu_ln_g"]), row(wl["sgu_ln_b"]), wl["w_spatial"],
              jnp.pad(wl["b_spatial"].T, ((0, 0), (0, 128 - GM_GROUPS)))],
        dn_in=[wl["conv_w"], pad128(wl["a_log"], DN_HEADS), pad128(wl["dt_bias"], DN_HEADS)],
        dn_g=[row(wl["dn_norm_g"])],
    )


def _layer_fwd(x0, wl, l):
    tag = lambda s: f"{s}_l{l}"
    pr = _layer_params(wl)
    w = BRANCH_WIDTH
    h0 = _row_fwd(_fn_rms, [(x0, 0, D_MODEL)], [pr["g"][0]], [(D_MODEL, BF16)], ts=512, name=tag("rms0"))[0]
    p = _mm(h0, wl["w_in"], bias=wl["b_in"].reshape(1, -1), name=tag("proj_in"))
    ya = _row_fwd(_fn_gmlp, [(p, OFF_GM, 2 * w)], pr["gmlp"], [(w, BF16)], ts=256, name=tag("gmlp"))[0]
    q, k, v, beta, g = _dn_in_fwd(p, pr["dn_in"], ts=256, name=tag("dn_in"))
    o, states = _delta_fwd(q, k, v, beta, g, name=tag("delta"))
    yb = _row_fwd(_fn_dn_out, [(o, 0, w), (p, OFF_BZ, w)], pr["dn_g"], [(w, BF16)], ts=512, name=tag("dn_out"))[0]
    yc = _sb_fwd(p, name=tag("sb"))
    ys = [ya, yb, yc]
    proj = [_mm(ys[i], wl["w_branch"][i], out_dtype=BF16, name=tag(f"branch{i}")) for i in range(3)]
    merge_rows = [(a, 0, D_MODEL) for a in proj] + [(p, OFF_GATE, 3 * D_MODEL)]
    m = _row_fwd(_fn_merge, merge_rows, [], [(D_MODEL, BF16)], ts=256, name=tag("merge"))[0]
    mixed = _mm(m, wl["w_out"], name=tag("out"))
    x1 = _row_fwd(_fn_resid_rms, [(x0, 0, D_MODEL), (mixed, 0, D_MODEL)], [pr["g"][1]], [(D_MODEL, F32)], ts=512,
                  name=tag("resid1"))[0]
    h2 = _row_fwd(_fn_rms, [(x1, 0, D_MODEL)], [pr["g"][2]], [(D_MODEL, BF16)], ts=512, name=tag("rms2"))[0]
    a, r = _mm(h2, wl["w_ff1"], name=tag("ff1"), out_dtype=(BF16, BF16),
               finish=lambda t: (jnp.maximum(t, 0.0), jnp.square(jnp.maximum(t, 0.0))))
    f = _mm(r, wl["w_ff2"], name=tag("ff2"))
    x2 = _row_fwd(_fn_resid_rms, [(x1, 0, D_MODEL), (f, 0, D_MODEL)], [pr["g"][3]], [(D_MODEL, F32)], ts=512,
                  name=tag("resid2"))[0]
    saved = dict(x0=x0, h0=h0, p=p, q=q, k=k, v=v, beta=beta, g=g, states=states, o=o, ys=ys,
                 proj=proj, m=m, mixed=mixed, x1=x1, h2=h2, a=a, r=r, f=f)
    return x2, saved


def _layer_bwd(dx2, sv, wl, l):
    tag = lambda s: f"{s}_l{l}"
    pr = _layer_params(wl)
    w = BRANCH_WIDTH
    full = lambda a: (a, 0, a.shape[1])
    p = sv["p"]
    (df,), (dg3,) = _row_bwd(_fn_rms_branch, [full(sv["f"])], [pr["g"][3]], [full(dx2)], ts=512, name=tag("resid2_b"),
                             matmul_only=(0,))
    da = _mm(df, wl["w_ff2"], trans_b=True, name=tag("ff2_dx"), out_dtype=BF16, beside=(sv["a"],),
             finish=lambda t, relu_a: 2.0 * relu_a.astype(F32) * t)
    dw_ff2 = _mm_tn(sv["r"], df, name=tag("ff2_dw"))
    dh2 = _mm(da, wl["w_ff1"], trans_b=True, name=tag("ff1_dx"))
    dw_ff1 = _mm_tn(sv["h2"], da, name=tag("ff1_dw"))
    (dx1,), (dg2,) = _row_bwd(_fn_rms_keep, [full(sv["x1"])], [pr["g"][2]], [full(dh2), full(dx2)], ts=512,
                              name=tag("rms2_b"))
    (dmixed,), (dg1,) = _row_bwd(_fn_rms_branch, [full(sv["mixed"])], [pr["g"][1]], [full(dx1)], ts=512,
                                 name=tag("resid1_b"), matmul_only=(0,))
    dm = _mm(dmixed, wl["w_out"], trans_b=True, name=tag("out_dx"))
    dw_out = _mm_tn(sv["m"], dmixed, name=tag("out_dw"))
    dp = lax.empty(p.shape, F32)
    merge_rows = [full(a) for a in sv["proj"]] + [(p, OFF_GATE, 3 * D_MODEL)]
    dmerge, _ = _row_bwd(_fn_merge, merge_rows, [], [full(dm)], ts=256, name=tag("merge_b"), into=(dp, 3),
                         matmul_only=(0, 1, 2))
    dproj, dp = dmerge[:3], dmerge[3]
    dys = [_mm(dproj[i], wl["w_branch"][i], trans_b=True, name=tag(f"branch{i}_dx")) for i in range(3)]
    dw_branch = jnp.stack([_mm_tn(sv["ys"][i], dproj[i], name=tag(f"branch{i}_dw")) for i in range(3)])
    (dp,), dgm = _row_bwd(_fn_gmlp, [(p, OFF_GM, 2 * w)], pr["gmlp"], [full(dys[0])], ts=256, name=tag("gmlp_b"),
                          into=(dp, 0))
    (do, dp), (d_dn_g,) = _row_bwd(_fn_dn_out, [full(sv["o"]), (p, OFF_BZ, w)], pr["dn_g"], [full(dys[1])], ts=512,
                                   name=tag("dn_out_b"), into=(dp, 1))
    dqkvbg = _delta_bwd(sv["q"], sv["k"], sv["v"], sv["beta"], sv["g"], sv["states"], do, name=tag("delta_b"))
    dp, d_dn_in = _dn_in_bwd(p, pr["dn_in"], dqkvbg, dp, ts=256, name=tag("dn_in_b"))
    for off, part in zip((OFF_CQ, OFF_CK, OFF_CV), _sb_bwd(p, dys[2], name=tag("sb_b"))):
        dp = lax.dynamic_update_slice(dp, part, (0, off))
    dh0 = _mm(dp, wl["w_in"], trans_b=True, name=tag("proj_in_dx"))
    dw_in, db_in = _mm_tn(sv["h0"], dp, name=tag("proj_in_dw"), col_sums=True)
    (dx0,), (dg0,) = _row_bwd(_fn_rms_keep, [full(sv["x0"])], [pr["g"][0]], [full(dh0), full(dx1)], ts=512,
                              name=tag("rms0_b"))
    grads = dict(
        norm_g=jnp.concatenate([dg0, dg1, dg2, dg3], axis=0), w_in=dw_in, b_in=db_in.reshape(-1),
        sgu_ln_g=dgm[0].reshape(-1), sgu_ln_b=dgm[1].reshape(-1), w_spatial=dgm[2],
        b_spatial=dgm[3][:, :GM_GROUPS].T, conv_w=d_dn_in[0], a_log=d_dn_in[1][0, DN_HEADS:2 * DN_HEADS],
        dt_bias=d_dn_in[2][0, DN_HEADS:2 * DN_HEADS], dn_norm_g=d_dn_g.reshape(-1), w_branch=dw_branch,
        w_out=dw_out, w_ff1=dw_ff1, w_ff2=dw_ff2)
    return dx0, grads


def _local_step(x, target, weights):
    saved = []
    h = x
    layers = []
    for l in range(DEPTH):
        wl = {n: weights[n][l] for n in WEIGHT_ORDER}
        layers.append(wl)
        h, sv = _layer_fwd(h, wl, l)
        saved.append(sv)
    sq, dh = _loss_head(h, target, name="loss_head")
    grads = [None] * DEPTH
    for l in reversed(range(DEPTH)):
        dh, grads[l] = _layer_bwd(dh, saved[l], layers[l], l)
    stacked = {n: jnp.stack([grads[l][n].astype(GRAD_DTYPE) for l in range(DEPTH)]) for n in WEIGHT_ORDER}
    return sq[0, 0], dh, stacked


def kernel(x, norm_g, w_in, b_in, sgu_ln_g, sgu_ln_b, w_spatial, b_spatial, conv_w, a_log, dt_bias, dn_norm_g, w_branch, w_out, w_ff1, w_ff2, loss_target, m_norm_g, m_w_in, m_b_in, m_sgu_ln_g, m_sgu_ln_b, m_w_spatial, m_b_spatial, m_conv_w, m_a_log, m_dt_bias, m_dn_norm_g, m_w_branch, m_w_out, m_w_ff1, m_w_ff2, v_norm_g, v_w_in, v_b_in, v_sgu_ln_g, v_sgu_ln_b, v_w_spatial, v_b_spatial, v_conv_w, v_a_log, v_dt_bias, v_dn_norm_g, v_w_branch, v_w_out, v_w_ff1, v_w_ff2):
    local = dict(norm_g=norm_g, w_in=w_in, b_in=b_in, sgu_ln_g=sgu_ln_g, sgu_ln_b=sgu_ln_b, w_spatial=w_spatial,
                 b_spatial=b_spatial, conv_w=conv_w, a_log=a_log, dt_bias=dt_bias, dn_norm_g=dn_norm_g,
                 w_branch=w_branch, w_out=w_out, w_ff1=w_ff1, w_ff2=w_ff2)
    mom1 = dict(norm_g=m_norm_g, w_in=m_w_in, b_in=m_b_in, sgu_ln_g=m_sgu_ln_g, sgu_ln_b=m_sgu_ln_b,
                w_spatial=m_w_spatial, b_spatial=m_b_spatial, conv_w=m_conv_w, a_log=m_a_log, dt_bias=m_dt_bias,
                dn_norm_g=m_dn_norm_g, w_branch=m_w_branch, w_out=m_w_out, w_ff1=m_w_ff1, w_ff2=m_w_ff2)
    mom2 = dict(norm_g=v_norm_g, w_in=v_w_in, b_in=v_b_in, sgu_ln_g=v_sgu_ln_g, sgu_ln_b=v_sgu_ln_b,
                w_spatial=v_w_spatial, b_spatial=v_b_spatial, conv_w=v_conv_w, a_log=v_a_log, dt_bias=v_dt_bias,
                dn_norm_g=v_dn_norm_g, w_branch=v_w_branch, w_out=v_w_out, w_ff1=v_w_ff1, w_ff2=v_w_ff2)
    shard_names = [n for n, _ in SHARDED]
    shard_shapes = [local[n].shape for n in shard_names]
    repl_shapes = [local[n].shape for n in REPLICATED]

    weights = {n: local[n] for n in REPLICATED}
    for names, dtype, call in ((MATMUL_WEIGHTS, BF16, "gather_matmul_weights"), (VECTOR_WEIGHTS, F32, "gather_vectors")):
        gathered = _gather_chips(_pack([local[n] for n in names]).astype(dtype), name=call)
        per_chip = [_unpack(gathered[k], [local[n].shape for n in names]) for k in range(4)]
        for i, n in enumerate(names):
            weights[n] = jnp.concatenate([per_chip[k][i] for k in range(4)], axis=dict(SHARDED)[n])
    weights["w_in"] = _rearrange_w_in(weights["w_in"])
    weights["b_in"] = _rearrange_w_in(weights["b_in"])

    sq, dx, grads = _local_step(x[0], loss_target[0], weights)
    loss = lax.psum(0.5 * sq / D_MODEL, ("x", "y", "c"))
    grads["w_in"] = _restore_w_in(grads["w_in"])
    grads["b_in"] = _restore_w_in(grads["b_in"])

    blocks = jnp.stack([_pack([_chip_slice(grads[n], axis, k) for n, axis in SHARDED] + [grads[n] for n in REPLICATED])
                        for k in range(4)])
    rows = blocks.shape[1]
    half = rows // 2
    c = lax.axis_index("c")
    got = _halves_to_sibling(blocks, name="grads_to_sibling")
    chip_sum = _add_own_half(blocks, got, name="grads_chip_sum", out_dtype=BF16)
    by_chip = _scatter_chips(chip_sum, name="grads_scatter")
    my_half = _add_rows([by_chip[k] for k in range(4)], name="grads_sum")
    other_half = _to_sibling(my_half, name="grads_half_swap")
    total = jnp.concatenate([jnp.where(c == 0, my_half, other_half), jnp.where(c == 0, other_half, my_half)], axis=0)
    g_out = dict(zip(shard_names + list(REPLICATED), _unpack(total, shard_shapes + repl_shapes)))

    d_out, m_out, v_out = {}, {}, {}
    for n in WEIGHT_ORDER:
        d_out[n], m_out[n], v_out[n] = _adamw(local[n], g_out[n], mom1[n], mom2[n], name=f"adamw_{n}")
    return (loss, dx[None], *[g_out[n] for n in WEIGHT_ORDER], *[d_out[n] for n in WEIGHT_ORDER],
            *[m_out[n] for n in WEIGHT_ORDER], *[v_out[n] for n in WEIGHT_ORDER])
```

```python
import functools

import jax
import jax.numpy as jnp
from jax import lax
from jax.experimental import pallas as pl
from jax.experimental.pallas import tpu as pltpu

F32 = jnp.float32
BF16 = jnp.bfloat16
GRAD_DTYPE = BF16

D_MODEL = 1024
DEPTH = 4
BRANCH_WIDTH = 512
GM_CHUNK = 128
GM_GROUPS = 8
DN_HEADS = 4
DN_HEAD_DIM = 128
CONV_WIDTH = 4
DN_HALO = 8
DN_BLOCK = 128
DN_SOLVE_PASSES, DN_STATE_PASSES, DN_OUT_PASSES = 1, 1, 1
SB_HEAD_DIM = 64
SB_BLOCK = 128
SB_QUERY_ROWS = 256
SB_DEAD_LOG = -88.0
D_FF = 4096
P_IN = 7688
P_PAD = 8192
NORM_EPS = 1e-6
ADAM_LR, ADAM_B1, ADAM_B2, ADAM_EPS, ADAM_WD, ADAM_STEP = 0.001, 0.9, 0.999, 1e-08, 0.01, 10

OFF_DN, DN_IN_WIDTH = 0, 2048
OFF_BD = 1536
OFF_GM = 2048
OFF_GATE = 3072
OFF_BZ = 6144
OFF_CQ, OFF_CK, OFF_CV = 6656, 7168, 7680

LANES = 1024
VMEM_LIMIT_BYTES = 56 * 1024 * 1024

_HI = lax.Precision.HIGHEST


def _params(sem):
    return pltpu.CompilerParams(dimension_semantics=sem, vmem_limit_bytes=VMEM_LIMIT_BYTES)


def _mm(a, b, *, name, out_dtype=F32, bias=None, trans_b=False, beside=(), finish=None, tm=1024, tn=2048, tk=1024):
    m, k = a.shape
    n = b.shape[0] if trans_b else b.shape[1]
    tm, tn, tk = min(tm, m), min(tn, n), min(tk, k)
    assert m % tm == 0 and n % tn == 0 and k % tk == 0, (a.shape, b.shape)
    nk = k // tk
    dn = (((1,), (1,)), ((), ())) if trans_b else (((1,), (0,)), ((), ()))
    several = isinstance(out_dtype, tuple)
    out_dtypes = out_dtype if several else (out_dtype,)
    n_in = 2 + (bias is not None) + len(beside)

    def body(*refs):
        a_ref, b_ref = refs[:2]
        o_refs, acc = refs[n_in:-1], refs[-1]
        kk = pl.program_id(2)
        part = lax.dot_general(a_ref[...].astype(BF16), b_ref[...].astype(BF16), dn, preferred_element_type=F32)

        @pl.when(kk == 0)
        def _():
            acc[...] = part

        @pl.when(kk > 0)
        def _():
            acc[...] += part

        @pl.when(kk == nk - 1)
        def _():
            r = acc[...]
            if bias is not None:
                r = r + refs[2][...]
            tiles = (r,) if finish is None else finish(r, *[t[...] for t in refs[n_in - len(beside):n_in]])
            for o_ref, tile in zip(o_refs, tiles if isinstance(tiles, tuple) else (tiles,)):
                o_ref[...] = tile.astype(o_ref.dtype)

    in_specs = [pl.BlockSpec((tm, tk), lambda i, j, kk: (i, kk))]
    if trans_b:
        in_specs.append(pl.BlockSpec((tn, tk), lambda i, j, kk: (j, kk)))
    else:
        in_specs.append(pl.BlockSpec((tk, tn), lambda i, j, kk: (kk, j)))
    args = [a, b]
    if bias is not None:
        in_specs.append(pl.BlockSpec((1, tn), lambda i, j, kk: (0, j)))
        args.append(bias)
    for t in beside:
        assert t.shape == (m, n)
        in_specs.append(pl.BlockSpec((tm, tn), lambda i, j, kk: (i, j)))
        args.append(t)
    res = pl.pallas_call(
        body, name=name, grid=(m // tm, n // tn, nk),
        in_specs=in_specs, out_specs=[pl.BlockSpec((tm, tn), lambda i, j, kk: (i, j)) for _ in out_dtypes],
        out_shape=[jax.ShapeDtypeStruct((m, n), dt) for dt in out_dtypes],
        scratch_shapes=[pltpu.VMEM((tm, tn), F32)],
        compiler_params=_params(("parallel", "parallel", "arbitrary")),
    )(*args)
    return list(res) if several else res[0]


def _mm_tn(a, b, *, name, col_sums=False, tm=1024, tn=2048, ts=1024):
    out_dtype = GRAD_DTYPE
    s, ka = a.shape
    n = b.shape[1]
    tm, tn, ts = min(tm, ka), min(tn, n), min(ts, s)
    assert ka % tm == 0 and n % tn == 0 and s % ts == 0, (a.shape, b.shape)
    steps = s // ts

    def body(a_ref, b_ref, o_ref, *rest):
        acc = rest[-1]
        step = pl.program_id(2)
        part = lax.dot_general(a_ref[...].astype(BF16), b_ref[...].astype(BF16), (((0,), (0,)), ((), ())),
                               preferred_element_type=F32)

        @pl.when(step == 0)
        def _():
            acc[...] = part

        @pl.when(step > 0)
        def _():
            acc[...] += part

        @pl.when(step == steps - 1)
        def _():
            o_ref[...] = acc[...].astype(out_dtype)

        if col_sums:
            sums_ref = rest[0]
            first_rows = jnp.logical_and(pl.program_id(1) == 0, step == 0)

            @pl.when(first_rows)
            def _():
                sums_ref[...] = jnp.zeros_like(sums_ref)

            @pl.when(pl.program_id(1) == 0)
            def _():
                sums_ref[...] += jnp.sum(b_ref[...].astype(F32), axis=0, keepdims=True)

    out_specs = [pl.BlockSpec((tm, tn), lambda j, i, r: (i, j))]
    out_shape = [jax.ShapeDtypeStruct((ka, n), out_dtype)]
    if col_sums:
        out_specs.append(pl.BlockSpec((1, tn), lambda j, i, r: (0, j)))
        out_shape.append(jax.ShapeDtypeStruct((1, n), F32))
    res = pl.pallas_call(
        body, name=name, grid=(n // tn, ka // tm, steps),
        in_specs=[pl.BlockSpec((ts, tm), lambda j, i, r: (r, i)), pl.BlockSpec((ts, tn), lambda j, i, r: (r, j))],
        out_specs=out_specs, out_shape=out_shape,
        scratch_shapes=[pltpu.VMEM((tm, tn), F32)],
        compiler_params=_params(("parallel", "arbitrary", "arbitrary")),
    )(a, b)
    return list(res) if col_sums else res[0]


def _col_block(i, *, c):
    return (i, c)


def _whole(i, *, nd):
    return (0,) * nd


def _row_specs(rows, ts):
    specs = []
    for arr, off, w in rows:
        assert off % w == 0 and arr.shape[0] % ts == 0
        specs.append(pl.BlockSpec((ts, w), functools.partial(_col_block, c=off // w)))
    return specs


def _row_fwd(fn, rows, params, outs, *, ts, name):
    s = rows[0][0].shape[0]
    ts = min(ts, s)
    nr, npar = len(rows), len(params)

    def body(*refs):
        rv = [r[...].astype(F32) for r in refs[:nr]]
        pv = [p[...] for p in refs[nr:nr + npar]]
        for o_ref, val in zip(refs[nr + npar:], fn(pv, rv)):
            o_ref[...] = val.astype(o_ref.dtype)

    in_specs = _row_specs(rows, ts) + [pl.BlockSpec(p.shape, functools.partial(_whole, nd=p.ndim)) for p in params]
    res = pl.pallas_call(
        body, name=name, grid=(s // ts,), in_specs=in_specs,
        out_specs=[pl.BlockSpec((ts, w), lambda i: (i, 0)) for w, _ in outs],
        out_shape=[jax.ShapeDtypeStruct((s, w), dt) for w, dt in outs],
        compiler_params=_params(("parallel",)),
    )(*[r[0] for r in rows], *params)
    return list(res)


def _row_bwd(fn, rows, params, cts, *, ts, name, into=None, matmul_only=()):
    s = rows[0][0].shape[0]
    ts = min(ts, s)
    nr, npar, nc = len(rows), len(params), len(cts)
    n_in = nr + npar + nc + (into is not None)

    def body(*refs):
        rv = [r[...].astype(F32) for r in refs[:nr]]
        pv = [p[...] for p in refs[nr:nr + npar]]
        cv = [c[...].astype(F32) for c in refs[nr + npar:nr + npar + nc]]
        out_refs = refs[n_in:]
        _, vjp = jax.vjp(lambda p, r: tuple(fn(p, r)), pv, rv)
        dp, dr = vjp(tuple(cv))
        for o_ref, val in zip(out_refs[:nr], dr):
            o_ref[...] = val.astype(o_ref.dtype)

        @pl.when(pl.program_id(0) == 0)
        def _():
            for o_ref in out_refs[nr:]:
                o_ref[...] = jnp.zeros_like(o_ref)

        for o_ref, val in zip(out_refs[nr:], dp):
            o_ref[...] += val

    in_specs = (_row_specs(rows, ts) + [pl.BlockSpec(p.shape, functools.partial(_whole, nd=p.ndim)) for p in params]
                + _row_specs(cts, ts))
    out_specs = [pl.BlockSpec((ts, w), lambda i: (i, 0)) for _, _, w in rows]
    out_shape = [jax.ShapeDtypeStruct((s, w), BF16 if k in matmul_only else F32) for k, (_, _, w) in enumerate(rows)]
    args = [r[0] for r in rows] + list(params) + [c[0] for c in cts]
    aliases = {}
    if into is not None:
        buffer, at = into
        in_specs.append(pl.BlockSpec(memory_space=pl.ANY))
        args.append(buffer)
        out_specs[at] = _row_specs([(buffer,) + tuple(rows[at][1:])], ts)[0]
        out_shape[at] = jax.ShapeDtypeStruct(buffer.shape, buffer.dtype)
        aliases = {n_in - 1: at}
    out_specs += [pl.BlockSpec(p.shape, functools.partial(_whole, nd=p.ndim)) for p in params]
    out_shape += [jax.ShapeDtypeStruct(p.shape, F32) for p in params]
    res = pl.pallas_call(
        body, name=name, grid=(s // ts,), in_specs=in_specs, out_specs=out_specs, out_shape=out_shape,
        input_output_aliases=aliases, compiler_params=_params(("arbitrary",)),
    )(*args)
    res = list(res)
    return res[:nr], res[nr:]


def _rms(x, g):
    return x * lax.rsqrt(jnp.mean(x * x, axis=-1, keepdims=True) + NORM_EPS) * g


def _gelu(x):
    return 0.5 * x * (1.0 + lax.erf(x * (2.0 ** -0.5)))


def _softplus(x):
    return jnp.maximum(x, 0.0) + jnp.log1p(jnp.exp(-jnp.abs(x)))


def _iota2(shape, dim):
    return lax.broadcasted_iota(jnp.int32, shape, dim)


def _fn_rms(pv, rv):
    return [_rms(rv[0], pv[0])]


def _fn_rms_keep(pv, rv):
    return [_rms(rv[0], pv[0]), rv[0]]


def _fn_resid_rms(pv, rv):
    return [rv[0] + _rms(rv[1], pv[0])]


def _fn_rms_branch(pv, rv):
    return [_rms(rv[0], pv[0])]


def _fn_merge(pv, rv):
    return [sum(jax.nn.sigmoid(rv[3][:, i * D_MODEL:(i + 1) * D_MODEL]) * rv[i] for i in range(3))]


def _fn_gmlp(pv, rv):
    ln_g, ln_b, w_sp, b_t = pv
    u = _gelu(rv[0][:, :BRANCH_WIDTH])
    v = _gelu(rv[0][:, BRANCH_WIDTH:])
    vc = v - jnp.mean(v, axis=-1, keepdims=True)
    v = vc * lax.rsqrt(jnp.mean(vc * vc, axis=-1, keepdims=True) + NORM_EPS) * ln_g + ln_b
    t = GM_CHUNK
    causal = _iota2((t, t), 1) <= _iota2((t, t), 0)
    first = _iota2((t, 128), 1) < 64
    expand = (_iota2((128, BRANCH_WIDTH), 0) == _iota2((128, BRANCH_WIDTH), 1) // 64).astype(F32)
    b_full = jnp.dot(b_t, expand, precision=_HI, preferred_element_type=F32)
    w_bf = [jnp.where(causal, w_sp[g], 0.0).astype(BF16) for g in range(GM_GROUPS)]
    chunks = []
    for c in range(rv[0].shape[0] // t):
        pairs = []
        for p in range(GM_GROUPS // 2):
            vp = v[c * t:(c + 1) * t, 128 * p:128 * (p + 1)].astype(BF16)
            m0 = jnp.dot(w_bf[2 * p], vp, preferred_element_type=F32)
            m1 = jnp.dot(w_bf[2 * p + 1], vp, preferred_element_type=F32)
            pairs.append(jnp.where(first, m0, m1))
        chunks.append(jnp.concatenate(pairs, axis=1) + b_full)
    return [u * jnp.concatenate(chunks, axis=0)]


def _head_expand(col0):
    return (_iota2((128, BRANCH_WIDTH), 0) == _iota2((128, BRANCH_WIDTH), 1) // DN_HEAD_DIM + col0).astype(F32)


@functools.partial(jax.custom_vjp, nondiff_argnums=(1,))
def _roll_rows(x, n):
    return pltpu.roll(x, n, 0)


def _roll_rows_fwd(x, n):
    return pltpu.roll(x, n, 0), None


def _roll_rows_bwd(n, _, ct):
    return (pltpu.roll(ct, ct.shape[0] - n, 0),)


_roll_rows.defvjp(_roll_rows_fwd, _roll_rows_bwd)


def _fn_dn_in(pv, rv):
    conv_w, a_log, dt_b = pv
    cur, before = rv
    x = cur[:, :3 * BRANCH_WIDTH]
    ext = jnp.concatenate([before[:, :3 * BRANCH_WIDTH], x], axis=0)
    c = conv_w[CONV_WIDTH - 1:CONV_WIDTH, :] * x
    for j in range(CONV_WIDTH - 1):
        c = c + conv_w[j:j + 1, :] * _roll_rows(ext, CONV_WIDTH - 1 - j)[DN_HALO:]
    a = c * jax.nn.sigmoid(c)
    outs = []
    for part in range(3):
        heads = []
        for h in range(DN_HEADS):
            lo = part * BRANCH_WIDTH + h * DN_HEAD_DIM
            xh = a[:, lo:lo + DN_HEAD_DIM]
            if part < 2:
                xh = xh * lax.rsqrt(jnp.sum(xh * xh, axis=-1, keepdims=True) + NORM_EPS)
            heads.append(xh)
        outs.append(jnp.concatenate(heads, axis=1))
    bd = cur[:, OFF_BD:OFF_BD + 128]
    beta = jax.nn.sigmoid(bd)
    g = -jnp.exp(a_log) * _softplus(bd + dt_b)
    outs.append(jnp.dot(beta, _head_expand(0), precision=_HI, preferred_element_type=F32))
    outs.append(jnp.dot(g, _head_expand(DN_HEADS), precision=_HI, preferred_element_type=F32))
    return outs


def _dn_in_specs(ts, first_block):
    per = ts // DN_HALO
    return [pl.BlockSpec((ts, DN_IN_WIDTH), lambda i: (first_block(i), OFF_DN // DN_IN_WIDTH)),
            pl.BlockSpec((DN_HALO, DN_IN_WIDTH),
                         lambda i: (jnp.maximum(first_block(i) * per - 1, 0), OFF_DN // DN_IN_WIDTH))]


def _dn_in_fwd(p, params, *, ts, name):
    s = p.shape[0]
    ts = min(ts, s)

    def body(cur_ref, before_ref, cw_ref, al_ref, db_ref, *o_refs):
        before = jnp.where(pl.program_id(0) == 0, 0.0, before_ref[...])
        outs = _fn_dn_in([cw_ref[...], al_ref[...], db_ref[...]], [cur_ref[...], before])
        for o_ref, val in zip(o_refs, outs):
            o_ref[...] = val

    whole = [pl.BlockSpec(a.shape, functools.partial(_whole, nd=a.ndim)) for a in params]
    res = pl.pallas_call(
        body, name=name, grid=(s // ts,), in_specs=_dn_in_specs(ts, lambda i: i) + whole,
        out_specs=[pl.BlockSpec((ts, BRANCH_WIDTH), lambda i: (i, 0))] * 5,
        out_shape=[jax.ShapeDtypeStruct((s, BRANCH_WIDTH), F32)] * 5,
        compiler_params=_params(("parallel",)),
    )(p, p, *params)
    return list(res)


def _dn_in_bwd(p, params, cts, dp, *, ts, name):
    s = p.shape[0]
    ts = min(ts, s)
    nb = s // ts
    block = lambda i: nb - 1 - i

    def body(cur_ref, before_ref, cw_ref, al_ref, db_ref, *rest):
        ct_refs, dx_ref, dpar_refs, halo = rest[:5], rest[6], rest[7:10], rest[10]
        first = pl.program_id(0) == 0

        @pl.when(first)
        def _():
            halo[...] = jnp.zeros_like(halo)
            for o_ref in dpar_refs:
                o_ref[...] = jnp.zeros_like(o_ref)

        before = jnp.where(pl.program_id(0) == nb - 1, 0.0, before_ref[...])
        _, vjp = jax.vjp(lambda pv, cur, bef: tuple(_fn_dn_in(pv, [cur, bef])),
                         [cw_ref[...], al_ref[...], db_ref[...]], cur_ref[...], before)
        dpar, dcur, dbefore = vjp(tuple(c[...] for c in ct_refs))
        dx_ref[...] = jnp.concatenate([dcur[:ts - DN_HALO], dcur[ts - DN_HALO:] + halo[...]], axis=0)
        halo[...] = dbefore
        for o_ref, val in zip(dpar_refs, dpar):
            o_ref[...] += val

    whole = [pl.BlockSpec(a.shape, functools.partial(_whole, nd=a.ndim)) for a in params]
    tile = pl.BlockSpec((ts, BRANCH_WIDTH), lambda i: (block(i), 0))
    res = pl.pallas_call(
        body, name=name, grid=(nb,),
        in_specs=_dn_in_specs(ts, block) + whole + [tile] * 5 + [pl.BlockSpec(memory_space=pl.ANY)],
        out_specs=[pl.BlockSpec((ts, DN_IN_WIDTH), lambda i: (block(i), OFF_DN // DN_IN_WIDTH))] + whole,
        out_shape=[jax.ShapeDtypeStruct(dp.shape, dp.dtype)] + [jax.ShapeDtypeStruct(a.shape, F32) for a in params],
        scratch_shapes=[pltpu.VMEM((DN_HALO, DN_IN_WIDTH), F32)],
        input_output_aliases={10: 0}, compiler_params=_params(("arbitrary",)),
    )(p, p, *params, *cts, dp)
    return res[0], list(res[1:])


def _fn_dn_out(pv, rv):
    heads = []
    for h in range(DN_HEADS):
        sl = slice(h * DN_HEAD_DIM, (h + 1) * DN_HEAD_DIM)
        z = rv[1][:, sl]
        heads.append(_rms(rv[0][:, sl], pv[0]) * (z * jax.nn.sigmoid(z)))
    return [jnp.concatenate(heads, axis=1)]


_DIMS = {"nn": (((1,), (0,)), ((), ())), "nt": (((1,), (1,)), ((), ())), "tn": (((0,), (0,)), ((), ()))}
_DIMS_BWD = {"nn": (("nt", "c", "b"), ("tn", "a", "c")), "nt": (("nn", "c", "b"), ("tn", "c", "a")),
             "tn": (("nt", "b", "c"), ("nn", "a", "c"))}


def _bf16_dot(a, b, kind):
    return lax.dot_general(a.astype(BF16), b.astype(BF16), _DIMS[kind], preferred_element_type=F32)


def _pdot_raw(a, b, kind, mode):
    if mode == 1:
        return _bf16_dot(a, b, kind)
    if mode == 6:
        return lax.dot_general(a, b, _DIMS[kind], precision=_HI, preferred_element_type=F32)
    b_hi, b_lo = _split_bf16(b)
    if mode == 3:
        a_hi, a_lo = _split_bf16(a)
        return _bf16_dot(a_hi, b_hi, kind) + (_bf16_dot(a_hi, b_lo, kind) + _bf16_dot(a_lo, b_hi, kind))
    b_rest = (b - b_hi.astype(F32) - b_lo.astype(F32)).astype(BF16)
    return _bf16_dot(a, b_hi, kind) + (_bf16_dot(a, b_lo, kind) + _bf16_dot(a, b_rest, kind))


@functools.partial(jax.custom_vjp, nondiff_argnums=(2, 3))
def _pdot(a, b, kind, mode):
    return _pdot_raw(a, b, kind, mode)


def _pdot_fwd(a, b, kind, mode):
    return _pdot_raw(a, b, kind, mode), (a, b)


def _pdot_bwd(kind, mode, res, ct):
    ops = {"a": res[0], "b": res[1], "c": ct}
    (ka, a1, a2), (kb, b1, b2) = _DIMS_BWD[kind]
    if mode == "count":
        return jnp.zeros_like(res[0]), _pdot(ops[b1], ops[b2], kb, mode)
    return _pdot(ops[a1], ops[a2], ka, mode), _pdot(ops[b1], ops[b2], kb, mode)


_pdot.defvjp(_pdot_fwd, _pdot_bwd)


@jax.custom_vjp
def _unit_lower_inverses(mats):
    c = mats[0].shape[0]
    row, col = _iota2((c, c), 0), _iota2((c, c), 1)
    x = [(row == col).astype(F32) for _ in mats]
    shift = 0
    while (1 << shift) < c:
        pair = jnp.right_shift(row, shift + 1) == jnp.right_shift(col, shift + 1)
        between = pair & (jnp.right_shift(row, shift) != jnp.right_shift(col, shift))
        q = [jnp.where(between, a, 0.0) for a in mats]
        qd = [_pdot_raw(qi, xi, "nn", DN_SOLVE_PASSES) for qi, xi in zip(q, x)]
        x = [xi - _pdot_raw(xi, m, "nn", DN_SOLVE_PASSES) for xi, m in zip(x, qd)]
        shift += 1
    return tuple(x)


def _unit_lower_inverses_fwd(mats):
    x = _unit_lower_inverses(mats)
    return x, x


def _unit_lower_inverses_bwd(x, dx):
    inner = [_pdot_raw(d, xi, "nt", DN_SOLVE_PASSES) for d, xi in zip(dx, x)]
    return (tuple(-_pdot_raw(xi, m, "tn", DN_SOLVE_PASSES) for xi, m in zip(x, inner)),)


_unit_lower_inverses.defvjp(_unit_lower_inverses_fwd, _unit_lower_inverses_bwd)


def _head(h):
    return slice(h * DN_HEAD_DIM, (h + 1) * DN_HEAD_DIM)


def _delta_chunk(states, q, k, v, beta, g):
    c = DN_BLOCK
    heads = range(DN_HEADS)
    row, col = _iota2((c, c), 0), _iota2((c, c), 1)
    tri, strict = col <= row, col < row
    counts = jnp.concatenate([tri.astype(BF16), jnp.ones((c, c), BF16)], axis=0)
    sums = _pdot(counts, g, "nn", "count")
    gc = [sums[:c, _head(h)] for h in heads]
    gl = [sums[c:, _head(h)] for h in heads]
    qh, kh, vh, bh = ([t[:, _head(h)] for h in heads] for t in (q * (DN_HEAD_DIM ** -0.5), k, v, beta))
    decay = [jnp.where(tri, jnp.exp(jnp.where(tri, gc[h] - gc[h].T, 0.0)), 0.0) for h in heads]
    solve, carry, out = DN_SOLVE_PASSES, DN_STATE_PASSES, DN_OUT_PASSES
    kk = [_pdot(kh[h], kh[h], "nt", solve) for h in heads]
    x = _unit_lower_inverses(tuple(jnp.where(strict, bh[h] * kk[h] * decay[h], 0.0) for h in heads))
    eg = [jnp.exp(gc[h]) for h in heads]
    u = [_pdot(x[h], vh[h] * bh[h], "nn", solve) for h in heads]
    wk = [_pdot(x[h], kh[h] * (bh[h] * eg[h]), "nn", solve) for h in heads]
    qk = [jnp.where(tri, _pdot(qh[h], kh[h], "nt", out) * decay[h], 0.0) for h in heads]
    v_new = [u[h] - _pdot(wk[h], states[h], "nn", carry) for h in heads]
    o = [_pdot(qh[h] * eg[h], states[h], "nn", out) + _pdot(qk[h], v_new[h], "nn", out) for h in heads]
    nxt = [states[h] * jnp.exp(gl[h]) + _pdot(kh[h] * jnp.exp(gl[h] - gc[h]), v_new[h], "tn", carry)
           for h in heads]
    return jnp.concatenate(o, axis=1), tuple(nxt)


def _delta_fwd(q, k, v, beta, g, *, name):
    s = q.shape[0]
    c = DN_BLOCK
    nc = s // c

    def body(q_ref, k_ref, v_ref, b_ref, g_ref, o_ref, sp_ref, st):
        @pl.when(pl.program_id(0) == 0)
        def _():
            st[...] = jnp.zeros_like(st)

        states = tuple(st[h] for h in range(DN_HEADS))
        for h in range(DN_HEADS):
            sp_ref[0, h] = states[h]
        o, nxt = _delta_chunk(states, q_ref[...], k_ref[...], v_ref[...], b_ref[...], g_ref[...])
        for h in range(DN_HEADS):
            st[h] = nxt[h]
        o_ref[...] = o

    blk = pl.BlockSpec((c, BRANCH_WIDTH), lambda n: (n, 0))
    return pl.pallas_call(
        body, name=name, grid=(nc,), in_specs=[blk] * 5,
        out_specs=[blk, pl.BlockSpec((1, DN_HEADS, DN_HEAD_DIM, DN_HEAD_DIM), lambda n: (n, 0, 0, 0))],
        out_shape=[jax.ShapeDtypeStruct((s, BRANCH_WIDTH), F32),
                   jax.ShapeDtypeStruct((nc, DN_HEADS, DN_HEAD_DIM, DN_HEAD_DIM), F32)],
        scratch_shapes=[pltpu.VMEM((DN_HEADS, DN_HEAD_DIM, DN_HEAD_DIM), F32)],
        compiler_params=_params(("arbitrary",)),
    )(q, k, v, beta, g)


def _delta_bwd(q, k, v, beta, g, states, do, *, name):
    s = q.shape[0]
    c = DN_BLOCK
    nc = s // c

    def body(q_ref, k_ref, v_ref, b_ref, g_ref, sp_ref, do_ref, dq_ref, dk_ref, dv_ref, db_ref, dg_ref, dst):
        @pl.when(pl.program_id(0) == 0)
        def _():
            dst[...] = jnp.zeros_like(dst)

        states = tuple(sp_ref[0, h] for h in range(DN_HEADS))
        _, vjp = jax.vjp(_delta_chunk, states, q_ref[...], k_ref[...], v_ref[...], b_ref[...], g_ref[...])
        d = vjp((do_ref[...], tuple(dst[h] for h in range(DN_HEADS))))
        for h in range(DN_HEADS):
            dst[h] = d[0][h]
        for o_ref, val in zip((dq_ref, dk_ref, dv_ref, db_ref, dg_ref), d[1:]):
            o_ref[...] = val

    blk = pl.BlockSpec((c, BRANCH_WIDTH), lambda n: (nc - 1 - n, 0))
    res = pl.pallas_call(
        body, name=name, grid=(nc,),
        in_specs=[blk] * 5 + [pl.BlockSpec((1, DN_HEADS, DN_HEAD_DIM, DN_HEAD_DIM), lambda n: (nc - 1 - n, 0, 0, 0)), blk],
        out_specs=[blk] * 5, out_shape=[jax.ShapeDtypeStruct((s, BRANCH_WIDTH), F32)] * 5,
        scratch_shapes=[pltpu.VMEM((DN_HEADS, DN_HEAD_DIM, DN_HEAD_DIM), F32)],
        compiler_params=_params(("arbitrary",)),
    )(q, k, v, beta, g, states, do)
    return list(res)


def _split_bf16(x):
    hi = x.astype(BF16)
    return hi, (x - hi.astype(F32)).astype(BF16)


def _sb_consts():
    tq, tk = SB_QUERY_ROWS, SB_BLOCK
    row, col = _iota2((tk, tk), 0), _iota2((tk, tk), 1)
    ones = jnp.ones((tk, tk), BF16)
    later = jnp.concatenate([(row > col).astype(BF16), ones], axis=1)
    from_here = jnp.concatenate([(row >= col).astype(BF16), ones], axis=1)
    first = _iota2((tq, 128), 1) < SB_HEAD_DIM
    return later, from_here, first


def _sb_causal(d):
    tq, tk = SB_QUERY_ROWS, SB_BLOCK
    return _iota2((tq - d * tk, tk), 1) < _iota2((tq - d * tk, tk), 0)


def _sums(x, mat):
    hi, lo = _split_bf16(x)
    return jnp.dot(hi, mat, preferred_element_type=F32) + jnp.dot(lo, mat, preferred_element_type=F32)


def _sb_weights(qs, kb, accs, later, causal):
    tk = SB_BLOCK
    z = [lax.dot_general(qh, kb, (((1,), (1,)), ((), ())), preferred_element_type=F32) for qh in qs]
    lk = [-_softplus(zh) for zh in z]
    if causal is not None:
        lk = [jnp.where(causal, v, 0.0) for v in lk]
    cs = [_sums(v, later) for v in lk]
    e = [z[h] + lk[h] + cs[h][:, :tk] + accs[h] for h in range(2)]
    if causal is not None:
        e = [jnp.where(causal, v, -1e30) for v in e]
    return lk, [jnp.exp(v) for v in e], [v[:, tk:] for v in cs]


def _sb_alive(accs):
    return jnp.max(jnp.maximum(accs[0], accs[1])) > SB_DEAD_LOG


def _sb_sweep(i, block, carry, accs_of, stop=None):
    per = SB_QUERY_ROWS // SB_BLOCK
    for d in reversed(range(per)):
        r0 = d * SB_BLOCK
        seen = block(i * per + d, jax.tree.map(lambda a: a[r0:], carry), _sb_causal(d), r0)
        carry = seen if r0 == 0 else jax.tree.map(lambda old, new: jnp.concatenate([old[:r0], new], axis=0), carry, seen)
    if stop is not None:
        return lax.fori_loop(0, i * per - 1 - stop, lambda jj, cr: block(i * per - 1 - jj, cr, None), carry)

    def step(state):
        j, _, cr = state
        cr = block(j, cr, None)
        return j - 1, _sb_alive(accs_of(cr)), cr

    j, _, carry = lax.while_loop(lambda st: jnp.logical_and(st[0] >= 0, st[1]), step,
                                 (i * per - 1, _sb_alive(accs_of(carry)), carry))
    return carry, j


def _sb_fwd(p, *, name):
    s = p.shape[0]
    t, tk = SB_QUERY_ROWS, SB_BLOCK
    scale = SB_HEAD_DIM ** -0.5

    def body(q_ref, k_ref, v_ref, o_ref):
        i = pl.program_id(1)
        later, _, first = _sb_consts()
        q = q_ref[...] * scale
        qs = (jnp.where(first, q, 0.0).astype(BF16), jnp.where(first, 0.0, q).astype(BF16))

        def block(j, carry, causal, r0=0):
            start = pl.multiple_of(j * tk, tk)
            kb = k_ref[pl.ds(start, tk), :].astype(BF16)
            vb = v_ref[pl.ds(start, tk), :].astype(BF16)
            _, w, tot = _sb_weights([qh[r0:] for qh in qs], kb, [carry[h][1] for h in range(2)], later, causal)
            out = [jnp.dot(w[h].astype(BF16), vb, preferred_element_type=F32) for h in range(2)]
            return tuple((carry[h][0] + out[h], carry[h][1] + tot[h]) for h in range(2))

        zero = jnp.zeros((t, 128), F32)
        carry, _ = _sb_sweep(i, block, ((zero, zero), (zero, zero)), lambda cr: (cr[0][1], cr[1][1]))
        o_ref[...] = jnp.where(first, carry[0][0], carry[1][0])

    return pl.pallas_call(
        body, name=name, grid=(BRANCH_WIDTH // 128, s // t),
        in_specs=[pl.BlockSpec((t, 128), lambda pr, i: (i, OFF_CQ // 128 + pr)),
                  pl.BlockSpec((s, 128), lambda pr, i: (0, OFF_CK // 128 + pr)),
                  pl.BlockSpec((s, 128), lambda pr, i: (0, OFF_CV // 128 + pr))],
        out_specs=pl.BlockSpec((t, 128), lambda pr, i: (i, pr)),
        out_shape=jax.ShapeDtypeStruct((s, BRANCH_WIDTH), F32),
        compiler_params=_params(("arbitrary", "arbitrary")),
    )(p, p, p)


def _sb_bwd(p, do, *, name):
    s = p.shape[0]
    t, tk = SB_QUERY_ROWS, SB_BLOCK
    scale = SB_HEAD_DIM ** -0.5

    def body(q_ref, k_ref, v_ref, do_ref, dq_ref, dk_ref, dv_ref):
        i = pl.program_id(1)

        @pl.when(i == 0)
        def _():
            dk_ref[...] = jnp.zeros_like(dk_ref)
            dv_ref[...] = jnp.zeros_like(dv_ref)

        later, from_here, first = _sb_consts()
        q = q_ref[...] * scale
        do = do_ref[...]
        qs = (jnp.where(first, q, 0.0).astype(BF16), jnp.where(first, 0.0, q).astype(BF16))
        dos = (jnp.where(first, do, 0.0).astype(BF16), jnp.where(first, 0.0, do).astype(BF16))

        def total(j, carry, causal, r0=0):
            start = pl.multiple_of(j * tk, tk)
            kb = k_ref[pl.ds(start, tk), :].astype(BF16)
            vb = v_ref[pl.ds(start, tk), :].astype(BF16)
            _, w, tot = _sb_weights([qh[r0:] for qh in qs], kb, [carry[h][0] for h in range(2)], later, causal)
            dw = [lax.dot_general(dos[h][r0:], vb, (((1,), (1,)), ((), ())), preferred_element_type=F32)
                  for h in range(2)]
            tde = [_sums(dw[h] * w[h], from_here)[:, tk:] for h in range(2)]
            return tuple((carry[h][0] + tot[h], carry[h][1] + tde[h]) for h in range(2))

        zero = jnp.zeros((t, 128), F32)
        sums, stop = _sb_sweep(i, total, ((zero, zero), (zero, zero)), lambda cr: (cr[0][0], cr[1][0]))
        deltas = (sums[0][1], sums[1][1])

        def block(j, carry, causal, r0=0):
            start = pl.multiple_of(j * tk, tk)
            kb = k_ref[pl.ds(start, tk), :].astype(BF16)
            vb = v_ref[pl.ds(start, tk), :].astype(BF16)
            both = range(2)
            tn = (((0,), (0,)), ((), ()))
            qr, dor, total_de = [x[r0:] for x in qs], [x[r0:] for x in dos], [x[r0:] for x in deltas]
            lk, w, tot = _sb_weights(qr, kb, [carry[h][1] for h in both], later, causal)
            dw = [lax.dot_general(dor[h], vb, (((1,), (1,)), ((), ())), preferred_element_type=F32) for h in both]
            de = [dw[h] * w[h] for h in both]
            cs = [_sums(de[h], from_here) for h in both]
            keep = [jnp.exp(lk[h]) for h in both]
            dz = [de[h] * keep[h] - (total_de[h] - (cs[h][:, :tk] + carry[h][2])) * (1.0 - keep[h]) for h in both]
            if causal is not None:
                dz = [jnp.where(causal, v, 0.0) for v in dz]
            dzb = [v.astype(BF16) for v in dz]
            dq = [jnp.dot(dzb[h], kb, preferred_element_type=F32) for h in both]
            dk = [lax.dot_general(dzb[h], qr[h], tn, preferred_element_type=F32) for h in both]
            dv = [lax.dot_general(w[h].astype(BF16), dor[h], tn, preferred_element_type=F32) for h in both]
            dk_ref[pl.ds(start, tk), :] += dk[0] + dk[1]
            dv_ref[pl.ds(start, tk), :] += dv[0] + dv[1]
            return tuple((carry[h][0] + dq[h], carry[h][1] + tot[h], carry[h][2] + cs[h][:, tk:]) for h in both)

        carry = _sb_sweep(i, block, ((zero, zero, zero), (zero, zero, zero)), None, stop=stop)
        dq_ref[...] = jnp.where(first, carry[0][0], carry[1][0]) * scale

    qblk = lambda off: pl.BlockSpec((t, 128), lambda pr, i: (i, off // 128 + pr))
    full = lambda off: pl.BlockSpec((s, 128), lambda pr, i: (0, off // 128 + pr))
    res = pl.pallas_call(
        body, name=name, grid=(BRANCH_WIDTH // 128, s // t),
        in_specs=[qblk(OFF_CQ), full(OFF_CK), full(OFF_CV), qblk(0)],
        out_specs=[qblk(0), full(0), full(0)],
        out_shape=[jax.ShapeDtypeStruct((s, BRANCH_WIDTH), F32)] * 3,
        compiler_params=_params(("arbitrary", "arbitrary")),
    )(p, p, p, do)
    return list(res)


def _loss_head(y, target, *, name, ts=512):
    s, d = y.shape
    ts = min(ts, s)

    def body(y_ref, t_ref, sq_ref, dy_ref):
        @pl.when(pl.program_id(0) == 0)
        def _():
            sq_ref[...] = jnp.zeros_like(sq_ref)

        err = y_ref[...] - t_ref[...]
        dy_ref[...] = err * (1.0 / d)
        tot = jnp.sum(jnp.sum(err * err, axis=1, keepdims=True), axis=0, keepdims=True)
        sq_ref[...] += jnp.broadcast_to(tot, sq_ref.shape)

    blk = pl.BlockSpec((ts, d), lambda i: (i, 0))
    return pl.pallas_call(
        body, name=name, grid=(s // ts,), in_specs=[blk, blk],
        out_specs=[pl.BlockSpec((1, 128), lambda i: (0, 0)), blk],
        out_shape=[jax.ShapeDtypeStruct((1, 128), F32), jax.ShapeDtypeStruct((s, d), F32)],
        compiler_params=_params(("arbitrary",)),
    )(y, target)


def _row_tile(r, limit=512):
    return max(t for t in range(16, limit + 1, 16) if r % t == 0)


def _adamw(w, g, m, v, *, name):
    shape = w.shape
    lanes = shape[-1]
    w, g, m, v = (a.reshape(-1, lanes) for a in (w, g, m, v))
    r = w.shape[0]
    ts = r if r <= 256 else _row_tile(r, 256 if lanes > LANES else 512)

    def body(w_ref, g_ref, m_ref, v_ref, d_ref, nm_ref, nv_ref):
        gv = g_ref[...]
        m_new = ADAM_B1 * m_ref[...] + (1.0 - ADAM_B1) * gv
        v_new = ADAM_B2 * v_ref[...] + (1.0 - ADAM_B2) * jnp.square(gv)
        m_hat = m_new / (1.0 - ADAM_B1 ** ADAM_STEP)
        v_hat = v_new / (1.0 - ADAM_B2 ** ADAM_STEP)
        d_ref[...] = -ADAM_LR * (m_hat / (jnp.sqrt(v_hat) + ADAM_EPS) + ADAM_WD * w_ref[...])
        nm_ref[...] = m_new
        nv_ref[...] = v_new

    blk = pl.BlockSpec((ts, lanes), lambda i: (i, 0))
    res = pl.pallas_call(
        body, name=name, grid=(r // ts,), in_specs=[blk] * 4, out_specs=[blk] * 3,
        out_shape=[jax.ShapeDtypeStruct((r, lanes), F32)] * 3,
        compiler_params=_params(("parallel",)),
    )(w, g, m, v)
    return [a.reshape(shape) for a in res]


def _add_rows(terms, *, name, out_dtype=F32):
    r = terms[0].shape[0]
    ts = _row_tile(r)

    def body(*refs):
        acc = refs[0][...].astype(F32)
        for ref in refs[1:-1]:
            acc = acc + ref[...].astype(F32)
        refs[-1][...] = acc.astype(out_dtype)

    blk = pl.BlockSpec((ts, LANES), lambda i: (i, 0))
    return pl.pallas_call(
        body, name=name, grid=(r // ts,), in_specs=[blk] * len(terms), out_specs=blk,
        out_shape=jax.ShapeDtypeStruct((r, LANES), out_dtype), compiler_params=_params(("parallel",)),
    )(*terms)


_HBM = pl.BlockSpec(memory_space=pltpu.HBM)
_MESH = pl.DeviceIdType.MESH


def _other_chips(x, y):
    return [(1 - x, y), (x, 1 - y), (1 - x, 1 - y)]


def _gather_chips(shard, *, name):
    r, lanes = shard.shape
    half = r // 2
    assert half * 2 == r

    def body(in_ref, out_ref, send_sems, recv_sems):
        x, y, c = lax.axis_index("x"), lax.axis_index("y"), lax.axis_index("c")
        me = 2 * x + y
        sibling = (x, y, 1 - c)
        chips = _other_chips(x, y)

        def copy(sem, chip, core_half, to):
            rows = out_ref.at[chip, pl.ds(core_half * half, half)]
            return pltpu.make_async_remote_copy(src_ref=rows, dst_ref=rows, send_sem=send_sems.at[sem],
                                                recv_sem=recv_sems.at[sem], device_id=to, device_id_type=_MESH)

        first = []
        for kk, (px, py) in enumerate(chips):
            cp = pltpu.make_async_remote_copy(
                src_ref=in_ref.at[pl.ds(c * half, half)], dst_ref=out_ref.at[me, pl.ds(c * half, half)],
                send_sem=send_sems.at[kk], recv_sem=recv_sems.at[kk], device_id=(px, py, c), device_id_type=_MESH)
            cp.start()
            first.append(cp)
        passed = [copy(3 + kk, 2 * px + py, c, sibling) for kk, (px, py) in enumerate(chips)]
        for kk, (px, py) in enumerate(chips):
            copy(kk, 2 * px + py, c, (px, py, c)).wait_recv()
            passed[kk].start()
        for kk, (px, py) in enumerate(chips):
            copy(3 + kk, 2 * px + py, 1 - c, sibling).wait_recv()
        for cp in first + passed:
            cp.wait_send()

    gathered = pl.pallas_call(
        body, name=name, in_specs=[_HBM], out_specs=_HBM,
        out_shape=jax.ShapeDtypeStruct((4, r, lanes), shard.dtype),
        scratch_shapes=[pltpu.SemaphoreType.DMA((6,)), pltpu.SemaphoreType.DMA((6,))],
    )(shard)
    me = 2 * lax.axis_index("x") + lax.axis_index("y")
    return lax.dynamic_update_slice(gathered, shard[None], (me, 0, 0))


def _to_sibling(block, *, name):
    def body(in_ref, out_ref, send_sem, recv_sem):
        x, y, c = lax.axis_index("x"), lax.axis_index("y"), lax.axis_index("c")
        cp = pltpu.make_async_remote_copy(src_ref=in_ref, dst_ref=out_ref, send_sem=send_sem, recv_sem=recv_sem,
                                          device_id=(x, y, 1 - c), device_id_type=_MESH)
        cp.start()
        cp.wait()

    return pl.pallas_call(
        body, name=name, in_specs=[_HBM], out_specs=_HBM, out_shape=jax.ShapeDtypeStruct(block.shape, block.dtype),
        scratch_shapes=[pltpu.SemaphoreType.DMA, pltpu.SemaphoreType.DMA],
    )(block)


def _halves_to_sibling(blocks, *, name):
    n, r, lanes = blocks.shape
    half = r // 2

    def body(in_ref, out_ref, send_sem, recv_sem):
        x, y, c = lax.axis_index("x"), lax.axis_index("y"), lax.axis_index("c")
        cp = pltpu.make_async_remote_copy(src_ref=in_ref.at[pl.ds(0, n), pl.ds((1 - c) * half, half)], dst_ref=out_ref,
                                          send_sem=send_sem, recv_sem=recv_sem, device_id=(x, y, 1 - c),
                                          device_id_type=_MESH)
        cp.start()
        cp.wait()

    return pl.pallas_call(
        body, name=name, in_specs=[_HBM], out_specs=_HBM, out_shape=jax.ShapeDtypeStruct((n, half, lanes), blocks.dtype),
        scratch_shapes=[pltpu.SemaphoreType.DMA, pltpu.SemaphoreType.DMA],
    )(blocks)


def _add_own_half(blocks, got, *, name, out_dtype):
    n, r, lanes = blocks.shape
    half = r // 2
    ts = _row_tile(half)
    steps = half // ts
    core = lax.axis_index("c").astype(jnp.int32).reshape(1)

    def body(core_ref, a_ref, b_ref, o_ref):
        o_ref[...] = (a_ref[...].astype(F32) + b_ref[...].astype(F32)).astype(out_dtype)

    return pl.pallas_call(
        body, name=name,
        grid_spec=pltpu.PrefetchScalarGridSpec(
            num_scalar_prefetch=1, grid=(n, steps),
            in_specs=[pl.BlockSpec((1, ts, lanes), lambda k, i, core_ref: (k, core_ref[0] * steps + i, 0)),
                      pl.BlockSpec((1, ts, lanes), lambda k, i, core_ref: (k, i, 0))],
            out_specs=pl.BlockSpec((1, ts, lanes), lambda k, i, core_ref: (k, i, 0))),
        out_shape=jax.ShapeDtypeStruct((n, half, lanes), out_dtype),
        compiler_params=_params(("parallel", "parallel")),
    )(core, blocks, got)


def _scatter_chips(blocks, *, name):
    _, r, lanes = blocks.shape

    def body(in_ref, out_ref, send_sems, recv_sems, local_sem):
        x, y, c = lax.axis_index("x"), lax.axis_index("y"), lax.axis_index("c")
        me = 2 * x + y
        mine = pltpu.make_async_copy(in_ref.at[me], out_ref.at[me], local_sem)
        mine.start()
        copies = []
        for kk, (px, py) in enumerate(_other_chips(x, y)):
            cp = pltpu.make_async_remote_copy(src_ref=in_ref.at[2 * px + py], dst_ref=out_ref.at[me],
                                              send_sem=send_sems.at[kk], recv_sem=recv_sems.at[kk],
                                              device_id=(px, py, c), device_id_type=_MESH)
            cp.start()
            copies.append(cp)
        for kk, (px, py) in enumerate(_other_chips(x, y)):
            pltpu.make_async_remote_copy(src_ref=in_ref.at[me], dst_ref=out_ref.at[2 * px + py],
                                         send_sem=send_sems.at[kk], recv_sem=recv_sems.at[kk], device_id=(px, py, c),
                                         device_id_type=_MESH).wait_recv()
        for cp in copies:
            cp.wait_send()
        mine.wait()

    return pl.pallas_call(
        body, name=name, in_specs=[_HBM], out_specs=_HBM, out_shape=jax.ShapeDtypeStruct((4, r, lanes), blocks.dtype),
        scratch_shapes=[pltpu.SemaphoreType.DMA((3,)), pltpu.SemaphoreType.DMA((3,)), pltpu.SemaphoreType.DMA],
    )(blocks)


SHARDED = (("norm_g", 2), ("w_in", 2), ("conv_w", 2), ("w_branch", 3), ("w_out", 1), ("w_ff1", 2), ("w_ff2", 1))
MATMUL_WEIGHTS = ("w_in", "w_branch", "w_out", "w_ff1", "w_ff2")
VECTOR_WEIGHTS = ("norm_g", "conv_w")
REPLICATED = ("b_in", "sgu_ln_g", "sgu_ln_b", "w_spatial", "b_spatial", "a_log", "dt_bias", "dn_norm_g")
WEIGHT_ORDER = ("norm_g", "w_in", "b_in", "sgu_ln_g", "sgu_ln_b", "w_spatial", "b_spatial", "conv_w", "a_log",
                "dt_bias", "dn_norm_g", "w_branch", "w_out", "w_ff1", "w_ff2")
PACK_ROW_MULTIPLE = 32


def _size(shape):
    n = 1
    for dim in shape:
        n *= dim
    return n


def _wide(shape):
    return len(shape) >= 2 and shape[-1] > LANES


def _rows_of(shape):
    if _wide(shape):
        return (_size(shape) // shape[-1]) * -(-shape[-1] // LANES)
    return -(-_size(shape) // LANES)


def _pack(arrays):
    parts = []
    for a in arrays:
        if _wide(a.shape):
            rows = a.reshape(-1, a.shape[-1])
            pad = -a.shape[-1] % LANES
            if pad:
                rows = jnp.concatenate([rows, jnp.zeros((rows.shape[0], pad), a.dtype)], axis=1)
            parts += [rows[:, c:c + LANES] for c in range(0, rows.shape[1], LANES)]
            continue
        flat = a.reshape(-1)
        pad = _rows_of(a.shape) * LANES - flat.shape[0]
        if pad:
            flat = jnp.concatenate([flat, jnp.zeros((pad,), flat.dtype)])
        parts.append(flat.reshape(-1, LANES))
    rows = sum(p.shape[0] for p in parts)
    pad = -rows % PACK_ROW_MULTIPLE
    if pad:
        parts.append(jnp.zeros((pad, LANES), parts[0].dtype))
    return jnp.concatenate(parts, axis=0)


def _unpack(buf, shapes):
    out, row = [], 0
    for shape in shapes:
        rows = _rows_of(shape)
        part = buf[row:row + rows]
        if _wide(shape):
            chunks = -(-shape[-1] // LANES)
            each = rows // chunks
            whole = jnp.concatenate([part[c * each:(c + 1) * each] for c in range(chunks)], axis=1)
            out.append(whole[:, :shape[-1]].reshape(shape))
        else:
            out.append(part.reshape(-1)[:_size(shape)].reshape(shape))
        row += rows
    return out


def _chip_slice(a, axis, k):
    size = a.shape[axis] // 4
    return lax.slice_in_dim(a, k * size, (k + 1) * size, axis=axis)


def _rearrange_w_in(w):
    pad = jnp.zeros(w.shape[:-1] + (P_PAD - P_IN,), w.dtype)
    return jnp.concatenate([w[..., 1024:2560], w[..., 3072:3080], pad, w[..., 0:1024], w[..., 4616:7688],
                            w[..., 2560:3072], w[..., 3080:4616]], axis=-1)


def _restore_w_in(w):
    return jnp.concatenate([w[..., 2048:3072], w[..., 0:1536], w[..., 6144:6656], w[..., 1536:1544],
                            w[..., 6656:8192], w[..., 3072:6144]], axis=-1)


def _layer_params(wl):
    row = lambda v: v.reshape(1, -1)
    pad128 = lambda v, at: jnp.pad(v, (at, 128 - at - v.shape[0])).reshape(1, 128)
    return dict(
        g=[row(wl["norm_g"][i]) for i in range(4)],
        gmlp=[row(wl["sgu_ln_g"]), row(wl["sgu_ln_b"]), wl["w_spatial"],
              jnp.pad(wl["b_spatial"].T, ((0, 0), (0, 128 - GM_GROUPS)))],
        dn_in=[wl["conv_w"], pad128(wl["a_log"], DN_HEADS), pad128(wl["dt_bias"], DN_HEADS)],
        dn_g=[row(wl["dn_norm_g"])],
    )


def _layer_fwd(x0, wl, l):
    tag = lambda s: f"{s}_l{l}"
    pr = _layer_params(wl)
    w = BRANCH_WIDTH
    h0 = _row_fwd(_fn_rms, [(x0, 0, D_MODEL)], [pr["g"][0]], [(D_MODEL, BF16)], ts=512, name=tag("rms0"))[0]
    p = _mm(h0, wl["w_in"], bias=wl["b_in"].reshape(1, -1), name=tag("proj_in"))
    ya = _row_fwd(_fn_gmlp, [(p, OFF_GM, 2 * w)], pr["gmlp"], [(w, BF16)], ts=256, name=tag("gmlp"))[0]
    q, k, v, beta, g = _dn_in_fwd(p, pr["dn_in"], ts=256, name=tag("dn_in"))
    o, states = _delta_fwd(q, k, v, beta, g, name=tag("delta"))
    yb = _row_fwd(_fn_dn_out, [(o, 0, w), (p, OFF_BZ, w)], pr["dn_g"], [(w, BF16)], ts=512, name=tag("dn_out"))[0]
    yc = _sb_fwd(p, name=tag("sb"))
    ys = [ya, yb, yc]
    proj = [_mm(ys[i], wl["w_branch"][i], out_dtype=BF16, name=tag(f"branch{i}")) for i in range(3)]
    merge_rows = [(a, 0, D_MODEL) for a in proj] + [(p, OFF_GATE, 3 * D_MODEL)]
    m = _row_fwd(_fn_merge, merge_rows, [], [(D_MODEL, BF16)], ts=256, name=tag("merge"))[0]
    mixed = _mm(m, wl["w_out"], name=tag("out"))
    x1 = _row_fwd(_fn_resid_rms, [(x0, 0, D_MODEL), (mixed, 0, D_MODEL)], [pr["g"][1]], [(D_MODEL, F32)], ts=512,
                  name=tag("resid1"))[0]
    h2 = _row_fwd(_fn_rms, [(x1, 0, D_MODEL)], [pr["g"][2]], [(D_MODEL, BF16)], ts=512, name=tag("rms2"))[0]
    a, r = _mm(h2, wl["w_ff1"], name=tag("ff1"), out_dtype=(BF16, BF16),
               finish=lambda t: (jnp.maximum(t, 0.0), jnp.square(jnp.maximum(t, 0.0))))
    f = _mm(r, wl["w_ff2"], name=tag("ff2"))
    x2 = _row_fwd(_fn_resid_rms, [(x1, 0, D_MODEL), (f, 0, D_MODEL)], [pr["g"][3]], [(D_MODEL, F32)], ts=512,
                  name=tag("resid2"))[0]
    saved = dict(x0=x0, h0=h0, p=p, q=q, k=k, v=v, beta=beta, g=g, states=states, o=o, ys=ys,
                 proj=proj, m=m, mixed=mixed, x1=x1, h2=h2, a=a, r=r, f=f)
    return x2, saved


def _layer_bwd(dx2, sv, wl, l):
    tag = lambda s: f"{s}_l{l}"
    pr = _layer_params(wl)
    w = BRANCH_WIDTH
    full = lambda a: (a, 0, a.shape[1])
    p = sv["p"]
    (df,), (dg3,) = _row_bwd(_fn_rms_branch, [full(sv["f"])], [pr["g"][3]], [full(dx2)], ts=512, name=tag("resid2_b"),
                             matmul_only=(0,))
    da = _mm(df, wl["w_ff2"], trans_b=True, name=tag("ff2_dx"), out_dtype=BF16, beside=(sv["a"],),
             finish=lambda t, relu_a: 2.0 * relu_a.astype(F32) * t)
    dw_ff2 = _mm_tn(sv["r"], df, name=tag("ff2_dw"))
    dh2 = _mm(da, wl["w_ff1"], trans_b=True, name=tag("ff1_dx"))
    dw_ff1 = _mm_tn(sv["h2"], da, name=tag("ff1_dw"))
    (dx1,), (dg2,) = _row_bwd(_fn_rms_keep, [full(sv["x1"])], [pr["g"][2]], [full(dh2), full(dx2)], ts=512,
                              name=tag("rms2_b"))
    (dmixed,), (dg1,) = _row_bwd(_fn_rms_branch, [full(sv["mixed"])], [pr["g"][1]], [full(dx1)], ts=512,
                                 name=tag("resid1_b"), matmul_only=(0,))
    dm = _mm(dmixed, wl["w_out"], trans_b=True, name=tag("out_dx"))
    dw_out = _mm_tn(sv["m"], dmixed, name=tag("out_dw"))
    dp = lax.empty(p.shape, F32)
    merge_rows = [full(a) for a in sv["proj"]] + [(p, OFF_GATE, 3 * D_MODEL)]
    dmerge, _ = _row_bwd(_fn_merge, merge_rows, [], [full(dm)], ts=256, name=tag("merge_b"), into=(dp, 3),
                         matmul_only=(0, 1, 2))
    dproj, dp = dmerge[:3], dmerge[3]
    dys = [_mm(dproj[i], wl["w_branch"][i], trans_b=True, name=tag(f"branch{i}_dx")) for i in range(3)]
    dw_branch = jnp.stack([_mm_tn(sv["ys"][i], dproj[i], name=tag(f"branch{i}_dw")) for i in range(3)])
    (dp,), dgm = _row_bwd(_fn_gmlp, [(p, OFF_GM, 2 * w)], pr["gmlp"], [full(dys[0])], ts=256, name=tag("gmlp_b"),
                          into=(dp, 0))
    (do, dp), (d_dn_g,) = _row_bwd(_fn_dn_out, [full(sv["o"]), (p, OFF_BZ, w)], pr["dn_g"], [full(dys[1])], ts=512,
                                   name=tag("dn_out_b"), into=(dp, 1))
    dqkvbg = _delta_bwd(sv["q"], sv["k"], sv["v"], sv["beta"], sv["g"], sv["states"], do, name=tag("delta_b"))
    dp, d_dn_in = _dn_in_bwd(p, pr["dn_in"], dqkvbg, dp, ts=256, name=tag("dn_in_b"))
    for off, part in zip((OFF_CQ, OFF_CK, OFF_CV), _sb_bwd(p, dys[2], name=tag("sb_b"))):
        dp = lax.dynamic_update_slice(dp, part, (0, off))
    dh0 = _mm(dp, wl["w_in"], trans_b=True, name=tag("proj_in_dx"))
    dw_in, db_in = _mm_tn(sv["h0"], dp, name=tag("proj_in_dw"), col_sums=True)
    (dx0,), (dg0,) = _row_bwd(_fn_rms_keep, [full(sv["x0"])], [pr["g"][0]], [full(dh0), full(dx1)], ts=512,
                              name=tag("rms0_b"))
    grads = dict(
        norm_g=jnp.concatenate([dg0, dg1, dg2, dg3], axis=0), w_in=dw_in, b_in=db_in.reshape(-1),
        sgu_ln_g=dgm[0].reshape(-1), sgu_ln_b=dgm[1].reshape(-1), w_spatial=dgm[2],
        b_spatial=dgm[3][:, :GM_GROUPS].T, conv_w=d_dn_in[0], a_log=d_dn_in[1][0, DN_HEADS:2 * DN_HEADS],
        dt_bias=d_dn_in[2][0, DN_HEADS:2 * DN_HEADS], dn_norm_g=d_dn_g.reshape(-1), w_branch=dw_branch,
        w_out=dw_out, w_ff1=dw_ff1, w_ff2=dw_ff2)
    return dx0, grads


def _local_step(x, target, weights):
    saved = []
    h = x
    layers = []
    for l in range(DEPTH):
        wl = {n: weights[n][l] for n in WEIGHT_ORDER}
        layers.append(wl)
        h, sv = _layer_fwd(h, wl, l)
        saved.append(sv)
    sq, dh = _loss_head(h, target, name="loss_head")
    grads = [None] * DEPTH
    for l in reversed(range(DEPTH)):
        dh, grads[l] = _layer_bwd(dh, saved[l], layers[l], l)
    stacked = {n: jnp.stack([grads[l][n].astype(GRAD_DTYPE) for l in range(DEPTH)]) for n in WEIGHT_ORDER}
    return sq[0, 0], dh, stacked


def kernel(x, norm_g, w_in, b_in, sgu_ln_g, sgu_ln_b, w_spatial, b_spatial, conv_w, a_log, dt_bias, dn_norm_g, w_branch, w_out, w_ff1, w_ff2, loss_target, m_norm_g, m_w_in, m_b_in, m_sgu_ln_g, m_sgu_ln_b, m_w_spatial, m_b_spatial, m_conv_w, m_a_log, m_dt_bias, m_dn_norm_g, m_w_branch, m_w_out, m_w_ff1, m_w_ff2, v_norm_g, v_w_in, v_b_in, v_sgu_ln_g, v_sgu_ln_b, v_w_spatial, v_b_spatial, v_conv_w, v_a_log, v_dt_bias, v_dn_norm_g, v_w_branch, v_w_out, v_w_ff1, v_w_ff2):
    local = dict(norm_g=norm_g, w_in=w_in, b_in=b_in, sgu_ln_g=sgu_ln_g, sgu_ln_b=sgu_ln_b, w_spatial=w_spatial,
                 b_spatial=b_spatial, conv_w=conv_w, a_log=a_log, dt_bias=dt_bias, dn_norm_g=dn_norm_g,
                 w_branch=w_branch, w_out=w_out, w_ff1=w_ff1, w_ff2=w_ff2)
    mom1 = dict(norm_g=m_norm_g, w_in=m_w_in, b_in=m_b_in, sgu_ln_g=m_sgu_ln_g, sgu_ln_b=m_sgu_ln_b,
                w_spatial=m_w_spatial, b_spatial=m_b_spatial, conv_w=m_conv_w, a_log=m_a_log, dt_bias=m_dt_bias,
                dn_norm_g=m_dn_norm_g, w_branch=m_w_branch, w_out=m_w_out, w_ff1=m_w_ff1, w_ff2=m_w_ff2)
    mom2 = dict(norm_g=v_norm_g, w_in=v_w_in, b_in=v_b_in, sgu_ln_g=v_sgu_ln_g, sgu_ln_b=v_sgu_ln_b,
                w_spatial=v_w_spatial, b_spatial=v_b_spatial, conv_w=v_conv_w, a_log=v_a_log, dt_bias=v_dt_bias,
                dn_norm_g=v_dn_norm_g, w_branch=v_w_branch, w_out=v_w_out, w_ff1=v_w_ff1, w_ff2=v_w_ff2)
    shard_names = [n for n, _ in SHARDED]
    shard_shapes = [local[n].shape for n in shard_names]
    repl_shapes = [local[n].shape for n in REPLICATED]

    weights = {n: local[n] for n in REPLICATED}
    for names, dtype, call in ((MATMUL_WEIGHTS, BF16, "gather_matmul_weights"), (VECTOR_WEIGHTS, F32, "gather_vectors")):
        gathered = _gather_chips(_pack([local[n] for n in names]).astype(dtype), name=call)
        per_chip = [_unpack(gathered[k], [local[n].shape for n in names]) for k in range(4)]
        for i, n in enumerate(names):
            weights[n] = jnp.concatenate([per_chip[k][i] for k in range(4)], axis=dict(SHARDED)[n])
    weights["w_in"] = _rearrange_w_in(weights["w_in"])
    weights["b_in"] = _rearrange_w_in(weights["b_in"])

    sq, dx, grads = _local_step(x[0], loss_target[0], weights)
    loss = lax.psum(0.5 * sq / D_MODEL, ("x", "y", "c"))
    grads["w_in"] = _restore_w_in(grads["w_in"])
    grads["b_in"] = _restore_w_in(grads["b_in"])

    blocks = jnp.stack([_pack([_chip_slice(grads[n], axis, k) for n, axis in SHARDED] + [grads[n] for n in REPLICATED])
                        for k in range(4)])
    rows = blocks.shape[1]
    half = rows // 2
    c = lax.axis_index("c")
    got = _halves_to_sibling(blocks, name="grads_to_sibling")
    chip_sum = _add_own_half(blocks, got, name="grads_chip_sum", out_dtype=BF16)
    by_chip = _scatter_chips(chip_sum, name="grads_scatter")
    my_half = _add_rows([by_chip[k] for k in range(4)], name="grads_sum")
    other_half = _to_sibling(my_half, name="grads_half_swap")
    total = jnp.concatenate([jnp.where(c == 0, my_half, other_half), jnp.where(c == 0, other_half, my_half)], axis=0)
    g_out = dict(zip(shard_names + list(REPLICATED), _unpack(total, shard_shapes + repl_shapes)))

    d_out, m_out, v_out = {}, {}, {}
    for n in WEIGHT_ORDER:
        d_out[n], m_out[n], v_out[n] = _adamw(local[n], g_out[n], mom1[n], mom2[n], name=f"adamw_{n}")
    return (loss, dx[None], *[g_out[n] for n in WEIGHT_ORDER], *[d_out[n] for n in WEIGHT_ORDER],
            *[m_out[n] for n in WEIGHT_ORDER], *[v_out[n] for n in WEIGHT_ORDER])
```

```python
import functools

import jax
import jax.numpy as jnp
from jax import lax
from jax.experimental import pallas as pl
from jax.experimental.pallas import tpu as pltpu

F32 = jnp.float32
BF16 = jnp.bfloat16
GRAD_DTYPE = BF16

D_MODEL = 1024
DEPTH = 4
BRANCH_WIDTH = 512
GM_CHUNK = 128
GM_GROUPS = 8
DN_HEADS = 4
DN_HEAD_DIM = 128
CONV_WIDTH = 4
DN_HALO = 8
DN_BLOCK = 128
DN_SOLVE_PASSES, DN_STATE_PASSES, DN_OUT_PASSES = 1, 1, 1
SB_HEAD_DIM = 64
SB_BLOCK = 128
SB_QUERY_ROWS = 256
SB_FWD_LANES = 256
SB_DEAD_LOG = -88.0
D_FF = 4096
P_IN = 7688
P_PAD = 8192
NORM_EPS = 1e-6
ADAM_LR, ADAM_B1, ADAM_B2, ADAM_EPS, ADAM_WD, ADAM_STEP = 0.001, 0.9, 0.999, 1e-08, 0.01, 10

OFF_DN, DN_IN_WIDTH = 0, 2048
OFF_BD = 1536
OFF_GM = 2048
OFF_GATE = 3072
OFF_BZ = 6144
OFF_CQ, OFF_CK, OFF_CV = 6656, 7168, 7680

LANES = 1024
VMEM_LIMIT_BYTES = 56 * 1024 * 1024

_HI = lax.Precision.HIGHEST


def _params(sem):
    return pltpu.CompilerParams(dimension_semantics=sem, vmem_limit_bytes=VMEM_LIMIT_BYTES)


def _mm(a, b, *, name, out_dtype=F32, bias=None, trans_b=False, beside=(), finish=None, tm=1024, tn=2048, tk=1024):
    m, k = a.shape
    n = b.shape[0] if trans_b else b.shape[1]
    tm, tn, tk = min(tm, m), min(tn, n), min(tk, k)
    assert m % tm == 0 and n % tn == 0 and k % tk == 0, (a.shape, b.shape)
    nk = k // tk
    dn = (((1,), (1,)), ((), ())) if trans_b else (((1,), (0,)), ((), ()))
    several = isinstance(out_dtype, tuple)
    out_dtypes = out_dtype if several else (out_dtype,)
    n_in = 2 + (bias is not None) + len(beside)

    def body(*refs):
        a_ref, b_ref = refs[:2]
        o_refs, acc = refs[n_in:-1], refs[-1]
        kk = pl.program_id(2)
        part = lax.dot_general(a_ref[...].astype(BF16), b_ref[...].astype(BF16), dn, preferred_element_type=F32)

        @pl.when(kk == 0)
        def _():
            acc[...] = part

        @pl.when(kk > 0)
        def _():
            acc[...] += part

        @pl.when(kk == nk - 1)
        def _():
            r = acc[...]
            if bias is not None:
                r = r + refs[2][...]
            tiles = (r,) if finish is None else finish(r, *[t[...] for t in refs[n_in - len(beside):n_in]])
            for o_ref, tile in zip(o_refs, tiles if isinstance(tiles, tuple) else (tiles,)):
                o_ref[...] = tile.astype(o_ref.dtype)

    in_specs = [pl.BlockSpec((tm, tk), lambda i, j, kk: (i, kk))]
    if trans_b:
        in_specs.append(pl.BlockSpec((tn, tk), lambda i, j, kk: (j, kk)))
    else:
        in_specs.append(pl.BlockSpec((tk, tn), lambda i, j, kk: (kk, j)))
    args = [a, b]
    if bias is not None:
        in_specs.append(pl.BlockSpec((1, tn), lambda i, j, kk: (0, j)))
        args.append(bias)
    for t in beside:
        assert t.shape == (m, n)
        in_specs.append(pl.BlockSpec((tm, tn), lambda i, j, kk: (i, j)))
        args.append(t)
    res = pl.pallas_call(
        body, name=name, grid=(m // tm, n // tn, nk),
        in_specs=in_specs, out_specs=[pl.BlockSpec((tm, tn), lambda i, j, kk: (i, j)) for _ in out_dtypes],
        out_shape=[jax.ShapeDtypeStruct((m, n), dt) for dt in out_dtypes],
        scratch_shapes=[pltpu.VMEM((tm, tn), F32)],
        compiler_params=_params(("parallel", "parallel", "arbitrary")),
    )(*args)
    return list(res) if several else res[0]


def _mm_tn(a, b, *, name, col_sums=False, tm=1024, tn=2048, ts=1024):
    out_dtype = GRAD_DTYPE
    s, ka = a.shape
    n = b.shape[1]
    tm, tn, ts = min(tm, ka), min(tn, n), min(ts, s)
    assert ka % tm == 0 and n % tn == 0 and s % ts == 0, (a.shape, b.shape)
    steps = s // ts

    def body(a_ref, b_ref, o_ref, *rest):
        acc = rest[-1]
        step = pl.program_id(2)
        part = lax.dot_general(a_ref[...].astype(BF16), b_ref[...].astype(BF16), (((0,), (0,)), ((), ())),
                               preferred_element_type=F32)

        @pl.when(step == 0)
        def _():
            acc[...] = part

        @pl.when(step > 0)
        def _():
            acc[...] += part

        @pl.when(step == steps - 1)
        def _():
            o_ref[...] = acc[...].astype(out_dtype)

        if col_sums:
            sums_ref = rest[0]
            first_rows = jnp.logical_and(pl.program_id(1) == 0, step == 0)

            @pl.when(first_rows)
            def _():
                sums_ref[...] = jnp.zeros_like(sums_ref)

            @pl.when(pl.program_id(1) == 0)
            def _():
                sums_ref[...] += jnp.sum(b_ref[...].astype(F32), axis=0, keepdims=True)

    out_specs = [pl.BlockSpec((tm, tn), lambda j, i, r: (i, j))]
    out_shape = [jax.ShapeDtypeStruct((ka, n), out_dtype)]
    if col_sums:
        out_specs.append(pl.BlockSpec((1, tn), lambda j, i, r: (0, j)))
        out_shape.append(jax.ShapeDtypeStruct((1, n), F32))
    res = pl.pallas_call(
        body, name=name, grid=(n // tn, ka // tm, steps),
        in_specs=[pl.BlockSpec((ts, tm), lambda j, i, r: (r, i)), pl.BlockSpec((ts, tn), lambda j, i, r: (r, j))],
        out_specs=out_specs, out_shape=out_shape,
        scratch_shapes=[pltpu.VMEM((tm, tn), F32)],
        compiler_params=_params(("parallel", "arbitrary", "arbitrary")),
    )(a, b)
    return list(res) if col_sums else res[0]


def _col_block(i, *, c):
    return (i, c)


def _whole(i, *, nd):
    return (0,) * nd


def _row_specs(rows, ts):
    specs = []
    for arr, off, w in rows:
        assert off % w == 0 and arr.shape[0] % ts == 0
        specs.append(pl.BlockSpec((ts, w), functools.partial(_col_block, c=off // w)))
    return specs


def _row_fwd(fn, rows, params, outs, *, ts, name):
    s = rows[0][0].shape[0]
    ts = min(ts, s)
    nr, npar = len(rows), len(params)

    def body(*refs):
        rv = [r[...].astype(F32) for r in refs[:nr]]
        pv = [p[...] for p in refs[nr:nr + npar]]
        for o_ref, val in zip(refs[nr + npar:], fn(pv, rv)):
            o_ref[...] = val.astype(o_ref.dtype)

    in_specs = _row_specs(rows, ts) + [pl.BlockSpec(p.shape, functools.partial(_whole, nd=p.ndim)) for p in params]
    res = pl.pallas_call(
        body, name=name, grid=(s // ts,), in_specs=in_specs,
        out_specs=[pl.BlockSpec((ts, w), lambda i: (i, 0)) for w, _ in outs],
        out_shape=[jax.ShapeDtypeStruct((s, w), dt) for w, dt in outs],
        compiler_params=_params(("parallel",)),
    )(*[r[0] for r in rows], *params)
    return list(res)


def _row_bwd(fn, rows, params, cts, *, ts, name, into=None, matmul_only=()):
    s = rows[0][0].shape[0]
    ts = min(ts, s)
    nr, npar, nc = len(rows), len(params), len(cts)
    n_in = nr + npar + nc + (into is not None)

    def body(*refs):
        rv = [r[...].astype(F32) for r in refs[:nr]]
        pv = [p[...] for p in refs[nr:nr + npar]]
        cv = [c[...].astype(F32) for c in refs[nr + npar:nr + npar + nc]]
        out_refs = refs[n_in:]
        _, vjp = jax.vjp(lambda p, r: tuple(fn(p, r)), pv, rv)
        dp, dr = vjp(tuple(cv))
        for o_ref, val in zip(out_refs[:nr], dr):
            o_ref[...] = val.astype(o_ref.dtype)

        @pl.when(pl.program_id(0) == 0)
        def _():
            for o_ref in out_refs[nr:]:
                o_ref[...] = jnp.zeros_like(o_ref)

        for o_ref, val in zip(out_refs[nr:], dp):
            o_ref[...] += val

    in_specs = (_row_specs(rows, ts) + [pl.BlockSpec(p.shape, functools.partial(_whole, nd=p.ndim)) for p in params]
                + _row_specs(cts, ts))
    out_specs = [pl.BlockSpec((ts, w), lambda i: (i, 0)) for _, _, w in rows]
    out_shape = [jax.ShapeDtypeStruct((s, w), BF16 if k in matmul_only else F32) for k, (_, _, w) in enumerate(rows)]
    args = [r[0] for r in rows] + list(params) + [c[0] for c in cts]
    aliases = {}
    if into is not None:
        buffer, at = into
        in_specs.append(pl.BlockSpec(memory_space=pl.ANY))
        args.append(buffer)
        out_specs[at] = _row_specs([(buffer,) + tuple(rows[at][1:])], ts)[0]
        out_shape[at] = jax.ShapeDtypeStruct(buffer.shape, buffer.dtype)
        aliases = {n_in - 1: at}
    out_specs += [pl.BlockSpec(p.shape, functools.partial(_whole, nd=p.ndim)) for p in params]
    out_shape += [jax.ShapeDtypeStruct(p.shape, F32) for p in params]
    res = pl.pallas_call(
        body, name=name, grid=(s // ts,), in_specs=in_specs, out_specs=out_specs, out_shape=out_shape,
        input_output_aliases=aliases, compiler_params=_params(("arbitrary",)),
    )(*args)
    res = list(res)
    return res[:nr], res[nr:]


def _rms(x, g):
    return x * lax.rsqrt(jnp.mean(x * x, axis=-1, keepdims=True) + NORM_EPS) * g


def _gelu(x):
    return 0.5 * x * (1.0 + lax.erf(x * (2.0 ** -0.5)))


def _softplus(x):
    return jnp.maximum(x, 0.0) + jnp.log1p(jnp.exp(-jnp.abs(x)))


def _iota2(shape, dim):
    return lax.broadcasted_iota(jnp.int32, shape, dim)


def _fn_rms(pv, rv):
    return [_rms(rv[0], pv[0])]


def _fn_rms_keep(pv, rv):
    return [_rms(rv[0], pv[0]), rv[0]]


def _fn_resid_rms(pv, rv):
    return [rv[0] + _rms(rv[1], pv[0])]


def _fn_rms_branch(pv, rv):
    return [_rms(rv[0], pv[0])]


def _fn_merge(pv, rv):
    return [sum(jax.nn.sigmoid(rv[3][:, i * D_MODEL:(i + 1) * D_MODEL]) * rv[i] for i in range(3))]


def _fn_gmlp(pv, rv):
    ln_g, ln_b, w_sp, b_t = pv
    u = _gelu(rv[0][:, :BRANCH_WIDTH])
    v = _gelu(rv[0][:, BRANCH_WIDTH:])
    vc = v - jnp.mean(v, axis=-1, keepdims=True)
    v = vc * lax.rsqrt(jnp.mean(vc * vc, axis=-1, keepdims=True) + NORM_EPS) * ln_g + ln_b
    t = GM_CHUNK
    causal = _iota2((t, t), 1) <= _iota2((t, t), 0)
    first = _iota2((t, 128), 1) < 64
    expand = (_iota2((128, BRANCH_WIDTH), 0) == _iota2((128, BRANCH_WIDTH), 1) // 64).astype(F32)
    b_full = jnp.dot(b_t, expand, precision=_HI, preferred_element_type=F32)
    w_bf = [jnp.where(causal, w_sp[g], 0.0).astype(BF16) for g in range(GM_GROUPS)]
    chunks = []
    for c in range(rv[0].shape[0] // t):
        pairs = []
        for p in range(GM_GROUPS // 2):
            vp = v[c * t:(c + 1) * t, 128 * p:128 * (p + 1)].astype(BF16)
            m0 = jnp.dot(w_bf[2 * p], vp, preferred_element_type=F32)
            m1 = jnp.dot(w_bf[2 * p + 1], vp, preferred_element_type=F32)
            pairs.append(jnp.where(first, m0, m1))
        chunks.append(jnp.concatenate(pairs, axis=1) + b_full)
    return [u * jnp.concatenate(chunks, axis=0)]


def _head_expand(col0):
    return (_iota2((128, BRANCH_WIDTH), 0) == _iota2((128, BRANCH_WIDTH), 1) // DN_HEAD_DIM + col0).astype(F32)


@functools.partial(jax.custom_vjp, nondiff_argnums=(1,))
def _roll_rows(x, n):
    return pltpu.roll(x, n, 0)


def _roll_rows_fwd(x, n):
    return pltpu.roll(x, n, 0), None


def _roll_rows_bwd(n, _, ct):
    return (pltpu.roll(ct, ct.shape[0] - n, 0),)


_roll_rows.defvjp(_roll_rows_fwd, _roll_rows_bwd)


def _fn_dn_in(pv, rv):
    conv_w, a_log, dt_b = pv
    cur, before = rv
    x = cur[:, :3 * BRANCH_WIDTH]
    ext = jnp.concatenate([before[:, :3 * BRANCH_WIDTH], x], axis=0)
    c = conv_w[CONV_WIDTH - 1:CONV_WIDTH, :] * x
    for j in range(CONV_WIDTH - 1):
        c = c + conv_w[j:j + 1, :] * _roll_rows(ext, CONV_WIDTH - 1 - j)[DN_HALO:]
    a = c * jax.nn.sigmoid(c)
    outs = []
    for part in range(3):
        heads = []
        for h in range(DN_HEADS):
            lo = part * BRANCH_WIDTH + h * DN_HEAD_DIM
            xh = a[:, lo:lo + DN_HEAD_DIM]
            if part < 2:
                xh = xh * lax.rsqrt(jnp.sum(xh * xh, axis=-1, keepdims=True) + NORM_EPS)
            heads.append(xh)
        outs.append(jnp.concatenate(heads, axis=1))
    bd = cur[:, OFF_BD:OFF_BD + 128]
    beta = jax.nn.sigmoid(bd)
    g = -jnp.exp(a_log) * _softplus(bd + dt_b)
    outs.append(jnp.dot(beta, _head_expand(0), precision=_HI, preferred_element_type=F32))
    outs.append(jnp.dot(g, _head_expand(DN_HEADS), precision=_HI, preferred_element_type=F32))
    return outs


def _dn_in_specs(ts, first_block):
    per = ts // DN_HALO
    return [pl.BlockSpec((ts, DN_IN_WIDTH), lambda i: (first_block(i), OFF_DN // DN_IN_WIDTH)),
            pl.BlockSpec((DN_HALO, DN_IN_WIDTH),
                         lambda i: (jnp.maximum(first_block(i) * per - 1, 0), OFF_DN // DN_IN_WIDTH))]


def _dn_in_fwd(p, params, *, ts, name):
    s = p.shape[0]
    ts = min(ts, s)

    def body(cur_ref, before_ref, cw_ref, al_ref, db_ref, *o_refs):
        before = jnp.where(pl.program_id(0) == 0, 0.0, before_ref[...])
        outs = _fn_dn_in([cw_ref[...], al_ref[...], db_ref[...]], [cur_ref[...], before])
        for o_ref, val in zip(o_refs, outs):
            o_ref[...] = val

    whole = [pl.BlockSpec(a.shape, functools.partial(_whole, nd=a.ndim)) for a in params]
    res = pl.pallas_call(
        body, name=name, grid=(s // ts,), in_specs=_dn_in_specs(ts, lambda i: i) + whole,
        out_specs=[pl.BlockSpec((ts, BRANCH_WIDTH), lambda i: (i, 0))] * 5,
        out_shape=[jax.ShapeDtypeStruct((s, BRANCH_WIDTH), F32)] * 5,
        compiler_params=_params(("parallel",)),
    )(p, p, *params)
    return list(res)


def _dn_in_bwd(p, params, cts, dp, *, ts, name):
    s = p.shape[0]
    ts = min(ts, s)
    nb = s // ts
    block = lambda i: nb - 1 - i

    def body(cur_ref, before_ref, cw_ref, al_ref, db_ref, *rest):
        ct_refs, dx_ref, dpar_refs, halo = rest[:5], rest[6], rest[7:10], rest[10]
        first = pl.program_id(0) == 0

        @pl.when(first)
        def _():
            halo[...] = jnp.zeros_like(halo)
            for o_ref in dpar_refs:
                o_ref[...] = jnp.zeros_like(o_ref)

        before = jnp.where(pl.program_id(0) == nb - 1, 0.0, before_ref[...])
        _, vjp = jax.vjp(lambda pv, cur, bef: tuple(_fn_dn_in(pv, [cur, bef])),
                         [cw_ref[...], al_ref[...], db_ref[...]], cur_ref[...], before)
        dpar, dcur, dbefore = vjp(tuple(c[...] for c in ct_refs))
        dx_ref[...] = jnp.concatenate([dcur[:ts - DN_HALO], dcur[ts - DN_HALO:] + halo[...]], axis=0)
        halo[...] = dbefore
        for o_ref, val in zip(dpar_refs, dpar):
            o_ref[...] += val

    whole = [pl.BlockSpec(a.shape, functools.partial(_whole, nd=a.ndim)) for a in params]
    tile = pl.BlockSpec((ts, BRANCH_WIDTH), lambda i: (block(i), 0))
    res = pl.pallas_call(
        body, name=name, grid=(nb,),
        in_specs=_dn_in_specs(ts, block) + whole + [tile] * 5 + [pl.BlockSpec(memory_space=pl.ANY)],
        out_specs=[pl.BlockSpec((ts, DN_IN_WIDTH), lambda i: (block(i), OFF_DN // DN_IN_WIDTH))] + whole,
        out_shape=[jax.ShapeDtypeStruct(dp.shape, dp.dtype)] + [jax.ShapeDtypeStruct(a.shape, F32) for a in params],
        scratch_shapes=[pltpu.VMEM((DN_HALO, DN_IN_WIDTH), F32)],
        input_output_aliases={10: 0}, compiler_params=_params(("arbitrary",)),
    )(p, p, *params, *cts, dp)
    return res[0], list(res[1:])


def _fn_dn_out(pv, rv):
    heads = []
    for h in range(DN_HEADS):
        sl = slice(h * DN_HEAD_DIM, (h + 1) * DN_HEAD_DIM)
        z = rv[1][:, sl]
        heads.append(_rms(rv[0][:, sl], pv[0]) * (z * jax.nn.sigmoid(z)))
    return [jnp.concatenate(heads, axis=1)]


_DIMS = {"nn": (((1,), (0,)), ((), ())), "nt": (((1,), (1,)), ((), ())), "tn": (((0,), (0,)), ((), ()))}
_DIMS_BWD = {"nn": (("nt", "c", "b"), ("tn", "a", "c")), "nt": (("nn", "c", "b"), ("tn", "c", "a")),
             "tn": (("nt", "b", "c"), ("nn", "a", "c"))}


def _bf16_dot(a, b, kind):
    return lax.dot_general(a.astype(BF16), b.astype(BF16), _DIMS[kind], preferred_element_type=F32)


def _pdot_raw(a, b, kind, mode):
    if mode == 1:
        return _bf16_dot(a, b, kind)
    if mode == 6:
        return lax.dot_general(a, b, _DIMS[kind], precision=_HI, preferred_element_type=F32)
    b_hi, b_lo = _split_bf16(b)
    if mode == 3:
        a_hi, a_lo = _split_bf16(a)
        return _bf16_dot(a_hi, b_hi, kind) + (_bf16_dot(a_hi, b_lo, kind) + _bf16_dot(a_lo, b_hi, kind))
    b_rest = (b - b_hi.astype(F32) - b_lo.astype(F32)).astype(BF16)
    return _bf16_dot(a, b_hi, kind) + (_bf16_dot(a, b_lo, kind) + _bf16_dot(a, b_rest, kind))


@functools.partial(jax.custom_vjp, nondiff_argnums=(2, 3))
def _pdot(a, b, kind, mode):
    return _pdot_raw(a, b, kind, mode)


def _pdot_fwd(a, b, kind, mode):
    return _pdot_raw(a, b, kind, mode), (a, b)


def _pdot_bwd(kind, mode, res, ct):
    ops = {"a": res[0], "b": res[1], "c": ct}
    (ka, a1, a2), (kb, b1, b2) = _DIMS_BWD[kind]
    if mode == "count":
        return jnp.zeros_like(res[0]), _pdot(ops[b1], ops[b2], kb, mode)
    return _pdot(ops[a1], ops[a2], ka, mode), _pdot(ops[b1], ops[b2], kb, mode)


_pdot.defvjp(_pdot_fwd, _pdot_bwd)


@jax.custom_vjp
def _unit_lower_inverses(mats):
    c = mats[0].shape[0]
    row, col = _iota2((c, c), 0), _iota2((c, c), 1)
    x = [(row == col).astype(F32) for _ in mats]
    shift = 0
    while (1 << shift) < c:
        pair = jnp.right_shift(row, shift + 1) == jnp.right_shift(col, shift + 1)
        between = pair & (jnp.right_shift(row, shift) != jnp.right_shift(col, shift))
        q = [jnp.where(between, a, 0.0) for a in mats]
        qd = [_pdot_raw(qi, xi, "nn", DN_SOLVE_PASSES) for qi, xi in zip(q, x)]
        x = [xi - _pdot_raw(xi, m, "nn", DN_SOLVE_PASSES) for xi, m in zip(x, qd)]
        shift += 1
    return tuple(x)


def _unit_lower_inverses_fwd(mats):
    x = _unit_lower_inverses(mats)
    return x, x


def _unit_lower_inverses_bwd(x, dx):
    inner = [_pdot_raw(d, xi, "nt", DN_SOLVE_PASSES) for d, xi in zip(dx, x)]
    return (tuple(-_pdot_raw(xi, m, "tn", DN_SOLVE_PASSES) for xi, m in zip(x, inner)),)


_unit_lower_inverses.defvjp(_unit_lower_inverses_fwd, _unit_lower_inverses_bwd)


def _head(h):
    return slice(h * DN_HEAD_DIM, (h + 1) * DN_HEAD_DIM)


def _delta_chunk(states, q, k, v, beta, g):
    c = DN_BLOCK
    heads = range(DN_HEADS)
    row, col = _iota2((c, c), 0), _iota2((c, c), 1)
    tri, strict = col <= row, col < row
    counts = jnp.concatenate([tri.astype(BF16), jnp.ones((c, c), BF16)], axis=0)
    sums = _pdot(counts, g, "nn", "count")
    gc = [sums[:c, _head(h)] for h in heads]
    gl = [sums[c:, _head(h)] for h in heads]
    qh, kh, vh, bh = ([t[:, _head(h)] for h in heads] for t in (q * (DN_HEAD_DIM ** -0.5), k, v, beta))
    decay = [jnp.where(tri, jnp.exp(jnp.where(tri, gc[h] - gc[h].T, 0.0)), 0.0) for h in heads]
    solve, carry, out = DN_SOLVE_PASSES, DN_STATE_PASSES, DN_OUT_PASSES
    kk = [_pdot(kh[h], kh[h], "nt", solve) for h in heads]
    x = _unit_lower_inverses(tuple(jnp.where(strict, bh[h] * kk[h] * decay[h], 0.0) for h in heads))
    eg = [jnp.exp(gc[h]) for h in heads]
    u = [_pdot(x[h], vh[h] * bh[h], "nn", solve) for h in heads]
    wk = [_pdot(x[h], kh[h] * (bh[h] * eg[h]), "nn", solve) for h in heads]
    qk = [jnp.where(tri, _pdot(qh[h], kh[h], "nt", out) * decay[h], 0.0) for h in heads]
    v_new = [u[h] - _pdot(wk[h], states[h], "nn", carry) for h in heads]
    o = [_pdot(qh[h] * eg[h], states[h], "nn", out) + _pdot(qk[h], v_new[h], "nn", out) for h in heads]
    nxt = [states[h] * jnp.exp(gl[h]) + _pdot(kh[h] * jnp.exp(gl[h] - gc[h]), v_new[h], "tn", carry)
           for h in heads]
    return jnp.concatenate(o, axis=1), tuple(nxt)


def _delta_fwd(q, k, v, beta, g, *, name):
    s = q.shape[0]
    c = DN_BLOCK
    nc = s // c

    def body(q_ref, k_ref, v_ref, b_ref, g_ref, o_ref, sp_ref, st):
        @pl.when(pl.program_id(0) == 0)
        def _():
            st[...] = jnp.zeros_like(st)

        states = tuple(st[h] for h in range(DN_HEADS))
        for h in range(DN_HEADS):
            sp_ref[0, h] = states[h]
        o, nxt = _delta_chunk(states, q_ref[...], k_ref[...], v_ref[...], b_ref[...], g_ref[...])
        for h in range(DN_HEADS):
            st[h] = nxt[h]
        o_ref[...] = o

    blk = pl.BlockSpec((c, BRANCH_WIDTH), lambda n: (n, 0))
    return pl.pallas_call(
        body, name=name, grid=(nc,), in_specs=[blk] * 5,
        out_specs=[blk, pl.BlockSpec((1, DN_HEADS, DN_HEAD_DIM, DN_HEAD_DIM), lambda n: (n, 0, 0, 0))],
        out_shape=[jax.ShapeDtypeStruct((s, BRANCH_WIDTH), F32),
                   jax.ShapeDtypeStruct((nc, DN_HEADS, DN_HEAD_DIM, DN_HEAD_DIM), F32)],
        scratch_shapes=[pltpu.VMEM((DN_HEADS, DN_HEAD_DIM, DN_HEAD_DIM), F32)],
        compiler_params=_params(("arbitrary",)),
    )(q, k, v, beta, g)


def _delta_bwd(q, k, v, beta, g, states, do, *, name):
    s = q.shape[0]
    c = DN_BLOCK
    nc = s // c

    def body(q_ref, k_ref, v_ref, b_ref, g_ref, sp_ref, do_ref, dq_ref, dk_ref, dv_ref, db_ref, dg_ref, dst):
        @pl.when(pl.program_id(0) == 0)
        def _():
            dst[...] = jnp.zeros_like(dst)

        states = tuple(sp_ref[0, h] for h in range(DN_HEADS))
        _, vjp = jax.vjp(_delta_chunk, states, q_ref[...], k_ref[...], v_ref[...], b_ref[...], g_ref[...])
        d = vjp((do_ref[...], tuple(dst[h] for h in range(DN_HEADS))))
        for h in range(DN_HEADS):
            dst[h] = d[0][h]
        for o_ref, val in zip((dq_ref, dk_ref, dv_ref, db_ref, dg_ref), d[1:]):
            o_ref[...] = val

    blk = pl.BlockSpec((c, BRANCH_WIDTH), lambda n: (nc - 1 - n, 0))
    res = pl.pallas_call(
        body, name=name, grid=(nc,),
        in_specs=[blk] * 5 + [pl.BlockSpec((1, DN_HEADS, DN_HEAD_DIM, DN_HEAD_DIM), lambda n: (nc - 1 - n, 0, 0, 0)), blk],
        out_specs=[blk] * 5, out_shape=[jax.ShapeDtypeStruct((s, BRANCH_WIDTH), F32)] * 5,
        scratch_shapes=[pltpu.VMEM((DN_HEADS, DN_HEAD_DIM, DN_HEAD_DIM), F32)],
        compiler_params=_params(("arbitrary",)),
    )(q, k, v, beta, g, states, do)
    return list(res)


def _split_bf16(x):
    hi = x.astype(BF16)
    return hi, (x - hi.astype(F32)).astype(BF16)


def _sb_consts():
    tq, tk = SB_QUERY_ROWS, SB_BLOCK
    row, col = _iota2((tk, tk), 0), _iota2((tk, tk), 1)
    ones = jnp.ones((tk, tk), BF16)
    later = jnp.concatenate([(row > col).astype(BF16), ones], axis=1)
    from_here = jnp.concatenate([(row >= col).astype(BF16), ones], axis=1)
    first = _iota2((tq, 128), 1) < SB_HEAD_DIM
    return later, from_here, first


def _sb_causal(d):
    tq, tk = SB_QUERY_ROWS, SB_BLOCK
    return _iota2((tq - d * tk, tk), 1) < _iota2((tq - d * tk, tk), 0)


def _sums(x, mat):
    hi, lo = _split_bf16(x)
    return jnp.dot(hi, mat, preferred_element_type=F32) + jnp.dot(lo, mat, preferred_element_type=F32)


def _sb_weights(qs, kb, accs, later, causal):
    tk = SB_BLOCK
    z = [lax.dot_general(qh, kb, (((1,), (1,)), ((), ())), preferred_element_type=F32) for qh in qs]
    lk = [-_softplus(zh) for zh in z]
    if causal is not None:
        lk = [jnp.where(causal, v, 0.0) for v in lk]
    cs = [_sums(v, later) for v in lk]
    e = [z[h] + lk[h] + cs[h][:, :tk] + accs[h] for h in range(len(qs))]
    if causal is not None:
        e = [jnp.where(causal, v, -1e30) for v in e]
    return lk, [jnp.exp(v) for v in e], [v[:, tk:] for v in cs]


def _sb_alive(accs):
    return jnp.max(functools.reduce(jnp.maximum, accs)) > SB_DEAD_LOG


def _sb_sweep(i, block, carry, accs_of, stop=None):
    per = SB_QUERY_ROWS // SB_BLOCK
    for d in reversed(range(per)):
        r0 = d * SB_BLOCK
        seen = block(i * per + d, jax.tree.map(lambda a: a[r0:], carry), _sb_causal(d), r0)
        carry = seen if r0 == 0 else jax.tree.map(lambda old, new: jnp.concatenate([old[:r0], new], axis=0), carry, seen)
    if stop is not None:
        return lax.fori_loop(0, i * per - 1 - stop, lambda jj, cr: block(i * per - 1 - jj, cr, None), carry)

    def step(state):
        j, _, cr = state
        cr = block(j, cr, None)
        return j - 1, _sb_alive(accs_of(cr)), cr

    j, _, carry = lax.while_loop(lambda st: jnp.logical_and(st[0] >= 0, st[1]), step,
                                 (i * per - 1, _sb_alive(accs_of(carry)), carry))
    return carry, j


def _sb_fwd(p, *, name):
    s = p.shape[0]
    t, tk = SB_QUERY_ROWS, SB_BLOCK
    scale = SB_HEAD_DIM ** -0.5

    lanes = SB_FWD_LANES
    heads = range(lanes // SB_HEAD_DIM)

    def body(q_ref, k_ref, v_ref, o_ref):
        i = pl.program_id(1)
        later, _, _ = _sb_consts()
        head_of = _iota2((t, lanes), 1) // SB_HEAD_DIM
        q = q_ref[...] * scale
        qs = [jnp.where(head_of == h, q, 0.0).astype(BF16) for h in heads]

        def block(j, carry, causal, r0=0):
            start = pl.multiple_of(j * tk, tk)
            kb = k_ref[pl.ds(start, tk), :].astype(BF16)
            vb = v_ref[pl.ds(start, tk), :].astype(BF16)
            _, w, tot = _sb_weights([qh[r0:] for qh in qs], kb, [carry[h][1] for h in heads], later, causal)
            out = [jnp.dot(w[h].astype(BF16), vb, preferred_element_type=F32) for h in heads]
            return tuple((carry[h][0] + out[h], carry[h][1] + tot[h]) for h in heads)

        start = tuple((jnp.zeros((t, lanes), F32), jnp.zeros((t, 128), F32)) for _ in heads)
        carry, _ = _sb_sweep(i, block, start, lambda cr: [c[1] for c in cr])
        out = carry[0][0]
        for h in heads[1:]:
            out = jnp.where(head_of == h, carry[h][0], out)
        o_ref[...] = out

    return pl.pallas_call(
        body, name=name, grid=(BRANCH_WIDTH // lanes, s // t),
        in_specs=[pl.BlockSpec((t, lanes), lambda pr, i: (i, OFF_CQ // lanes + pr)),
                  pl.BlockSpec((s, lanes), lambda pr, i: (0, OFF_CK // lanes + pr)),
                  pl.BlockSpec((s, lanes), lambda pr, i: (0, OFF_CV // lanes + pr))],
        out_specs=pl.BlockSpec((t, lanes), lambda pr, i: (i, pr)),
        out_shape=jax.ShapeDtypeStruct((s, BRANCH_WIDTH), F32),
        compiler_params=_params(("arbitrary", "arbitrary")),
    )(p, p, p)


def _sb_bwd(p, do, *, name):
    s = p.shape[0]
    t, tk = SB_QUERY_ROWS, SB_BLOCK
    scale = SB_HEAD_DIM ** -0.5

    def body(q_ref, k_ref, v_ref, do_ref, dq_ref, dk_ref, dv_ref):
        i = pl.program_id(1)

        @pl.when(i == 0)
        def _():
            dk_ref[...] = jnp.zeros_like(dk_ref)
            dv_ref[...] = jnp.zeros_like(dv_ref)

        later, from_here, first = _sb_consts()
        q = q_ref[...] * scale
        do = do_ref[...]
        qs = (jnp.where(first, q, 0.0).astype(BF16), jnp.where(first, 0.0, q).astype(BF16))
        dos = (jnp.where(first, do, 0.0).astype(BF16), jnp.where(first, 0.0, do).astype(BF16))

        def total(j, carry, causal, r0=0):
            start = pl.multiple_of(j * tk, tk)
            kb = k_ref[pl.ds(start, tk), :].astype(BF16)
            vb = v_ref[pl.ds(start, tk), :].astype(BF16)
            _, w, tot = _sb_weights([qh[r0:] for qh in qs], kb, [carry[h][0] for h in range(2)], later, causal)
            dw = [lax.dot_general(dos[h][r0:], vb, (((1,), (1,)), ((), ())), preferred_element_type=F32)
                  for h in range(2)]
            tde = [_sums(dw[h] * w[h], from_here)[:, tk:] for h in range(2)]
            return tuple((carry[h][0] + tot[h], carry[h][1] + tde[h]) for h in range(2))

        zero = jnp.zeros((t, 128), F32)
        sums, stop = _sb_sweep(i, total, ((zero, zero), (zero, zero)), lambda cr: (cr[0][0], cr[1][0]))
        deltas = (sums[0][1], sums[1][1])

        def block(j, carry, causal, r0=0):
            start = pl.multiple_of(j * tk, tk)
            kb = k_ref[pl.ds(start, tk), :].astype(BF16)
            vb = v_ref[pl.ds(start, tk), :].astype(BF16)
            both = range(2)
            tn = (((0,), (0,)), ((), ()))
            qr, dor, total_de = [x[r0:] for x in qs], [x[r0:] for x in dos], [x[r0:] for x in deltas]
            lk, w, tot = _sb_weights(qr, kb, [carry[h][1] for h in both], later, causal)
            dw = [lax.dot_general(dor[h], vb, (((1,), (1,)), ((), ())), preferred_element_type=F32) for h in both]
            de = [dw[h] * w[h] for h in both]
            cs = [_sums(de[h], from_here) for h in both]
            keep = [jnp.exp(lk[h]) for h in both]
            dz = [de[h] * keep[h] - (total_de[h] - (cs[h][:, :tk] + carry[h][2])) * (1.0 - keep[h]) for h in both]
            if causal is not None:
                dz = [jnp.where(causal, v, 0.0) for v in dz]
            dzb = [v.astype(BF16) for v in dz]
            dq = [jnp.dot(dzb[h], kb, preferred_element_type=F32) for h in both]
            dk = [lax.dot_general(dzb[h], qr[h], tn, preferred_element_type=F32) for h in both]
            dv = [lax.dot_general(w[h].astype(BF16), dor[h], tn, preferred_element_type=F32) for h in both]
            dk_ref[pl.ds(start, tk), :] += dk[0] + dk[1]
            dv_ref[pl.ds(start, tk), :] += dv[0] + dv[1]
            return tuple((carry[h][0] + dq[h], carry[h][1] + tot[h], carry[h][2] + cs[h][:, tk:]) for h in both)

        carry = _sb_sweep(i, block, ((zero, zero, zero), (zero, zero, zero)), None, stop=stop)
        dq_ref[...] = jnp.where(first, carry[0][0], carry[1][0]) * scale

    qblk = lambda off: pl.BlockSpec((t, 128), lambda pr, i: (i, off // 128 + pr))
    full = lambda off: pl.BlockSpec((s, 128), lambda pr, i: (0, off // 128 + pr))
    res = pl.pallas_call(
        body, name=name, grid=(BRANCH_WIDTH // 128, s // t),
        in_specs=[qblk(OFF_CQ), full(OFF_CK), full(OFF_CV), qblk(0)],
        out_specs=[qblk(0), full(0), full(0)],
        out_shape=[jax.ShapeDtypeStruct((s, BRANCH_WIDTH), F32)] * 3,
        compiler_params=_params(("arbitrary", "arbitrary")),
    )(p, p, p, do)
    return list(res)


def _loss_head(y, target, *, name, ts=512):
    s, d = y.shape
    ts = min(ts, s)

    def body(y_ref, t_ref, sq_ref, dy_ref):
        @pl.when(pl.program_id(0) == 0)
        def _():
            sq_ref[...] = jnp.zeros_like(sq_ref)

        err = y_ref[...] - t_ref[...]
        dy_ref[...] = err * (1.0 / d)
        tot = jnp.sum(jnp.sum(err * err, axis=1, keepdims=True), axis=0, keepdims=True)
        sq_ref[...] += jnp.broadcast_to(tot, sq_ref.shape)

    blk = pl.BlockSpec((ts, d), lambda i: (i, 0))
    return pl.pallas_call(
        body, name=name, grid=(s // ts,), in_specs=[blk, blk],
        out_specs=[pl.BlockSpec((1, 128), lambda i: (0, 0)), blk],
        out_shape=[jax.ShapeDtypeStruct((1, 128), F32), jax.ShapeDtypeStruct((s, d), F32)],
        compiler_params=_params(("arbitrary",)),
    )(y, target)


def _row_tile(r, limit=512):
    return max(t for t in range(16, limit + 1, 16) if r % t == 0)


def _adamw(w, g, m, v, *, name):
    shape = w.shape
    lanes = shape[-1]
    w, g, m, v = (a.reshape(-1, lanes) for a in (w, g, m, v))
    r = w.shape[0]
    ts = r if r <= 256 else _row_tile(r, 256 if lanes > LANES else 512)

    def body(w_ref, g_ref, m_ref, v_ref, d_ref, nm_ref, nv_ref):
        gv = g_ref[...]
        m_new = ADAM_B1 * m_ref[...] + (1.0 - ADAM_B1) * gv
        v_new = ADAM_B2 * v_ref[...] + (1.0 - ADAM_B2) * jnp.square(gv)
        m_hat = m_new / (1.0 - ADAM_B1 ** ADAM_STEP)
        v_hat = v_new / (1.0 - ADAM_B2 ** ADAM_STEP)
        d_ref[...] = -ADAM_LR * (m_hat / (jnp.sqrt(v_hat) + ADAM_EPS) + ADAM_WD * w_ref[...])
        nm_ref[...] = m_new
        nv_ref[...] = v_new

    blk = pl.BlockSpec((ts, lanes), lambda i: (i, 0))
    res = pl.pallas_call(
        body, name=name, grid=(r // ts,), in_specs=[blk] * 4, out_specs=[blk] * 3,
        out_shape=[jax.ShapeDtypeStruct((r, lanes), F32)] * 3,
        compiler_params=_params(("parallel",)),
    )(w, g, m, v)
    return [a.reshape(shape) for a in res]


def _add_rows(terms, *, name, out_dtype=F32):
    r = terms[0].shape[0]
    ts = _row_tile(r)

    def body(*refs):
        acc = refs[0][...].astype(F32)
        for ref in refs[1:-1]:
            acc = acc + ref[...].astype(F32)
        refs[-1][...] = acc.astype(out_dtype)

    blk = pl.BlockSpec((ts, LANES), lambda i: (i, 0))
    return pl.pallas_call(
        body, name=name, grid=(r // ts,), in_specs=[blk] * len(terms), out_specs=blk,
        out_shape=jax.ShapeDtypeStruct((r, LANES), out_dtype), compiler_params=_params(("parallel",)),
    )(*terms)


_HBM = pl.BlockSpec(memory_space=pltpu.HBM)
_MESH = pl.DeviceIdType.MESH


def _other_chips(x, y):
    return [(1 - x, y), (x, 1 - y), (1 - x, 1 - y)]


def _gather_chips(shard, *, name):
    r, lanes = shard.shape
    half = r // 2
    assert half * 2 == r

    def body(in_ref, out_ref, send_sems, recv_sems):
        x, y, c = lax.axis_index("x"), lax.axis_index("y"), lax.axis_index("c")
        me = 2 * x + y
        sibling = (x, y, 1 - c)
        chips = _other_chips(x, y)

        def copy(sem, chip, core_half, to):
            rows = out_ref.at[chip, pl.ds(core_half * half, half)]
            return pltpu.make_async_remote_copy(src_ref=rows, dst_ref=rows, send_sem=send_sems.at[sem],
                                                recv_sem=recv_sems.at[sem], device_id=to, device_id_type=_MESH)

        first = []
        for kk, (px, py) in enumerate(chips):
            cp = pltpu.make_async_remote_copy(
                src_ref=in_ref.at[pl.ds(c * half, half)], dst_ref=out_ref.at[me, pl.ds(c * half, half)],
                send_sem=send_sems.at[kk], recv_sem=recv_sems.at[kk], device_id=(px, py, c), device_id_type=_MESH)
            cp.start()
            first.append(cp)
        passed = [copy(3 + kk, 2 * px + py, c, sibling) for kk, (px, py) in enumerate(chips)]
        for kk, (px, py) in enumerate(chips):
            copy(kk, 2 * px + py, c, (px, py, c)).wait_recv()
            passed[kk].start()
        for kk, (px, py) in enumerate(chips):
            copy(3 + kk, 2 * px + py, 1 - c, sibling).wait_recv()
        for cp in first + passed:
            cp.wait_send()

    gathered = pl.pallas_call(
        body, name=name, in_specs=[_HBM], out_specs=_HBM,
        out_shape=jax.ShapeDtypeStruct((4, r, lanes), shard.dtype),
        scratch_shapes=[pltpu.SemaphoreType.DMA((6,)), pltpu.SemaphoreType.DMA((6,))],
    )(shard)
    me = 2 * lax.axis_index("x") + lax.axis_index("y")
    return lax.dynamic_update_slice(gathered, shard[None], (me, 0, 0))


def _to_sibling(block, *, name):
    def body(in_ref, out_ref, send_sem, recv_sem):
        x, y, c = lax.axis_index("x"), lax.axis_index("y"), lax.axis_index("c")
        cp = pltpu.make_async_remote_copy(src_ref=in_ref, dst_ref=out_ref, send_sem=send_sem, recv_sem=recv_sem,
                                          device_id=(x, y, 1 - c), device_id_type=_MESH)
        cp.start()
        cp.wait()

    return pl.pallas_call(
        body, name=name, in_specs=[_HBM], out_specs=_HBM, out_shape=jax.ShapeDtypeStruct(block.shape, block.dtype),
        scratch_shapes=[pltpu.SemaphoreType.DMA, pltpu.SemaphoreType.DMA],
    )(block)


def _halves_to_sibling(blocks, *, name):
    n, r, lanes = blocks.shape
    half = r // 2

    def body(in_ref, out_ref, send_sem, recv_sem):
        x, y, c = lax.axis_index("x"), lax.axis_index("y"), lax.axis_index("c")
        cp = pltpu.make_async_remote_copy(src_ref=in_ref.at[pl.ds(0, n), pl.ds((1 - c) * half, half)], dst_ref=out_ref,
                                          send_sem=send_sem, recv_sem=recv_sem, device_id=(x, y, 1 - c),
                                          device_id_type=_MESH)
        cp.start()
        cp.wait()

    return pl.pallas_call(
        body, name=name, in_specs=[_HBM], out_specs=_HBM, out_shape=jax.ShapeDtypeStruct((n, half, lanes), blocks.dtype),
        scratch_shapes=[pltpu.SemaphoreType.DMA, pltpu.SemaphoreType.DMA],
    )(blocks)


def _add_own_half(blocks, got, *, name, out_dtype):
    n, r, lanes = blocks.shape
    half = r // 2
    ts = _row_tile(half)
    steps = half // ts
    core = lax.axis_index("c").astype(jnp.int32).reshape(1)

    def body(core_ref, a_ref, b_ref, o_ref):
        o_ref[...] = (a_ref[...].astype(F32) + b_ref[...].astype(F32)).astype(out_dtype)

    return pl.pallas_call(
        body, name=name,
        grid_spec=pltpu.PrefetchScalarGridSpec(
            num_scalar_prefetch=1, grid=(n, steps),
            in_specs=[pl.BlockSpec((1, ts, lanes), lambda k, i, core_ref: (k, core_ref[0] * steps + i, 0)),
                      pl.BlockSpec((1, ts, lanes), lambda k, i, core_ref: (k, i, 0))],
            out_specs=pl.BlockSpec((1, ts, lanes), lambda k, i, core_ref: (k, i, 0))),
        out_shape=jax.ShapeDtypeStruct((n, half, lanes), out_dtype),
        compiler_params=_params(("parallel", "parallel")),
    )(core, blocks, got)


def _scatter_chips(blocks, *, name):
    _, r, lanes = blocks.shape

    def body(in_ref, out_ref, send_sems, recv_sems, local_sem):
        x, y, c = lax.axis_index("x"), lax.axis_index("y"), lax.axis_index("c")
        me = 2 * x + y
        mine = pltpu.make_async_copy(in_ref.at[me], out_ref.at[me], local_sem)
        mine.start()
        copies = []
        for kk, (px, py) in enumerate(_other_chips(x, y)):
            cp = pltpu.make_async_remote_copy(src_ref=in_ref.at[2 * px + py], dst_ref=out_ref.at[me],
                                              send_sem=send_sems.at[kk], recv_sem=recv_sems.at[kk],
                                              device_id=(px, py, c), device_id_type=_MESH)
            cp.start()
            copies.append(cp)
        for kk, (px, py) in enumerate(_other_chips(x, y)):
            pltpu.make_async_remote_copy(src_ref=in_ref.at[me], dst_ref=out_ref.at[2 * px + py],
                                         send_sem=send_sems.at[kk], recv_sem=recv_sems.at[kk], device_id=(px, py, c),
                                         device_id_type=_MESH).wait_recv()
        for cp in copies:
            cp.wait_send()
        mine.wait()

    return pl.pallas_call(
        body, name=name, in_specs=[_HBM], out_specs=_HBM, out_shape=jax.ShapeDtypeStruct((4, r, lanes), blocks.dtype),
        scratch_shapes=[pltpu.SemaphoreType.DMA((3,)), pltpu.SemaphoreType.DMA((3,)), pltpu.SemaphoreType.DMA],
    )(blocks)


SHARDED = (("norm_g", 2), ("w_in", 2), ("conv_w", 2), ("w_branch", 3), ("w_out", 1), ("w_ff1", 2), ("w_ff2", 1))
MATMUL_WEIGHTS = ("w_in", "w_branch", "w_out", "w_ff1", "w_ff2")
VECTOR_WEIGHTS = ("norm_g", "conv_w")
REPLICATED = ("b_in", "sgu_ln_g", "sgu_ln_b", "w_spatial", "b_spatial", "a_log", "dt_bias", "dn_norm_g")
WEIGHT_ORDER = ("norm_g", "w_in", "b_in", "sgu_ln_g", "sgu_ln_b", "w_spatial", "b_spatial", "conv_w", "a_log",
                "dt_bias", "dn_norm_g", "w_branch", "w_out", "w_ff1", "w_ff2")
PACK_ROW_MULTIPLE = 32


def _size(shape):
    n = 1
    for dim in shape:
        n *= dim
    return n


def _wide(shape):
    return len(shape) >= 2 and shape[-1] > LANES


def _rows_of(shape):
    if _wide(shape):
        return (_size(shape) // shape[-1]) * -(-shape[-1] // LANES)
    return -(-_size(shape) // LANES)


def _pack(arrays):
    parts = []
    for a in arrays:
        if _wide(a.shape):
            rows = a.reshape(-1, a.shape[-1])
            pad = -a.shape[-1] % LANES
            if pad:
                rows = jnp.concatenate([rows, jnp.zeros((rows.shape[0], pad), a.dtype)], axis=1)
            parts += [rows[:, c:c + LANES] for c in range(0, rows.shape[1], LANES)]
            continue
        flat = a.reshape(-1)
        pad = _rows_of(a.shape) * LANES - flat.shape[0]
        if pad:
            flat = jnp.concatenate([flat, jnp.zeros((pad,), flat.dtype)])
        parts.append(flat.reshape(-1, LANES))
    rows = sum(p.shape[0] for p in parts)
    pad = -rows % PACK_ROW_MULTIPLE
    if pad:
        parts.append(jnp.zeros((pad, LANES), parts[0].dtype))
    return jnp.concatenate(parts, axis=0)


def _unpack(buf, shapes):
    out, row = [], 0
    for shape in shapes:
        rows = _rows_of(shape)
        part = buf[row:row + rows]
        if _wide(shape):
            chunks = -(-shape[-1] // LANES)
            each = rows // chunks
            whole = jnp.concatenate([part[c * each:(c + 1) * each] for c in range(chunks)], axis=1)
            out.append(whole[:, :shape[-1]].reshape(shape))
        else:
            out.append(part.reshape(-1)[:_size(shape)].reshape(shape))
        row += rows
    return out


def _chip_slice(a, axis, k):
    size = a.shape[axis] // 4
    return lax.slice_in_dim(a, k * size, (k + 1) * size, axis=axis)


def _rearrange_w_in(w):
    pad = jnp.zeros(w.shape[:-1] + (P_PAD - P_IN,), w.dtype)
    return jnp.concatenate([w[..., 1024:2560], w[..., 3072:3080], pad, w[..., 0:1024], w[..., 4616:7688],
                            w[..., 2560:3072], w[..., 3080:4616]], axis=-1)


def _restore_w_in(w):
    return jnp.concatenate([w[..., 2048:3072], w[..., 0:1536], w[..., 6144:6656], w[..., 1536:1544],
                            w[..., 6656:8192], w[..., 3072:6144]], axis=-1)


def _layer_params(wl):
    row = lambda v: v.reshape(1, -1)
    pad128 = lambda v, at: jnp.pad(v, (at, 128 - at - v.shape[0])).reshape(1, 128)
    return dict(
        g=[row(wl["norm_g"][i]) for i in range(4)],
        gmlp=[row(wl["sgu_ln_g"]), row(wl["sgu_ln_b"]), wl["w_spatial"],
              jnp.pad(wl["b_spatial"].T, ((0, 0), (0, 128 - GM_GROUPS)))],
        dn_in=[wl["conv_w"], pad128(wl["a_log"], DN_HEADS), pad128(wl["dt_bias"], DN_HEADS)],
        dn_g=[row(wl["dn_norm_g"])],
    )


def _layer_fwd(x0, wl, l):
    tag = lambda s: f"{s}_l{l}"
    pr = _layer_params(wl)
    w = BRANCH_WIDTH
    h0 = _row_fwd(_fn_rms, [(x0, 0, D_MODEL)], [pr["g"][0]], [(D_MODEL, BF16)], ts=512, name=tag("rms0"))[0]
    p = _mm(h0, wl["w_in"], bias=wl["b_in"].reshape(1, -1), name=tag("proj_in"))
    ya = _row_fwd(_fn_gmlp, [(p, OFF_GM, 2 * w)], pr["gmlp"], [(w, BF16)], ts=256, name=tag("gmlp"))[0]
    q, k, v, beta, g = _dn_in_fwd(p, pr["dn_in"], ts=256, name=tag("dn_in"))
    o, states = _delta_fwd(q, k, v, beta, g, name=tag("delta"))
    yb = _row_fwd(_fn_dn_out, [(o, 0, w), (p, OFF_BZ, w)], pr["dn_g"], [(w, BF16)], ts=512, name=tag("dn_out"))[0]
    yc = _sb_fwd(p, name=tag("sb"))
    ys = [ya, yb, yc]
    proj = [_mm(ys[i], wl["w_branch"][i], out_dtype=BF16, name=tag(f"branch{i}")) for i in range(3)]
    merge_rows = [(a, 0, D_MODEL) for a in proj] + [(p, OFF_GATE, 3 * D_MODEL)]
    m = _row_fwd(_fn_merge, merge_rows, [], [(D_MODEL, BF16)], ts=256, name=tag("merge"))[0]
    mixed = _mm(m, wl["w_out"], name=tag("out"))
    x1 = _row_fwd(_fn_resid_rms, [(x0, 0, D_MODEL), (mixed, 0, D_MODEL)], [pr["g"][1]], [(D_MODEL, F32)], ts=512,
                  name=tag("resid1"))[0]
    h2 = _row_fwd(_fn_rms, [(x1, 0, D_MODEL)], [pr["g"][2]], [(D_MODEL, BF16)], ts=512, name=tag("rms2"))[0]
    a, r = _mm(h2, wl["w_ff1"], name=tag("ff1"), out_dtype=(BF16, BF16),
               finish=lambda t: (jnp.maximum(t, 0.0), jnp.square(jnp.maximum(t, 0.0))))
    f = _mm(r, wl["w_ff2"], name=tag("ff2"))
    x2 = _row_fwd(_fn_resid_rms, [(x1, 0, D_MODEL), (f, 0, D_MODEL)], [pr["g"][3]], [(D_MODEL, F32)], ts=512,
                  name=tag("resid2"))[0]
    saved = dict(x0=x0, h0=h0, p=p, q=q, k=k, v=v, beta=beta, g=g, states=states, o=o, ys=ys,
                 proj=proj, m=m, mixed=mixed, x1=x1, h2=h2, a=a, r=r, f=f)
    return x2, saved


def _layer_bwd(dx2, sv, wl, l):
    tag = lambda s: f"{s}_l{l}"
    pr = _layer_params(wl)
    w = BRANCH_WIDTH
    full = lambda a: (a, 0, a.shape[1])
    p = sv["p"]
    (df,), (dg3,) = _row_bwd(_fn_rms_branch, [full(sv["f"])], [pr["g"][3]], [full(dx2)], ts=512, name=tag("resid2_b"),
                             matmul_only=(0,))
    da = _mm(df, wl["w_ff2"], trans_b=True, name=tag("ff2_dx"), out_dtype=BF16, beside=(sv["a"],),
             finish=lambda t, relu_a: 2.0 * relu_a.astype(F32) * t)
    dw_ff2 = _mm_tn(sv["r"], df, name=tag("ff2_dw"))
    dh2 = _mm(da, wl["w_ff1"], trans_b=True, name=tag("ff1_dx"))
    dw_ff1 = _mm_tn(sv["h2"], da, name=tag("ff1_dw"))
    (dx1,), (dg2,) = _row_bwd(_fn_rms_keep, [full(sv["x1"])], [pr["g"][2]], [full(dh2), full(dx2)], ts=512,
                              name=tag("rms2_b"))
    (dmixed,), (dg1,) = _row_bwd(_fn_rms_branch, [full(sv["mixed"])], [pr["g"][1]], [full(dx1)], ts=512,
                                 name=tag("resid1_b"), matmul_only=(0,))
    dm = _mm(dmixed, wl["w_out"], trans_b=True, name=tag("out_dx"))
    dw_out = _mm_tn(sv["m"], dmixed, name=tag("out_dw"))
    dp = lax.empty(p.shape, F32)
    merge_rows = [full(a) for a in sv["proj"]] + [(p, OFF_GATE, 3 * D_MODEL)]
    dmerge, _ = _row_bwd(_fn_merge, merge_rows, [], [full(dm)], ts=256, name=tag("merge_b"), into=(dp, 3),
                         matmul_only=(0, 1, 2))
    dproj, dp = dmerge[:3], dmerge[3]
    dys = [_mm(dproj[i], wl["w_branch"][i], trans_b=True, name=tag(f"branch{i}_dx")) for i in range(3)]
    dw_branch = jnp.stack([_mm_tn(sv["ys"][i], dproj[i], name=tag(f"branch{i}_dw")) for i in range(3)])
    (dp,), dgm = _row_bwd(_fn_gmlp, [(p, OFF_GM, 2 * w)], pr["gmlp"], [full(dys[0])], ts=256, name=tag("gmlp_b"),
                          into=(dp, 0))
    (do, dp), (d_dn_g,) = _row_bwd(_fn_dn_out, [full(sv["o"]), (p, OFF_BZ, w)], pr["dn_g"], [full(dys[1])], ts=512,
                                   name=tag("dn_out_b"), into=(dp, 1))
    dqkvbg = _delta_bwd(sv["q"], sv["k"], sv["v"], sv["beta"], sv["g"], sv["states"], do, name=tag("delta_b"))
    dp, d_dn_in = _dn_in_bwd(p, pr["dn_in"], dqkvbg, dp, ts=256, name=tag("dn_in_b"))
    for off, part in zip((OFF_CQ, OFF_CK, OFF_CV), _sb_bwd(p, dys[2], name=tag("sb_b"))):
        dp = lax.dynamic_update_slice(dp, part, (0, off))
    dh0 = _mm(dp, wl["w_in"], trans_b=True, name=tag("proj_in_dx"))
    dw_in, db_in = _mm_tn(sv["h0"], dp, name=tag("proj_in_dw"), col_sums=True)
    (dx0,), (dg0,) = _row_bwd(_fn_rms_keep, [full(sv["x0"])], [pr["g"][0]], [full(dh0), full(dx1)], ts=512,
                              name=tag("rms0_b"))
    grads = dict(
        norm_g=jnp.concatenate([dg0, dg1, dg2, dg3], axis=0), w_in=dw_in, b_in=db_in.reshape(-1),
        sgu_ln_g=dgm[0].reshape(-1), sgu_ln_b=dgm[1].reshape(-1), w_spatial=dgm[2],
        b_spatial=dgm[3][:, :GM_GROUPS].T, conv_w=d_dn_in[0], a_log=d_dn_in[1][0, DN_HEADS:2 * DN_HEADS],
        dt_bias=d_dn_in[2][0, DN_HEADS:2 * DN_HEADS], dn_norm_g=d_dn_g.reshape(-1), w_branch=dw_branch,
        w_out=dw_out, w_ff1=dw_ff1, w_ff2=dw_ff2)
    return dx0, grads


def _local_step(x, target, weights):
    saved = []
    h = x
    layers = []
    for l in range(DEPTH):
        wl = {n: weights[n][l] for n in WEIGHT_ORDER}
        layers.append(wl)
        h, sv = _layer_fwd(h, wl, l)
        saved.append(sv)
    sq, dh = _loss_head(h, target, name="loss_head")
    grads = [None] * DEPTH
    for l in reversed(range(DEPTH)):
        dh, grads[l] = _layer_bwd(dh, saved[l], layers[l], l)
    stacked = {n: jnp.stack([grads[l][n].astype(GRAD_DTYPE) for l in range(DEPTH)]) for n in WEIGHT_ORDER}
    return sq[0, 0], dh, stacked


def kernel(x, norm_g, w_in, b_in, sgu_ln_g, sgu_ln_b, w_spatial, b_spatial, conv_w, a_log, dt_bias, dn_norm_g, w_branch, w_out, w_ff1, w_ff2, loss_target, m_norm_g, m_w_in, m_b_in, m_sgu_ln_g, m_sgu_ln_b, m_w_spatial, m_b_spatial, m_conv_w, m_a_log, m_dt_bias, m_dn_norm_g, m_w_branch, m_w_out, m_w_ff1, m_w_ff2, v_norm_g, v_w_in, v_b_in, v_sgu_ln_g, v_sgu_ln_b, v_w_spatial, v_b_spatial, v_conv_w, v_a_log, v_dt_bias, v_dn_norm_g, v_w_branch, v_w_out, v_w_ff1, v_w_ff2):
    local = dict(norm_g=norm_g, w_in=w_in, b_in=b_in, sgu_ln_g=sgu_ln_g, sgu_ln_b=sgu_ln_b, w_spatial=w_spatial,
                 b_spatial=b_spatial, conv_w=conv_w, a_log=a_log, dt_bias=dt_bias, dn_norm_g=dn_norm_g,
                 w_branch=w_branch, w_out=w_out, w_ff1=w_ff1, w_ff2=w_ff2)
    mom1 = dict(norm_g=m_norm_g, w_in=m_w_in, b_in=m_b_in, sgu_ln_g=m_sgu_ln_g, sgu_ln_b=m_sgu_ln_b,
                w_spatial=m_w_spatial, b_spatial=m_b_spatial, conv_w=m_conv_w, a_log=m_a_log, dt_bias=m_dt_bias,
                dn_norm_g=m_dn_norm_g, w_branch=m_w_branch, w_out=m_w_out, w_ff1=m_w_ff1, w_ff2=m_w_ff2)
    mom2 = dict(norm_g=v_norm_g, w_in=v_w_in, b_in=v_b_in, sgu_ln_g=v_sgu_ln_g, sgu_ln_b=v_sgu_ln_b,
                w_spatial=v_w_spatial, b_spatial=v_b_spatial, conv_w=v_conv_w, a_log=v_a_log, dt_bias=v_dt_bias,
                dn_norm_g=v_dn_norm_g, w_branch=v_w_branch, w_out=v_w_out, w_ff1=v_w_ff1, w_ff2=v_w_ff2)
    shard_names = [n for n, _ in SHARDED]
    shard_shapes = [local[n].shape for n in shard_names]
    repl_shapes = [local[n].shape for n in REPLICATED]

    weights = {n: local[n] for n in REPLICATED}
    for names, dtype, call in ((MATMUL_WEIGHTS, BF16, "gather_matmul_weights"), (VECTOR_WEIGHTS, F32, "gather_vectors")):
        gathered = _gather_chips(_pack([local[n] for n in names]).astype(dtype), name=call)
        per_chip = [_unpack(gathered[k], [local[n].shape for n in names]) for k in range(4)]
        for i, n in enumerate(names):
            weights[n] = jnp.concatenate([per_chip[k][i] for k in range(4)], axis=dict(SHARDED)[n])
    weights["w_in"] = _rearrange_w_in(weights["w_in"])
    weights["b_in"] = _rearrange_w_in(weights["b_in"])

    sq, dx, grads = _local_step(x[0], loss_target[0], weights)
    loss = lax.psum(0.5 * sq / D_MODEL, ("x", "y", "c"))
    grads["w_in"] = _restore_w_in(grads["w_in"])
    grads["b_in"] = _restore_w_in(grads["b_in"])

    blocks = jnp.stack([_pack([_chip_slice(grads[n], axis, k) for n, axis in SHARDED] + [grads[n] for n in REPLICATED])
                        for k in range(4)])
    rows = blocks.shape[1]
    half = rows // 2
    c = lax.axis_index("c")
    got = _halves_to_sibling(blocks, name="grads_to_sibling")
    chip_sum = _add_own_half(blocks, got, name="grads_chip_sum", out_dtype=BF16)
    by_chip = _scatter_chips(chip_sum, name="grads_scatter")
    my_half = _add_rows([by_chip[k] for k in range(4)], name="grads_sum")
    other_half = _to_sibling(my_half, name="grads_half_swap")
    total = jnp.concatenate([jnp.where(c == 0, my_half, other_half), jnp.where(c == 0, other_half, my_half)], axis=0)
    g_out = dict(zip(shard_names + list(REPLICATED), _unpack(total, shard_shapes + repl_shapes)))

    d_out, m_out, v_out = {}, {}, {}
    for n in WEIGHT_ORDER:
        d_out[n], m_out[n], v_out[n] = _adamw(local[n], g_out[n], mom1[n], mom2[n], name=f"adamw_{n}")
    return (loss, dx[None], *[g_out[n] for n in WEIGHT_ORDER], *[d_out[n] for n in WEIGHT_ORDER],
            *[m_out[n] for n in WEIGHT_ORDER], *[v_out[n] for n in WEIGHT_ORDER])
```
